```python
import math
import jax, jax.numpy as jnp
from jax import lax
import numpy as np

D_MODEL = 1024
BATCH = 4
SEQ = 4096
DEPTH = 1

HEAD_DIM = 64
DSA_WIDTH = D_MODEL // 2
DIFF_WIDTH = D_MODEL - DSA_WIDTH
N_DSA_HEADS = DSA_WIDTH // HEAD_DIM
DIFF_V_DIM = 2 * HEAD_DIM
N_DIFF_HEADS = DIFF_WIDTH // DIFF_V_DIM
N_IDX_HEADS = 8
IDX_DIM = 64
TOPK_MAX = 256
D_FF = 4 * D_MODEL
Q_BLOCK = 128
ROPE_THETA = 10000.0
NORM_EPS = 1e-6
LN_EPS = 1e-5

SPLIT_SIZES = (
    DSA_WIDTH,
    DSA_WIDTH,
    DSA_WIDTH,
    N_IDX_HEADS * IDX_DIM,
    IDX_DIM,
    N_IDX_HEADS,
    2 * N_DIFF_HEADS * HEAD_DIM,
    2 * N_DIFF_HEADS * HEAD_DIM,
    DIFF_WIDTH,
)
IN_WIDTH = sum(SPLIT_SIZES)
SPLIT_POINTS = tuple(int(p) for p in np.cumsum(SPLIT_SIZES)[:-1])

kernel_name = "hymba_dsa_diffattn_adaln_layer"


def lambda_init(layer_idx):
    return 0.8 - 0.6 * math.exp(-0.3 * layer_idx)


def rms_norm(x, gain):
    x32 = x.astype(jnp.float32)
    y = x32 * lax.rsqrt(jnp.mean(jnp.square(x32), axis=-1, keepdims=True) + NORM_EPS)
    return (y * gain.astype(jnp.float32)).astype(x.dtype)


def layer_norm(x, w, b):
    x32 = x.astype(jnp.float32)
    mu = jnp.mean(x32, axis=-1, keepdims=True)
    var = jnp.mean(jnp.square(x32 - mu), axis=-1, keepdims=True)
    y = (x32 - mu) * lax.rsqrt(var + LN_EPS) * w.astype(jnp.float32) + b.astype(jnp.float32)
    return y.astype(x.dtype)


def modulate(h, shift, scale):
    return h * (1.0 + scale[:, None, :]) + shift[:, None, :]


def rope(x, positions):
    d = x.shape[-1]
    half = d // 2
    inv_freq = ROPE_THETA ** (-jnp.arange(half, dtype=jnp.float32) / half)
    ang = positions.astype(jnp.float32)[..., None] * inv_freq
    cos = jnp.cos(ang)[:, :, None, :]
    sin = jnp.sin(ang)[:, :, None, :]
    x32 = x.astype(jnp.float32)
    x1, x2 = x32[..., :half], x32[..., half:]
    out = jnp.concatenate([x1 * cos - x2 * sin, x2 * cos + x1 * sin], axis=-1)
    return out.astype(x.dtype)


def dsa_attention(q, k, v, q_idx, k_idx, w_idx):
    B, L, H, dh = q.shape
    top_k = min(TOPK_MAX, L // 4)
    n_blocks = L // Q_BLOCK
    key_pos = jnp.arange(L)
    q32 = q.astype(jnp.float32) * (dh ** -0.5)
    k32 = k.astype(jnp.float32)
    v32 = v.astype(jnp.float32)
    qi32 = q_idx.astype(jnp.float32)
    ki32 = k_idx.astype(jnp.float32)
    wi32 = w_idx.astype(jnp.float32) * (N_IDX_HEADS ** -0.5) * (IDX_DIM ** -0.5)

    def block(i):
        start = i * Q_BLOCK
        qb = lax.dynamic_slice_in_dim(q32, start, Q_BLOCK, axis=1)
        qib = lax.dynamic_slice_in_dim(qi32, start, Q_BLOCK, axis=1)
        wb = lax.dynamic_slice_in_dim(wi32, start, Q_BLOCK, axis=1)
        q_pos = start + jnp.arange(Q_BLOCK)
        causal = key_pos[None, :] <= q_pos[:, None]
        rel = jax.nn.relu(jnp.einsum('bqhd,bsd->bqhs', qib, ki32))
        score = jnp.einsum('bqhs,bqh->bqs', rel, wb)
        score = jnp.where(causal[None], score, -jnp.inf)
        _, sel = lax.top_k(score, top_k)
        valid = sel <= q_pos[None, :, None]
        k_sel = jax.vmap(lambda kb, ib: kb[ib])(k32, sel)
        v_sel = jax.vmap(lambda vb, ib: vb[ib])(v32, sel)
        s = jnp.einsum('bqhd,bqkhd->bhqk', qb, k_sel)
        s = jnp.where(valid[:, None], s, -jnp.inf)
        p = jax.nn.softmax(s, axis=-1)
        return jnp.einsum('bhqk,bqkhd->bqhd', p, v_sel)

    out = lax.map(block, jnp.arange(n_blocks))
    out = jnp.transpose(out, (1, 0, 2, 3, 4)).reshape(B, L, H, dh)
    return out.astype(q.dtype)


def diff_attention(q, k, v, lam, subln_w, lam_init):
    B, L, H, _, dh = q.shape
    n_blocks = L // Q_BLOCK
    key_pos = jnp.arange(L)
    q32 = q.astype(jnp.float32) * (dh ** -0.5)
    k32 = k.astype(jnp.float32)
    v32 = v.astype(jnp.float32)

    def block(i):
        start = i * Q_BLOCK
        qb = lax.dynamic_slice_in_dim(q32, start, Q_BLOCK, axis=1)
        q_pos = start + jnp.arange(Q_BLOCK)
        causal = key_pos[None, :] <= q_pos[:, None]
        s = jnp.einsum('bqhcd,bshcd->bhcqs', qb, k32)
        s = jnp.where(causal[None, None, None], s, -jnp.inf)
        p = jax.nn.softmax(s, axis=-1)
        a = p[:, :, 0] - lam * p[:, :, 1]
        return jnp.einsum('bhqs,bshe->bqhe', a, v32)

    out = lax.map(block, jnp.arange(n_blocks))
    out = jnp.transpose(out, (1, 0, 2, 3, 4)).reshape(B, L, H, 2 * dh)
    out = out * lax.rsqrt(jnp.mean(jnp.square(out), axis=-1, keepdims=True) + NORM_EPS)
    out = out * subln_w.astype(jnp.float32) * (1.0 - lam_init)
    return out.astype(q.dtype)


def setup_inputs(seed: int = 0) -> dict:
    key = jax.random.key(seed)
    ks = jax.random.split(key, 20)
    f32 = jnp.float32
    D = D_MODEL
    x = jax.random.normal(ks[0], (BATCH, SEQ, D), f32)
    c = jax.random.normal(ks[1], (BATCH, D), f32)
    offset = jax.random.randint(ks[2], (BATCH, 1), 0, 1024, dtype=jnp.int32)
    positions = (jnp.arange(SEQ, dtype=jnp.int32)[None, :] + offset).astype(jnp.int32)
    w_ada = jax.random.normal(ks[3], (DEPTH, D, 6 * D), f32) * D ** -0.5
    b_ada = jax.random.normal(ks[4], (DEPTH, 6 * D), f32) * 0.02
    norm1_w = 1.0 + 0.02 * jax.random.normal(ks[5], (DEPTH, D), f32)
    w_in = jax.random.normal(ks[6], (DEPTH, D, IN_WIDTH), f32) * D ** -0.5
    idx_k_ln_w = 1.0 + 0.02 * jax.random.normal(ks[7], (DEPTH, IDX_DIM), f32)
    idx_k_ln_b = 0.02 * jax.random.normal(ks[8], (DEPTH, IDX_DIM), f32)
    lambda_q1 = 0.1 * jax.random.normal(ks[9], (DEPTH, HEAD_DIM), f32)
    lambda_k1 = 0.1 * jax.random.normal(ks[10], (DEPTH, HEAD_DIM), f32)
    lambda_q2 = 0.1 * jax.random.normal(ks[11], (DEPTH, HEAD_DIM), f32)
    lambda_k2 = 0.1 * jax.random.normal(ks[12], (DEPTH, HEAD_DIM), f32)
    subln_w = 1.0 + 0.02 * jax.random.normal(ks[13], (DEPTH, DIFF_V_DIM), f32)
    w_out = jax.random.normal(ks[14], (DEPTH, DSA_WIDTH + DIFF_WIDTH, D), f32) * (DSA_WIDTH + DIFF_WIDTH) ** -0.5
    norm2_w = 1.0 + 0.02 * jax.random.normal(ks[15], (DEPTH, D), f32)
    w_ff1 = jax.random.normal(ks[16], (DEPTH, D, D_FF), f32) * D ** -0.5
    w_ff2 = jax.random.normal(ks[17], (DEPTH, D_FF, D), f32) * D_FF ** -0.5
    norm_f_w = 1.0 + 0.02 * jax.random.normal(ks[18], (D,), f32)
    return {"x": x, "c": c, "positions": positions, "w_ada": w_ada, "b_ada": b_ada,
            "norm1_w": norm1_w, "w_in": w_in, "idx_k_ln_w": idx_k_ln_w, "idx_k_ln_b": idx_k_ln_b,
            "lambda_q1": lambda_q1, "lambda_k1": lambda_k1, "lambda_q2": lambda_q2, "lambda_k2": lambda_k2,
            "subln_w": subln_w, "w_out": w_out, "norm2_w": norm2_w, "w_ff1": w_ff1, "w_ff2": w_ff2,
            "norm_f_w": norm_f_w}


def reference(x, c, positions, w_ada, b_ada, norm1_w, w_in, idx_k_ln_w, idx_k_ln_b,
              lambda_q1, lambda_k1, lambda_q2, lambda_k2, subln_w, w_out, norm2_w,
              w_ff1, w_ff2, norm_f_w):
    B, L, D = x.shape
    for l in range(DEPTH):
        mod = jnp.dot(jax.nn.silu(c), w_ada[l]) + b_ada[l]
        sh1, sc1, g1, sh2, sc2, g2 = jnp.split(mod, 6, axis=-1)

        h = modulate(rms_norm(x, norm1_w[l]), sh1, sc1)
        proj = jnp.dot(h, w_in[l])
        (a_q, a_k, a_v, i_q, i_k, i_w, d_q, d_k, d_v) = jnp.split(proj, SPLIT_POINTS, axis=-1)

        a_q = rope(a_q.reshape(B, L, N_DSA_HEADS, HEAD_DIM), positions)
        a_k = rope(a_k.reshape(B, L, N_DSA_HEADS, HEAD_DIM), positions)
        a_v = a_v.reshape(B, L, N_DSA_HEADS, HEAD_DIM)
        i_q = rope(i_q.reshape(B, L, N_IDX_HEADS, IDX_DIM), positions)
        i_k = layer_norm(i_k, idx_k_ln_w[l], idx_k_ln_b[l])
        i_k = rope(i_k[:, :, None, :], positions)[:, :, 0, :]
        out_a = dsa_attention(a_q, a_k, a_v, i_q, i_k, i_w)

        d_q = rope(d_q.reshape(B, L, 2 * N_DIFF_HEADS, HEAD_DIM), positions)
        d_k = rope(d_k.reshape(B, L, 2 * N_DIFF_HEADS, HEAD_DIM), positions)
        d_q = d_q.reshape(B, L, N_DIFF_HEADS, 2, HEAD_DIM)
        d_k = d_k.reshape(B, L, N_DIFF_HEADS, 2, HEAD_DIM)
        d_v = d_v.reshape(B, L, N_DIFF_HEADS, DIFF_V_DIM)
        lam_init = lambda_init(l)
        lam = (jnp.exp(jnp.sum(lambda_q1[l].astype(jnp.float32) * lambda_k1[l].astype(jnp.float32)))
               - jnp.exp(jnp.sum(lambda_q2[l].astype(jnp.float32) * lambda_k2[l].astype(jnp.float32)))
               + lam_init)
        out_b = diff_attention(d_q, d_k, d_v, lam, subln_w[l], lam_init)

        mix = jnp.concatenate([out_a.reshape(B, L, DSA_WIDTH),
                               out_b.reshape(B, L, DIFF_WIDTH)], axis=-1)
        x = x + g1[:, None, :] * jnp.dot(mix, w_out[l])

        h = modulate(rms_norm(x, norm2_w[l]), sh2, sc2)
        ff = jnp.dot(jnp.square(jax.nn.relu(jnp.dot(h, w_ff1[l]))), w_ff2[l])
        x = x + g2[:, None, :] * ff
    return rms_norm(x, norm_f_w)
```

```python
import functools
import math

import jax
import jax.numpy as jnp
from jax import lax
from jax.experimental import pallas as pl
from jax.experimental.pallas import tpu as pltpu

F32 = jnp.float32
BF16 = jnp.bfloat16

HEAD_DIM = 64
N_DSA_HEADS = 8
N_IDX_HEADS = 8
IDX_DIM = 64
N_DIFF_HEADS = 4
DIFF_V_DIM = 128
TOPK_MAX = 256
ROPE_THETA = 10000.0
NORM_EPS = 1e-6
LN_EPS = 1e-5
LANES = 128
MASK_VALUE = -1e30
F32_LOWEST = float(jnp.finfo(jnp.float32).min)
VMEM_LIMIT_BYTES = 56 * 1024 * 1024


def _dot(a, b):
    return jnp.dot(a, b, preferred_element_type=F32)


def _dot_nt(a, b):
    return lax.dot_general(a, b, (((1,), (1,)), ((), ())), preferred_element_type=F32)


def _const_spec(shape):
    zeros = (0,) * len(shape)
    return pl.BlockSpec(shape, lambda *_: zeros, pipeline_mode=pl.Buffered(1))


def _adaln_kernel(c_ref, w_ref, b_ref, o_ref):
    c = c_ref[...]
    s = c / (1.0 + jnp.exp(-c))
    o_ref[...] = jnp.dot(s, w_ref[...], preferred_element_type=F32,
                         precision=lax.Precision.HIGHEST) + b_ref[...]


def _adaln(c, w, b):
    bsz, d = c.shape
    n = w.shape[1]
    tn = 1536
    return pl.pallas_call(
        _adaln_kernel,
        grid=(n // tn,),
        in_specs=[pl.BlockSpec((bsz, d), lambda j: (0, 0)),
                  pl.BlockSpec((d, tn), lambda j: (0, j)),
                  pl.BlockSpec((1, tn), lambda j: (0, j))],
        out_specs=pl.BlockSpec((bsz, tn), lambda j: (0, j)),
        out_shape=jax.ShapeDtypeStruct((bsz, n), F32),
        compiler_params=pltpu.CompilerParams(vmem_limit_bytes=VMEM_LIMIT_BYTES),
        name="adaln",
    )(c, w, b.reshape(1, n))


def _rope_group(x, cos, sin_signed, first_half):
    nxt = pltpu.roll(x, LANES - HEAD_DIM // 2, axis=1)
    prv = pltpu.roll(x, HEAD_DIM // 2, axis=1)
    return x * cos + jnp.where(first_half, nxt, prv) * sin_signed


def _in_proj_kernel(x_ref, pos_ref, sh_ref, sc_ref, nw_ref, invf_ref, wm_ref, ws_ref, ww_ref,
                    lnw_ref, lnb_ref,
                    aq_ref, ak_ref, av_ref, iq_ref, ik_ref, iw_ref, dq_ref, dk_ref, dv_ref):
    x = x_ref[0]
    ms = jnp.mean(x * x, axis=-1, keepdims=True)
    h = x * lax.rsqrt(ms + NORM_EPS) * nw_ref[...]
    h = h * (1.0 + sc_ref[0]) + sh_ref[0]
    hb = h.astype(BF16)

    lane = lax.broadcasted_iota(jnp.int32, (1, LANES), 1)
    first_half = (lane % HEAD_DIM) < (HEAD_DIM // 2)
    ang = pos_ref[0].astype(F32) * invf_ref[...]
    cos = jnp.cos(ang)
    sin = jnp.sin(ang)
    sin_signed = jnp.where(first_half, -sin, sin)

    def roped(col0, out_ref, scale):
        for g in range(4):
            w = wm_ref[:, col0 + g * LANES: col0 + (g + 1) * LANES]
            y = _rope_group(_dot(hb, w), cos, sin_signed, first_half)
            if scale != 1.0:
                y = y * scale
            out_ref[0, :, g * LANES:(g + 1) * LANES] = y.astype(out_ref.dtype)

    def plain(col0, out_ref):
        for g in range(4):
            w = wm_ref[:, col0 + g * LANES: col0 + (g + 1) * LANES]
            out_ref[0, :, g * LANES:(g + 1) * LANES] = _dot(hb, w).astype(out_ref.dtype)

    qscale = HEAD_DIM ** -0.5
    roped(0, aq_ref, qscale)
    roped(512, ak_ref, 1.0)
    plain(1024, av_ref)
    roped(1536, iq_ref, 1.0)
    roped(2048, dq_ref, qscale)
    roped(2560, dk_ref, 1.0)
    plain(3072, dv_ref)

    sm = _dot(hb, ws_ref[...])
    lo_half = lane < IDX_DIM
    mu = jnp.sum(sm, axis=-1, keepdims=True) * (1.0 / IDX_DIM)
    d = jnp.where(lo_half, sm - mu, 0.0)
    var = jnp.sum(d * d, axis=-1, keepdims=True) * (1.0 / IDX_DIM)
    y = d * lax.rsqrt(var + LN_EPS) * lnw_ref[...] + lnb_ref[...]
    y = y + pltpu.roll(y, IDX_DIM, axis=1)
    ik_ref[0] = _rope_group(y, cos, sin_signed, first_half).astype(ik_ref.dtype)

    wv = _dot(hb, ww_ref[...])
    iw_ref[0] = wv[:, :N_IDX_HEADS] * ((N_IDX_HEADS ** -0.5) * (IDX_DIM ** -0.5))


def _in_proj(x, pos3, sh1, sc1, nw, invf, wm, ws, ww, lnw, lnb, *, tile):
    bsz, seq, d = x.shape
    tok = lambda b, i: (b, i, 0)
    per_b = lambda b, i: (b, 0, 0)
    wide = jax.ShapeDtypeStruct((bsz, seq, 512), BF16)
    out_shape = [wide, wide, wide, wide,
                 jax.ShapeDtypeStruct((bsz, seq, LANES), BF16),
                 jax.ShapeDtypeStruct((bsz, seq, N_IDX_HEADS), F32),
                 wide, wide, wide]
    wide_spec = pl.BlockSpec((1, tile, 512), tok)
    out_specs = [wide_spec, wide_spec, wide_spec, wide_spec,
                 pl.BlockSpec((1, tile, LANES), tok),
                 pl.BlockSpec((1, tile, N_IDX_HEADS), tok),
                 wide_spec, wide_spec, wide_spec]
    return pl.pallas_call(
        _in_proj_kernel,
        grid=(bsz, seq // tile),
        in_specs=[pl.BlockSpec((1, tile, d), tok),
                  pl.BlockSpec((1, tile, 1), tok),
                  pl.BlockSpec((1, 1, d), per_b),
                  pl.BlockSpec((1, 1, d), per_b),
                  _const_spec((1, d)),
                  _const_spec((1, LANES)),
                  _const_spec(wm.shape),
                  _const_spec(ws.shape),
                  _const_spec(ww.shape),
                  _const_spec((1, LANES)),
                  _const_spec((1, LANES))],
        out_specs=out_specs,
        out_shape=out_shape,
        compiler_params=pltpu.CompilerParams(
            dimension_semantics=("parallel", "parallel"), vmem_limit_bytes=VMEM_LIMIT_BYTES),
        name="in_proj",
    )(x, pos3, sh1, sc1, nw, invf, wm, ws, ww, lnw, lnb)


def _head_q(q_all, h, lane_lo):
    pair = q_all[:, (h // 2) * LANES:(h // 2 + 1) * LANES]
    keep = lane_lo if h % 2 == 0 else jnp.logical_not(lane_lo)
    return jnp.where(keep, pair, jnp.zeros_like(pair))


def _dsa_kernel(iq_ref, ik_ref, iw_ref, aq_ref, ak_ref, av_ref, o_ref, score_ref, *, blk, top_k):
    i = pl.program_id(1)
    n_chunks = i + 1
    lane = lax.broadcasted_iota(jnp.int32, (1, LANES), 1)
    lane_lo = lane < HEAD_DIM
    row = i * blk + lax.broadcasted_iota(jnp.int32, (blk, 1), 0)
    col0 = lax.broadcasted_iota(jnp.int32, (1, blk), 1)
    kf = float(top_k)

    iq_all = iq_ref[0]
    w_all = iw_ref[0]
    q_heads = [_head_q(iq_all, h, lane_lo) for h in range(N_IDX_HEADS)]
    w_heads = [w_all[:, h:h + 1] for h in range(N_IDX_HEADS)]

    def score_body(c, carry):
        mn, mx = carry
        start = pl.multiple_of(c * blk, blk)
        kk = ik_ref[0, pl.ds(start, blk), :]
        acc = jnp.zeros((blk, blk), F32)
        for h in range(N_IDX_HEADS):
            acc = acc + jnp.maximum(_dot_nt(q_heads[h], kk), 0.0) * w_heads[h]
        causal = (c * blk + col0) <= row
        score_ref[c] = jnp.where(causal, acc, -jnp.inf)
        mx = jnp.maximum(mx, jnp.max(jnp.where(causal, acc, -jnp.inf), axis=-1, keepdims=True))
        mn = jnp.minimum(mn, jnp.min(jnp.where(causal, acc, jnp.inf), axis=-1, keepdims=True))
        return mn, mx

    mn, mx = lax.fori_loop(0, n_chunks, score_body,
                           (jnp.full((blk, 1), jnp.inf, F32), jnp.full((blk, 1), -jnp.inf, F32)))

    def count(cand, strict):
        def body(c, acc):
            s = score_ref[c]
            hit = (s > cand) if strict else (s >= cand)
            m = jnp.where(hit, 1.0, 0.0)
            for j in range(blk // LANES):
                acc = acc + m[:, j * LANES:(j + 1) * LANES]
            return acc
        acc = lax.fori_loop(0, n_chunks, body, jnp.zeros((blk, LANES), F32))
        return jnp.sum(acc, axis=-1, keepdims=True)

    n_valid = (row + 1).astype(F32)
    need = n_valid > kf
    c_max = count(mx, False)
    tie_at_max = jnp.logical_and(need, c_max >= kf)
    lo0 = jnp.where(tie_at_max, mx, mn)
    cnt0 = jnp.where(tie_at_max, c_max, n_valid)
    done0 = jnp.logical_or(jnp.logical_not(need), jnp.logical_or(tie_at_max, cnt0 == kf))
    active0 = jnp.where(done0, 0.0, 1.0)

    def bis_cond(st):
        return st[4] > 0.0

    def bis_body(st):
        lo, hi, cnt_lo, active, _ = st
        mid = 0.5 * lo + 0.5 * hi
        conv = jnp.logical_or(mid <= lo, mid >= hi)
        cnt = count(mid, False)
        upd = jnp.logical_and(active > 0.0, jnp.logical_not(conv))
        ge = cnt >= kf
        up_lo = jnp.logical_and(upd, ge)
        up_hi = jnp.logical_and(upd, jnp.logical_not(ge))
        lo = jnp.where(up_lo, mid, lo)
        cnt_lo = jnp.where(up_lo, cnt, cnt_lo)
        hi = jnp.where(up_hi, mid, hi)
        finished = jnp.logical_or(conv, jnp.logical_and(up_lo, cnt == kf))
        active = jnp.where(finished, 0.0, active)
        return lo, hi, cnt_lo, active, jnp.max(active)

    lo, _, cnt_lo, _, _ = lax.while_loop(bis_cond, bis_body, (lo0, mx, cnt0, active0, jnp.max(active0)))
    thr = jnp.where(need, lo, F32_LOWEST)

    excess = jnp.logical_and(need, cnt_lo > kf)

    @pl.when(jnp.max(jnp.where(excess, 1.0, 0.0)) > 0.0)
    def _():
        allow = kf - count(thr, True)
        r_i = lax.broadcasted_iota(jnp.int32, (blk, blk), 0)
        c_i = lax.broadcasted_iota(jnp.int32, (blk, blk), 1)
        before = jnp.where(r_i < c_i, 1.0, 0.0).astype(BF16)

        def tie_body(c, seen):
            s = score_ref[c]
            eq = jnp.logical_and(s == thr, excess)
            eqf = jnp.where(eq, 1.0, 0.0)
            rank = seen + _dot(eqf.astype(BF16), before)
            drop = jnp.logical_and(eq, rank >= allow)
            score_ref[c] = jnp.where(drop, -jnp.inf, s)
            return seen + jnp.sum(eqf, axis=-1, keepdims=True)

        lax.fori_loop(0, n_chunks, tie_body, jnp.zeros((blk, 1), F32))

    aq_all = aq_ref[0]
    outs = []
    for h in range(N_DSA_HEADS):
        qh = _head_q(aq_all, h, lane_lo)
        pair = slice((h // 2) * LANES, (h // 2 + 1) * LANES)

        def att_body(c, carry, qh=qh, pair=pair):
            m, l, acc = carry
            start = pl.multiple_of(c * blk, blk)
            kk = ak_ref[0, pl.ds(start, blk), pair]
            vv = av_ref[0, pl.ds(start, blk), pair]
            s = _dot_nt(qh, kk)
            s = jnp.where(score_ref[c] >= thr, s, MASK_VALUE)
            m_new = jnp.maximum(m, jnp.max(s, axis=-1, keepdims=True))
            alpha = jnp.exp(m - m_new)
            p = jnp.exp(s - m_new)
            l = alpha * l + jnp.sum(p, axis=-1, keepdims=True)
            acc = alpha * acc + _dot(p.astype(BF16), vv)
            return m_new, l, acc

        _, l, acc = lax.fori_loop(
            0, n_chunks, att_body,
            (jnp.full((blk, 1), MASK_VALUE, F32), jnp.zeros((blk, 1), F32), jnp.zeros((blk, LANES), F32)))
        outs.append(acc / l)

    for j in range(N_DSA_HEADS // 2):
        o_ref[0, :, j * LANES:(j + 1) * LANES] = jnp.where(lane_lo, outs[2 * j], outs[2 * j + 1]).astype(o_ref.dtype)


def _dsa(iq, ik2, iw, aq, ak, av, *, blk):
    bsz, seq, _ = aq.shape
    top_k = min(TOPK_MAX, seq // 4)
    qblk = lambda b, i: (b, i, 0)
    whole = lambda b, i: (b, 0, 0)
    return pl.pallas_call(
        functools.partial(_dsa_kernel, blk=blk, top_k=top_k),
        grid=(bsz, seq // blk),
        in_specs=[pl.BlockSpec((1, blk, 512), qblk),
                  pl.BlockSpec((1, seq, LANES), whole),
                  pl.BlockSpec((1, blk, N_IDX_HEADS), qblk),
                  pl.BlockSpec((1, blk, 512), qblk),
                  pl.BlockSpec((1, seq, 512), whole),
                  pl.BlockSpec((1, seq, 512), whole)],
        out_specs=pl.BlockSpec((1, blk, 512), qblk),
        out_shape=jax.ShapeDtypeStruct((bsz, seq, 512), BF16),
        scratch_shapes=[pltpu.VMEM((seq // blk, blk, blk), F32)],
        compiler_params=pltpu.CompilerParams(
            dimension_semantics=("parallel", "arbitrary"), vmem_limit_bytes=VMEM_LIMIT_BYTES),
        name="dsa",
    )(iq, ik2, iw, aq, ak, av)


def _diff_kernel(lam_ref, q_ref, k_ref, v_ref, sw_ref, o_ref, *, blk, lam_init):
    i = pl.program_id(1)
    lane = lax.broadcasted_iota(jnp.int32, (1, LANES), 1)
    lane_lo = lane < HEAD_DIM
    row = lax.broadcasted_iota(jnp.int32, (blk, 1), 0)
    col = lax.broadcasted_iota(jnp.int32, (1, blk), 1)
    diag_ok = col <= row

    lam_vecs = lam_ref[...]
    lam = (jnp.exp(jnp.sum(lam_vecs[0:1] * lam_vecs[1:2], axis=-1, keepdims=True))
           - jnp.exp(jnp.sum(lam_vecs[2:3] * lam_vecs[3:4], axis=-1, keepdims=True)) + lam_init)

    q_all = q_ref[0]
    for h in range(N_DIFF_HEADS):
        cols = slice(h * LANES, (h + 1) * LANES)
        res = []
        for comp in range(2):
            qh = _head_q(q_all, 2 * h + comp, lane_lo)

            def step(c, carry, masked, qh=qh, cols=cols):
                m, l, acc = carry
                start = pl.multiple_of(c * blk, blk)
                kk = k_ref[0, pl.ds(start, blk), cols]
                vv = v_ref[0, pl.ds(start, blk), cols]
                s = _dot_nt(qh, kk)
                if masked:
                    s = jnp.where(diag_ok, s, MASK_VALUE)
                m_new = jnp.maximum(m, jnp.max(s, axis=-1, keepdims=True))
                alpha = jnp.exp(m - m_new)
                p = jnp.exp(s - m_new)
                l = alpha * l + jnp.sum(p, axis=-1, keepdims=True)
                acc = alpha * acc + _dot(p.astype(BF16), vv)
                return m_new, l, acc

            init = (jnp.full((blk, 1), MASK_VALUE, F32), jnp.zeros((blk, 1), F32),
                    jnp.zeros((blk, LANES), F32))
            carry = lax.fori_loop(0, i, functools.partial(step, masked=False), init)
            _, l, acc = step(i, carry, True)
            res.append(acc / l)
        out = res[0] - lam * res[1]
        out = out * lax.rsqrt(jnp.mean(out * out, axis=-1, keepdims=True) + NORM_EPS)
        out = out * sw_ref[...] * (1.0 - lam_init)
        o_ref[0, :, cols] = out.astype(o_ref.dtype)


def _diff_attn(lam_vecs, dq, dk, dv, subln_w, *, blk, lam_init):
    bsz, seq, _ = dq.shape
    qblk = lambda b, i: (b, i, 0)
    whole = lambda b, i: (b, 0, 0)
    return pl.pallas_call(
        functools.partial(_diff_kernel, blk=blk, lam_init=lam_init),
        grid=(bsz, seq // blk),
        in_specs=[_const_spec(lam_vecs.shape),
                  pl.BlockSpec((1, blk, 512), qblk),
                  pl.BlockSpec((1, seq, 512), whole),
                  pl.BlockSpec((1, seq, 512), whole),
                  _const_spec((1, LANES))],
        out_specs=pl.BlockSpec((1, blk, 512), qblk),
        out_shape=jax.ShapeDtypeStruct((bsz, seq, 512), BF16),
        compiler_params=pltpu.CompilerParams(
            dimension_semantics=("parallel", "arbitrary"), vmem_limit_bytes=VMEM_LIMIT_BYTES),
        name="diff_attn",
    )(lam_vecs, dq, dk, dv, subln_w)


def _out_mlp_kernel(x_ref, ma_ref, mb_ref, g1_ref, sh_ref, sc_ref, g2_ref, n2_ref, nf_ref,
                    woa_ref, wob_ref, w1_ref, w2_ref, o_ref, *, ff_chunk):
    x = x_ref[0]
    o = _dot(ma_ref[0], woa_ref[...]) + _dot(mb_ref[0], wob_ref[...])
    x1 = x + g1_ref[0] * o
    ms = jnp.mean(x1 * x1, axis=-1, keepdims=True)
    h = x1 * lax.rsqrt(ms + NORM_EPS) * n2_ref[...]
    hb = (h * (1.0 + sc_ref[0]) + sh_ref[0]).astype(BF16)
    d_ff = w1_ref.shape[1]
    ff = jnp.zeros_like(x)
    for j in range(d_ff // ff_chunk):
        u = jnp.maximum(_dot(hb, w1_ref[:, j * ff_chunk:(j + 1) * ff_chunk]), 0.0)
        ff = ff + _dot((u * u).astype(BF16), w2_ref[j * ff_chunk:(j + 1) * ff_chunk, :])
    x2 = x1 + g2_ref[0] * ff
    ms2 = jnp.mean(x2 * x2, axis=-1, keepdims=True)
    o_ref[0] = x2 * lax.rsqrt(ms2 + NORM_EPS) * nf_ref[...]


def _out_mlp(x, mix_a, mix_b, g1, sh2, sc2, g2, n2, nf, wo_a, wo_b, w1, w2, *, tile):
    bsz, seq, d = x.shape
    tok = lambda b, i: (b, i, 0)
    per_b = lambda b, i: (b, 0, 0)
    mod_spec = pl.BlockSpec((1, 1, d), per_b)
    return pl.pallas_call(
        functools.partial(_out_mlp_kernel, ff_chunk=1024),
        grid=(bsz, seq // tile),
        in_specs=[pl.BlockSpec((1, tile, d), tok),
                  pl.BlockSpec((1, tile, 512), tok),
                  pl.BlockSpec((1, tile, 512), tok),
                  mod_spec, mod_spec, mod_spec, mod_spec,
                  _const_spec((1, d)), _const_spec((1, d)),
                  _const_spec(wo_a.shape), _const_spec(wo_b.shape),
                  _const_spec(w1.shape), _const_spec(w2.shape)],
        out_specs=pl.BlockSpec((1, tile, d), tok),
        out_shape=jax.ShapeDtypeStruct((bsz, seq, d), F32),
        compiler_params=pltpu.CompilerParams(
            dimension_semantics=("parallel", "parallel"), vmem_limit_bytes=VMEM_LIMIT_BYTES),
        name="out_mlp",
    )(x, mix_a, mix_b, g1, sh2, sc2, g2, n2, nf, wo_a, wo_b, w1, w2)


def _pad_cols(w, width):
    return jnp.pad(w, ((0, 0), (0, width - w.shape[1])))


def kernel(x, c, positions, w_ada, b_ada, norm1_w, w_in, idx_k_ln_w, idx_k_ln_b, lambda_q1, lambda_k1,
           lambda_q2, lambda_k2, subln_w, w_out, norm2_w, w_ff1, w_ff2, norm_f_w):
    bsz, seq, d = x.shape
    depth = w_ada.shape[0]
    assert depth == 1, "the fused final RMSNorm assumes a single layer"
    half = HEAD_DIM // 2
    inv_freq = ROPE_THETA ** (-jnp.arange(half, dtype=F32) / half)
    invf = jnp.tile(inv_freq, LANES // half).reshape(1, LANES)
    pos3 = positions.reshape(bsz, seq, 1)
    blk = min(256, seq)
    tile = min(512, seq)

    for l in range(depth):
        mod = _adaln(c, w_ada[l], b_ada[l])
        sh1, sc1, g1, sh2, sc2, g2 = [m.reshape(bsz, 1, d) for m in jnp.split(mod, 6, axis=-1)]

        w = w_in[l]
        wm = jnp.concatenate([w[:, :2048], w[:, 2120:]], axis=1).astype(BF16)
        ws = _pad_cols(w[:, 2048:2112], LANES).astype(BF16)
        ww = _pad_cols(w[:, 2112:2120], LANES).astype(BF16)
        lnw = _pad_cols(idx_k_ln_w[l].reshape(1, IDX_DIM), LANES)
        lnb = _pad_cols(idx_k_ln_b[l].reshape(1, IDX_DIM), LANES)

        aq, ak, av, iq, ik2, iw, dq, dk, dv = _in_proj(
            x, pos3, sh1, sc1, norm1_w[l].reshape(1, d), invf, wm, ws, ww, lnw, lnb, tile=tile)

        out_a = _dsa(iq, ik2, iw, aq, ak, av, blk=blk)

        lam_vecs = jnp.stack([lambda_q1[l], lambda_k1[l], lambda_q2[l], lambda_k2[l]]).astype(F32)
        lam_init = 0.8 - 0.6 * math.exp(-0.3 * l)
        out_b = _diff_attn(lam_vecs, dq, dk, dv, subln_w[l].reshape(1, DIFF_V_DIM), blk=blk, lam_init=lam_init)

        wo = w_out[l].astype(BF16)
        x = _out_mlp(x, out_a, out_b, g1, sh2, sc2, g2, norm2_w[l].reshape(1, d), norm_f_w.reshape(1, d),
                     wo[:512], wo[512:], w_ff1[l].astype(BF16), w_ff2[l].astype(BF16), tile=tile)
    return x
```

```python
import functools
import math

import jax
import jax.numpy as jnp
from jax import lax
from jax.experimental import pallas as pl
from jax.experimental.pallas import tpu as pltpu

F32 = jnp.float32
BF16 = jnp.bfloat16

HEAD_DIM = 64
N_DSA_HEADS = 8
N_IDX_HEADS = 8
IDX_DIM = 64
N_DIFF_HEADS = 4
DIFF_V_DIM = 128
TOPK_MAX = 256
ROPE_THETA = 10000.0
NORM_EPS = 1e-6
LN_EPS = 1e-5
LANES = 128
BF16_SUBLANES = 16
ATT_BLOCK = 256
MASK_VALUE = -1e30
F32_MAX = float(jnp.finfo(jnp.float32).max)
MAX_BISECT_STEPS = 512
VMEM_LIMIT_BYTES = 56 * 1024 * 1024


def _dot(a, b):
    return jnp.dot(a, b, preferred_element_type=F32)


def _dot_nt(a, b):
    return lax.dot_general(a, b, (((1,), (1,)), ((), ())), preferred_element_type=F32)


def _const_spec(shape):
    zeros = (0,) * len(shape)
    return pl.BlockSpec(shape, lambda *_: zeros, pipeline_mode=pl.Buffered(1))


def _adaln_kernel(c_ref, w_ref, b_ref, o_ref):
    c = c_ref[...]
    s = c / (1.0 + jnp.exp(-c))
    o_ref[...] = jnp.dot(s, w_ref[...], preferred_element_type=F32,
                         precision=lax.Precision.HIGHEST) + b_ref[...]


def _adaln(c, w, b):
    bsz, d = c.shape
    n = w.shape[1]
    tn = 1536
    return pl.pallas_call(
        _adaln_kernel,
        grid=(n // tn,),
        in_specs=[pl.BlockSpec((bsz, d), lambda j: (0, 0)),
                  pl.BlockSpec((d, tn), lambda j: (0, j)),
                  pl.BlockSpec((1, tn), lambda j: (0, j))],
        out_specs=pl.BlockSpec((bsz, tn), lambda j: (0, j)),
        out_shape=jax.ShapeDtypeStruct((bsz, n), F32),
        compiler_params=pltpu.CompilerParams(vmem_limit_bytes=VMEM_LIMIT_BYTES),
        name="adaln",
    )(c, w, b.reshape(1, n))


def _rope_group(x, cos, sin_signed, first_half):
    nxt = pltpu.roll(x, LANES - HEAD_DIM // 2, axis=1)
    prv = pltpu.roll(x, HEAD_DIM // 2, axis=1)
    return x * cos + jnp.where(first_half, nxt, prv) * sin_signed


def _in_proj_kernel(x_ref, pos_ref, sh_ref, sc_ref, nw_ref, invf_ref, wm_ref, ws_ref, wwt_ref,
                    wavt_ref, wdvt_ref, lnw_ref, lnb_ref,
                    aq_ref, ak_ref, avt_ref, iq_ref, ik_ref, iwt_ref, dq_ref, dk_ref, dvt_ref):
    x = x_ref[0]
    ms = jnp.mean(x * x, axis=-1, keepdims=True)
    h = x * lax.rsqrt(ms + NORM_EPS) * nw_ref[...]
    h = h * (1.0 + sc_ref[0]) + sh_ref[0]
    hb = h.astype(BF16)

    lane = lax.broadcasted_iota(jnp.int32, (1, LANES), 1)
    first_half = (lane % HEAD_DIM) < (HEAD_DIM // 2)
    ang = pos_ref[0].astype(F32) * invf_ref[...]
    cos = jnp.cos(ang)
    sin = jnp.sin(ang)
    sin_signed = jnp.where(first_half, -sin, sin)

    def roped(col0, out_ref, scale):
        for g in range(4):
            w = wm_ref[:, col0 + g * LANES: col0 + (g + 1) * LANES]
            y = _rope_group(_dot(hb, w), cos, sin_signed, first_half)
            if scale != 1.0:
                y = y * scale
            out_ref[0, :, g * LANES:(g + 1) * LANES] = y.astype(out_ref.dtype)

    qscale = HEAD_DIM ** -0.5
    roped(0, aq_ref, qscale)
    roped(512, ak_ref, 1.0)
    roped(1024, iq_ref, 1.0)
    roped(1536, dq_ref, qscale)
    roped(2048, dk_ref, 1.0)

    n_sub = avt_ref.shape[1]
    chunk = avt_ref.shape[3]
    for j in range(n_sub):
        hj = hb[j * chunk:(j + 1) * chunk]
        avt_ref[0, j] = _dot_nt(wavt_ref[...], hj).astype(avt_ref.dtype)
        dvt_ref[0, j] = _dot_nt(wdvt_ref[...], hj).astype(dvt_ref.dtype)

    sm = _dot(hb, ws_ref[...])
    lo_half = lane < IDX_DIM
    mu = jnp.sum(sm, axis=-1, keepdims=True) * (1.0 / IDX_DIM)
    d = jnp.where(lo_half, sm - mu, 0.0)
    var = jnp.sum(d * d, axis=-1, keepdims=True) * (1.0 / IDX_DIM)
    y = d * lax.rsqrt(var + LN_EPS) * lnw_ref[...] + lnb_ref[...]
    y = y + pltpu.roll(y, IDX_DIM, axis=1)
    ik_ref[0] = _rope_group(y, cos, sin_signed, first_half).astype(ik_ref.dtype)

    wt = _dot_nt(wwt_ref[...], hb)
    iwt_ref[0] = wt[:N_IDX_HEADS] * ((N_IDX_HEADS ** -0.5) * (IDX_DIM ** -0.5))


def _in_proj(x, pos3, sh1, sc1, nw, invf, wm, ws, wwt, wavt, wdvt, lnw, lnb, *, tile, chunk):
    bsz, seq, d = x.shape
    tok = lambda b, i: (b, i, 0)
    per_b = lambda b, i: (b, 0, 0)
    wide = jax.ShapeDtypeStruct((bsz, seq, 512), BF16)
    wide_t = jax.ShapeDtypeStruct((bsz, seq // chunk, 512, chunk), BF16)
    wide_spec = pl.BlockSpec((1, tile, 512), tok)
    wide_t_spec = pl.BlockSpec((1, tile // chunk, 512, chunk), lambda b, i: (b, i, 0, 0))
    out_shape = [wide, wide, wide_t, wide,
                 jax.ShapeDtypeStruct((bsz, seq, LANES), BF16),
                 jax.ShapeDtypeStruct((bsz, N_IDX_HEADS, seq), F32),
                 wide, wide, wide_t]
    out_specs = [wide_spec, wide_spec, wide_t_spec, wide_spec,
                 pl.BlockSpec((1, tile, LANES), tok),
                 pl.BlockSpec((1, N_IDX_HEADS, tile), lambda b, i: (b, 0, i)),
                 wide_spec, wide_spec, wide_t_spec]
    return pl.pallas_call(
        _in_proj_kernel,
        grid=(bsz, seq // tile),
        in_specs=[pl.BlockSpec((1, tile, d), tok),
                  pl.BlockSpec((1, tile, 1), tok),
                  pl.BlockSpec((1, 1, d), per_b),
                  pl.BlockSpec((1, 1, d), per_b),
                  _const_spec((1, d)),
                  _const_spec((1, LANES)),
                  _const_spec(wm.shape),
                  _const_spec(ws.shape),
                  _const_spec(wwt.shape),
                  _const_spec(wavt.shape),
                  _const_spec(wdvt.shape),
                  _const_spec((1, LANES)),
                  _const_spec((1, LANES))],
        out_specs=out_specs,
        out_shape=out_shape,
        compiler_params=pltpu.CompilerParams(
            dimension_semantics=("parallel", "parallel"), vmem_limit_bytes=VMEM_LIMIT_BYTES),
        name="in_proj",
    )(x, pos3, sh1, sc1, nw, invf, wm, ws, wwt, wavt, wdvt, lnw, lnb)


def _head_q(q_all, h, lane_lo):
    pair = q_all[:, (h // 2) * LANES:(h // 2 + 1) * LANES]
    keep = lane_lo if h % 2 == 0 else jnp.logical_not(lane_lo)
    return jnp.where(keep, pair, jnp.zeros_like(pair))


def _softmax_update(s_list, vt_list, m_ref, l_ref, acc_ref):
    n = len(s_list)
    m_old = [m_ref[g] for g in range(n)]
    m_new = [jnp.maximum(m_old[g], jnp.max(s_list[g], axis=0, keepdims=True)) for g in range(n)]
    p_list = [jnp.exp(s_list[g] - m_new[g]) for g in range(n)]
    alpha = [jnp.exp(m_old[g] - m_new[g]) for g in range(n)]
    for g in range(n):
        l_ref[g] = alpha[g] * l_ref[g] + jnp.sum(p_list[g], axis=0, keepdims=True)
        m_ref[g] = m_new[g]
    pv = [_dot(vt_list[g], p_list[g].astype(BF16)) for g in range(n)]
    for g in range(n):
        acc_ref[g] = alpha[g] * acc_ref[g] + pv[g]


def _init_softmax_state(m_ref, l_ref, acc_ref):
    m_ref[...] = jnp.full(m_ref.shape, MASK_VALUE, F32)
    l_ref[...] = jnp.zeros(l_ref.shape, F32)
    acc_ref[...] = jnp.zeros(acc_ref.shape, F32)


def _dsa_kernel(iq_ref, ik_ref, iwt_ref, aq_ref, ak_ref, avt_ref, o_ref,
                score_ref, m_ref, l_ref, acc_ref, *, blk, top_k):
    i = pl.program_id(1)
    n_chunks = i + 1
    lane = lax.broadcasted_iota(jnp.int32, (1, LANES), 1)
    lane_lo = lane < HEAD_DIM
    qpos = i * blk + lax.broadcasted_iota(jnp.int32, (1, blk), 1)
    krow = lax.broadcasted_iota(jnp.int32, (blk, 1), 0)
    kf = float(top_k)

    iq_all = iq_ref[0]
    wt = iwt_ref[0]
    q_heads = [_head_q(iq_all, h, lane_lo) for h in range(N_IDX_HEADS)]
    w_rows = [wt[h:h + 1, :] for h in range(N_IDX_HEADS)]

    def score_body(c, carry):
        mn, mx = carry
        start = pl.multiple_of(c * blk, blk)
        kk = ik_ref[0, pl.ds(start, blk), :]
        acc = jnp.zeros((blk, blk), F32)
        for h in range(N_IDX_HEADS):
            acc = acc + jnp.maximum(_dot_nt(kk, q_heads[h]), 0.0) * w_rows[h]
        causal = (c * blk + krow) <= qpos
        sc = jnp.where(causal, acc, -jnp.inf)
        score_ref[c] = sc
        mx = jnp.maximum(mx, jnp.max(sc, axis=0, keepdims=True))
        mn = jnp.minimum(mn, jnp.min(jnp.where(causal, acc, jnp.inf), axis=0, keepdims=True))
        return mn, mx

    mn, mx = lax.fori_loop(0, n_chunks, score_body,
                           (jnp.full((1, blk), jnp.inf, F32), jnp.full((1, blk), -jnp.inf, F32)))

    def count_ge(cand):
        def body(c, acc):
            return acc + jnp.sum(jnp.where(score_ref[c] >= cand, 1.0, 0.0), axis=0, keepdims=True)
        return lax.fori_loop(0, n_chunks, body, jnp.zeros((1, blk), F32))

    def count_ge_gt(cand):
        def body(c, carry):
            ge, gt = carry
            s = score_ref[c]
            ge = ge + jnp.sum(jnp.where(s >= cand, 1.0, 0.0), axis=0, keepdims=True)
            gt = gt + jnp.sum(jnp.where(s > cand, 1.0, 0.0), axis=0, keepdims=True)
            return ge, gt
        z = jnp.zeros((1, blk), F32)
        return lax.fori_loop(0, n_chunks, body, (z, z))

    n_valid = (qpos + 1).astype(F32)
    need = n_valid > kf
    ge0, gt0 = count_ge_gt(jnp.zeros((1, blk), F32))
    kth_is_zero = jnp.logical_and(gt0 < kf, ge0 >= kf)
    nonneg = ge0 >= kf
    lo0 = jnp.where(nonneg, 0.0, mn)
    cnt0 = jnp.where(nonneg, ge0, n_valid)
    above_max = jnp.where(mx > 0.0, jnp.minimum(mx * 2.0, F32_MAX), 1.0)
    hi0 = jnp.where(nonneg, above_max, 0.0)
    done0 = jnp.logical_or(jnp.logical_not(need), jnp.logical_or(kth_is_zero, cnt0 == kf))
    active0 = jnp.where(done0, 0.0, 1.0)

    def bis_cond(st):
        return jnp.logical_and(st[4] > 0.0, st[5] < MAX_BISECT_STEPS)

    def bis_body(st):
        lo, hi, cnt_lo, active, _, it = st
        mid = 0.5 * lo + 0.5 * hi
        conv = jnp.logical_or(mid <= lo, mid >= hi)
        cnt = count_ge(mid)
        upd = jnp.logical_and(active > 0.0, jnp.logical_not(conv))
        ge = cnt >= kf
        up_lo = jnp.logical_and(upd, ge)
        up_hi = jnp.logical_and(upd, jnp.logical_not(ge))
        lo = jnp.where(up_lo, mid, lo)
        cnt_lo = jnp.where(up_lo, cnt, cnt_lo)
        hi = jnp.where(up_hi, mid, hi)
        finished = jnp.logical_or(conv, jnp.logical_and(up_lo, cnt == kf))
        active = jnp.where(finished, 0.0, active)
        return lo, hi, cnt_lo, active, jnp.max(active), it + 1

    lo, _, cnt_lo, _, _, _ = lax.while_loop(
        bis_cond, bis_body, (lo0, hi0, cnt0, active0, jnp.max(active0), jnp.int32(0)))
    thr = jnp.where(need, lo, -F32_MAX)

    excess = jnp.logical_and(need, cnt_lo > kf)

    @pl.when(jnp.max(jnp.where(excess, 1.0, 0.0)) > 0.0)
    def _():
        _, gt = count_ge_gt(thr)
        allow = kf - gt
        r_i = lax.broadcasted_iota(jnp.int32, (blk, blk), 0)
        c_i = lax.broadcasted_iota(jnp.int32, (blk, blk), 1)
        earlier = jnp.where(c_i < r_i, 1.0, 0.0).astype(BF16)

        def tie_body(c, seen):
            s = score_ref[c]
            eq = jnp.logical_and(s == thr, excess)
            eqf = jnp.where(eq, 1.0, 0.0)
            rank = seen + _dot(earlier, eqf.astype(BF16))
            drop = jnp.logical_and(eq, rank >= allow)
            score_ref[c] = jnp.where(drop, -jnp.inf, s)
            return seen + jnp.sum(eqf, axis=0, keepdims=True)

        lax.fori_loop(0, n_chunks, tie_body, jnp.zeros((1, blk), F32))

    _init_softmax_state(m_ref, l_ref, acc_ref)
    aq_all = aq_ref[0]
    qa_heads = [_head_q(aq_all, h, lane_lo) for h in range(N_DSA_HEADS)]

    def att_body(c, carry):
        start = pl.multiple_of(c * blk, blk)
        bias = jnp.where(score_ref[c] >= thr, 0.0, MASK_VALUE)
        s_list, vt_list = [], []
        for h in range(N_DSA_HEADS):
            pair = slice((h // 2) * LANES, (h // 2 + 1) * LANES)
            kk = ak_ref[0, pl.ds(start, blk), pair]
            s_list.append(_dot_nt(kk, qa_heads[h]) + bias)
            vt_list.append(avt_ref[0, c, pair, :])
        _softmax_update(s_list, vt_list, m_ref, l_ref, acc_ref)
        return carry

    lax.fori_loop(0, n_chunks, att_body, 0)

    for j in range(N_DSA_HEADS // 2):
        a = acc_ref[2 * j] / l_ref[2 * j]
        b = acc_ref[2 * j + 1] / l_ref[2 * j + 1]
        o_t = jnp.concatenate([a[:HEAD_DIM], b[HEAD_DIM:]], axis=0)
        o_ref[0, :, j * LANES:(j + 1) * LANES] = o_t.T.astype(o_ref.dtype)


def _dsa(iq, ik2, iwt, aq, ak, avt, *, blk):
    bsz, seq, _ = aq.shape
    top_k = min(TOPK_MAX, seq // 4)
    qblk = lambda b, i: (b, i, 0)
    whole = lambda b, i: (b, 0, 0)
    return pl.pallas_call(
        functools.partial(_dsa_kernel, blk=blk, top_k=top_k),
        grid=(bsz, seq // blk),
        in_specs=[pl.BlockSpec((1, blk, 512), qblk),
                  pl.BlockSpec((1, seq, LANES), whole),
                  pl.BlockSpec((1, N_IDX_HEADS, blk), lambda b, i: (b, 0, i)),
                  pl.BlockSpec((1, blk, 512), qblk),
                  pl.BlockSpec((1, seq, 512), whole),
                  pl.BlockSpec((1, seq // blk, 512, blk), lambda b, i: (b, 0, 0, 0))],
        out_specs=pl.BlockSpec((1, blk, 512), qblk),
        out_shape=jax.ShapeDtypeStruct((bsz, seq, 512), BF16),
        scratch_shapes=[pltpu.VMEM((seq // blk, blk, blk), F32),
                        pltpu.VMEM((N_DSA_HEADS, 1, blk), F32),
                        pltpu.VMEM((N_DSA_HEADS, 1, blk), F32),
                        pltpu.VMEM((N_DSA_HEADS, LANES, blk), F32)],
        compiler_params=pltpu.CompilerParams(
            dimension_semantics=("parallel", "arbitrary"), vmem_limit_bytes=VMEM_LIMIT_BYTES),
        name="dsa",
    )(iq, ik2, iwt, aq, ak, avt)


def _diff_kernel(lam_ref, q_ref, k_ref, vt_ref, sw_ref, o_ref, m_ref, l_ref, acc_ref, *, blk, lam_init):
    i = pl.program_id(1)
    lane = lax.broadcasted_iota(jnp.int32, (1, LANES), 1)
    lane_lo = lane < HEAD_DIM
    krow = lax.broadcasted_iota(jnp.int32, (blk, 1), 0)
    qcol = lax.broadcasted_iota(jnp.int32, (1, blk), 1)

    _init_softmax_state(m_ref, l_ref, acc_ref)
    q_all = q_ref[0]
    q_maps = [_head_q(q_all, g, lane_lo) for g in range(2 * N_DIFF_HEADS)]

    def step(c, diagonal):
        start = pl.multiple_of(c * blk, blk)
        if diagonal:
            bias = jnp.where(krow <= qcol, 0.0, MASK_VALUE)
        s_list, vt_list = [], []
        for h in range(N_DIFF_HEADS):
            cols = slice(h * LANES, (h + 1) * LANES)
            kk = k_ref[0, pl.ds(start, blk), cols]
            for comp in range(2):
                s = _dot_nt(kk, q_maps[2 * h + comp])
                s_list.append(s + bias if diagonal else s)
                vt_list.append(vt_ref[0, c, cols, :])
        _softmax_update(s_list, vt_list, m_ref, l_ref, acc_ref)

    def body(c, carry):
        step(c, False)
        return carry

    lax.fori_loop(0, i, body, 0)
    step(i, True)

    lam_vecs = lam_ref[...]
    lam = (jnp.exp(jnp.sum(lam_vecs[0:1] * lam_vecs[1:2], axis=-1, keepdims=True))
           - jnp.exp(jnp.sum(lam_vecs[2:3] * lam_vecs[3:4], axis=-1, keepdims=True)) + lam_init)
    for h in range(N_DIFF_HEADS):
        out = acc_ref[2 * h] / l_ref[2 * h] - lam * (acc_ref[2 * h + 1] / l_ref[2 * h + 1])
        out = out * lax.rsqrt(jnp.mean(out * out, axis=0, keepdims=True) + NORM_EPS)
        out = out * sw_ref[...] * (1.0 - lam_init)
        o_ref[0, :, h * LANES:(h + 1) * LANES] = out.T.astype(o_ref.dtype)


def _diff_attn(lam_vecs, dq, dk, dvt, subln_col, *, blk, lam_init):
    bsz, seq, _ = dq.shape
    qblk = lambda b, i: (b, i, 0)
    whole = lambda b, i: (b, 0, 0)
    n_maps = 2 * N_DIFF_HEADS
    return pl.pallas_call(
        functools.partial(_diff_kernel, blk=blk, lam_init=lam_init),
        grid=(bsz, seq // blk),
        in_specs=[_const_spec(lam_vecs.shape),
                  pl.BlockSpec((1, blk, 512), qblk),
                  pl.BlockSpec((1, seq, 512), whole),
                  pl.BlockSpec((1, seq // blk, 512, blk), lambda b, i: (b, 0, 0, 0)),
                  _const_spec((DIFF_V_DIM, 1))],
        out_specs=pl.BlockSpec((1, blk, 512), qblk),
        out_shape=jax.ShapeDtypeStruct((bsz, seq, 512), BF16),
        scratch_shapes=[pltpu.VMEM((n_maps, 1, blk), F32),
                        pltpu.VMEM((n_maps, 1, blk), F32),
                        pltpu.VMEM((n_maps, DIFF_V_DIM, blk), F32)],
        compiler_params=pltpu.CompilerParams(
            dimension_semantics=("parallel", "arbitrary"), vmem_limit_bytes=VMEM_LIMIT_BYTES),
        name="diff_attn",
    )(lam_vecs, dq, dk, dvt, subln_col)


def _out_mlp_kernel(x_ref, ma_ref, mb_ref, g1_ref, sh_ref, sc_ref, g2_ref, n2_ref, nf_ref,
                    woa_ref, wob_ref, w1_ref, w2_ref, o_ref, *, ff_chunk):
    x = x_ref[0]
    o = _dot(ma_ref[0], woa_ref[...]) + _dot(mb_ref[0], wob_ref[...])
    x1 = x + g1_ref[0] * o
    ms = jnp.mean(x1 * x1, axis=-1, keepdims=True)
    h = x1 * lax.rsqrt(ms + NORM_EPS) * n2_ref[...]
    hb = (h * (1.0 + sc_ref[0]) + sh_ref[0]).astype(BF16)
    d_ff = w1_ref.shape[1]
    ff = jnp.zeros_like(x)
    for j in range(d_ff // ff_chunk):
        u = jnp.maximum(_dot(hb, w1_ref[:, j * ff_chunk:(j + 1) * ff_chunk]), 0.0)
        ff = ff + _dot((u * u).astype(BF16), w2_ref[j * ff_chunk:(j + 1) * ff_chunk, :])
    x2 = x1 + g2_ref[0] * ff
    ms2 = jnp.mean(x2 * x2, axis=-1, keepdims=True)
    o_ref[0] = x2 * lax.rsqrt(ms2 + NORM_EPS) * nf_ref[...]


def _out_mlp(x, mix_a, mix_b, g1, sh2, sc2, g2, n2, nf, wo_a, wo_b, w1, w2, *, tile):
    bsz, seq, d = x.shape
    tok = lambda b, i: (b, i, 0)
    per_b = lambda b, i: (b, 0, 0)
    mod_spec = pl.BlockSpec((1, 1, d), per_b)
    return pl.pallas_call(
        functools.partial(_out_mlp_kernel, ff_chunk=1024),
        grid=(bsz, seq // tile),
        in_specs=[pl.BlockSpec((1, tile, d), tok),
                  pl.BlockSpec((1, tile, 512), tok),
                  pl.BlockSpec((1, tile, 512), tok),
                  mod_spec, mod_spec, mod_spec, mod_spec,
                  _const_spec((1, d)), _const_spec((1, d)),
                  _const_spec(wo_a.shape), _const_spec(wo_b.shape),
                  _const_spec(w1.shape), _const_spec(w2.shape)],
        out_specs=pl.BlockSpec((1, tile, d), tok),
        out_shape=jax.ShapeDtypeStruct((bsz, seq, d), F32),
        compiler_params=pltpu.CompilerParams(
            dimension_semantics=("parallel", "parallel"), vmem_limit_bytes=VMEM_LIMIT_BYTES),
        name="out_mlp",
    )(x, mix_a, mix_b, g1, sh2, sc2, g2, n2, nf, wo_a, wo_b, w1, w2)


def _pad_cols(w, width):
    return jnp.pad(w, ((0, 0), (0, width - w.shape[1])))


def kernel(x, c, positions, w_ada, b_ada, norm1_w, w_in, idx_k_ln_w, idx_k_ln_b, lambda_q1, lambda_k1,
           lambda_q2, lambda_k2, subln_w, w_out, norm2_w, w_ff1, w_ff2, norm_f_w):
    bsz, seq, d = x.shape
    depth = w_ada.shape[0]
    assert depth == 1, "the fused final RMSNorm assumes a single layer"
    half = HEAD_DIM // 2
    inv_freq = ROPE_THETA ** (-jnp.arange(half, dtype=F32) / half)
    invf = jnp.tile(inv_freq, LANES // half).reshape(1, LANES)
    pos3 = positions.reshape(bsz, seq, 1)
    blk = min(ATT_BLOCK, seq)
    tile = min(512, seq)

    for l in range(depth):
        mod = _adaln(c, w_ada[l], b_ada[l])
        sh1, sc1, g1, sh2, sc2, g2 = [m.reshape(bsz, 1, d) for m in jnp.split(mod, 6, axis=-1)]

        w = w_in[l]
        wm = jnp.concatenate([w[:, :1024], w[:, 1536:2048], w[:, 2120:3144]], axis=1).astype(BF16)
        ws = _pad_cols(w[:, 2048:2112], LANES).astype(BF16)
        wwt = jnp.pad(w[:, 2112:2120].T, ((0, BF16_SUBLANES - N_IDX_HEADS), (0, 0))).astype(BF16)
        wavt = w[:, 1024:1536].T.astype(BF16)
        wdvt = w[:, 3144:3656].T.astype(BF16)
        lnw = _pad_cols(idx_k_ln_w[l].reshape(1, IDX_DIM), LANES)
        lnb = _pad_cols(idx_k_ln_b[l].reshape(1, IDX_DIM), LANES)

        aq, ak, avt, iq, ik2, iwt, dq, dk, dvt = _in_proj(
            x, pos3, sh1, sc1, norm1_w[l].reshape(1, d), invf, wm, ws, wwt, wavt, wdvt, lnw, lnb,
            tile=tile, chunk=blk)

        out_a = _dsa(iq, ik2, iwt, aq, ak, avt, blk=blk)

        lam_vecs = jnp.stack([lambda_q1[l], lambda_k1[l], lambda_q2[l], lambda_k2[l]]).astype(F32)
        lam_init = 0.8 - 0.6 * math.exp(-0.3 * l)
        out_b = _diff_attn(lam_vecs, dq, dk, dvt, subln_w[l].reshape(DIFF_V_DIM, 1), blk=blk, lam_init=lam_init)

        wo = w_out[l].astype(BF16)
        x = _out_mlp(x, out_a, out_b, g1, sh2, sc2, g2, norm2_w[l].reshape(1, d), norm_f_w.reshape(1, d),
                     wo[:512], wo[512:], w_ff1[l].astype(BF16), w_ff2[l].astype(BF16), tile=tile)
    return x
```

```python
import functools
import math

import jax
import jax.numpy as jnp
from jax import lax
from jax.experimental import pallas as pl
from jax.experimental.pallas import tpu as pltpu

F32 = jnp.float32
BF16 = jnp.bfloat16

HEAD_DIM = 64
N_DSA_HEADS = 8
N_IDX_HEADS = 8
IDX_DIM = 64
N_DIFF_HEADS = 4
DIFF_V_DIM = 128
TOPK_MAX = 256
ROPE_THETA = 10000.0
NORM_EPS = 1e-6
LN_EPS = 1e-5
LANES = 128
BF16_SUBLANES = 16
ATT_BLOCK = 256
MASK_VALUE = -1e30
F32_MAX = float(jnp.finfo(jnp.float32).max)
MAX_BISECT_STEPS = 512
VMEM_LIMIT_BYTES = 56 * 1024 * 1024


def _dot(a, b):
    return jnp.dot(a, b, preferred_element_type=F32)


def _dot_nt(a, b):
    return lax.dot_general(a, b, (((1,), (1,)), ((), ())), preferred_element_type=F32)


def _const_spec(shape):
    zeros = (0,) * len(shape)
    return pl.BlockSpec(shape, lambda *_: zeros, pipeline_mode=pl.Buffered(1))


def _adaln_kernel(c_ref, w_ref, b_ref, o_ref):
    c = c_ref[...]
    s = c / (1.0 + jnp.exp(-c))
    o_ref[...] = jnp.dot(s, w_ref[...], preferred_element_type=F32,
                         precision=lax.Precision.HIGHEST) + b_ref[...]


def _adaln(c, w, b):
    bsz, d = c.shape
    n = w.shape[1]
    tn = 1536
    return pl.pallas_call(
        _adaln_kernel,
        grid=(n // tn,),
        in_specs=[pl.BlockSpec((bsz, d), lambda j: (0, 0)),
                  pl.BlockSpec((d, tn), lambda j: (0, j)),
                  pl.BlockSpec((1, tn), lambda j: (0, j))],
        out_specs=pl.BlockSpec((bsz, tn), lambda j: (0, j)),
        out_shape=jax.ShapeDtypeStruct((bsz, n), F32),
        compiler_params=pltpu.CompilerParams(vmem_limit_bytes=VMEM_LIMIT_BYTES),
        name="adaln",
    )(c, w, b.reshape(1, n))


def _rope_group(x, cos, sin_signed, first_half):
    nxt = pltpu.roll(x, LANES - HEAD_DIM // 2, axis=1)
    prv = pltpu.roll(x, HEAD_DIM // 2, axis=1)
    return x * cos + jnp.where(first_half, nxt, prv) * sin_signed


def _in_proj_kernel(x_ref, pos_ref, sh_ref, sc_ref, nw_ref, invf_ref, wm_ref, ws_ref, wwt_ref,
                    wavt_ref, wdvt_ref, lnw_ref, lnb_ref,
                    aq_ref, ak_ref, avt_ref, iq_ref, ik_ref, iwt_ref, dq_ref, dk_ref, dvt_ref):
    x = x_ref[0]
    ms = jnp.mean(x * x, axis=-1, keepdims=True)
    h = x * lax.rsqrt(ms + NORM_EPS) * nw_ref[...]
    h = h * (1.0 + sc_ref[0]) + sh_ref[0]
    hb = h.astype(BF16)

    lane = lax.broadcasted_iota(jnp.int32, (1, LANES), 1)
    first_half = (lane % HEAD_DIM) < (HEAD_DIM // 2)
    ang = pos_ref[0].astype(F32) * invf_ref[...]
    cos = jnp.cos(ang)
    sin = jnp.sin(ang)
    sin_signed = jnp.where(first_half, -sin, sin)

    def roped(col0, out_ref, scale):
        for g in range(4):
            w = wm_ref[:, col0 + g * LANES: col0 + (g + 1) * LANES]
            y = _rope_group(_dot(hb, w), cos, sin_signed, first_half)
            if scale != 1.0:
                y = y * scale
            out_ref[0, :, g * LANES:(g + 1) * LANES] = y.astype(out_ref.dtype)

    qscale = HEAD_DIM ** -0.5 * math.log2(math.e)
    roped(0, aq_ref, qscale)
    roped(512, ak_ref, 1.0)
    roped(1024, iq_ref, 1.0)
    roped(1536, dq_ref, qscale)
    roped(2048, dk_ref, 1.0)

    n_sub = avt_ref.shape[1]
    chunk = avt_ref.shape[3]
    for j in range(n_sub):
        hj = hb[j * chunk:(j + 1) * chunk]
        avt_ref[0, j] = _dot_nt(wavt_ref[...], hj).astype(avt_ref.dtype)
        dvt_ref[0, j] = _dot_nt(wdvt_ref[...], hj).astype(dvt_ref.dtype)

    sm = _dot(hb, ws_ref[...])
    lo_half = lane < IDX_DIM
    mu = jnp.sum(sm, axis=-1, keepdims=True) * (1.0 / IDX_DIM)
    d = jnp.where(lo_half, sm - mu, 0.0)
    var = jnp.sum(d * d, axis=-1, keepdims=True) * (1.0 / IDX_DIM)
    y = d * lax.rsqrt(var + LN_EPS) * lnw_ref[...] + lnb_ref[...]
    y = y + pltpu.roll(y, IDX_DIM, axis=1)
    ik_ref[0] = _rope_group(y, cos, sin_signed, first_half).astype(ik_ref.dtype)

    wt = _dot_nt(wwt_ref[...], hb)
    iwt_ref[0] = wt[:N_IDX_HEADS] * ((N_IDX_HEADS ** -0.5) * (IDX_DIM ** -0.5))


def _in_proj(x, pos3, sh1, sc1, nw, invf, wm, ws, wwt, wavt, wdvt, lnw, lnb, *, tile, chunk):
    bsz, seq, d = x.shape
    tok = lambda b, i: (b, i, 0)
    per_b = lambda b, i: (b, 0, 0)
    wide = jax.ShapeDtypeStruct((bsz, seq, 512), BF16)
    wide_t = jax.ShapeDtypeStruct((bsz, seq // chunk, 512, chunk), BF16)
    wide_spec = pl.BlockSpec((1, tile, 512), tok)
    wide_t_spec = pl.BlockSpec((1, tile // chunk, 512, chunk), lambda b, i: (b, i, 0, 0))
    out_shape = [wide, wide, wide_t, wide,
                 jax.ShapeDtypeStruct((bsz, seq, LANES), BF16),
                 jax.ShapeDtypeStruct((bsz, N_IDX_HEADS, seq), F32),
                 wide, wide, wide_t]
    out_specs = [wide_spec, wide_spec, wide_t_spec, wide_spec,
                 pl.BlockSpec((1, tile, LANES), tok),
                 pl.BlockSpec((1, N_IDX_HEADS, tile), lambda b, i: (b, 0, i)),
                 wide_spec, wide_spec, wide_t_spec]
    return pl.pallas_call(
        _in_proj_kernel,
        grid=(bsz, seq // tile),
        in_specs=[pl.BlockSpec((1, tile, d), tok),
                  pl.BlockSpec((1, tile, 1), tok),
                  pl.BlockSpec((1, 1, d), per_b),
                  pl.BlockSpec((1, 1, d), per_b),
                  _const_spec((1, d)),
                  _const_spec((1, LANES)),
                  _const_spec(wm.shape),
                  _const_spec(ws.shape),
                  _const_spec(wwt.shape),
                  _const_spec(wavt.shape),
                  _const_spec(wdvt.shape),
                  _const_spec((1, LANES)),
                  _const_spec((1, LANES))],
        out_specs=out_specs,
        out_shape=out_shape,
        compiler_params=pltpu.CompilerParams(
            dimension_semantics=("parallel", "parallel"), vmem_limit_bytes=VMEM_LIMIT_BYTES),
        name="in_proj",
    )(x, pos3, sh1, sc1, nw, invf, wm, ws, wwt, wavt, wdvt, lnw, lnb)


def _head_q(q_all, h, lane_lo):
    pair = q_all[:, (h // 2) * LANES:(h // 2 + 1) * LANES]
    keep = lane_lo if h % 2 == 0 else jnp.logical_not(lane_lo)
    return jnp.where(keep, pair, jnp.zeros_like(pair))


def _softmax_update(s_list, vt_list, m_ref, l_ref, acc_ref):
    n = len(s_list)
    dv = vt_list[0].shape[0]
    ones = jnp.ones((BF16_SUBLANES, vt_list[0].shape[1]), BF16)
    m_old = [m_ref[g] for g in range(n)]
    m_new = [jnp.maximum(m_old[g], jnp.max(s_list[g], axis=0, keepdims=True)) for g in range(n)]
    p_list = [jnp.exp2(s_list[g] - m_new[g]).astype(BF16) for g in range(n)]
    alpha = [jnp.exp2(m_old[g] - m_new[g]) for g in range(n)]
    pv = [_dot(jnp.concatenate([vt_list[g], ones], axis=0), p_list[g]) for g in range(n)]
    for g in range(n):
        m_ref[g] = m_new[g]
        l_ref[g] = alpha[g] * l_ref[g] + pv[g][dv:dv + 1]
        acc_ref[g] = alpha[g] * acc_ref[g] + pv[g][:dv]


def _init_softmax_state(m_ref, l_ref, acc_ref):
    m_ref[...] = jnp.full(m_ref.shape, MASK_VALUE, F32)
    l_ref[...] = jnp.zeros(l_ref.shape, F32)
    acc_ref[...] = jnp.zeros(acc_ref.shape, F32)


def _dsa_kernel(iq_ref, ik_ref, iwt_ref, aq_ref, ak_ref, avt_ref, o_ref,
                score_ref, m_ref, l_ref, acc_ref, *, blk, top_k):
    i = pl.program_id(1)
    n_chunks = i + 1
    lane = lax.broadcasted_iota(jnp.int32, (1, LANES), 1)
    lane_lo = lane < HEAD_DIM
    qpos = i * blk + lax.broadcasted_iota(jnp.int32, (1, blk), 1)
    krow = lax.broadcasted_iota(jnp.int32, (blk, 1), 0)
    kf = float(top_k)

    iq_all = iq_ref[0]
    wt = iwt_ref[0]
    q_heads = [_head_q(iq_all, h, lane_lo) for h in range(N_IDX_HEADS)]
    w_rows = [wt[h:h + 1, :] for h in range(N_IDX_HEADS)]

    def score_body(c, carry):
        mn, mx = carry
        start = pl.multiple_of(c * blk, blk)
        kk = ik_ref[0, pl.ds(start, blk), :]
        acc = jnp.zeros((blk, blk), F32)
        for h in range(N_IDX_HEADS):
            acc = acc + jnp.maximum(_dot_nt(kk, q_heads[h]), 0.0) * w_rows[h]
        causal = (c * blk + krow) <= qpos
        sc = jnp.where(causal, acc, -jnp.inf)
        score_ref[c] = sc
        mx = jnp.maximum(mx, jnp.max(sc, axis=0, keepdims=True))
        mn = jnp.minimum(mn, jnp.min(jnp.where(causal, acc, jnp.inf), axis=0, keepdims=True))
        return mn, mx

    mn, mx = lax.fori_loop(0, n_chunks, score_body,
                           (jnp.full((1, blk), jnp.inf, F32), jnp.full((1, blk), -jnp.inf, F32)))

    n_pairs = (n_chunks + 1) // 2

    @pl.when(n_chunks % 2 == 1)
    def _():
        score_ref[n_chunks] = jnp.full((blk, blk), -jnp.inf, F32)

    def pair_sum(hit):
        m = jnp.where(hit, 1.0, 0.0)
        return jnp.sum(m[0] + m[1], axis=0, keepdims=True)

    def count_ge(cand):
        def body(j, acc):
            return acc + pair_sum(score_ref[pl.ds(2 * j, 2)] >= cand)
        return lax.fori_loop(0, n_pairs, body, jnp.zeros((1, blk), F32))

    def count_ge_gt(cand):
        def body(j, carry):
            ge, gt = carry
            s = score_ref[pl.ds(2 * j, 2)]
            return ge + pair_sum(s >= cand), gt + pair_sum(s > cand)
        z = jnp.zeros((1, blk), F32)
        return lax.fori_loop(0, n_pairs, body, (z, z))

    n_valid = (qpos + 1).astype(F32)
    need = n_valid > kf
    ge0, gt0 = count_ge_gt(jnp.zeros((1, blk), F32))
    kth_is_zero = jnp.logical_and(gt0 < kf, ge0 >= kf)
    nonneg = ge0 >= kf
    lo0 = jnp.where(nonneg, 0.0, mn)
    cnt0 = jnp.where(nonneg, ge0, n_valid)
    above_max = jnp.where(mx > 0.0, jnp.minimum(mx * 2.0, F32_MAX), 1.0)
    hi0 = jnp.where(nonneg, above_max, 0.0)
    done0 = jnp.logical_or(jnp.logical_not(need), jnp.logical_or(kth_is_zero, cnt0 == kf))
    active0 = jnp.where(done0, 0.0, 1.0)

    def bis_cond(st):
        return jnp.logical_and(st[4] > 0.0, st[5] < MAX_BISECT_STEPS)

    def bis_body(st):
        lo, hi, cnt_lo, active, _, it = st
        mid = 0.5 * lo + 0.5 * hi
        conv = jnp.logical_or(mid <= lo, mid >= hi)
        cnt = count_ge(mid)
        upd = jnp.logical_and(active > 0.0, jnp.logical_not(conv))
        ge = cnt >= kf
        up_lo = jnp.logical_and(upd, ge)
        up_hi = jnp.logical_and(upd, jnp.logical_not(ge))
        lo = jnp.where(up_lo, mid, lo)
        cnt_lo = jnp.where(up_lo, cnt, cnt_lo)
        hi = jnp.where(up_hi, mid, hi)
        finished = jnp.logical_or(conv, jnp.logical_and(up_lo, cnt == kf))
        active = jnp.where(finished, 0.0, active)
        return lo, hi, cnt_lo, active, jnp.max(active), it + 1

    lo, _, cnt_lo, _, _, _ = lax.while_loop(
        bis_cond, bis_body, (lo0, hi0, cnt0, active0, jnp.max(active0), jnp.int32(0)))
    thr = jnp.where(need, lo, -F32_MAX)

    excess = jnp.logical_and(need, cnt_lo > kf)

    @pl.when(jnp.max(jnp.where(excess, 1.0, 0.0)) > 0.0)
    def _():
        _, gt = count_ge_gt(thr)
        allow = kf - gt
        r_i = lax.broadcasted_iota(jnp.int32, (blk, blk), 0)
        c_i = lax.broadcasted_iota(jnp.int32, (blk, blk), 1)
        earlier = jnp.where(c_i < r_i, 1.0, 0.0).astype(BF16)

        def tie_body(c, seen):
            s = score_ref[c]
            eq = jnp.logical_and(s == thr, excess)
            eqf = jnp.where(eq, 1.0, 0.0)
            rank = seen + _dot(earlier, eqf.astype(BF16))
            drop = jnp.logical_and(eq, rank >= allow)
            score_ref[c] = jnp.where(drop, -jnp.inf, s)
            return seen + jnp.sum(eqf, axis=0, keepdims=True)

        lax.fori_loop(0, n_chunks, tie_body, jnp.zeros((1, blk), F32))

    _init_softmax_state(m_ref, l_ref, acc_ref)
    aq_all = aq_ref[0]
    qa_heads = [_head_q(aq_all, h, lane_lo) for h in range(N_DSA_HEADS)]

    def att_body(c, carry):
        start = pl.multiple_of(c * blk, blk)
        bias = jnp.where(score_ref[c] >= thr, 0.0, MASK_VALUE)
        s_list, vt_list = [], []
        for h in range(N_DSA_HEADS):
            pair = slice((h // 2) * LANES, (h // 2 + 1) * LANES)
            kk = ak_ref[0, pl.ds(start, blk), pair]
            s_list.append(_dot_nt(kk, qa_heads[h]) + bias)
            vt_list.append(avt_ref[0, c, pair, :])
        _softmax_update(s_list, vt_list, m_ref, l_ref, acc_ref)
        return carry

    lax.fori_loop(0, n_chunks, att_body, 0)

    for j in range(N_DSA_HEADS // 2):
        a = acc_ref[2 * j] / l_ref[2 * j]
        b = acc_ref[2 * j + 1] / l_ref[2 * j + 1]
        o_t = jnp.concatenate([a[:HEAD_DIM], b[HEAD_DIM:]], axis=0)
        o_ref[0, :, j * LANES:(j + 1) * LANES] = o_t.T.astype(o_ref.dtype)


def _dsa(iq, ik2, iwt, aq, ak, avt, *, blk):
    bsz, seq, _ = aq.shape
    top_k = min(TOPK_MAX, seq // 4)
    qblk = lambda b, i: (b, i, 0)
    whole = lambda b, i: (b, 0, 0)
    return pl.pallas_call(
        functools.partial(_dsa_kernel, blk=blk, top_k=top_k),
        grid=(bsz, seq // blk),
        in_specs=[pl.BlockSpec((1, blk, 512), qblk),
                  pl.BlockSpec((1, seq, LANES), whole),
                  pl.BlockSpec((1, N_IDX_HEADS, blk), lambda b, i: (b, 0, i)),
                  pl.BlockSpec((1, blk, 512), qblk),
                  pl.BlockSpec((1, seq, 512), whole),
                  pl.BlockSpec((1, seq // blk, 512, blk), lambda b, i: (b, 0, 0, 0))],
        out_specs=pl.BlockSpec((1, blk, 512), qblk),
        out_shape=jax.ShapeDtypeStruct((bsz, seq, 512), BF16),
        scratch_shapes=[pltpu.VMEM((2 * ((seq // blk + 1) // 2), blk, blk), F32),
                        pltpu.VMEM((N_DSA_HEADS, 1, blk), F32),
                        pltpu.VMEM((N_DSA_HEADS, 1, blk), F32),
                        pltpu.VMEM((N_DSA_HEADS, LANES, blk), F32)],
        compiler_params=pltpu.CompilerParams(
            dimension_semantics=("parallel", "arbitrary"), vmem_limit_bytes=VMEM_LIMIT_BYTES),
        name="dsa",
    )(iq, ik2, iwt, aq, ak, avt)


def _diff_kernel(lam_ref, q_ref, k_ref, vt_ref, sw_ref, o_ref, m_ref, l_ref, acc_ref, *, blk, lam_init):
    i = pl.program_id(1)
    lane = lax.broadcasted_iota(jnp.int32, (1, LANES), 1)
    lane_lo = lane < HEAD_DIM
    krow = lax.broadcasted_iota(jnp.int32, (blk, 1), 0)
    qcol = lax.broadcasted_iota(jnp.int32, (1, blk), 1)

    _init_softmax_state(m_ref, l_ref, acc_ref)
    q_all = q_ref[0]
    q_maps = [_head_q(q_all, g, lane_lo) for g in range(2 * N_DIFF_HEADS)]

    def step(c, diagonal):
        start = pl.multiple_of(c * blk, blk)
        if diagonal:
            bias = jnp.where(krow <= qcol, 0.0, MASK_VALUE)
        s_list, vt_list = [], []
        for h in range(N_DIFF_HEADS):
            cols = slice(h * LANES, (h + 1) * LANES)
            kk = k_ref[0, pl.ds(start, blk), cols]
            for comp in range(2):
                s = _dot_nt(kk, q_maps[2 * h + comp])
                s_list.append(s + bias if diagonal else s)
                vt_list.append(vt_ref[0, c, cols, :])
        _softmax_update(s_list, vt_list, m_ref, l_ref, acc_ref)

    def body(c, carry):
        step(c, False)
        return carry

    lax.fori_loop(0, i, body, 0)
    step(i, True)

    lam_vecs = lam_ref[...]
    lam = (jnp.exp(jnp.sum(lam_vecs[0:1] * lam_vecs[1:2], axis=-1, keepdims=True))
           - jnp.exp(jnp.sum(lam_vecs[2:3] * lam_vecs[3:4], axis=-1, keepdims=True)) + lam_init)
    for h in range(N_DIFF_HEADS):
        out = acc_ref[2 * h] / l_ref[2 * h] - lam * (acc_ref[2 * h + 1] / l_ref[2 * h + 1])
        out = out * lax.rsqrt(jnp.mean(out * out, axis=0, keepdims=True) + NORM_EPS)
        out = out * sw_ref[...] * (1.0 - lam_init)
        o_ref[0, :, h * LANES:(h + 1) * LANES] = out.T.astype(o_ref.dtype)


def _diff_attn(lam_vecs, dq, dk, dvt, subln_col, *, blk, lam_init):
    bsz, seq, _ = dq.shape
    qblk = lambda b, i: (b, i, 0)
    whole = lambda b, i: (b, 0, 0)
    n_maps = 2 * N_DIFF_HEADS
    return pl.pallas_call(
        functools.partial(_diff_kernel, blk=blk, lam_init=lam_init),
        grid=(bsz, seq // blk),
        in_specs=[_const_spec(lam_vecs.shape),
                  pl.BlockSpec((1, blk, 512), qblk),
                  pl.BlockSpec((1, seq, 512), whole),
                  pl.BlockSpec((1, seq // blk, 512, blk), lambda b, i: (b, 0, 0, 0)),
                  _const_spec((DIFF_V_DIM, 1))],
        out_specs=pl.BlockSpec((1, blk, 512), qblk),
        out_shape=jax.ShapeDtypeStruct((bsz, seq, 512), BF16),
        scratch_shapes=[pltpu.VMEM((n_maps, 1, blk), F32),
                        pltpu.VMEM((n_maps, 1, blk), F32),
                        pltpu.VMEM((n_maps, DIFF_V_DIM, blk), F32)],
        compiler_params=pltpu.CompilerParams(
            dimension_semantics=("parallel", "arbitrary"), vmem_limit_bytes=VMEM_LIMIT_BYTES),
        name="diff_attn",
    )(lam_vecs, dq, dk, dvt, subln_col)


def _out_mlp_kernel(x_ref, ma_ref, mb_ref, g1_ref, sh_ref, sc_ref, g2_ref, n2_ref, nf_ref,
                    woa_ref, wob_ref, w1_ref, w2_ref, o_ref, *, ff_chunk):
    x = x_ref[0]
    o = _dot(ma_ref[0], woa_ref[...]) + _dot(mb_ref[0], wob_ref[...])
    x1 = x + g1_ref[0] * o
    ms = jnp.mean(x1 * x1, axis=-1, keepdims=True)
    h = x1 * lax.rsqrt(ms + NORM_EPS) * n2_ref[...]
    hb = (h * (1.0 + sc_ref[0]) + sh_ref[0]).astype(BF16)
    d_ff = w1_ref.shape[1]
    ff = jnp.zeros_like(x)
    for j in range(d_ff // ff_chunk):
        u = jnp.maximum(_dot(hb, w1_ref[:, j * ff_chunk:(j + 1) * ff_chunk]), 0.0)
        ff = ff + _dot((u * u).astype(BF16), w2_ref[j * ff_chunk:(j + 1) * ff_chunk, :])
    x2 = x1 + g2_ref[0] * ff
    ms2 = jnp.mean(x2 * x2, axis=-1, keepdims=True)
    o_ref[0] = x2 * lax.rsqrt(ms2 + NORM_EPS) * nf_ref[...]


def _out_mlp(x, mix_a, mix_b, g1, sh2, sc2, g2, n2, nf, wo_a, wo_b, w1, w2, *, tile):
    bsz, seq, d = x.shape
    tok = lambda b, i: (b, i, 0)
    per_b = lambda b, i: (b, 0, 0)
    mod_spec = pl.BlockSpec((1, 1, d), per_b)
    return pl.pallas_call(
        functools.partial(_out_mlp_kernel, ff_chunk=1024),
        grid=(bsz, seq // tile),
        in_specs=[pl.BlockSpec((1, tile, d), tok),
                  pl.BlockSpec((1, tile, 512), tok),
                  pl.BlockSpec((1, tile, 512), tok),
                  mod_spec, mod_spec, mod_spec, mod_spec,
                  _const_spec((1, d)), _const_spec((1, d)),
                  _const_spec(wo_a.shape), _const_spec(wo_b.shape),
                  _const_spec(w1.shape), _const_spec(w2.shape)],
        out_specs=pl.BlockSpec((1, tile, d), tok),
        out_shape=jax.ShapeDtypeStruct((bsz, seq, d), F32),
        compiler_params=pltpu.CompilerParams(
            dimension_semantics=("parallel", "parallel"), vmem_limit_bytes=VMEM_LIMIT_BYTES),
        name="out_mlp",
    )(x, mix_a, mix_b, g1, sh2, sc2, g2, n2, nf, wo_a, wo_b, w1, w2)


def _pad_cols(w, width):
    return jnp.pad(w, ((0, 0), (0, width - w.shape[1])))


def kernel(x, c, positions, w_ada, b_ada, norm1_w, w_in, idx_k_ln_w, idx_k_ln_b, lambda_q1, lambda_k1,
           lambda_q2, lambda_k2, subln_w, w_out, norm2_w, w_ff1, w_ff2, norm_f_w):
    bsz, seq, d = x.shape
    depth = w_ada.shape[0]
    assert depth == 1, "the fused final RMSNorm assumes a single layer"
    half = HEAD_DIM // 2
    inv_freq = ROPE_THETA ** (-jnp.arange(half, dtype=F32) / half)
    invf = jnp.tile(inv_freq, LANES // half).reshape(1, LANES)
    pos3 = positions.reshape(bsz, seq, 1)
    blk = min(ATT_BLOCK, seq)
    tile = min(512, seq)

    for l in range(depth):
        mod = _adaln(c, w_ada[l], b_ada[l])
        sh1, sc1, g1, sh2, sc2, g2 = [m.reshape(bsz, 1, d) for m in jnp.split(mod, 6, axis=-1)]

        w = w_in[l]
        wm = jnp.concatenate([w[:, :1024], w[:, 1536:2048], w[:, 2120:3144]], axis=1).astype(BF16)
        ws = _pad_cols(w[:, 2048:2112], LANES).astype(BF16)
        wwt = jnp.pad(w[:, 2112:2120].T, ((0, BF16_SUBLANES - N_IDX_HEADS), (0, 0))).astype(BF16)
        wavt = w[:, 1024:1536].T.astype(BF16)
        wdvt = w[:, 3144:3656].T.astype(BF16)
        lnw = _pad_cols(idx_k_ln_w[l].reshape(1, IDX_DIM), LANES)
        lnb = _pad_cols(idx_k_ln_b[l].reshape(1, IDX_DIM), LANES)

        aq, ak, avt, iq, ik2, iwt, dq, dk, dvt = _in_proj(
            x, pos3, sh1, sc1, norm1_w[l].reshape(1, d), invf, wm, ws, wwt, wavt, wdvt, lnw, lnb,
            tile=tile, chunk=blk)

        out_a = _dsa(iq, ik2, iwt, aq, ak, avt, blk=blk)

        lam_vecs = jnp.stack([lambda_q1[l], lambda_k1[l], lambda_q2[l], lambda_k2[l]]).astype(F32)
        lam_init = 0.8 - 0.6 * math.exp(-0.3 * l)
        out_b = _diff_attn(lam_vecs, dq, dk, dvt, subln_w[l].reshape(DIFF_V_DIM, 1), blk=blk, lam_init=lam_init)

        wo = w_out[l].astype(BF16)
        x = _out_mlp(x, out_a, out_b, g1, sh2, sc2, g2, norm2_w[l].reshape(1, d), norm_f_w.reshape(1, d),
                     wo[:512], wo[512:], w_ff1[l].astype(BF16), w_ff2[l].astype(BF16), tile=tile)
    return x
```

```python
import functools
import math

import jax
import jax.numpy as jnp
from jax import lax
from jax.experimental import pallas as pl
from jax.experimental.pallas import tpu as pltpu

F32 = jnp.float32
BF16 = jnp.bfloat16

HEAD_DIM = 64
N_DSA_HEADS = 8
N_IDX_HEADS = 8
IDX_DIM = 64
N_DIFF_HEADS = 4
DIFF_V_DIM = 128
TOPK_MAX = 256
ROPE_THETA = 10000.0
NORM_EPS = 1e-6
LN_EPS = 1e-5
LANES = 128
BF16_SUBLANES = 16
ATT_BLOCK = 256
MASK_VALUE = -1e30
F32_MAX = float(jnp.finfo(jnp.float32).max)
MAX_BISECT_STEPS = 512
BISECT_FIRST_STEPS = 10
BISECT_STEPS_PER_TEST = 2
MIN_SAFE_NORMALISER = 2.0 ** -100
VMEM_LIMIT_BYTES = 56 * 1024 * 1024


def _dot(a, b):
    return jnp.dot(a, b, preferred_element_type=F32)


def _dot_nt(a, b):
    return lax.dot_general(a, b, (((1,), (1,)), ((), ())), preferred_element_type=F32)


def _const_spec(shape):
    zeros = (0,) * len(shape)
    return pl.BlockSpec(shape, lambda *_: zeros, pipeline_mode=pl.Buffered(1))


def _adaln_kernel(c_ref, w_ref, b_ref, o_ref):
    c = c_ref[...]
    s = c / (1.0 + jnp.exp(-c))
    o_ref[...] = jnp.dot(s, w_ref[...], preferred_element_type=F32,
                         precision=lax.Precision.HIGHEST) + b_ref[...]


def _adaln(c, w, b):
    bsz, d = c.shape
    n = w.shape[1]
    tn = 1536
    return pl.pallas_call(
        _adaln_kernel,
        grid=(n // tn,),
        in_specs=[pl.BlockSpec((bsz, d), lambda j: (0, 0)),
                  pl.BlockSpec((d, tn), lambda j: (0, j)),
                  pl.BlockSpec((1, tn), lambda j: (0, j))],
        out_specs=pl.BlockSpec((bsz, tn), lambda j: (0, j)),
        out_shape=jax.ShapeDtypeStruct((bsz, n), F32),
        compiler_params=pltpu.CompilerParams(vmem_limit_bytes=VMEM_LIMIT_BYTES),
        name="adaln",
    )(c, w, b.reshape(1, n))


def _rope_group(x, cos, sin_signed, first_half):
    nxt = pltpu.roll(x, LANES - HEAD_DIM // 2, axis=1)
    prv = pltpu.roll(x, HEAD_DIM // 2, axis=1)
    return x * cos + jnp.where(first_half, nxt, prv) * sin_signed


def _in_proj_kernel(x_ref, pos_ref, sh_ref, sc_ref, nw_ref, invf_ref, wm_ref, ws_ref, wwt_ref,
                    wavt_ref, wdvt_ref, lnw_ref, lnb_ref,
                    aq_ref, ak_ref, avt_ref, iq_ref, ik_ref, iwt_ref, dq_ref, dk_ref, dvt_ref):
    x = x_ref[0]
    ms = jnp.mean(x * x, axis=-1, keepdims=True)
    h = x * lax.rsqrt(ms + NORM_EPS) * nw_ref[...]
    h = h * (1.0 + sc_ref[0]) + sh_ref[0]
    hb = h.astype(BF16)

    lane = lax.broadcasted_iota(jnp.int32, (1, LANES), 1)
    first_half = (lane % HEAD_DIM) < (HEAD_DIM // 2)
    ang = pos_ref[0].astype(F32) * invf_ref[...]
    cos = jnp.cos(ang)
    sin = jnp.sin(ang)
    sin_signed = jnp.where(first_half, -sin, sin)

    def roped(col0, out_ref, scale):
        for g in range(4):
            w = wm_ref[:, col0 + g * LANES: col0 + (g + 1) * LANES]
            y = _rope_group(_dot(hb, w), cos, sin_signed, first_half)
            if scale != 1.0:
                y = y * scale
            out_ref[0, :, g * LANES:(g + 1) * LANES] = y.astype(out_ref.dtype)

    qscale = HEAD_DIM ** -0.5 * math.log2(math.e)
    roped(0, aq_ref, qscale)
    roped(512, ak_ref, 1.0)
    roped(1024, iq_ref, 1.0)
    roped(1536, dq_ref, qscale)
    roped(2048, dk_ref, 1.0)

    n_sub = avt_ref.shape[1]
    chunk = avt_ref.shape[3]
    for j in range(n_sub):
        hj = hb[j * chunk:(j + 1) * chunk]
        avt_ref[0, j] = _dot_nt(wavt_ref[...], hj).astype(avt_ref.dtype)
        dvt_ref[0, j] = _dot_nt(wdvt_ref[...], hj).astype(dvt_ref.dtype)

    sm = _dot(hb, ws_ref[...])
    lo_half = lane < IDX_DIM
    mu = jnp.sum(sm, axis=-1, keepdims=True) * (1.0 / IDX_DIM)
    d = jnp.where(lo_half, sm - mu, 0.0)
    var = jnp.sum(d * d, axis=-1, keepdims=True) * (1.0 / IDX_DIM)
    y = d * lax.rsqrt(var + LN_EPS) * lnw_ref[...] + lnb_ref[...]
    y = y + pltpu.roll(y, IDX_DIM, axis=1)
    ik_ref[0] = _rope_group(y, cos, sin_signed, first_half).astype(ik_ref.dtype)

    wt = _dot_nt(wwt_ref[...], hb)
    iwt_ref[0] = wt[:N_IDX_HEADS] * ((N_IDX_HEADS ** -0.5) * (IDX_DIM ** -0.5))


def _in_proj(x, pos3, sh1, sc1, nw, invf, wm, ws, wwt, wavt, wdvt, lnw, lnb, *, tile, chunk):
    bsz, seq, d = x.shape
    tok = lambda b, i: (b, i, 0)
    per_b = lambda b, i: (b, 0, 0)
    wide = jax.ShapeDtypeStruct((bsz, seq, 512), BF16)
    wide_t = jax.ShapeDtypeStruct((bsz, seq // chunk, 512, chunk), BF16)
    wide_spec = pl.BlockSpec((1, tile, 512), tok)
    wide_t_spec = pl.BlockSpec((1, tile // chunk, 512, chunk), lambda b, i: (b, i, 0, 0))
    out_shape = [wide, wide, wide_t, wide,
                 jax.ShapeDtypeStruct((bsz, seq, LANES), BF16),
                 jax.ShapeDtypeStruct((bsz, N_IDX_HEADS, seq), F32),
                 wide, wide, wide_t]
    out_specs = [wide_spec, wide_spec, wide_t_spec, wide_spec,
                 pl.BlockSpec((1, tile, LANES), tok),
                 pl.BlockSpec((1, N_IDX_HEADS, tile), lambda b, i: (b, 0, i)),
                 wide_spec, wide_spec, wide_t_spec]
    return pl.pallas_call(
        _in_proj_kernel,
        grid=(bsz, seq // tile),
        in_specs=[pl.BlockSpec((1, tile, d), tok),
                  pl.BlockSpec((1, tile, 1), tok),
                  pl.BlockSpec((1, 1, d), per_b),
                  pl.BlockSpec((1, 1, d), per_b),
                  _const_spec((1, d)),
                  _const_spec((1, LANES)),
                  _const_spec(wm.shape),
                  _const_spec(ws.shape),
                  _const_spec(wwt.shape),
                  _const_spec(wavt.shape),
                  _const_spec(wdvt.shape),
                  _const_spec((1, LANES)),
                  _const_spec((1, LANES))],
        out_specs=out_specs,
        out_shape=out_shape,
        compiler_params=pltpu.CompilerParams(
            dimension_semantics=("parallel", "parallel"), vmem_limit_bytes=VMEM_LIMIT_BYTES),
        name="in_proj",
    )(x, pos3, sh1, sc1, nw, invf, wm, ws, wwt, wavt, wdvt, lnw, lnb)


def _head_q(q_all, h, lane_lo):
    pair = q_all[:, (h // 2) * LANES:(h // 2 + 1) * LANES]
    keep = lane_lo if h % 2 == 0 else jnp.logical_not(lane_lo)
    return jnp.where(keep, pair, jnp.zeros_like(pair))


def _softmax_update(s_list, vt_list, m_ref, l_ref, acc_ref):
    n = len(s_list)
    dv = vt_list[0].shape[0]
    ones = jnp.ones((BF16_SUBLANES, vt_list[0].shape[1]), BF16)
    m_old = [m_ref[g] for g in range(n)]
    m_new = [jnp.maximum(m_old[g], jnp.max(s_list[g], axis=0, keepdims=True)) for g in range(n)]
    p_list = [jnp.exp2(s_list[g] - m_new[g]).astype(BF16) for g in range(n)]
    alpha = [jnp.exp2(m_old[g] - m_new[g]) for g in range(n)]
    pv = [_dot(jnp.concatenate([vt_list[g], ones], axis=0), p_list[g]) for g in range(n)]
    for g in range(n):
        m_ref[g] = m_new[g]
        l_ref[g] = alpha[g] * l_ref[g] + pv[g][dv:dv + 1]
        acc_ref[g] = alpha[g] * acc_ref[g] + pv[g][:dv]


def _init_softmax_state(m_ref, l_ref, acc_ref):
    m_ref[...] = jnp.full(m_ref.shape, MASK_VALUE, F32)
    l_ref[...] = jnp.zeros(l_ref.shape, F32)
    acc_ref[...] = jnp.zeros(acc_ref.shape, F32)


def _bounded_update(s_list, vt_list, l_ref, acc_ref):
    n = len(s_list)
    dv = vt_list[0].shape[0]
    ones = jnp.ones((BF16_SUBLANES, vt_list[0].shape[1]), BF16)
    p_list = [jnp.exp2(s_list[g]).astype(BF16) for g in range(n)]
    pv = [_dot(jnp.concatenate([vt_list[g], ones], axis=0), p_list[g]) for g in range(n)]
    for g in range(n):
        l_ref[g] = l_ref[g] + pv[g][dv:dv + 1]
        acc_ref[g] = acc_ref[g] + pv[g][:dv]


def _head_selector(n_lanes):
    r = lax.broadcasted_iota(jnp.int32, (BF16_SUBLANES, n_lanes), 0)
    l = lax.broadcasted_iota(jnp.int32, (BF16_SUBLANES, n_lanes), 1)
    return jnp.where(l // HEAD_DIM == r, 1.0, 0.0).astype(BF16)


def _head_sq_norms(x, head_sel):
    xf = x.astype(F32)
    return _dot_nt(head_sel, (xf * xf).astype(BF16))


def _max_key_sq_norms(k_ref, kmax_ref, head_sel, blk):
    def body(c, mx):
        start = pl.multiple_of(c * blk, blk)
        n2 = _head_sq_norms(k_ref[0, pl.ds(start, blk), :], head_sel)
        return jnp.maximum(mx, jnp.max(n2, axis=1, keepdims=True))
    kmax = lax.fori_loop(0, k_ref.shape[1] // blk, body, jnp.zeros((BF16_SUBLANES, 1), F32))
    kmax_ref[...] = jnp.broadcast_to(kmax, kmax_ref.shape)


def _softmax_underflowed(l_ref):
    return jnp.logical_not(jnp.min(l_ref[...]) >= MIN_SAFE_NORMALISER)


def _dsa_kernel(iq_ref, ik_ref, iwt_ref, aq_ref, ak_ref, avt_ref, o_ref,
                score_ref, m_ref, l_ref, acc_ref, kmax_ref, *, blk, top_k):
    i = pl.program_id(1)
    n_chunks = i + 1
    lane = lax.broadcasted_iota(jnp.int32, (1, LANES), 1)
    lane_lo = lane < HEAD_DIM
    qpos = i * blk + lax.broadcasted_iota(jnp.int32, (1, blk), 1)
    krow = lax.broadcasted_iota(jnp.int32, (blk, 1), 0)
    kf = float(top_k)

    iq_all = iq_ref[0]
    wt = iwt_ref[0]
    q_heads = [_head_q(iq_all, h, lane_lo) for h in range(N_IDX_HEADS)]
    w_rows = [wt[h:h + 1, :] for h in range(N_IDX_HEADS)]

    def score_body(c, carry):
        mn, mx = carry
        start = pl.multiple_of(c * blk, blk)
        kk = ik_ref[0, pl.ds(start, blk), :]
        acc = jnp.zeros((blk, blk), F32)
        for h in range(N_IDX_HEADS):
            acc = acc + jnp.maximum(_dot_nt(kk, q_heads[h]), 0.0) * w_rows[h]
        causal = (c * blk + krow) <= qpos
        sc = jnp.where(causal, acc, -jnp.inf)
        score_ref[c] = sc
        mx = jnp.maximum(mx, jnp.max(sc, axis=0, keepdims=True))
        mn = jnp.minimum(mn, jnp.min(jnp.where(causal, acc, jnp.inf), axis=0, keepdims=True))
        return mn, mx

    mn, mx = lax.fori_loop(0, n_chunks, score_body,
                           (jnp.full((1, blk), jnp.inf, F32), jnp.full((1, blk), -jnp.inf, F32)))

    n_pairs = (n_chunks + 1) // 2

    @pl.when(n_chunks % 2 == 1)
    def _():
        score_ref[n_chunks] = jnp.full((blk, blk), -jnp.inf, F32)

    def pair_sum(hit):
        m = jnp.where(hit, 1.0, 0.0)
        return jnp.sum(m[0] + m[1], axis=0, keepdims=True)

    def count_ge(cand):
        def body(j, acc):
            return acc + pair_sum(score_ref[pl.ds(2 * j, 2)] >= cand)
        return lax.fori_loop(0, n_pairs, body, jnp.zeros((1, blk), F32))

    def count_ge_gt(cand):
        def body(j, carry):
            ge, gt = carry
            s = score_ref[pl.ds(2 * j, 2)]
            return ge + pair_sum(s >= cand), gt + pair_sum(s > cand)
        z = jnp.zeros((1, blk), F32)
        return lax.fori_loop(0, n_pairs, body, (z, z))

    n_valid = (qpos + 1).astype(F32)
    need = n_valid > kf
    ge0, gt0 = count_ge_gt(jnp.zeros((1, blk), F32))
    kth_is_zero = jnp.logical_and(gt0 < kf, ge0 >= kf)
    nonneg = ge0 >= kf
    lo0 = jnp.where(nonneg, 0.0, mn)
    cnt0 = jnp.where(nonneg, ge0, n_valid)
    above_max = jnp.where(mx > 0.0, jnp.minimum(mx * 2.0, F32_MAX), 1.0)
    hi0 = jnp.where(nonneg, above_max, 0.0)
    done0 = jnp.logical_or(jnp.logical_not(need), jnp.logical_or(kth_is_zero, cnt0 == kf))
    active0 = jnp.where(done0, 0.0, 1.0)

    def halve(_, st):
        lo, hi, cnt_lo, active = st
        mid = 0.5 * lo + 0.5 * hi
        conv = jnp.logical_or(mid <= lo, mid >= hi)
        cnt = count_ge(mid)
        upd = jnp.logical_and(active > 0.0, jnp.logical_not(conv))
        ge = cnt >= kf
        up_lo = jnp.logical_and(upd, ge)
        up_hi = jnp.logical_and(upd, jnp.logical_not(ge))
        lo = jnp.where(up_lo, mid, lo)
        cnt_lo = jnp.where(up_lo, cnt, cnt_lo)
        hi = jnp.where(up_hi, mid, hi)
        finished = jnp.logical_or(conv, jnp.logical_and(up_lo, cnt == kf))
        return lo, hi, cnt_lo, jnp.where(finished, 0.0, active)

    def bis_cond(st):
        return jnp.logical_and(st[4] > 0.0, st[5] < MAX_BISECT_STEPS)

    def bis_body(st):
        it = st[5]
        n_steps = jnp.where(it == 0, BISECT_FIRST_STEPS, BISECT_STEPS_PER_TEST)
        lo, hi, cnt_lo, active = lax.fori_loop(0, n_steps, halve, st[:4])
        return lo, hi, cnt_lo, active, jnp.max(active), it + n_steps

    lo, _, cnt_lo, _, _, _ = lax.while_loop(
        bis_cond, bis_body, (lo0, hi0, cnt0, active0, jnp.max(active0), jnp.int32(0)))
    thr = jnp.where(need, lo, -F32_MAX)

    excess = jnp.logical_and(need, cnt_lo > kf)

    @pl.when(jnp.max(jnp.where(excess, 1.0, 0.0)) > 0.0)
    def _():
        _, gt = count_ge_gt(thr)
        allow = kf - gt
        r_i = lax.broadcasted_iota(jnp.int32, (blk, blk), 0)
        c_i = lax.broadcasted_iota(jnp.int32, (blk, blk), 1)
        earlier = jnp.where(c_i < r_i, 1.0, 0.0).astype(BF16)

        def tie_body(c, seen):
            s = score_ref[c]
            eq = jnp.logical_and(s == thr, excess)
            eqf = jnp.where(eq, 1.0, 0.0)
            rank = seen + _dot(earlier, eqf.astype(BF16))
            drop = jnp.logical_and(eq, rank >= allow)
            score_ref[c] = jnp.where(drop, -jnp.inf, s)
            return seen + jnp.sum(eqf, axis=0, keepdims=True)

        lax.fori_loop(0, n_chunks, tie_body, jnp.zeros((1, blk), F32))

    aq_all = aq_ref[0]
    qa_heads = [_head_q(aq_all, h, lane_lo) for h in range(N_DSA_HEADS)]

    def attend(bounded, shift):
        def att_body(c, carry):
            start = pl.multiple_of(c * blk, blk)
            bias = jnp.where(score_ref[c] >= thr, shift, MASK_VALUE)
            s_list, vt_list = [], []
            for h in range(N_DSA_HEADS):
                pair = slice((h // 2) * LANES, (h // 2 + 1) * LANES)
                kk = ak_ref[0, pl.ds(start, blk), pair]
                s_list.append(_dot_nt(kk, qa_heads[h]) + bias)
                vt_list.append(avt_ref[0, c, pair, :])
            if bounded:
                _bounded_update(s_list, vt_list, l_ref, acc_ref)
            else:
                _softmax_update(s_list, vt_list, m_ref, l_ref, acc_ref)
            return carry

        _init_softmax_state(m_ref, l_ref, acc_ref)
        lax.fori_loop(0, n_chunks, att_body, 0)

    head_sel = _head_selector(aq_all.shape[1])

    @pl.when(i == 0)
    def _():
        _max_key_sq_norms(ak_ref, kmax_ref, head_sel, blk)

    bound = jnp.sqrt(_head_sq_norms(aq_all, head_sel) * kmax_ref[:, :1])
    attend(True, -jnp.max(bound, axis=0, keepdims=True))

    @pl.when(_softmax_underflowed(l_ref))
    def _():
        attend(False, 0.0)

    for j in range(N_DSA_HEADS // 2):
        a = acc_ref[2 * j] / l_ref[2 * j]
        b = acc_ref[2 * j + 1] / l_ref[2 * j + 1]
        o_t = jnp.concatenate([a[:HEAD_DIM], b[HEAD_DIM:]], axis=0)
        o_ref[0, :, j * LANES:(j + 1) * LANES] = o_t.T.astype(o_ref.dtype)


def _dsa(iq, ik2, iwt, aq, ak, avt, *, blk):
    bsz, seq, _ = aq.shape
    top_k = min(TOPK_MAX, seq // 4)
    qblk = lambda b, i: (b, i, 0)
    whole = lambda b, i: (b, 0, 0)
    return pl.pallas_call(
        functools.partial(_dsa_kernel, blk=blk, top_k=top_k),
        grid=(bsz, seq // blk),
        in_specs=[pl.BlockSpec((1, blk, 512), qblk),
                  pl.BlockSpec((1, seq, LANES), whole),
                  pl.BlockSpec((1, N_IDX_HEADS, blk), lambda b, i: (b, 0, i)),
                  pl.BlockSpec((1, blk, 512), qblk),
                  pl.BlockSpec((1, seq, 512), whole),
                  pl.BlockSpec((1, seq // blk, 512, blk), lambda b, i: (b, 0, 0, 0))],
        out_specs=pl.BlockSpec((1, blk, 512), qblk),
        out_shape=jax.ShapeDtypeStruct((bsz, seq, 512), BF16),
        scratch_shapes=[pltpu.VMEM((2 * ((seq // blk + 1) // 2), blk, blk), F32),
                        pltpu.VMEM((N_DSA_HEADS, 1, blk), F32),
                        pltpu.VMEM((N_DSA_HEADS, 1, blk), F32),
                        pltpu.VMEM((N_DSA_HEADS, LANES, blk), F32),
                        pltpu.VMEM((BF16_SUBLANES, LANES), F32)],
        compiler_params=pltpu.CompilerParams(
            dimension_semantics=("parallel", "arbitrary"), vmem_limit_bytes=VMEM_LIMIT_BYTES),
        name="dsa",
    )(iq, ik2, iwt, aq, ak, avt)


def _diff_kernel(lam_ref, q_ref, k_ref, vt_ref, sw_ref, o_ref, m_ref, l_ref, acc_ref, kmax_ref,
                 *, blk, lam_init):
    i = pl.program_id(1)
    n_maps = 2 * N_DIFF_HEADS
    lane = lax.broadcasted_iota(jnp.int32, (1, LANES), 1)
    lane_lo = lane < HEAD_DIM
    krow = lax.broadcasted_iota(jnp.int32, (blk, 1), 0)
    qcol = lax.broadcasted_iota(jnp.int32, (1, blk), 1)

    q_all = q_ref[0]
    q_maps = [_head_q(q_all, g, lane_lo) for g in range(n_maps)]

    def attend(shifts):
        def step(c, diagonal):
            start = pl.multiple_of(c * blk, blk)
            if diagonal:
                bias = jnp.where(krow <= qcol, 0.0, MASK_VALUE)
            s_list, vt_list = [], []
            for h in range(N_DIFF_HEADS):
                cols = slice(h * LANES, (h + 1) * LANES)
                kk = k_ref[0, pl.ds(start, blk), cols]
                for comp in range(2):
                    g = 2 * h + comp
                    s = _dot_nt(kk, q_maps[g])
                    if shifts is not None:
                        s = s - shifts[g]
                    s_list.append(s + bias if diagonal else s)
                    vt_list.append(vt_ref[0, c, cols, :])
            if shifts is not None:
                _bounded_update(s_list, vt_list, l_ref, acc_ref)
            else:
                _softmax_update(s_list, vt_list, m_ref, l_ref, acc_ref)

        def body(c, carry):
            step(c, False)
            return carry

        _init_softmax_state(m_ref, l_ref, acc_ref)
        lax.fori_loop(0, i, body, 0)
        step(i, True)

    head_sel = _head_selector(q_all.shape[1])

    @pl.when(i == 0)
    def _():
        _max_key_sq_norms(k_ref, kmax_ref, head_sel, blk)

    bound = jnp.sqrt(_head_sq_norms(q_all, head_sel) * kmax_ref[:, :1])
    attend([bound[g:g + 1] for g in range(n_maps)])

    @pl.when(_softmax_underflowed(l_ref))
    def _():
        attend(None)

    lam_vecs = lam_ref[...]
    lam = (jnp.exp(jnp.sum(lam_vecs[0:1] * lam_vecs[1:2], axis=-1, keepdims=True))
           - jnp.exp(jnp.sum(lam_vecs[2:3] * lam_vecs[3:4], axis=-1, keepdims=True)) + lam_init)
    for h in range(N_DIFF_HEADS):
        out = acc_ref[2 * h] / l_ref[2 * h] - lam * (acc_ref[2 * h + 1] / l_ref[2 * h + 1])
        out = out * lax.rsqrt(jnp.mean(out * out, axis=0, keepdims=True) + NORM_EPS)
        out = out * sw_ref[...] * (1.0 - lam_init)
        o_ref[0, :, h * LANES:(h + 1) * LANES] = out.T.astype(o_ref.dtype)


def _diff_attn(lam_vecs, dq, dk, dvt, subln_col, *, blk, lam_init):
    bsz, seq, _ = dq.shape
    qblk = lambda b, i: (b, i, 0)
    whole = lambda b, i: (b, 0, 0)
    n_maps = 2 * N_DIFF_HEADS
    return pl.pallas_call(
        functools.partial(_diff_kernel, blk=blk, lam_init=lam_init),
        grid=(bsz, seq // blk),
        in_specs=[_const_spec(lam_vecs.shape),
                  pl.BlockSpec((1, blk, 512), qblk),
                  pl.BlockSpec((1, seq, 512), whole),
                  pl.BlockSpec((1, seq // blk, 512, blk), lambda b, i: (b, 0, 0, 0)),
                  _const_spec((DIFF_V_DIM, 1))],
        out_specs=pl.BlockSpec((1, blk, 512), qblk),
        out_shape=jax.ShapeDtypeStruct((bsz, seq, 512), BF16),
        scratch_shapes=[pltpu.VMEM((n_maps, 1, blk), F32),
                        pltpu.VMEM((n_maps, 1, blk), F32),
                        pltpu.VMEM((n_maps, DIFF_V_DIM, blk), F32),
                        pltpu.VMEM((BF16_SUBLANES, LANES), F32)],
        compiler_params=pltpu.CompilerParams(
            dimension_semantics=("parallel", "arbitrary"), vmem_limit_bytes=VMEM_LIMIT_BYTES),
        name="diff_attn",
    )(lam_vecs, dq, dk, dvt, subln_col)


def _out_mlp_kernel(x_ref, ma_ref, mb_ref, g1_ref, sh_ref, sc_ref, g2_ref, n2_ref, nf_ref,
                    woa_ref, wob_ref, w1_ref, w2_ref, o_ref, *, ff_chunk):
    x = x_ref[0]
    o = _dot(ma_ref[0], woa_ref[...]) + _dot(mb_ref[0], wob_ref[...])
    x1 = x + g1_ref[0] * o
    ms = jnp.mean(x1 * x1, axis=-1, keepdims=True)
    h = x1 * lax.rsqrt(ms + NORM_EPS) * n2_ref[...]
    hb = (h * (1.0 + sc_ref[0]) + sh_ref[0]).astype(BF16)
    d_ff = w1_ref.shape[1]
    ff = jnp.zeros_like(x)
    for j in range(d_ff // ff_chunk):
        u = jnp.maximum(_dot(hb, w1_ref[:, j * ff_chunk:(j + 1) * ff_chunk]), 0.0)
        ff = ff + _dot((u * u).astype(BF16), w2_ref[j * ff_chunk:(j + 1) * ff_chunk, :])
    x2 = x1 + g2_ref[0] * ff
    ms2 = jnp.mean(x2 * x2, axis=-1, keepdims=True)
    o_ref[0] = x2 * lax.rsqrt(ms2 + NORM_EPS) * nf_ref[...]


def _out_mlp(x, mix_a, mix_b, g1, sh2, sc2, g2, n2, nf, wo_a, wo_b, w1, w2, *, tile):
    bsz, seq, d = x.shape
    tok = lambda b, i: (b, i, 0)
    per_b = lambda b, i: (b, 0, 0)
    mod_spec = pl.BlockSpec((1, 1, d), per_b)
    return pl.pallas_call(
        functools.partial(_out_mlp_kernel, ff_chunk=1024),
        grid=(bsz, seq // tile),
        in_specs=[pl.BlockSpec((1, tile, d), tok),
                  pl.BlockSpec((1, tile, 512), tok),
                  pl.BlockSpec((1, tile, 512), tok),
                  mod_spec, mod_spec, mod_spec, mod_spec,
                  _const_spec((1, d)), _const_spec((1, d)),
                  _const_spec(wo_a.shape), _const_spec(wo_b.shape),
                  _const_spec(w1.shape), _const_spec(w2.shape)],
        out_specs=pl.BlockSpec((1, tile, d), tok),
        out_shape=jax.ShapeDtypeStruct((bsz, seq, d), F32),
        compiler_params=pltpu.CompilerParams(
            dimension_semantics=("parallel", "parallel"), vmem_limit_bytes=VMEM_LIMIT_BYTES),
        name="out_mlp",
    )(x, mix_a, mix_b, g1, sh2, sc2, g2, n2, nf, wo_a, wo_b, w1, w2)


def _pad_cols(w, width):
    return jnp.pad(w, ((0, 0), (0, width - w.shape[1])))


def kernel(x, c, positions, w_ada, b_ada, norm1_w, w_in, idx_k_ln_w, idx_k_ln_b, lambda_q1, lambda_k1,
           lambda_q2, lambda_k2, subln_w, w_out, norm2_w, w_ff1, w_ff2, norm_f_w):
    bsz, seq, d = x.shape
    depth = w_ada.shape[0]
    assert depth == 1, "the fused final RMSNorm assumes a single layer"
    half = HEAD_DIM // 2
    inv_freq = ROPE_THETA ** (-jnp.arange(half, dtype=F32) / half)
    invf = jnp.tile(inv_freq, LANES // half).reshape(1, LANES)
    pos3 = positions.reshape(bsz, seq, 1)
    blk = min(ATT_BLOCK, seq)
    tile = min(512, seq)

    for l in range(depth):
        mod = _adaln(c, w_ada[l], b_ada[l])
        sh1, sc1, g1, sh2, sc2, g2 = [m.reshape(bsz, 1, d) for m in jnp.split(mod, 6, axis=-1)]

        w = w_in[l]
        wm = jnp.concatenate([w[:, :1024], w[:, 1536:2048], w[:, 2120:3144]], axis=1).astype(BF16)
        ws = _pad_cols(w[:, 2048:2112], LANES).astype(BF16)
        wwt = jnp.pad(w[:, 2112:2120].T, ((0, BF16_SUBLANES - N_IDX_HEADS), (0, 0))).astype(BF16)
        wavt = w[:, 1024:1536].T.astype(BF16)
        wdvt = w[:, 3144:3656].T.astype(BF16)
        lnw = _pad_cols(idx_k_ln_w[l].reshape(1, IDX_DIM), LANES)
        lnb = _pad_cols(idx_k_ln_b[l].reshape(1, IDX_DIM), LANES)

        aq, ak, avt, iq, ik2, iwt, dq, dk, dvt = _in_proj(
            x, pos3, sh1, sc1, norm1_w[l].reshape(1, d), invf, wm, ws, wwt, wavt, wdvt, lnw, lnb,
            tile=tile, chunk=blk)

        out_a = _dsa(iq, ik2, iwt, aq, ak, avt, blk=blk)

        lam_vecs = jnp.stack([lambda_q1[l], lambda_k1[l], lambda_q2[l], lambda_k2[l]]).astype(F32)
        lam_init = 0.8 - 0.6 * math.exp(-0.3 * l)
        out_b = _diff_attn(lam_vecs, dq, dk, dvt, subln_w[l].reshape(DIFF_V_DIM, 1), blk=blk, lam_init=lam_init)

        wo = w_out[l].astype(BF16)
        x = _out_mlp(x, out_a, out_b, g1, sh2, sc2, g2, norm2_w[l].reshape(1, d), norm_f_w.reshape(1, d),
                     wo[:512], wo[512:], w_ff1[l].astype(BF16), w_ff2[l].astype(BF16), tile=tile)
    return x
```

```python
import functools
import math

import jax
import jax.numpy as jnp
from jax import lax
from jax.experimental import pallas as pl
from jax.experimental.pallas import tpu as pltpu

F32 = jnp.float32
BF16 = jnp.bfloat16

HEAD_DIM = 64
N_DSA_HEADS = 8
N_IDX_HEADS = 8
IDX_DIM = 64
N_DIFF_HEADS = 4
DIFF_V_DIM = 128
TOPK_MAX = 256
ROPE_THETA = 10000.0
NORM_EPS = 1e-6
LN_EPS = 1e-5
LANES = 128
BF16_SUBLANES = 16
ATT_BLOCK = 256
MASK_VALUE = -1e30
F32_MAX = float(jnp.finfo(jnp.float32).max)
MAX_BISECT_STEPS = 512
BISECT_FIRST_STEPS = 10
BISECT_STEPS_PER_TEST = 2
MIN_SAFE_NORMALISER = 2.0 ** -100
VMEM_LIMIT_BYTES = 56 * 1024 * 1024


def _dot(a, b):
    return jnp.dot(a, b, preferred_element_type=F32)


def _dot_nt(a, b):
    return lax.dot_general(a, b, (((1,), (1,)), ((), ())), preferred_element_type=F32)


def _const_spec(shape):
    zeros = (0,) * len(shape)
    return pl.BlockSpec(shape, lambda *_: zeros, pipeline_mode=pl.Buffered(1))


def _adaln_kernel(c_ref, w_ref, b_ref, o_ref):
    c = c_ref[...]
    s = c / (1.0 + jnp.exp(-c))
    o_ref[...] = jnp.dot(s, w_ref[...], preferred_element_type=F32,
                         precision=lax.Precision.HIGHEST) + b_ref[...]


def _adaln(c, w, b):
    bsz, d = c.shape
    n = w.shape[1]
    tn = 1536
    return pl.pallas_call(
        _adaln_kernel,
        grid=(n // tn,),
        in_specs=[pl.BlockSpec((bsz, d), lambda j: (0, 0)),
                  pl.BlockSpec((d, tn), lambda j: (0, j)),
                  pl.BlockSpec((1, tn), lambda j: (0, j))],
        out_specs=pl.BlockSpec((bsz, tn), lambda j: (0, j)),
        out_shape=jax.ShapeDtypeStruct((bsz, n), F32),
        compiler_params=pltpu.CompilerParams(vmem_limit_bytes=VMEM_LIMIT_BYTES),
        name="adaln",
    )(c, w, b.reshape(1, n))


def _rope_group(x, cos, sin_signed, first_half):
    nxt = pltpu.roll(x, LANES - HEAD_DIM // 2, axis=1)
    prv = pltpu.roll(x, HEAD_DIM // 2, axis=1)
    return x * cos + jnp.where(first_half, nxt, prv) * sin_signed


def _in_proj_kernel(x_ref, pos_ref, sh_ref, sc_ref, nw_ref, invf_ref, wm_ref, ws_ref, wwt_ref,
                    wavt_ref, wdvt_ref, lnw_ref, lnb_ref,
                    aq_ref, ak_ref, avt_ref, iq_ref, ik_ref, iwt_ref, dq_ref, dk_ref, dvt_ref):
    x = x_ref[0]
    ms = jnp.mean(x * x, axis=-1, keepdims=True)
    h = x * lax.rsqrt(ms + NORM_EPS) * nw_ref[...]
    h = h * (1.0 + sc_ref[0]) + sh_ref[0]
    hb = h.astype(BF16)

    lane = lax.broadcasted_iota(jnp.int32, (1, LANES), 1)
    first_half = (lane % HEAD_DIM) < (HEAD_DIM // 2)
    ang = pos_ref[0].astype(F32) * invf_ref[...]
    cos = jnp.cos(ang)
    sin = jnp.sin(ang)
    sin_signed = jnp.where(first_half, -sin, sin)

    def roped(col0, out_ref, scale):
        wide = _dot(hb, wm_ref[:, col0:col0 + 4 * LANES])
        for g in range(4):
            y = _rope_group(wide[:, g * LANES:(g + 1) * LANES], cos, sin_signed, first_half)
            if scale != 1.0:
                y = y * scale
            out_ref[0, :, g * LANES:(g + 1) * LANES] = y.astype(out_ref.dtype)

    qscale = HEAD_DIM ** -0.5 * math.log2(math.e)
    roped(0, aq_ref, qscale)
    roped(512, ak_ref, 1.0)
    roped(1024, iq_ref, 1.0)
    roped(1536, dq_ref, qscale)
    roped(2048, dk_ref, 1.0)

    n_sub = avt_ref.shape[1]
    chunk = avt_ref.shape[3]
    for j in range(n_sub):
        hj = hb[j * chunk:(j + 1) * chunk]
        avt_ref[0, j] = _dot_nt(wavt_ref[...], hj).astype(avt_ref.dtype)
        dvt_ref[0, j] = _dot_nt(wdvt_ref[...], hj).astype(dvt_ref.dtype)

    sm = _dot(hb, ws_ref[...])
    lo_half = lane < IDX_DIM
    mu = jnp.sum(sm, axis=-1, keepdims=True) * (1.0 / IDX_DIM)
    d = jnp.where(lo_half, sm - mu, 0.0)
    var = jnp.sum(d * d, axis=-1, keepdims=True) * (1.0 / IDX_DIM)
    y = d * lax.rsqrt(var + LN_EPS) * lnw_ref[...] + lnb_ref[...]
    y = y + pltpu.roll(y, IDX_DIM, axis=1)
    ik_ref[0] = _rope_group(y, cos, sin_signed, first_half).astype(ik_ref.dtype)

    wt = _dot_nt(wwt_ref[...], hb)
    iwt_ref[0] = wt[:N_IDX_HEADS] * ((N_IDX_HEADS ** -0.5) * (IDX_DIM ** -0.5))


def _in_proj(x, pos3, sh1, sc1, nw, invf, wm, ws, wwt, wavt, wdvt, lnw, lnb, *, tile, chunk):
    bsz, seq, d = x.shape
    tok = lambda b, i: (b, i, 0)
    per_b = lambda b, i: (b, 0, 0)
    wide = jax.ShapeDtypeStruct((bsz, seq, 512), BF16)
    wide_t = jax.ShapeDtypeStruct((bsz, seq // chunk, 512, chunk), BF16)
    wide_spec = pl.BlockSpec((1, tile, 512), tok)
    wide_t_spec = pl.BlockSpec((1, tile // chunk, 512, chunk), lambda b, i: (b, i, 0, 0))
    out_shape = [wide, wide, wide_t, wide,
                 jax.ShapeDtypeStruct((bsz, seq, LANES), BF16),
                 jax.ShapeDtypeStruct((bsz, N_IDX_HEADS, seq), F32),
                 wide, wide, wide_t]
    out_specs = [wide_spec, wide_spec, wide_t_spec, wide_spec,
                 pl.BlockSpec((1, tile, LANES), tok),
                 pl.BlockSpec((1, N_IDX_HEADS, tile), lambda b, i: (b, 0, i)),
                 wide_spec, wide_spec, wide_t_spec]
    return pl.pallas_call(
        _in_proj_kernel,
        grid=(bsz, seq // tile),
        in_specs=[pl.BlockSpec((1, tile, d), tok),
                  pl.BlockSpec((1, tile, 1), tok),
                  pl.BlockSpec((1, 1, d), per_b),
                  pl.BlockSpec((1, 1, d), per_b),
                  _const_spec((1, d)),
                  _const_spec((1, LANES)),
                  _const_spec(wm.shape),
                  _const_spec(ws.shape),
                  _const_spec(wwt.shape),
                  _const_spec(wavt.shape),
                  _const_spec(wdvt.shape),
                  _const_spec((1, LANES)),
                  _const_spec((1, LANES))],
        out_specs=out_specs,
        out_shape=out_shape,
        compiler_params=pltpu.CompilerParams(
            dimension_semantics=("parallel", "parallel"), vmem_limit_bytes=VMEM_LIMIT_BYTES),
        name="in_proj",
    )(x, pos3, sh1, sc1, nw, invf, wm, ws, wwt, wavt, wdvt, lnw, lnb)


def _head_q(q_all, h, lane_lo):
    pair = q_all[:, (h // 2) * LANES:(h // 2 + 1) * LANES]
    keep = lane_lo if h % 2 == 0 else jnp.logical_not(lane_lo)
    return jnp.where(keep, pair, jnp.zeros_like(pair))


def _softmax_update(s_list, vt_list, m_ref, l_ref, acc_ref):
    n = len(s_list)
    dv = vt_list[0].shape[0]
    ones = jnp.ones((BF16_SUBLANES, vt_list[0].shape[1]), BF16)
    m_old = [m_ref[g] for g in range(n)]
    m_new = [jnp.maximum(m_old[g], jnp.max(s_list[g], axis=0, keepdims=True)) for g in range(n)]
    p_list = [jnp.exp2(s_list[g] - m_new[g]).astype(BF16) for g in range(n)]
    alpha = [jnp.exp2(m_old[g] - m_new[g]) for g in range(n)]
    pv = [_dot(jnp.concatenate([vt_list[g], ones], axis=0), p_list[g]) for g in range(n)]
    for g in range(n):
        m_ref[g] = m_new[g]
        l_ref[g] = alpha[g] * l_ref[g] + pv[g][dv:dv + 1]
        acc_ref[g] = alpha[g] * acc_ref[g] + pv[g][:dv]


def _init_softmax_state(m_ref, l_ref, acc_ref):
    m_ref[...] = jnp.full(m_ref.shape, MASK_VALUE, F32)
    l_ref[...] = jnp.zeros(l_ref.shape, F32)
    acc_ref[...] = jnp.zeros(acc_ref.shape, F32)


def _bounded_update(s_list, vt_list, l_ref, acc_ref):
    n = len(s_list)
    dv = vt_list[0].shape[0]
    ones = jnp.ones((BF16_SUBLANES, vt_list[0].shape[1]), BF16)
    p_list = [jnp.exp2(s_list[g]).astype(BF16) for g in range(n)]
    pv = [_dot(jnp.concatenate([vt_list[g], ones], axis=0), p_list[g]) for g in range(n)]
    for g in range(n):
        l_ref[g] = l_ref[g] + pv[g][dv:dv + 1]
        acc_ref[g] = acc_ref[g] + pv[g][:dv]


def _head_selector(n_lanes):
    r = lax.broadcasted_iota(jnp.int32, (BF16_SUBLANES, n_lanes), 0)
    l = lax.broadcasted_iota(jnp.int32, (BF16_SUBLANES, n_lanes), 1)
    return jnp.where(l // HEAD_DIM == r, 1.0, 0.0).astype(BF16)


def _head_sq_norms(x, head_sel):
    xf = x.astype(F32)
    return _dot_nt(head_sel, (xf * xf).astype(BF16))


def _max_key_sq_norms(k_ref, kmax_ref, head_sel, blk):
    def body(c, mx):
        start = pl.multiple_of(c * blk, blk)
        n2 = _head_sq_norms(k_ref[0, pl.ds(start, blk), :], head_sel)
        return jnp.maximum(mx, jnp.max(n2, axis=1, keepdims=True))
    kmax = lax.fori_loop(0, k_ref.shape[1] // blk, body, jnp.zeros((BF16_SUBLANES, 1), F32))
    kmax_ref[...] = jnp.broadcast_to(kmax, kmax_ref.shape)


def _softmax_underflowed(l_ref):
    return jnp.logical_not(jnp.min(l_ref[...]) >= MIN_SAFE_NORMALISER)


def _dsa_kernel(iq_ref, ik_ref, iwt_ref, aq_ref, ak_ref, avt_ref, o_ref,
                score_ref, m_ref, l_ref, acc_ref, kmax_ref, *, blk, top_k):
    i = pl.program_id(1)
    n_chunks = i + 1
    lane = lax.broadcasted_iota(jnp.int32, (1, LANES), 1)
    lane_lo = lane < HEAD_DIM
    qpos = i * blk + lax.broadcasted_iota(jnp.int32, (1, blk), 1)
    krow = lax.broadcasted_iota(jnp.int32, (blk, 1), 0)
    kf = float(top_k)

    iq_all = iq_ref[0]
    wt = iwt_ref[0]
    q_heads = [_head_q(iq_all, h, lane_lo) for h in range(N_IDX_HEADS)]
    w_rows = [wt[h:h + 1, :] for h in range(N_IDX_HEADS)]

    n_pairs = (n_chunks + 1) // 2

    def score_body(j, carry):
        mn, mx = carry
        for u in range(2):
            c = 2 * j + u
            start = pl.multiple_of(c * blk, blk)
            kk = ik_ref[0, pl.ds(start, blk), :]
            acc = jnp.zeros((blk, blk), F32)
            for h in range(N_IDX_HEADS):
                acc = acc + jnp.maximum(_dot_nt(kk, q_heads[h]), 0.0) * w_rows[h]
            causal = (c * blk + krow) <= qpos
            sc = jnp.where(causal, acc, -jnp.inf)
            score_ref[c] = sc
            mx = jnp.maximum(mx, jnp.max(sc, axis=0, keepdims=True))
            mn = jnp.minimum(mn, jnp.min(jnp.where(causal, acc, jnp.inf), axis=0, keepdims=True))
        return mn, mx

    mn, mx = lax.fori_loop(0, n_pairs, score_body,
                           (jnp.full((1, blk), jnp.inf, F32), jnp.full((1, blk), -jnp.inf, F32)))


    def pair_sum(hit):
        m = jnp.where(hit, 1.0, 0.0)
        return jnp.sum(m[0] + m[1], axis=0, keepdims=True)

    def count_ge(cand):
        def body(j, acc):
            return acc + pair_sum(score_ref[pl.ds(2 * j, 2)] >= cand)
        return lax.fori_loop(0, n_pairs, body, jnp.zeros((1, blk), F32))

    def count_ge_gt(cand):
        def body(j, carry):
            ge, gt = carry
            s = score_ref[pl.ds(2 * j, 2)]
            return ge + pair_sum(s >= cand), gt + pair_sum(s > cand)
        z = jnp.zeros((1, blk), F32)
        return lax.fori_loop(0, n_pairs, body, (z, z))

    n_valid = (qpos + 1).astype(F32)
    need = n_valid > kf
    ge0, gt0 = count_ge_gt(jnp.zeros((1, blk), F32))
    kth_is_zero = jnp.logical_and(gt0 < kf, ge0 >= kf)
    nonneg = ge0 >= kf
    lo0 = jnp.where(nonneg, 0.0, mn)
    cnt0 = jnp.where(nonneg, ge0, n_valid)
    above_max = jnp.where(mx > 0.0, jnp.minimum(mx * 2.0, F32_MAX), 1.0)
    hi0 = jnp.where(nonneg, above_max, 0.0)
    done0 = jnp.logical_or(jnp.logical_not(need), jnp.logical_or(kth_is_zero, cnt0 == kf))
    active0 = jnp.where(done0, 0.0, 1.0)

    def halve(_, st):
        lo, hi, cnt_lo, active = st
        mid = 0.5 * lo + 0.5 * hi
        conv = jnp.logical_or(mid <= lo, mid >= hi)
        cnt = count_ge(mid)
        upd = jnp.logical_and(active > 0.0, jnp.logical_not(conv))
        ge = cnt >= kf
        up_lo = jnp.logical_and(upd, ge)
        up_hi = jnp.logical_and(upd, jnp.logical_not(ge))
        lo = jnp.where(up_lo, mid, lo)
        cnt_lo = jnp.where(up_lo, cnt, cnt_lo)
        hi = jnp.where(up_hi, mid, hi)
        finished = jnp.logical_or(conv, jnp.logical_and(up_lo, cnt == kf))
        return lo, hi, cnt_lo, jnp.where(finished, 0.0, active)

    def bis_cond(st):
        return jnp.logical_and(st[4] > 0.0, st[5] < MAX_BISECT_STEPS)

    def bis_body(st):
        it = st[5]
        n_steps = jnp.where(it == 0, BISECT_FIRST_STEPS, BISECT_STEPS_PER_TEST)
        lo, hi, cnt_lo, active = lax.fori_loop(0, n_steps, halve, st[:4])
        return lo, hi, cnt_lo, active, jnp.max(active), it + n_steps

    lo, _, cnt_lo, _, _, _ = lax.while_loop(
        bis_cond, bis_body, (lo0, hi0, cnt0, active0, jnp.max(active0), jnp.int32(0)))
    thr = jnp.where(need, lo, -F32_MAX)

    excess = jnp.logical_and(need, cnt_lo > kf)

    @pl.when(jnp.max(jnp.where(excess, 1.0, 0.0)) > 0.0)
    def _():
        _, gt = count_ge_gt(thr)
        allow = kf - gt
        r_i = lax.broadcasted_iota(jnp.int32, (blk, blk), 0)
        c_i = lax.broadcasted_iota(jnp.int32, (blk, blk), 1)
        earlier = jnp.where(c_i < r_i, 1.0, 0.0).astype(BF16)

        def tie_body(c, seen):
            s = score_ref[c]
            eq = jnp.logical_and(s == thr, excess)
            eqf = jnp.where(eq, 1.0, 0.0)
            rank = seen + _dot(earlier, eqf.astype(BF16))
            drop = jnp.logical_and(eq, rank >= allow)
            score_ref[c] = jnp.where(drop, -jnp.inf, s)
            return seen + jnp.sum(eqf, axis=0, keepdims=True)

        lax.fori_loop(0, n_chunks, tie_body, jnp.zeros((1, blk), F32))

    aq_all = aq_ref[0]
    qa_heads = [_head_q(aq_all, h, lane_lo) for h in range(N_DSA_HEADS)]

    def attend(bounded, shift):
        def att_body(c, carry):
            start = pl.multiple_of(c * blk, blk)
            bias = jnp.where(score_ref[c] >= thr, shift, MASK_VALUE)
            s_list, vt_list = [], []
            for h in range(N_DSA_HEADS):
                pair = slice((h // 2) * LANES, (h // 2 + 1) * LANES)
                kk = ak_ref[0, pl.ds(start, blk), pair]
                s_list.append(_dot_nt(kk, qa_heads[h]) + bias)
                vt_list.append(avt_ref[0, c, pair, :])
            if bounded:
                _bounded_update(s_list, vt_list, l_ref, acc_ref)
            else:
                _softmax_update(s_list, vt_list, m_ref, l_ref, acc_ref)
            return carry

        _init_softmax_state(m_ref, l_ref, acc_ref)
        lax.fori_loop(0, n_chunks, att_body, 0)

    head_sel = _head_selector(aq_all.shape[1])

    @pl.when(i == 0)
    def _():
        _max_key_sq_norms(ak_ref, kmax_ref, head_sel, blk)

    bound = jnp.sqrt(_head_sq_norms(aq_all, head_sel) * kmax_ref[:, :1])
    attend(True, -jnp.max(bound, axis=0, keepdims=True))

    @pl.when(_softmax_underflowed(l_ref))
    def _():
        attend(False, 0.0)

    for j in range(N_DSA_HEADS // 2):
        a = acc_ref[2 * j] / l_ref[2 * j]
        b = acc_ref[2 * j + 1] / l_ref[2 * j + 1]
        o_t = jnp.concatenate([a[:HEAD_DIM], b[HEAD_DIM:]], axis=0)
        o_ref[0, :, j * LANES:(j + 1) * LANES] = o_t.T.astype(o_ref.dtype)


def _dsa(iq, ik2, iwt, aq, ak, avt, *, blk):
    bsz, seq, _ = aq.shape
    top_k = min(TOPK_MAX, seq // 4)
    assert (seq // blk) % 2 == 0, "key chunks are walked in pairs"
    qblk = lambda b, i: (b, i, 0)
    whole = lambda b, i: (b, 0, 0)
    return pl.pallas_call(
        functools.partial(_dsa_kernel, blk=blk, top_k=top_k),
        grid=(bsz, seq // blk),
        in_specs=[pl.BlockSpec((1, blk, 512), qblk),
                  pl.BlockSpec((1, seq, LANES), whole),
                  pl.BlockSpec((1, N_IDX_HEADS, blk), lambda b, i: (b, 0, i)),
                  pl.BlockSpec((1, blk, 512), qblk),
                  pl.BlockSpec((1, seq, 512), whole),
                  pl.BlockSpec((1, seq // blk, 512, blk), lambda b, i: (b, 0, 0, 0))],
        out_specs=pl.BlockSpec((1, blk, 512), qblk),
        out_shape=jax.ShapeDtypeStruct((bsz, seq, 512), BF16),
        scratch_shapes=[pltpu.VMEM((2 * ((seq // blk + 1) // 2), blk, blk), F32),
                        pltpu.VMEM((N_DSA_HEADS, 1, blk), F32),
                        pltpu.VMEM((N_DSA_HEADS, 1, blk), F32),
                        pltpu.VMEM((N_DSA_HEADS, LANES, blk), F32),
                        pltpu.VMEM((BF16_SUBLANES, LANES), F32)],
        compiler_params=pltpu.CompilerParams(
            dimension_semantics=("parallel", "arbitrary"), vmem_limit_bytes=VMEM_LIMIT_BYTES),
        name="dsa",
    )(iq, ik2, iwt, aq, ak, avt)


def _diff_kernel(lam_ref, q_ref, k_ref, vt_ref, sw_ref, o_ref, m_ref, l_ref, acc_ref, kmax_ref,
                 *, blk, lam_init):
    i = pl.program_id(1)
    n_maps = 2 * N_DIFF_HEADS
    lane = lax.broadcasted_iota(jnp.int32, (1, LANES), 1)
    lane_lo = lane < HEAD_DIM
    krow = lax.broadcasted_iota(jnp.int32, (blk, 1), 0)
    qcol = lax.broadcasted_iota(jnp.int32, (1, blk), 1)

    q_all = q_ref[0]
    q_maps = [_head_q(q_all, g, lane_lo) for g in range(n_maps)]

    def attend(shifts):
        def step(c, diagonal):
            start = pl.multiple_of(c * blk, blk)
            if diagonal:
                bias = jnp.where(krow <= qcol, 0.0, MASK_VALUE)
            s_list, vt_list = [], []
            for h in range(N_DIFF_HEADS):
                cols = slice(h * LANES, (h + 1) * LANES)
                kk = k_ref[0, pl.ds(start, blk), cols]
                for comp in range(2):
                    g = 2 * h + comp
                    s = _dot_nt(kk, q_maps[g])
                    if shifts is not None:
                        s = s - shifts[g]
                    s_list.append(s + bias if diagonal else s)
                    vt_list.append(vt_ref[0, c, cols, :])
            if shifts is not None:
                _bounded_update(s_list, vt_list, l_ref, acc_ref)
            else:
                _softmax_update(s_list, vt_list, m_ref, l_ref, acc_ref)

        def body(c, carry):
            step(c, False)
            return carry

        _init_softmax_state(m_ref, l_ref, acc_ref)
        lax.fori_loop(0, i, body, 0)
        step(i, True)

    head_sel = _head_selector(q_all.shape[1])

    @pl.when(i == 0)
    def _():
        _max_key_sq_norms(k_ref, kmax_ref, head_sel, blk)

    bound = jnp.sqrt(_head_sq_norms(q_all, head_sel) * kmax_ref[:, :1])
    attend([bound[g:g + 1] for g in range(n_maps)])

    @pl.when(_softmax_underflowed(l_ref))
    def _():
        attend(None)

    lam_vecs = lam_ref[...]
    lam = (jnp.exp(jnp.sum(lam_vecs[0:1] * lam_vecs[1:2], axis=-1, keepdims=True))
           - jnp.exp(jnp.sum(lam_vecs[2:3] * lam_vecs[3:4], axis=-1, keepdims=True)) + lam_init)
    for h in range(N_DIFF_HEADS):
        out = acc_ref[2 * h] / l_ref[2 * h] - lam * (acc_ref[2 * h + 1] / l_ref[2 * h + 1])
        out = out * lax.rsqrt(jnp.mean(out * out, axis=0, keepdims=True) + NORM_EPS)
        out = out * sw_ref[...] * (1.0 - lam_init)
        o_ref[0, :, h * LANES:(h + 1) * LANES] = out.T.astype(o_ref.dtype)


def _diff_attn(lam_vecs, dq, dk, dvt, subln_col, *, blk, lam_init):
    bsz, seq, _ = dq.shape
    qblk = lambda b, i: (b, i, 0)
    whole = lambda b, i: (b, 0, 0)
    n_maps = 2 * N_DIFF_HEADS
    return pl.pallas_call(
        functools.partial(_diff_kernel, blk=blk, lam_init=lam_init),
        grid=(bsz, seq // blk),
        in_specs=[_const_spec(lam_vecs.shape),
                  pl.BlockSpec((1, blk, 512), qblk),
                  pl.BlockSpec((1, seq, 512), whole),
                  pl.BlockSpec((1, seq // blk, 512, blk), lambda b, i: (b, 0, 0, 0)),
                  _const_spec((DIFF_V_DIM, 1))],
        out_specs=pl.BlockSpec((1, blk, 512), qblk),
        out_shape=jax.ShapeDtypeStruct((bsz, seq, 512), BF16),
        scratch_shapes=[pltpu.VMEM((n_maps, 1, blk), F32),
                        pltpu.VMEM((n_maps, 1, blk), F32),
                        pltpu.VMEM((n_maps, DIFF_V_DIM, blk), F32),
                        pltpu.VMEM((BF16_SUBLANES, LANES), F32)],
        compiler_params=pltpu.CompilerParams(
            dimension_semantics=("parallel", "arbitrary"), vmem_limit_bytes=VMEM_LIMIT_BYTES),
        name="diff_attn",
    )(lam_vecs, dq, dk, dvt, subln_col)


def _out_mlp_kernel(x_ref, ma_ref, mb_ref, g1_ref, sh_ref, sc_ref, g2_ref, n2_ref, nf_ref,
                    woa_ref, wob_ref, w1_ref, w2_ref, o_ref, *, ff_chunk):
    x = x_ref[0]
    o = _dot(ma_ref[0], woa_ref[...]) + _dot(mb_ref[0], wob_ref[...])
    x1 = x + g1_ref[0] * o
    ms = jnp.mean(x1 * x1, axis=-1, keepdims=True)
    h = x1 * lax.rsqrt(ms + NORM_EPS) * n2_ref[...]
    hb = (h * (1.0 + sc_ref[0]) + sh_ref[0]).astype(BF16)
    d_ff = w1_ref.shape[1]
    ff = jnp.zeros_like(x)
    for j in range(d_ff // ff_chunk):
        u = jnp.maximum(_dot(hb, w1_ref[:, j * ff_chunk:(j + 1) * ff_chunk]), 0.0)
        ff = ff + _dot((u * u).astype(BF16), w2_ref[j * ff_chunk:(j + 1) * ff_chunk, :])
    x2 = x1 + g2_ref[0] * ff
    ms2 = jnp.mean(x2 * x2, axis=-1, keepdims=True)
    o_ref[0] = x2 * lax.rsqrt(ms2 + NORM_EPS) * nf_ref[...]


def _out_mlp(x, mix_a, mix_b, g1, sh2, sc2, g2, n2, nf, wo_a, wo_b, w1, w2, *, tile):
    bsz, seq, d = x.shape
    tok = lambda b, i: (b, i, 0)
    per_b = lambda b, i: (b, 0, 0)
    mod_spec = pl.BlockSpec((1, 1, d), per_b)
    return pl.pallas_call(
        functools.partial(_out_mlp_kernel, ff_chunk=1024),
        grid=(bsz, seq // tile),
        in_specs=[pl.BlockSpec((1, tile, d), tok),
                  pl.BlockSpec((1, tile, 512), tok),
                  pl.BlockSpec((1, tile, 512), tok),
                  mod_spec, mod_spec, mod_spec, mod_spec,
                  _const_spec((1, d)), _const_spec((1, d)),
                  _const_spec(wo_a.shape), _const_spec(wo_b.shape),
                  _const_spec(w1.shape), _const_spec(w2.shape)],
        out_specs=pl.BlockSpec((1, tile, d), tok),
        out_shape=jax.ShapeDtypeStruct((bsz, seq, d), F32),
        compiler_params=pltpu.CompilerParams(
            dimension_semantics=("parallel", "parallel"), vmem_limit_bytes=VMEM_LIMIT_BYTES),
        name="out_mlp",
    )(x, mix_a, mix_b, g1, sh2, sc2, g2, n2, nf, wo_a, wo_b, w1, w2)


def _pad_cols(w, width):
    return jnp.pad(w, ((0, 0), (0, width - w.shape[1])))


def kernel(x, c, positions, w_ada, b_ada, norm1_w, w_in, idx_k_ln_w, idx_k_ln_b, lambda_q1, lambda_k1,
           lambda_q2, lambda_k2, subln_w, w_out, norm2_w, w_ff1, w_ff2, norm_f_w):
    bsz, seq, d = x.shape
    depth = w_ada.shape[0]
    assert depth == 1, "the fused final RMSNorm assumes a single layer"
    half = HEAD_DIM // 2
    inv_freq = ROPE_THETA ** (-jnp.arange(half, dtype=F32) / half)
    invf = jnp.tile(inv_freq, LANES // half).reshape(1, LANES)
    pos3 = positions.reshape(bsz, seq, 1)
    blk = min(ATT_BLOCK, seq)
    tile = min(512, seq)

    for l in range(depth):
        mod = _adaln(c, w_ada[l], b_ada[l])
        sh1, sc1, g1, sh2, sc2, g2 = [m.reshape(bsz, 1, d) for m in jnp.split(mod, 6, axis=-1)]

        w = w_in[l]
        wm = jnp.concatenate([w[:, :1024], w[:, 1536:2048], w[:, 2120:3144]], axis=1).astype(BF16)
        ws = _pad_cols(w[:, 2048:2112], LANES).astype(BF16)
        wwt = jnp.pad(w[:, 2112:2120].T, ((0, BF16_SUBLANES - N_IDX_HEADS), (0, 0))).astype(BF16)
        wavt = w[:, 1024:1536].T.astype(BF16)
        wdvt = w[:, 3144:3656].T.astype(BF16)
        lnw = _pad_cols(idx_k_ln_w[l].reshape(1, IDX_DIM), LANES)
        lnb = _pad_cols(idx_k_ln_b[l].reshape(1, IDX_DIM), LANES)

        aq, ak, avt, iq, ik2, iwt, dq, dk, dvt = _in_proj(
            x, pos3, sh1, sc1, norm1_w[l].reshape(1, d), invf, wm, ws, wwt, wavt, wdvt, lnw, lnb,
            tile=tile, chunk=blk)

        out_a = _dsa(iq, ik2, iwt, aq, ak, avt, blk=blk)

        lam_vecs = jnp.stack([lambda_q1[l], lambda_k1[l], lambda_q2[l], lambda_k2[l]]).astype(F32)
        lam_init = 0.8 - 0.6 * math.exp(-0.3 * l)
        out_b = _diff_attn(lam_vecs, dq, dk, dvt, subln_w[l].reshape(DIFF_V_DIM, 1), blk=blk, lam_init=lam_init)

        wo = w_out[l].astype(BF16)
        x = _out_mlp(x, out_a, out_b, g1, sh2, sc2, g2, norm2_w[l].reshape(1, d), norm_f_w.reshape(1, d),
                     wo[:512], wo[512:], w_ff1[l].astype(BF16), w_ff2[l].astype(BF16), tile=tile)
    return x
```

```python
import functools
import math

import jax
import jax.numpy as jnp
from jax import lax
from jax.experimental import pallas as pl
from jax.experimental.pallas import tpu as pltpu

F32 = jnp.float32
BF16 = jnp.bfloat16

HEAD_DIM = 64
N_DSA_HEADS = 8
N_IDX_HEADS = 8
IDX_DIM = 64
N_DIFF_HEADS = 4
DIFF_V_DIM = 128
TOPK_MAX = 256
ROPE_THETA = 10000.0
NORM_EPS = 1e-6
LN_EPS = 1e-5
LANES = 128
F32_SUBLANES = 8
BF16_SUBLANES = 16
ATT_BLOCK = 256
MASK_VALUE = -1e30
F32_MAX = float(jnp.finfo(jnp.float32).max)
MAX_BISECT_STEPS = 512
BISECT_FIRST_STEPS = 16
BISECT_FUSED_STEPS = 16
BISECT_STEPS_PER_TEST = 2
MIN_SAFE_NORMALISER = 2.0 ** -100
VMEM_LIMIT_BYTES = 56 * 1024 * 1024


def _dot(a, b):
    return jnp.dot(a, b, preferred_element_type=F32)


def _dot_nt(a, b):
    return lax.dot_general(a, b, (((1,), (1,)), ((), ())), preferred_element_type=F32)


def _const_spec(shape):
    zeros = (0,) * len(shape)
    return pl.BlockSpec(shape, lambda *_: zeros, pipeline_mode=pl.Buffered(1))


def _adaln_kernel(c_ref, w_ref, b_ref, o_ref):
    c = c_ref[...]
    s = c / (1.0 + jnp.exp(-c))
    o_ref[...] = jnp.dot(s, w_ref[...], preferred_element_type=F32,
                         precision=lax.Precision.HIGHEST) + b_ref[...]


def _adaln(c, w, b):
    bsz, d = c.shape
    n = w.shape[1]
    tn = 1536
    return pl.pallas_call(
        _adaln_kernel,
        grid=(n // tn,),
        in_specs=[pl.BlockSpec((bsz, d), lambda j: (0, 0)),
                  pl.BlockSpec((d, tn), lambda j: (0, j)),
                  pl.BlockSpec((1, tn), lambda j: (0, j))],
        out_specs=pl.BlockSpec((bsz, tn), lambda j: (0, j)),
        out_shape=jax.ShapeDtypeStruct((bsz, n), F32),
        compiler_params=pltpu.CompilerParams(vmem_limit_bytes=VMEM_LIMIT_BYTES),
        name="adaln",
    )(c, w, b.reshape(1, n))


def _rope_group(x, cos, sin_signed, first_half):
    nxt = pltpu.roll(x, LANES - HEAD_DIM // 2, axis=1)
    prv = pltpu.roll(x, HEAD_DIM // 2, axis=1)
    return x * cos + jnp.where(first_half, nxt, prv) * sin_signed


def _in_proj_kernel(x_ref, pos_ref, sh_ref, sc_ref, nw_ref, invf_ref, wm_ref, ws_ref, wwt_ref,
                    wavt_ref, wdvt_ref, lnw_ref, lnb_ref,
                    aq_ref, ak_ref, avt_ref, iq_ref, ik_ref, iwt_ref, dq_ref, dk_ref, dvt_ref):
    x = x_ref[0]
    ms = jnp.mean(x * x, axis=-1, keepdims=True)
    h = x * lax.rsqrt(ms + NORM_EPS) * nw_ref[...]
    h = h * (1.0 + sc_ref[0]) + sh_ref[0]
    hb = h.astype(BF16)

    lane = lax.broadcasted_iota(jnp.int32, (1, LANES), 1)
    first_half = (lane % HEAD_DIM) < (HEAD_DIM // 2)
    ang = pos_ref[0].astype(F32) * invf_ref[...]
    cos = jnp.cos(ang)
    sin = jnp.sin(ang)
    sin_signed = jnp.where(first_half, -sin, sin)

    def roped(col0, out_ref, scale):
        wide = _dot(hb, wm_ref[:, col0:col0 + 4 * LANES])
        for g in range(4):
            y = _rope_group(wide[:, g * LANES:(g + 1) * LANES], cos, sin_signed, first_half)
            if scale != 1.0:
                y = y * scale
            out_ref[0, :, g * LANES:(g + 1) * LANES] = y.astype(out_ref.dtype)

    qscale = HEAD_DIM ** -0.5 * math.log2(math.e)
    roped(0, aq_ref, qscale)
    roped(512, ak_ref, 1.0)
    roped(1024, iq_ref, 1.0)
    roped(1536, dq_ref, qscale)
    roped(2048, dk_ref, 1.0)

    n_sub = avt_ref.shape[1]
    chunk = avt_ref.shape[3]
    for j in range(n_sub):
        hj = hb[j * chunk:(j + 1) * chunk]
        avt_ref[0, j] = _dot_nt(wavt_ref[...], hj).astype(avt_ref.dtype)
        dvt_ref[0, j] = _dot_nt(wdvt_ref[...], hj).astype(dvt_ref.dtype)

    sm = _dot(hb, ws_ref[...])
    lo_half = lane < IDX_DIM
    mu = jnp.sum(sm, axis=-1, keepdims=True) * (1.0 / IDX_DIM)
    d = jnp.where(lo_half, sm - mu, 0.0)
    var = jnp.sum(d * d, axis=-1, keepdims=True) * (1.0 / IDX_DIM)
    y = d * lax.rsqrt(var + LN_EPS) * lnw_ref[...] + lnb_ref[...]
    y = y + pltpu.roll(y, IDX_DIM, axis=1)
    ik_ref[0] = _rope_group(y, cos, sin_signed, first_half).astype(ik_ref.dtype)

    wt = _dot_nt(wwt_ref[...], hb)
    iwt_ref[0] = wt[:N_IDX_HEADS] * ((N_IDX_HEADS ** -0.5) * (IDX_DIM ** -0.5))


def _in_proj(x, pos3, sh1, sc1, nw, invf, wm, ws, wwt, wavt, wdvt, lnw, lnb, *, tile, chunk):
    bsz, seq, d = x.shape
    tok = lambda b, i: (b, i, 0)
    per_b = lambda b, i: (b, 0, 0)
    wide = jax.ShapeDtypeStruct((bsz, seq, 512), BF16)
    wide_t = jax.ShapeDtypeStruct((bsz, seq // chunk, 512, chunk), BF16)
    wide_spec = pl.BlockSpec((1, tile, 512), tok)
    wide_t_spec = pl.BlockSpec((1, tile // chunk, 512, chunk), lambda b, i: (b, i, 0, 0))
    out_shape = [wide, wide, wide_t, wide,
                 jax.ShapeDtypeStruct((bsz, seq, LANES), BF16),
                 jax.ShapeDtypeStruct((bsz, N_IDX_HEADS, seq), F32),
                 wide, wide, wide_t]
    out_specs = [wide_spec, wide_spec, wide_t_spec, wide_spec,
                 pl.BlockSpec((1, tile, LANES), tok),
                 pl.BlockSpec((1, N_IDX_HEADS, tile), lambda b, i: (b, 0, i)),
                 wide_spec, wide_spec, wide_t_spec]
    return pl.pallas_call(
        _in_proj_kernel,
        grid=(bsz, seq // tile),
        in_specs=[pl.BlockSpec((1, tile, d), tok),
                  pl.BlockSpec((1, tile, 1), tok),
                  pl.BlockSpec((1, 1, d), per_b),
                  pl.BlockSpec((1, 1, d), per_b),
                  _const_spec((1, d)),
                  _const_spec((1, LANES)),
                  _const_spec(wm.shape),
                  _const_spec(ws.shape),
                  _const_spec(wwt.shape),
                  _const_spec(wavt.shape),
                  _const_spec(wdvt.shape),
                  _const_spec((1, LANES)),
                  _const_spec((1, LANES))],
        out_specs=out_specs,
        out_shape=out_shape,
        compiler_params=pltpu.CompilerParams(
            dimension_semantics=("parallel", "parallel"), vmem_limit_bytes=VMEM_LIMIT_BYTES),
        name="in_proj",
    )(x, pos3, sh1, sc1, nw, invf, wm, ws, wwt, wavt, wdvt, lnw, lnb)


def _head_q(q_all, h, lane_lo):
    pair = q_all[:, (h // 2) * LANES:(h // 2 + 1) * LANES]
    keep = lane_lo if h % 2 == 0 else jnp.logical_not(lane_lo)
    return jnp.where(keep, pair, jnp.zeros_like(pair))


def _softmax_update(s_list, vt_list, m_ref, l_ref, acc_ref):
    n = len(s_list)
    dv = vt_list[0].shape[0]
    ones = jnp.ones((BF16_SUBLANES, vt_list[0].shape[1]), BF16)
    m_old = [m_ref[g] for g in range(n)]
    m_new = [jnp.maximum(m_old[g], jnp.max(s_list[g], axis=0, keepdims=True)) for g in range(n)]
    p_list = [jnp.exp2(s_list[g] - m_new[g]).astype(BF16) for g in range(n)]
    alpha = [jnp.exp2(m_old[g] - m_new[g]) for g in range(n)]
    pv = [_dot(jnp.concatenate([vt_list[g], ones], axis=0), p_list[g]) for g in range(n)]
    for g in range(n):
        m_ref[g] = m_new[g]
        l_ref[g] = alpha[g] * l_ref[g] + pv[g][dv:dv + 1]
        acc_ref[g] = alpha[g] * acc_ref[g] + pv[g][:dv]


def _init_softmax_state(m_ref, l_ref, acc_ref):
    m_ref[...] = jnp.full(m_ref.shape, MASK_VALUE, F32)
    l_ref[...] = jnp.zeros(l_ref.shape, F32)
    acc_ref[...] = jnp.zeros(acc_ref.shape, F32)


def _bounded_update(s_list, vt_list, l_ref, acc_ref):
    n = len(s_list)
    dv = vt_list[0].shape[0]
    ones = jnp.ones((BF16_SUBLANES, vt_list[0].shape[1]), BF16)
    p_list = [jnp.exp2(s_list[g]).astype(BF16) for g in range(n)]
    pv = [_dot(jnp.concatenate([vt_list[g], ones], axis=0), p_list[g]) for g in range(n)]
    for g in range(n):
        l_ref[g] = l_ref[g] + pv[g][dv:dv + 1]
        acc_ref[g] = acc_ref[g] + pv[g][:dv]


def _head_selector(n_lanes):
    r = lax.broadcasted_iota(jnp.int32, (BF16_SUBLANES, n_lanes), 0)
    l = lax.broadcasted_iota(jnp.int32, (BF16_SUBLANES, n_lanes), 1)
    return jnp.where(l // HEAD_DIM == r, 1.0, 0.0).astype(BF16)


def _head_sq_norms(x, head_sel):
    xf = x.astype(F32)
    return _dot_nt(head_sel, (xf * xf).astype(BF16))


def _max_key_sq_norms(k_ref, kmax_ref, head_sel, blk):
    def body(c, mx):
        start = pl.multiple_of(c * blk, blk)
        n2 = _head_sq_norms(k_ref[0, pl.ds(start, blk), :], head_sel)
        return jnp.maximum(mx, jnp.max(n2, axis=1, keepdims=True))
    kmax = lax.fori_loop(0, k_ref.shape[1] // blk, body, jnp.zeros((BF16_SUBLANES, 1), F32))
    kmax_ref[...] = jnp.broadcast_to(kmax, kmax_ref.shape)


def _softmax_underflowed(l_ref):
    return jnp.logical_not(jnp.min(l_ref[...]) >= MIN_SAFE_NORMALISER)


def _mixer_kernel(iq_ref, ik_ref, iwt_ref, aq_ref, ak_ref, avt_ref, lam_ref, dq_ref, dk_ref, dvt_ref, sw_ref,
                  oa_ref, ob_ref,
                  score_ref, m_ref, l_ref, acc_ref, kmax_ref, dm_ref, dl_ref, dacc_ref, dkmax_ref, bis_ref,
                  *, blk, top_k, lam_init):
    i = pl.program_id(1)
    n_chunks = i + 1
    n_maps = 2 * N_DIFF_HEADS
    lane = lax.broadcasted_iota(jnp.int32, (1, LANES), 1)
    lane_lo = lane < HEAD_DIM
    qcol = lax.broadcasted_iota(jnp.int32, (1, blk), 1)
    qpos = i * blk + qcol
    krow = lax.broadcasted_iota(jnp.int32, (blk, 1), 0)
    kf = float(top_k)

    aq_all = aq_ref[0]
    dq_all = dq_ref[0]
    head_sel = _head_selector(aq_all.shape[1])

    @pl.when(i == 0)
    def _():
        _max_key_sq_norms(ak_ref, kmax_ref, head_sel, blk)
        _max_key_sq_norms(dk_ref, dkmax_ref, head_sel, blk)

    d_maps = [_head_q(dq_all, g, lane_lo) for g in range(n_maps)]
    d_bound = jnp.sqrt(_head_sq_norms(dq_all, head_sel) * dkmax_ref[:, :1])
    d_shifts = [d_bound[g:g + 1] for g in range(n_maps)]

    def diff_chunk(c, diagonal, bounded, after=None):
        start = pl.multiple_of(c * blk, blk)
        if diagonal:
            bias = jnp.where(krow <= qcol, 0.0, MASK_VALUE)
        s_list, vt_list = [], []
        for h in range(N_DIFF_HEADS):
            cols = slice(h * LANES, (h + 1) * LANES)
            kk = dk_ref[0, pl.ds(start, blk), cols]
            for comp in range(2):
                g = 2 * h + comp
                s = _dot_nt(kk, d_maps[g])
                if bounded:
                    s = s - (d_shifts[g] if after is None else d_shifts[g] + after)
                s_list.append(s + bias if diagonal else s)
                vt_list.append(dvt_ref[0, c, cols, :])
        if bounded:
            _bounded_update(s_list, vt_list, dl_ref, dacc_ref)
        else:
            _softmax_update(s_list, vt_list, dm_ref, dl_ref, dacc_ref)

    _init_softmax_state(dm_ref, dl_ref, dacc_ref)

    iq_all = iq_ref[0]
    wt = iwt_ref[0]
    q_heads = [_head_q(iq_all, h, lane_lo) for h in range(N_IDX_HEADS)]
    w_rows = [wt[h:h + 1, :] for h in range(N_IDX_HEADS)]

    n_pairs = (n_chunks + 1) // 2

    def score_body(j, carry):
        mn, mx = carry
        for u in range(2):
            c = 2 * j + u
            start = pl.multiple_of(c * blk, blk)
            kk = ik_ref[0, pl.ds(start, blk), :]
            acc = jnp.zeros((blk, blk), F32)
            for h in range(N_IDX_HEADS):
                acc = acc + jnp.maximum(_dot_nt(kk, q_heads[h]), 0.0) * w_rows[h]
            causal = (c * blk + krow) <= qpos
            sc = jnp.where(causal, acc, -jnp.inf)
            score_ref[c] = sc
            mx = jnp.maximum(mx, jnp.max(sc, axis=0, keepdims=True))
            mn = jnp.minimum(mn, jnp.min(jnp.where(causal, acc, jnp.inf), axis=0, keepdims=True))
        return mn, mx

    mn, mx = lax.fori_loop(0, n_pairs, score_body,
                           (jnp.full((1, blk), jnp.inf, F32), jnp.full((1, blk), -jnp.inf, F32)))

    def pair_sum(hit):
        m = jnp.where(hit, 1.0, 0.0)
        return jnp.sum((m[0] + m[1]).reshape(blk // F32_SUBLANES, F32_SUBLANES, blk), axis=0)

    zero8 = jnp.zeros((F32_SUBLANES, blk), F32)

    def count_ge(cand):
        def body(j, acc):
            return acc + pair_sum(score_ref[pl.ds(2 * j, 2)] >= cand)
        return jnp.sum(lax.fori_loop(0, n_pairs, body, zero8), axis=0, keepdims=True)

    def count_ge_gt(cand):
        def body(j, carry):
            ge, gt = carry
            s = score_ref[pl.ds(2 * j, 2)]
            return ge + pair_sum(s >= cand), gt + pair_sum(s > cand)
        ge, gt = lax.fori_loop(0, n_pairs, body, (zero8, zero8))
        return jnp.sum(ge, axis=0, keepdims=True), jnp.sum(gt, axis=0, keepdims=True)

    n_valid = (qpos + 1).astype(F32)
    need = n_valid > kf
    ge0, gt0 = count_ge_gt(jnp.zeros((1, blk), F32))
    kth_is_zero = jnp.logical_and(gt0 < kf, ge0 >= kf)
    nonneg = ge0 >= kf
    lo0 = jnp.where(nonneg, 0.0, mn)
    cnt0 = jnp.where(nonneg, ge0, n_valid)
    above_max = jnp.where(mx > 0.0, jnp.minimum(mx * 2.0, F32_MAX), 1.0)
    hi0 = jnp.where(nonneg, above_max, 0.0)
    done0 = jnp.logical_or(jnp.logical_not(need), jnp.logical_or(kth_is_zero, cnt0 == kf))
    active0 = jnp.where(done0, 0.0, 1.0)

    def midpoint(st):
        return 0.5 * st[0] + 0.5 * st[1]

    def halve(st, count):
        mid = midpoint(st)
        return narrow(st, mid, count(mid))

    def narrow(st, mid, cnt):
        lo, hi, cnt_lo, active = st
        conv = jnp.logical_or(mid <= lo, mid >= hi)
        upd = jnp.logical_and(active > 0.0, jnp.logical_not(conv))
        ge = cnt >= kf
        up_lo = jnp.logical_and(upd, ge)
        up_hi = jnp.logical_and(upd, jnp.logical_not(ge))
        lo = jnp.where(up_lo, mid, lo)
        cnt_lo = jnp.where(up_lo, cnt, cnt_lo)
        hi = jnp.where(up_hi, mid, hi)
        finished = jnp.logical_or(conv, jnp.logical_and(up_lo, cnt == kf))
        return lo, hi, cnt_lo, jnp.where(finished, 0.0, active)

    n_fused = jnp.minimum(i, BISECT_FUSED_STEPS)
    for k, v in enumerate((lo0, hi0, cnt0, active0)):
        bis_ref[k] = v

    for pairs in range(1, score_ref.shape[0] // 2 + 1):
        @pl.when(n_pairs == pairs)
        def _(pairs=pairs):
            def fused_step(c, st):
                mid = midpoint(st)
                acc = zero8
                for j in range(pairs):
                    acc = acc + pair_sum(score_ref[2 * j:2 * j + 2] >= mid)
                cnt = jnp.sum(acc, axis=0, keepdims=True)
                diff_chunk(c, False, True, after=cnt * 0.0)
                return narrow(st, mid, cnt)
            st = lax.fori_loop(0, n_fused, fused_step, tuple(bis_ref[k] for k in range(4)))
            for k in range(4):
                bis_ref[k] = st[k]

    st = tuple(bis_ref[k] for k in range(4))
    st = lax.fori_loop(0, jnp.maximum(BISECT_FIRST_STEPS - n_fused, 0),
                       lambda _, s: halve(s, count_ge), st)

    def bis_cond(st):
        return jnp.logical_and(st[4] > 0.0, st[5] < MAX_BISECT_STEPS)

    def bis_body(st):
        lo, hi, cnt_lo, active = lax.fori_loop(0, BISECT_STEPS_PER_TEST, lambda _, s: halve(s, count_ge), st[:4])
        return lo, hi, cnt_lo, active, jnp.max(active), st[5] + BISECT_STEPS_PER_TEST

    lo, _, cnt_lo, _, _, _ = lax.while_loop(bis_cond, bis_body, st + (jnp.max(st[3]), jnp.int32(0)))
    thr = jnp.where(need, lo, -F32_MAX)

    excess = jnp.logical_and(need, cnt_lo > kf)

    @pl.when(jnp.max(jnp.where(excess, 1.0, 0.0)) > 0.0)
    def _():
        _, gt = count_ge_gt(thr)
        allow = kf - gt
        r_i = lax.broadcasted_iota(jnp.int32, (blk, blk), 0)
        c_i = lax.broadcasted_iota(jnp.int32, (blk, blk), 1)
        earlier = jnp.where(c_i < r_i, 1.0, 0.0).astype(BF16)

        def tie_body(c, seen):
            s = score_ref[c]
            eq = jnp.logical_and(s == thr, excess)
            eqf = jnp.where(eq, 1.0, 0.0)
            rank = seen + _dot(earlier, eqf.astype(BF16))
            drop = jnp.logical_and(eq, rank >= allow)
            score_ref[c] = jnp.where(drop, -jnp.inf, s)
            return seen + jnp.sum(eqf, axis=0, keepdims=True)

        lax.fori_loop(0, n_chunks, tie_body, jnp.zeros((1, blk), F32))

    qa_heads = [_head_q(aq_all, h, lane_lo) for h in range(N_DSA_HEADS)]

    def attend(bounded, shift):
        def att_body(c, carry):
            start = pl.multiple_of(c * blk, blk)
            bias = jnp.where(score_ref[c] >= thr, shift, MASK_VALUE)
            s_list, vt_list = [], []
            for h in range(N_DSA_HEADS):
                pair = slice((h // 2) * LANES, (h // 2 + 1) * LANES)
                kk = ak_ref[0, pl.ds(start, blk), pair]
                s_list.append(_dot_nt(kk, qa_heads[h]) + bias)
                vt_list.append(avt_ref[0, c, pair, :])
            if bounded:
                _bounded_update(s_list, vt_list, l_ref, acc_ref)
            else:
                _softmax_update(s_list, vt_list, m_ref, l_ref, acc_ref)
            return carry

        _init_softmax_state(m_ref, l_ref, acc_ref)
        lax.fori_loop(0, n_chunks, att_body, 0)

    bound = jnp.sqrt(_head_sq_norms(aq_all, head_sel) * kmax_ref[:, :1])
    attend(True, -jnp.max(bound, axis=0, keepdims=True))

    @pl.when(_softmax_underflowed(l_ref))
    def _():
        attend(False, 0.0)

    for j in range(N_DSA_HEADS // 2):
        a = acc_ref[2 * j] / l_ref[2 * j]
        b = acc_ref[2 * j + 1] / l_ref[2 * j + 1]
        o_t = jnp.concatenate([a[:HEAD_DIM], b[HEAD_DIM:]], axis=0)
        oa_ref[0, :, j * LANES:(j + 1) * LANES] = o_t.T.astype(oa_ref.dtype)

    def diff_rest(c, carry):
        diff_chunk(c, False, True)
        return carry

    lax.fori_loop(n_fused, i, diff_rest, 0)
    diff_chunk(i, True, True)

    @pl.when(_softmax_underflowed(dl_ref))
    def _():
        def body(c, carry):
            diff_chunk(c, False, False)
            return carry
        _init_softmax_state(dm_ref, dl_ref, dacc_ref)
        lax.fori_loop(0, i, body, 0)
        diff_chunk(i, True, False)

    lam_vecs = lam_ref[...]
    lam = (jnp.exp(jnp.sum(lam_vecs[0:1] * lam_vecs[1:2], axis=-1, keepdims=True))
           - jnp.exp(jnp.sum(lam_vecs[2:3] * lam_vecs[3:4], axis=-1, keepdims=True)) + lam_init)
    for h in range(N_DIFF_HEADS):
        out = dacc_ref[2 * h] / dl_ref[2 * h] - lam * (dacc_ref[2 * h + 1] / dl_ref[2 * h + 1])
        out = out * lax.rsqrt(jnp.mean(out * out, axis=0, keepdims=True) + NORM_EPS)
        out = out * sw_ref[...] * (1.0 - lam_init)
        ob_ref[0, :, h * LANES:(h + 1) * LANES] = out.T.astype(ob_ref.dtype)


def _mixers(iq, ik2, iwt, aq, ak, avt, lam_vecs, dq, dk, dvt, subln_col, *, blk, lam_init):
    bsz, seq, _ = aq.shape
    top_k = min(TOPK_MAX, seq // 4)
    n_chunks = seq // blk
    assert n_chunks % 2 == 0, "key chunks are walked in pairs"
    qblk = lambda b, i: (b, i, 0)
    whole = lambda b, i: (b, 0, 0)
    whole_t = lambda b, i: (b, 0, 0, 0)
    n_maps = 2 * N_DIFF_HEADS
    q_spec = pl.BlockSpec((1, blk, 512), qblk)
    kv_spec = pl.BlockSpec((1, seq, 512), whole)
    vt_spec = pl.BlockSpec((1, n_chunks, 512, blk), whole_t)
    out = jax.ShapeDtypeStruct((bsz, seq, 512), BF16)
    return pl.pallas_call(
        functools.partial(_mixer_kernel, blk=blk, top_k=top_k, lam_init=lam_init),
        grid=(bsz, n_chunks),
        in_specs=[q_spec,
                  pl.BlockSpec((1, seq, LANES), whole),
                  pl.BlockSpec((1, N_IDX_HEADS, blk), lambda b, i: (b, 0, i)),
                  q_spec, kv_spec, vt_spec,
                  _const_spec(lam_vecs.shape),
                  q_spec, kv_spec, vt_spec,
                  _const_spec((DIFF_V_DIM, 1))],
        out_specs=[q_spec, q_spec],
        out_shape=[out, out],
        scratch_shapes=[pltpu.VMEM((n_chunks, blk, blk), F32),
                        pltpu.VMEM((N_DSA_HEADS, 1, blk), F32),
                        pltpu.VMEM((N_DSA_HEADS, 1, blk), F32),
                        pltpu.VMEM((N_DSA_HEADS, LANES, blk), F32),
                        pltpu.VMEM((BF16_SUBLANES, LANES), F32),
                        pltpu.VMEM((n_maps, 1, blk), F32),
                        pltpu.VMEM((n_maps, 1, blk), F32),
                        pltpu.VMEM((n_maps, DIFF_V_DIM, blk), F32),
                        pltpu.VMEM((BF16_SUBLANES, LANES), F32),
                        pltpu.VMEM((4, 1, blk), F32)],
        compiler_params=pltpu.CompilerParams(
            dimension_semantics=("parallel", "arbitrary"), vmem_limit_bytes=VMEM_LIMIT_BYTES),
        name="mixers",
    )(iq, ik2, iwt, aq, ak, avt, lam_vecs, dq, dk, dvt, subln_col)


def _out_mlp_kernel(x_ref, ma_ref, mb_ref, g1_ref, sh_ref, sc_ref, g2_ref, n2_ref, nf_ref,
                    woa_ref, wob_ref, w1_ref, w2_ref, o_ref, *, ff_chunk):
    x = x_ref[0]
    o = _dot(ma_ref[0], woa_ref[...]) + _dot(mb_ref[0], wob_ref[...])
    x1 = x + g1_ref[0] * o
    ms = jnp.mean(x1 * x1, axis=-1, keepdims=True)
    h = x1 * lax.rsqrt(ms + NORM_EPS) * n2_ref[...]
    hb = (h * (1.0 + sc_ref[0]) + sh_ref[0]).astype(BF16)
    d_ff = w1_ref.shape[1]
    ff = jnp.zeros_like(x)
    for j in range(d_ff // ff_chunk):
        u = jnp.maximum(_dot(hb, w1_ref[:, j * ff_chunk:(j + 1) * ff_chunk]), 0.0)
        ff = ff + _dot((u * u).astype(BF16), w2_ref[j * ff_chunk:(j + 1) * ff_chunk, :])
    x2 = x1 + g2_ref[0] * ff
    ms2 = jnp.mean(x2 * x2, axis=-1, keepdims=True)
    o_ref[0] = x2 * lax.rsqrt(ms2 + NORM_EPS) * nf_ref[...]


def _out_mlp(x, mix_a, mix_b, g1, sh2, sc2, g2, n2, nf, wo_a, wo_b, w1, w2, *, tile):
    bsz, seq, d = x.shape
    tok = lambda b, i: (b, i, 0)
    per_b = lambda b, i: (b, 0, 0)
    mod_spec = pl.BlockSpec((1, 1, d), per_b)
    return pl.pallas_call(
        functools.partial(_out_mlp_kernel, ff_chunk=1024),
        grid=(bsz, seq // tile),
        in_specs=[pl.BlockSpec((1, tile, d), tok),
                  pl.BlockSpec((1, tile, 512), tok),
                  pl.BlockSpec((1, tile, 512), tok),
                  mod_spec, mod_spec, mod_spec, mod_spec,
                  _const_spec((1, d)), _const_spec((1, d)),
                  _const_spec(wo_a.shape), _const_spec(wo_b.shape),
                  _const_spec(w1.shape), _const_spec(w2.shape)],
        out_specs=pl.BlockSpec((1, tile, d), tok),
        out_shape=jax.ShapeDtypeStruct((bsz, seq, d), F32),
        compiler_params=pltpu.CompilerParams(
            dimension_semantics=("parallel", "parallel"), vmem_limit_bytes=VMEM_LIMIT_BYTES),
        name="out_mlp",
    )(x, mix_a, mix_b, g1, sh2, sc2, g2, n2, nf, wo_a, wo_b, w1, w2)


def _pad_cols(w, width):
    return jnp.pad(w, ((0, 0), (0, width - w.shape[1])))


def kernel(x, c, positions, w_ada, b_ada, norm1_w, w_in, idx_k_ln_w, idx_k_ln_b, lambda_q1, lambda_k1,
           lambda_q2, lambda_k2, subln_w, w_out, norm2_w, w_ff1, w_ff2, norm_f_w):
    bsz, seq, d = x.shape
    depth = w_ada.shape[0]
    assert depth == 1, "the fused final RMSNorm assumes a single layer"
    half = HEAD_DIM // 2
    inv_freq = ROPE_THETA ** (-jnp.arange(half, dtype=F32) / half)
    invf = jnp.tile(inv_freq, LANES // half).reshape(1, LANES)
    pos3 = positions.reshape(bsz, seq, 1)
    blk = min(ATT_BLOCK, seq)
    tile = min(512, seq)

    for l in range(depth):
        mod = _adaln(c, w_ada[l], b_ada[l])
        sh1, sc1, g1, sh2, sc2, g2 = [m.reshape(bsz, 1, d) for m in jnp.split(mod, 6, axis=-1)]

        w = w_in[l]
        wm = jnp.concatenate([w[:, :1024], w[:, 1536:2048], w[:, 2120:3144]], axis=1).astype(BF16)
        ws = _pad_cols(w[:, 2048:2112], LANES).astype(BF16)
        wwt = jnp.pad(w[:, 2112:2120].T, ((0, BF16_SUBLANES - N_IDX_HEADS), (0, 0))).astype(BF16)
        wavt = w[:, 1024:1536].T.astype(BF16)
        wdvt = w[:, 3144:3656].T.astype(BF16)
        lnw = _pad_cols(idx_k_ln_w[l].reshape(1, IDX_DIM), LANES)
        lnb = _pad_cols(idx_k_ln_b[l].reshape(1, IDX_DIM), LANES)

        aq, ak, avt, iq, ik2, iwt, dq, dk, dvt = _in_proj(
            x, pos3, sh1, sc1, norm1_w[l].reshape(1, d), invf, wm, ws, wwt, wavt, wdvt, lnw, lnb,
            tile=tile, chunk=blk)

        lam_vecs = jnp.stack([lambda_q1[l], lambda_k1[l], lambda_q2[l], lambda_k2[l]]).astype(F32)
        lam_init = 0.8 - 0.6 * math.exp(-0.3 * l)
        out_a, out_b = _mixers(iq, ik2, iwt, aq, ak, avt, lam_vecs, dq, dk, dvt,
                               subln_w[l].reshape(DIFF_V_DIM, 1), blk=blk, lam_init=lam_init)

        wo = w_out[l].astype(BF16)
        x = _out_mlp(x, out_a, out_b, g1, sh2, sc2, g2, norm2_w[l].reshape(1, d), norm_f_w.reshape(1, d),
                     wo[:512], wo[512:], w_ff1[l].astype(BF16), w_ff2[l].astype(BF16), tile=tile)
    return x
```

```python
import functools
import math

import jax
import jax.numpy as jnp
from jax import lax
from jax.experimental import pallas as pl
from jax.experimental.pallas import tpu as pltpu

F32 = jnp.float32
BF16 = jnp.bfloat16

HEAD_DIM = 64
N_DSA_HEADS = 8
N_IDX_HEADS = 8
IDX_DIM = 64
N_DIFF_HEADS = 4
DIFF_V_DIM = 128
TOPK_MAX = 256
ROPE_THETA = 10000.0
NORM_EPS = 1e-6
LN_EPS = 1e-5
LANES = 128
F32_SUBLANES = 8
BF16_SUBLANES = 16
ATT_BLOCK = 256
MASK_VALUE = -1e30
F32_MAX = float(jnp.finfo(jnp.float32).max)
MAX_BISECT_STEPS = 512
BISECT_FIRST_STEPS = 16
BISECT_FUSED_STEPS = 16
BISECT_STEPS_PER_TEST = 2
MIN_SAFE_NORMALISER = 2.0 ** -100
VMEM_LIMIT_BYTES = 56 * 1024 * 1024


def _dot(a, b):
    return jnp.dot(a, b, preferred_element_type=F32)


def _dot_nt(a, b):
    return lax.dot_general(a, b, (((1,), (1,)), ((), ())), preferred_element_type=F32)


def _const_spec(shape):
    zeros = (0,) * len(shape)
    return pl.BlockSpec(shape, lambda *_: zeros, pipeline_mode=pl.Buffered(1))


def _adaln_kernel(c_ref, w_ref, b_ref, o_ref):
    c = c_ref[...]
    s = c / (1.0 + jnp.exp(-c))
    o_ref[...] = jnp.dot(s, w_ref[...], preferred_element_type=F32,
                         precision=lax.Precision.HIGHEST) + b_ref[...]


def _adaln(c, w, b):
    bsz, d = c.shape
    n = w.shape[1]
    tn = 1536
    return pl.pallas_call(
        _adaln_kernel,
        grid=(n // tn,),
        in_specs=[pl.BlockSpec((bsz, d), lambda j: (0, 0)),
                  pl.BlockSpec((d, tn), lambda j: (0, j)),
                  pl.BlockSpec((1, tn), lambda j: (0, j))],
        out_specs=pl.BlockSpec((bsz, tn), lambda j: (0, j)),
        out_shape=jax.ShapeDtypeStruct((bsz, n), F32),
        compiler_params=pltpu.CompilerParams(vmem_limit_bytes=VMEM_LIMIT_BYTES),
        name="adaln",
    )(c, w, b.reshape(1, n))


def _rope_group(x, cos, sin_signed, first_half):
    nxt = pltpu.roll(x, LANES - HEAD_DIM // 2, axis=1)
    prv = pltpu.roll(x, HEAD_DIM // 2, axis=1)
    return x * cos + jnp.where(first_half, nxt, prv) * sin_signed


def _in_proj_kernel(x_ref, pos_ref, sh_ref, sc_ref, nw_ref, invf_ref, wm_ref, ws_ref, wwt_ref,
                    wavt_ref, wdvt_ref, lnw_ref, lnb_ref,
                    aq_ref, ak_ref, avt_ref, iq_ref, ik_ref, iwt_ref, dq_ref, dk_ref, dvt_ref):
    x = x_ref[0]
    ms = jnp.mean(x * x, axis=-1, keepdims=True)
    h = x * lax.rsqrt(ms + NORM_EPS) * nw_ref[...]
    h = h * (1.0 + sc_ref[0]) + sh_ref[0]
    hb = h.astype(BF16)

    lane = lax.broadcasted_iota(jnp.int32, (1, LANES), 1)
    first_half = (lane % HEAD_DIM) < (HEAD_DIM // 2)
    ang = pos_ref[0].astype(F32) * invf_ref[...]
    cos = jnp.cos(ang)
    sin = jnp.sin(ang)
    sin_signed = jnp.where(first_half, -sin, sin)

    def roped(col0, out_ref, scale):
        wide = _dot(hb, wm_ref[:, col0:col0 + 4 * LANES])
        for g in range(4):
            y = _rope_group(wide[:, g * LANES:(g + 1) * LANES], cos, sin_signed, first_half)
            if scale != 1.0:
                y = y * scale
            out_ref[0, :, g * LANES:(g + 1) * LANES] = y.astype(out_ref.dtype)

    qscale = HEAD_DIM ** -0.5 * math.log2(math.e)
    roped(0, aq_ref, qscale)
    roped(512, ak_ref, 1.0)
    roped(1024, iq_ref, 1.0)
    roped(1536, dq_ref, qscale)
    roped(2048, dk_ref, 1.0)

    n_sub = avt_ref.shape[1]
    chunk = avt_ref.shape[3]
    for j in range(n_sub):
        hj = hb[j * chunk:(j + 1) * chunk]
        avt_ref[0, j] = _dot_nt(wavt_ref[...], hj).astype(avt_ref.dtype)
        dvt_ref[0, j] = _dot_nt(wdvt_ref[...], hj).astype(dvt_ref.dtype)

    sm = _dot(hb, ws_ref[...])
    lo_half = lane < IDX_DIM
    mu = jnp.sum(sm, axis=-1, keepdims=True) * (1.0 / IDX_DIM)
    d = jnp.where(lo_half, sm - mu, 0.0)
    var = jnp.sum(d * d, axis=-1, keepdims=True) * (1.0 / IDX_DIM)
    y = d * lax.rsqrt(var + LN_EPS) * lnw_ref[...] + lnb_ref[...]
    y = y + pltpu.roll(y, IDX_DIM, axis=1)
    ik_ref[0] = _rope_group(y, cos, sin_signed, first_half).astype(ik_ref.dtype)

    wt = _dot_nt(wwt_ref[...], hb)
    iwt_ref[0] = wt[:N_IDX_HEADS] * ((N_IDX_HEADS ** -0.5) * (IDX_DIM ** -0.5))


def _in_proj(x, pos3, sh1, sc1, nw, invf, wm, ws, wwt, wavt, wdvt, lnw, lnb, *, tile, chunk):
    bsz, seq, d = x.shape
    tok = lambda b, i: (b, i, 0)
    per_b = lambda b, i: (b, 0, 0)
    wide = jax.ShapeDtypeStruct((bsz, seq, 512), BF16)
    wide_t = jax.ShapeDtypeStruct((bsz, seq // chunk, 512, chunk), BF16)
    wide_spec = pl.BlockSpec((1, tile, 512), tok)
    wide_t_spec = pl.BlockSpec((1, tile // chunk, 512, chunk), lambda b, i: (b, i, 0, 0))
    out_shape = [wide, wide, wide_t, wide,
                 jax.ShapeDtypeStruct((bsz, seq, LANES), BF16),
                 jax.ShapeDtypeStruct((bsz, N_IDX_HEADS, seq), F32),
                 wide, wide, wide_t]
    out_specs = [wide_spec, wide_spec, wide_t_spec, wide_spec,
                 pl.BlockSpec((1, tile, LANES), tok),
                 pl.BlockSpec((1, N_IDX_HEADS, tile), lambda b, i: (b, 0, i)),
                 wide_spec, wide_spec, wide_t_spec]
    return pl.pallas_call(
        _in_proj_kernel,
        grid=(bsz, seq // tile),
        in_specs=[pl.BlockSpec((1, tile, d), tok),
                  pl.BlockSpec((1, tile, 1), tok),
                  pl.BlockSpec((1, 1, d), per_b),
                  pl.BlockSpec((1, 1, d), per_b),
                  _const_spec((1, d)),
                  _const_spec((1, LANES)),
                  _const_spec(wm.shape),
                  _const_spec(ws.shape),
                  _const_spec(wwt.shape),
                  _const_spec(wavt.shape),
                  _const_spec(wdvt.shape),
                  _const_spec((1, LANES)),
                  _const_spec((1, LANES))],
        out_specs=out_specs,
        out_shape=out_shape,
        compiler_params=pltpu.CompilerParams(
            dimension_semantics=("parallel", "parallel"), vmem_limit_bytes=VMEM_LIMIT_BYTES),
        name="in_proj",
    )(x, pos3, sh1, sc1, nw, invf, wm, ws, wwt, wavt, wdvt, lnw, lnb)


def _head_q(q_all, h, lane_lo):
    pair = q_all[:, (h // 2) * LANES:(h // 2 + 1) * LANES]
    keep = lane_lo if h % 2 == 0 else jnp.logical_not(lane_lo)
    return jnp.where(keep, pair, jnp.zeros_like(pair))


def _softmax_update(s_list, vt_list, m_ref, l_ref, acc_ref):
    n = len(s_list)
    dv = vt_list[0].shape[0]
    ones = jnp.ones((BF16_SUBLANES, vt_list[0].shape[1]), BF16)
    m_old = [m_ref[g] for g in range(n)]
    m_new = [jnp.maximum(m_old[g], jnp.max(s_list[g], axis=0, keepdims=True)) for g in range(n)]
    p_list = [jnp.exp2(s_list[g] - m_new[g]).astype(BF16) for g in range(n)]
    alpha = [jnp.exp2(m_old[g] - m_new[g]) for g in range(n)]
    pv = [_dot(jnp.concatenate([vt_list[g], ones], axis=0), p_list[g]) for g in range(n)]
    for g in range(n):
        m_ref[g] = m_new[g]
        l_ref[g] = alpha[g] * l_ref[g] + pv[g][dv:dv + 1]
        acc_ref[g] = alpha[g] * acc_ref[g] + pv[g][:dv]


def _init_softmax_state(m_ref, l_ref, acc_ref):
    m_ref[...] = jnp.full(m_ref.shape, MASK_VALUE, F32)
    l_ref[...] = jnp.zeros(l_ref.shape, F32)
    acc_ref[...] = jnp.zeros(acc_ref.shape, F32)


def _bounded_update(s_list, vt_list, l_ref, acc_ref, before_last=None):
    n = len(s_list)
    dv = vt_list[0].shape[0]
    ones = jnp.ones((BF16_SUBLANES, vt_list[0].shape[1]), BF16)
    p_list = [jnp.exp2(s_list[g]).astype(BF16) for g in range(n)]
    pv = [_dot(jnp.concatenate([vt_list[g], ones], axis=0), p_list[g]) for g in range(n)]
    for g in range(n):
        col_sum = pv[g][dv:dv + 1]
        if before_last is not None and g == n - 1:
            col_sum = col_sum + before_last
        l_ref[g] = l_ref[g] + col_sum
        acc_ref[g] = acc_ref[g] + pv[g][:dv]


def _head_selector(n_lanes):
    r = lax.broadcasted_iota(jnp.int32, (BF16_SUBLANES, n_lanes), 0)
    l = lax.broadcasted_iota(jnp.int32, (BF16_SUBLANES, n_lanes), 1)
    return jnp.where(l // HEAD_DIM == r, 1.0, 0.0).astype(BF16)


def _head_sq_norms(x, head_sel):
    return _dot_nt(head_sel, x * x)


def _max_key_sq_norms(k_ref, kmax_ref, rows_per_step):
    n_lanes = k_ref.shape[2]
    l = lax.broadcasted_iota(jnp.int32, (n_lanes, LANES), 0)
    h = lax.broadcasted_iota(jnp.int32, (n_lanes, LANES), 1)
    sel_t = jnp.where(l // HEAD_DIM == h, 1.0, 0.0).astype(BF16)

    def body(c, mx):
        start = pl.multiple_of(c * rows_per_step, rows_per_step)
        kk = k_ref[0, pl.ds(start, rows_per_step), :]
        return jnp.maximum(mx, jnp.max(_dot(kk * kk, sel_t), axis=0, keepdims=True))
    row = lax.fori_loop(0, k_ref.shape[1] // rows_per_step, body, jnp.zeros((1, LANES), F32))
    r = lax.broadcasted_iota(jnp.int32, kmax_ref.shape, 0)
    c = lax.broadcasted_iota(jnp.int32, kmax_ref.shape, 1)
    col = jnp.sum(jnp.where(r == c, row, 0.0), axis=1, keepdims=True)
    kmax_ref[...] = jnp.broadcast_to(col, kmax_ref.shape)


def _softmax_underflowed(l_ref):
    return jnp.logical_not(jnp.min(l_ref[...]) >= MIN_SAFE_NORMALISER)


def _mixer_kernel(iq_ref, ik_ref, iwt_ref, aq_ref, ak_ref, avt_ref, lam_ref, dq_ref, dk_ref, dvt_ref, sw_ref,
                  oa_ref, ob_ref,
                  score_ref, m_ref, l_ref, acc_ref, kmax_ref, dm_ref, dl_ref, dacc_ref, dkmax_ref, bis_ref,
                  *, blk, top_k, lam_init):
    i = pl.program_id(1)
    n_chunks = i + 1
    n_maps = 2 * N_DIFF_HEADS
    lane = lax.broadcasted_iota(jnp.int32, (1, LANES), 1)
    lane_lo = lane < HEAD_DIM
    qcol = lax.broadcasted_iota(jnp.int32, (1, blk), 1)
    qpos = i * blk + qcol
    krow = lax.broadcasted_iota(jnp.int32, (blk, 1), 0)
    kf = float(top_k)

    aq_all = aq_ref[0]
    dq_all = dq_ref[0]
    head_sel = _head_selector(aq_all.shape[1])

    @pl.when(i == 0)
    def _():
        rows_per_step = math.gcd(4 * blk, ak_ref.shape[1])
        _max_key_sq_norms(ak_ref, kmax_ref, rows_per_step)
        _max_key_sq_norms(dk_ref, dkmax_ref, rows_per_step)

    d_maps = [_head_q(dq_all, g, lane_lo) for g in range(n_maps)]
    d_bound = jnp.sqrt(_head_sq_norms(dq_all, head_sel) * dkmax_ref[:, :1])
    d_shifts = [d_bound[g:g + 1] for g in range(n_maps)]

    def diff_chunk(c, diagonal, bounded, after=None, before_last=None):
        start = pl.multiple_of(c * blk, blk)
        if diagonal:
            bias = jnp.where(krow <= qcol, 0.0, MASK_VALUE)
        s_list, vt_list = [], []
        for h in range(N_DIFF_HEADS):
            cols = slice(h * LANES, (h + 1) * LANES)
            kk = dk_ref[0, pl.ds(start, blk), cols]
            for comp in range(2):
                g = 2 * h + comp
                s = _dot_nt(kk, d_maps[g])
                if bounded:
                    s = s - (d_shifts[g] if after is None else d_shifts[g] + after)
                s_list.append(s + bias if diagonal else s)
                vt_list.append(dvt_ref[0, c, cols, :])
        if bounded:
            _bounded_update(s_list, vt_list, dl_ref, dacc_ref, before_last)
        else:
            _softmax_update(s_list, vt_list, dm_ref, dl_ref, dacc_ref)

    _init_softmax_state(dm_ref, dl_ref, dacc_ref)

    iq_all = iq_ref[0]
    wt = iwt_ref[0]
    q_heads = [_head_q(iq_all, h, lane_lo) for h in range(N_IDX_HEADS)]
    w_rows = [wt[h:h + 1, :] for h in range(N_IDX_HEADS)]

    n_pairs = (n_chunks + 1) // 2

    def score_body(j, carry):
        mn, mx = carry
        for u in range(2):
            c = 2 * j + u
            start = pl.multiple_of(c * blk, blk)
            kk = ik_ref[0, pl.ds(start, blk), :]
            acc = jnp.zeros((blk, blk), F32)
            for h in range(N_IDX_HEADS):
                acc = acc + jnp.maximum(_dot_nt(kk, q_heads[h]), 0.0) * w_rows[h]
            causal = (c * blk + krow) <= qpos
            sc = jnp.where(causal, acc, -jnp.inf)
            score_ref[c] = sc
            mx = jnp.maximum(mx, jnp.max(sc, axis=0, keepdims=True))
            mn = jnp.minimum(mn, jnp.min(jnp.where(causal, acc, jnp.inf), axis=0, keepdims=True))
        return mn, mx

    mn, mx = lax.fori_loop(0, n_pairs, score_body,
                           (jnp.full((1, blk), jnp.inf, F32), jnp.full((1, blk), -jnp.inf, F32)))

    def pair_sum(hit):
        m = jnp.where(hit, 1.0, 0.0)
        return jnp.sum((m[0] + m[1]).reshape(blk // F32_SUBLANES, F32_SUBLANES, blk), axis=0)

    zero8 = jnp.zeros((F32_SUBLANES, blk), F32)

    def count_ge(cand):
        def body(j, acc):
            return acc + pair_sum(score_ref[pl.ds(2 * j, 2)] >= cand)
        return jnp.sum(lax.fori_loop(0, n_pairs, body, zero8), axis=0, keepdims=True)

    def count_ge_gt(cand):
        def body(j, carry):
            ge, gt = carry
            s = score_ref[pl.ds(2 * j, 2)]
            return ge + pair_sum(s >= cand), gt + pair_sum(s > cand)
        ge, gt = lax.fori_loop(0, n_pairs, body, (zero8, zero8))
        return jnp.sum(ge, axis=0, keepdims=True), jnp.sum(gt, axis=0, keepdims=True)

    n_valid = (qpos + 1).astype(F32)
    need = n_valid > kf
    ge0, gt0 = count_ge_gt(jnp.zeros((1, blk), F32))
    kth_is_zero = jnp.logical_and(gt0 < kf, ge0 >= kf)
    nonneg = ge0 >= kf
    lo0 = jnp.where(nonneg, 0.0, mn)
    cnt0 = jnp.where(nonneg, ge0, n_valid)
    above_max = jnp.where(mx > 0.0, jnp.minimum(mx * 2.0, F32_MAX), 1.0)
    hi0 = jnp.where(nonneg, above_max, 0.0)
    done0 = jnp.logical_or(jnp.logical_not(need), jnp.logical_or(kth_is_zero, cnt0 == kf))
    active0 = jnp.where(done0, 0.0, 1.0)

    def midpoint(st):
        return 0.5 * st[0] + 0.5 * st[1]

    def halve(st, count):
        mid = midpoint(st)
        return narrow(st, mid, count(mid))

    def narrow(st, mid, cnt):
        lo, hi, cnt_lo, active = st
        conv = jnp.logical_or(mid <= lo, mid >= hi)
        upd = jnp.logical_and(active > 0.0, jnp.logical_not(conv))
        ge = cnt >= kf
        up_lo = jnp.logical_and(upd, ge)
        up_hi = jnp.logical_and(upd, jnp.logical_not(ge))
        lo = jnp.where(up_lo, mid, lo)
        cnt_lo = jnp.where(up_lo, cnt, cnt_lo)
        hi = jnp.where(up_hi, mid, hi)
        finished = jnp.logical_or(conv, jnp.logical_and(up_lo, cnt == kf))
        return lo, hi, cnt_lo, jnp.where(finished, 0.0, active)

    n_fused = jnp.minimum(i, BISECT_FUSED_STEPS)
    for k, v in enumerate((lo0, hi0, cnt0, active0)):
        bis_ref[k] = v

    for pairs in range(1, score_ref.shape[0] // 2 + 1):
        @pl.when(n_pairs == pairs)
        def _(pairs=pairs):
            def fused_step(c, st):
                mid = midpoint(st)
                halves = []
                for js in (range(0, (pairs + 1) // 2), range((pairs + 1) // 2, pairs)):
                    acc = zero8
                    for j in js:
                        acc = acc + pair_sum(score_ref[2 * j:2 * j + 2] >= mid)
                    halves.append(jnp.sum(acc, axis=0, keepdims=True))
                diff_chunk(c, False, True, after=halves[0] * 0.0,
                           before_last=halves[1] * 0.0 if pairs > 1 else None)
                return narrow(st, mid, halves[0] + halves[1])
            st = lax.fori_loop(0, n_fused, fused_step, tuple(bis_ref[k] for k in range(4)))
            for k in range(4):
                bis_ref[k] = st[k]

    st = tuple(bis_ref[k] for k in range(4))
    st = lax.fori_loop(0, jnp.maximum(BISECT_FIRST_STEPS - n_fused, 0),
                       lambda _, s: halve(s, count_ge), st)

    def bis_cond(st):
        return jnp.logical_and(st[4] > 0.0, st[5] < MAX_BISECT_STEPS)

    def bis_body(st):
        lo, hi, cnt_lo, active = lax.fori_loop(0, BISECT_STEPS_PER_TEST, lambda _, s: halve(s, count_ge), st[:4])
        return lo, hi, cnt_lo, active, jnp.max(active), st[5] + BISECT_STEPS_PER_TEST

    lo, _, cnt_lo, _, _, _ = lax.while_loop(bis_cond, bis_body, st + (jnp.max(st[3]), jnp.int32(0)))
    thr = jnp.where(need, lo, -F32_MAX)

    excess = jnp.logical_and(need, cnt_lo > kf)

    @pl.when(jnp.max(jnp.where(excess, 1.0, 0.0)) > 0.0)
    def _():
        _, gt = count_ge_gt(thr)
        allow = kf - gt
        r_i = lax.broadcasted_iota(jnp.int32, (blk, blk), 0)
        c_i = lax.broadcasted_iota(jnp.int32, (blk, blk), 1)
        earlier = jnp.where(c_i < r_i, 1.0, 0.0).astype(BF16)

        def tie_body(c, seen):
            s = score_ref[c]
            eq = jnp.logical_and(s == thr, excess)
            eqf = jnp.where(eq, 1.0, 0.0)
            rank = seen + _dot(earlier, eqf.astype(BF16))
            drop = jnp.logical_and(eq, rank >= allow)
            score_ref[c] = jnp.where(drop, -jnp.inf, s)
            return seen + jnp.sum(eqf, axis=0, keepdims=True)

        lax.fori_loop(0, n_chunks, tie_body, jnp.zeros((1, blk), F32))

    qa_heads = [_head_q(aq_all, h, lane_lo) for h in range(N_DSA_HEADS)]

    def attend(bounded, shift):
        def att_body(c, carry):
            start = pl.multiple_of(c * blk, blk)
            bias = jnp.where(score_ref[c] >= thr, shift, MASK_VALUE)
            s_list, vt_list = [], []
            for h in range(N_DSA_HEADS):
                pair = slice((h // 2) * LANES, (h // 2 + 1) * LANES)
                kk = ak_ref[0, pl.ds(start, blk), pair]
                s_list.append(_dot_nt(kk, qa_heads[h]) + bias)
                vt_list.append(avt_ref[0, c, pair, :])
            if bounded:
                _bounded_update(s_list, vt_list, l_ref, acc_ref)
            else:
                _softmax_update(s_list, vt_list, m_ref, l_ref, acc_ref)
            return carry

        _init_softmax_state(m_ref, l_ref, acc_ref)
        lax.fori_loop(0, n_chunks, att_body, 0)

    bound = jnp.sqrt(_head_sq_norms(aq_all, head_sel) * kmax_ref[:, :1])
    attend(True, -jnp.max(bound, axis=0, keepdims=True))

    @pl.when(_softmax_underflowed(l_ref))
    def _():
        attend(False, 0.0)

    for j in range(N_DSA_HEADS // 2):
        a = acc_ref[2 * j] * (1.0 / l_ref[2 * j])
        b = acc_ref[2 * j + 1] * (1.0 / l_ref[2 * j + 1])
        o_t = jnp.concatenate([a[:HEAD_DIM], b[HEAD_DIM:]], axis=0)
        oa_ref[0, :, j * LANES:(j + 1) * LANES] = o_t.T.astype(oa_ref.dtype)

    def diff_rest(c, carry):
        diff_chunk(c, False, True)
        return carry

    lax.fori_loop(n_fused, i, diff_rest, 0)
    diff_chunk(i, True, True)

    @pl.when(_softmax_underflowed(dl_ref))
    def _():
        def body(c, carry):
            diff_chunk(c, False, False)
            return carry
        _init_softmax_state(dm_ref, dl_ref, dacc_ref)
        lax.fori_loop(0, i, body, 0)
        diff_chunk(i, True, False)

    lam_vecs = lam_ref[...]
    lam = (jnp.exp(jnp.sum(lam_vecs[0:1] * lam_vecs[1:2], axis=-1, keepdims=True))
           - jnp.exp(jnp.sum(lam_vecs[2:3] * lam_vecs[3:4], axis=-1, keepdims=True)) + lam_init)
    for h in range(N_DIFF_HEADS):
        out = dacc_ref[2 * h] * (1.0 / dl_ref[2 * h]) - dacc_ref[2 * h + 1] * (lam / dl_ref[2 * h + 1])
        out = out * lax.rsqrt(jnp.mean(out * out, axis=0, keepdims=True) + NORM_EPS)
        out = out * sw_ref[...] * (1.0 - lam_init)
        ob_ref[0, :, h * LANES:(h + 1) * LANES] = out.T.astype(ob_ref.dtype)


def _mixers(iq, ik2, iwt, aq, ak, avt, lam_vecs, dq, dk, dvt, subln_col, *, blk, lam_init):
    bsz, seq, _ = aq.shape
    top_k = min(TOPK_MAX, seq // 4)
    n_chunks = seq // blk
    assert n_chunks % 2 == 0, "key chunks are walked in pairs"
    qblk = lambda b, i: (b, i, 0)
    whole = lambda b, i: (b, 0, 0)
    whole_t = lambda b, i: (b, 0, 0, 0)
    n_maps = 2 * N_DIFF_HEADS
    q_spec = pl.BlockSpec((1, blk, 512), qblk)
    kv_spec = pl.BlockSpec((1, seq, 512), whole)
    vt_spec = pl.BlockSpec((1, n_chunks, 512, blk), whole_t)
    out = jax.ShapeDtypeStruct((bsz, seq, 512), BF16)
    return pl.pallas_call(
        functools.partial(_mixer_kernel, blk=blk, top_k=top_k, lam_init=lam_init),
        grid=(bsz, n_chunks),
        in_specs=[q_spec,
                  pl.BlockSpec((1, seq, LANES), whole),
                  pl.BlockSpec((1, N_IDX_HEADS, blk), lambda b, i: (b, 0, i)),
                  q_spec, kv_spec, vt_spec,
                  _const_spec(lam_vecs.shape),
                  q_spec, kv_spec, vt_spec,
                  _const_spec((DIFF_V_DIM, 1))],
        out_specs=[q_spec, q_spec],
        out_shape=[out, out],
        scratch_shapes=[pltpu.VMEM((n_chunks, blk, blk), F32),
                        pltpu.VMEM((N_DSA_HEADS, 1, blk), F32),
                        pltpu.VMEM((N_DSA_HEADS, 1, blk), F32),
                        pltpu.VMEM((N_DSA_HEADS, LANES, blk), F32),
                        pltpu.VMEM((BF16_SUBLANES, LANES), F32),
                        pltpu.VMEM((n_maps, 1, blk), F32),
                        pltpu.VMEM((n_maps, 1, blk), F32),
                        pltpu.VMEM((n_maps, DIFF_V_DIM, blk), F32),
                        pltpu.VMEM((BF16_SUBLANES, LANES), F32),
                        pltpu.VMEM((4, 1, blk), F32)],
        compiler_params=pltpu.CompilerParams(
            dimension_semantics=("parallel", "arbitrary"), vmem_limit_bytes=VMEM_LIMIT_BYTES),
        name="mixers",
    )(iq, ik2, iwt, aq, ak, avt, lam_vecs, dq, dk, dvt, subln_col)


def _out_mlp_kernel(x_ref, ma_ref, mb_ref, g1_ref, sh_ref, sc_ref, g2_ref, n2_ref, nf_ref,
                    woa_ref, wob_ref, w1_ref, w2_ref, o_ref, *, ff_chunk):
    x = x_ref[0]
    o = _dot(ma_ref[0], woa_ref[...]) + _dot(mb_ref[0], wob_ref[...])
    x1 = x + g1_ref[0] * o
    ms = jnp.mean(x1 * x1, axis=-1, keepdims=True)
    h = x1 * lax.rsqrt(ms + NORM_EPS) * n2_ref[...]
    hb = (h * (1.0 + sc_ref[0]) + sh_ref[0]).astype(BF16)
    d_ff = w1_ref.shape[1]
    ff = jnp.zeros_like(x)
    for j in range(d_ff // ff_chunk):
        u = jnp.maximum(_dot(hb, w1_ref[:, j * ff_chunk:(j + 1) * ff_chunk]), 0.0)
        ff = ff + _dot((u * u).astype(BF16), w2_ref[j * ff_chunk:(j + 1) * ff_chunk, :])
    x2 = x1 + g2_ref[0] * ff
    ms2 = jnp.mean(x2 * x2, axis=-1, keepdims=True)
    o_ref[0] = x2 * lax.rsqrt(ms2 + NORM_EPS) * nf_ref[...]


def _out_mlp(x, mix_a, mix_b, g1, sh2, sc2, g2, n2, nf, wo_a, wo_b, w1, w2, *, tile):
    bsz, seq, d = x.shape
    tok = lambda b, i: (b, i, 0)
    per_b = lambda b, i: (b, 0, 0)
    mod_spec = pl.BlockSpec((1, 1, d), per_b)
    return pl.pallas_call(
        functools.partial(_out_mlp_kernel, ff_chunk=1024),
        grid=(bsz, seq // tile),
        in_specs=[pl.BlockSpec((1, tile, d), tok),
                  pl.BlockSpec((1, tile, 512), tok),
                  pl.BlockSpec((1, tile, 512), tok),
                  mod_spec, mod_spec, mod_spec, mod_spec,
                  _const_spec((1, d)), _const_spec((1, d)),
                  _const_spec(wo_a.shape), _const_spec(wo_b.shape),
                  _const_spec(w1.shape), _const_spec(w2.shape)],
        out_specs=pl.BlockSpec((1, tile, d), tok),
        out_shape=jax.ShapeDtypeStruct((bsz, seq, d), F32),
        compiler_params=pltpu.CompilerParams(
            dimension_semantics=("parallel", "parallel"), vmem_limit_bytes=VMEM_LIMIT_BYTES),
        name="out_mlp",
    )(x, mix_a, mix_b, g1, sh2, sc2, g2, n2, nf, wo_a, wo_b, w1, w2)


def _pad_cols(w, width):
    return jnp.pad(w, ((0, 0), (0, width - w.shape[1])))


def kernel(x, c, positions, w_ada, b_ada, norm1_w, w_in, idx_k_ln_w, idx_k_ln_b, lambda_q1, lambda_k1,
           lambda_q2, lambda_k2, subln_w, w_out, norm2_w, w_ff1, w_ff2, norm_f_w):
    bsz, seq, d = x.shape
    depth = w_ada.shape[0]
    assert depth == 1, "the fused final RMSNorm assumes a single layer"
    half = HEAD_DIM // 2
    inv_freq = ROPE_THETA ** (-jnp.arange(half, dtype=F32) / half)
    invf = jnp.tile(inv_freq, LANES // half).reshape(1, LANES)
    pos3 = positions.reshape(bsz, seq, 1)
    blk = min(ATT_BLOCK, seq)
    tile = min(512, seq)

    for l in range(depth):
        mod = _adaln(c, w_ada[l], b_ada[l])
        sh1, sc1, g1, sh2, sc2, g2 = [m.reshape(bsz, 1, d) for m in jnp.split(mod, 6, axis=-1)]

        w = w_in[l]
        wm = jnp.concatenate([w[:, :1024], w[:, 1536:2048], w[:, 2120:3144]], axis=1).astype(BF16)
        ws = _pad_cols(w[:, 2048:2112], LANES).astype(BF16)
        wwt = jnp.pad(w[:, 2112:2120].T, ((0, BF16_SUBLANES - N_IDX_HEADS), (0, 0))).astype(BF16)
        wavt = w[:, 1024:1536].T.astype(BF16)
        wdvt = w[:, 3144:3656].T.astype(BF16)
        lnw = _pad_cols(idx_k_ln_w[l].reshape(1, IDX_DIM), LANES)
        lnb = _pad_cols(idx_k_ln_b[l].reshape(1, IDX_DIM), LANES)

        aq, ak, avt, iq, ik2, iwt, dq, dk, dvt = _in_proj(
            x, pos3, sh1, sc1, norm1_w[l].reshape(1, d), invf, wm, ws, wwt, wavt, wdvt, lnw, lnb,
            tile=tile, chunk=blk)

        lam_vecs = jnp.stack([lambda_q1[l], lambda_k1[l], lambda_q2[l], lambda_k2[l]]).astype(F32)
        lam_init = 0.8 - 0.6 * math.exp(-0.3 * l)
        out_a, out_b = _mixers(iq, ik2, iwt, aq, ak, avt, lam_vecs, dq, dk, dvt,
                               subln_w[l].reshape(DIFF_V_DIM, 1), blk=blk, lam_init=lam_init)

        wo = w_out[l].astype(BF16)
        x = _out_mlp(x, out_a, out_b, g1, sh2, sc2, g2, norm2_w[l].reshape(1, d), norm_f_w.reshape(1, d),
                     wo[:512], wo[512:], w_ff1[l].astype(BF16), w_ff2[l].astype(BF16), tile=tile)
    return x
```

```python
import functools
import math

import jax
import jax.numpy as jnp
from jax import lax
from jax.experimental import pallas as pl
from jax.experimental.pallas import tpu as pltpu

F32 = jnp.float32
BF16 = jnp.bfloat16

HEAD_DIM = 64
N_DSA_HEADS = 8
N_IDX_HEADS = 8
IDX_DIM = 64
N_DIFF_HEADS = 4
DIFF_V_DIM = 128
TOPK_MAX = 256
ROPE_THETA = 10000.0
NORM_EPS = 1e-6
LN_EPS = 1e-5
LANES = 128
F32_SUBLANES = 8
BF16_SUBLANES = 16
ATT_BLOCK = 256
MASK_VALUE = -1e30
F32_MAX = float(jnp.finfo(jnp.float32).max)
MAX_BISECT_STEPS = 512
BISECT_FIRST_STEPS = 16
BISECT_FUSED_STEPS = 16
BISECT_STEPS_PER_TEST = 2
FUSED_COUNTS_UNDER_QK = 2
MIN_SAFE_NORMALISER = 2.0 ** -100
VMEM_LIMIT_BYTES = 56 * 1024 * 1024


def _dot(a, b):
    return jnp.dot(a, b, preferred_element_type=F32)


def _dot_nt(a, b):
    return lax.dot_general(a, b, (((1,), (1,)), ((), ())), preferred_element_type=F32)


def _dot_tn(a, b):
    return lax.dot_general(a, b, (((0,), (0,)), ((), ())), preferred_element_type=F32)


def _const_spec(shape):
    zeros = (0,) * len(shape)
    return pl.BlockSpec(shape, lambda *_: zeros, pipeline_mode=pl.Buffered(1))


def _adaln_kernel(c_ref, w_ref, b_ref, o_ref):
    c = c_ref[...]
    s = c / (1.0 + jnp.exp(-c))
    o_ref[...] = jnp.dot(s, w_ref[...], preferred_element_type=F32,
                         precision=lax.Precision.HIGHEST) + b_ref[...]


def _adaln(c, w, b):
    bsz, d = c.shape
    n = w.shape[1]
    tn = 1536
    return pl.pallas_call(
        _adaln_kernel,
        grid=(n // tn,),
        in_specs=[pl.BlockSpec((bsz, d), lambda j: (0, 0)),
                  pl.BlockSpec((d, tn), lambda j: (0, j)),
                  pl.BlockSpec((1, tn), lambda j: (0, j))],
        out_specs=pl.BlockSpec((bsz, tn), lambda j: (0, j)),
        out_shape=jax.ShapeDtypeStruct((bsz, n), F32),
        compiler_params=pltpu.CompilerParams(vmem_limit_bytes=VMEM_LIMIT_BYTES),
        name="adaln",
    )(c, w, b.reshape(1, n))


def _rope_group(x, cos, sin_signed, first_half):
    nxt = pltpu.roll(x, LANES - HEAD_DIM // 2, axis=1)
    prv = pltpu.roll(x, HEAD_DIM // 2, axis=1)
    return x * cos + jnp.where(first_half, nxt, prv) * sin_signed


def _in_proj_kernel(x_ref, pos_ref, sh_ref, sc_ref, nw_ref, invf_ref, wm_ref, ws_ref, wwt_ref,
                    wavt_ref, wdvt_ref, lnw_ref, lnb_ref,
                    aq_ref, ak_ref, avt_ref, iq_ref, ik_ref, iwt_ref, dq_ref, dk_ref, dvt_ref):
    x = x_ref[0]
    ms = jnp.mean(x * x, axis=-1, keepdims=True)
    h = x * lax.rsqrt(ms + NORM_EPS) * nw_ref[...]
    h = h * (1.0 + sc_ref[0]) + sh_ref[0]
    hb = h.astype(BF16)

    lane = lax.broadcasted_iota(jnp.int32, (1, LANES), 1)
    first_half = (lane % HEAD_DIM) < (HEAD_DIM // 2)
    ang = pos_ref[0].astype(F32) * invf_ref[...]
    cos = jnp.cos(ang)
    sin = jnp.sin(ang)
    sin_signed = jnp.where(first_half, -sin, sin)

    def roped(col0, out_ref, scale):
        wide = _dot(hb, wm_ref[:, col0:col0 + 4 * LANES])
        for g in range(4):
            y = _rope_group(wide[:, g * LANES:(g + 1) * LANES], cos, sin_signed, first_half)
            if scale != 1.0:
                y = y * scale
            out_ref[0, :, g * LANES:(g + 1) * LANES] = y.astype(out_ref.dtype)

    qscale = HEAD_DIM ** -0.5 * math.log2(math.e)
    roped(0, aq_ref, qscale)
    roped(512, ak_ref, 1.0)
    roped(1024, iq_ref, 1.0)
    roped(1536, dq_ref, qscale)
    roped(2048, dk_ref, 1.0)

    n_sub = avt_ref.shape[1]
    chunk = avt_ref.shape[3]
    for j in range(n_sub):
        hj = hb[j * chunk:(j + 1) * chunk]
        avt_ref[0, j] = _dot_nt(wavt_ref[...], hj).astype(avt_ref.dtype)
        dvt_ref[0, j] = _dot_nt(wdvt_ref[...], hj).astype(dvt_ref.dtype)

    sm = _dot(hb, ws_ref[...])
    lo_half = lane < IDX_DIM
    mu = jnp.sum(sm, axis=-1, keepdims=True) * (1.0 / IDX_DIM)
    d = jnp.where(lo_half, sm - mu, 0.0)
    var = jnp.sum(d * d, axis=-1, keepdims=True) * (1.0 / IDX_DIM)
    y = d * lax.rsqrt(var + LN_EPS) * lnw_ref[...] + lnb_ref[...]
    y = y + pltpu.roll(y, IDX_DIM, axis=1)
    ik_ref[0] = _rope_group(y, cos, sin_signed, first_half).astype(ik_ref.dtype)

    wt = _dot_nt(wwt_ref[...], hb)
    iwt_ref[0] = wt[:N_IDX_HEADS] * ((N_IDX_HEADS ** -0.5) * (IDX_DIM ** -0.5))


def _in_proj(x, pos3, sh1, sc1, nw, invf, wm, ws, wwt, wavt, wdvt, lnw, lnb, *, tile, chunk):
    bsz, seq, d = x.shape
    tok = lambda b, i: (b, i, 0)
    per_b = lambda b, i: (b, 0, 0)
    wide = jax.ShapeDtypeStruct((bsz, seq, 512), BF16)
    wide_t = jax.ShapeDtypeStruct((bsz, seq // chunk, 512, chunk), BF16)
    wide_spec = pl.BlockSpec((1, tile, 512), tok)
    wide_t_spec = pl.BlockSpec((1, tile // chunk, 512, chunk), lambda b, i: (b, i, 0, 0))
    out_shape = [wide, wide, wide_t, wide,
                 jax.ShapeDtypeStruct((bsz, seq, LANES), BF16),
                 jax.ShapeDtypeStruct((bsz, N_IDX_HEADS, seq), F32),
                 wide, wide, wide_t]
    out_specs = [wide_spec, wide_spec, wide_t_spec, wide_spec,
                 pl.BlockSpec((1, tile, LANES), tok),
                 pl.BlockSpec((1, N_IDX_HEADS, tile), lambda b, i: (b, 0, i)),
                 wide_spec, wide_spec, wide_t_spec]
    return pl.pallas_call(
        _in_proj_kernel,
        grid=(bsz, seq // tile),
        in_specs=[pl.BlockSpec((1, tile, d), tok),
                  pl.BlockSpec((1, tile, 1), tok),
                  pl.BlockSpec((1, 1, d), per_b),
                  pl.BlockSpec((1, 1, d), per_b),
                  _const_spec((1, d)),
                  _const_spec((1, LANES)),
                  _const_spec(wm.shape),
                  _const_spec(ws.shape),
                  _const_spec(wwt.shape),
                  _const_spec(wavt.shape),
                  _const_spec(wdvt.shape),
                  _const_spec((1, LANES)),
                  _const_spec((1, LANES))],
        out_specs=out_specs,
        out_shape=out_shape,
        compiler_params=pltpu.CompilerParams(
            dimension_semantics=("parallel", "parallel"), vmem_limit_bytes=VMEM_LIMIT_BYTES),
        name="in_proj",
    )(x, pos3, sh1, sc1, nw, invf, wm, ws, wwt, wavt, wdvt, lnw, lnb)


def _head_q(q_all, h, lane_lo):
    pair = q_all[:, (h // 2) * LANES:(h // 2 + 1) * LANES]
    keep = lane_lo if h % 2 == 0 else jnp.logical_not(lane_lo)
    return jnp.where(keep, pair, jnp.zeros_like(pair))


def _softmax_update(s_list, vt_list, m_ref, l_ref, acc_ref):
    n = len(s_list)
    dv = vt_list[0].shape[0]
    ones = jnp.ones((BF16_SUBLANES, vt_list[0].shape[1]), BF16)
    m_old = [m_ref[g] for g in range(n)]
    m_new = [jnp.maximum(m_old[g], jnp.max(s_list[g], axis=0, keepdims=True)) for g in range(n)]
    p_list = [jnp.exp2(s_list[g] - m_new[g]).astype(BF16) for g in range(n)]
    alpha = [jnp.exp2(m_old[g] - m_new[g]) for g in range(n)]
    pv = [_dot(jnp.concatenate([vt_list[g], ones], axis=0), p_list[g]) for g in range(n)]
    for g in range(n):
        m_ref[g] = m_new[g]
        l_ref[g] = alpha[g] * l_ref[g] + pv[g][dv:dv + 1]
        acc_ref[g] = alpha[g] * acc_ref[g] + pv[g][:dv]


def _init_softmax_state(m_ref, l_ref, acc_ref):
    m_ref[...] = jnp.full(m_ref.shape, MASK_VALUE, F32)
    l_ref[...] = jnp.zeros(l_ref.shape, F32)
    acc_ref[...] = jnp.zeros(acc_ref.shape, F32)


def _bounded_update(s_list, vt_list, l_ref, acc_ref, before_last=None):
    n = len(s_list)
    dv = vt_list[0].shape[0]
    ones = jnp.ones((BF16_SUBLANES, vt_list[0].shape[1]), BF16)
    p_list = [jnp.exp2(s_list[g]).astype(BF16) for g in range(n)]
    pv = [_dot(jnp.concatenate([vt_list[g], ones], axis=0), p_list[g]) for g in range(n)]
    for g in range(n):
        col_sum = pv[g][dv:dv + 1]
        if before_last is not None and g == n - 1:
            col_sum = col_sum + before_last
        l_ref[g] = l_ref[g] + col_sum
        acc_ref[g] = acc_ref[g] + pv[g][:dv]


def _head_selector(n_lanes):
    r = lax.broadcasted_iota(jnp.int32, (BF16_SUBLANES, n_lanes), 0)
    l = lax.broadcasted_iota(jnp.int32, (BF16_SUBLANES, n_lanes), 1)
    return jnp.where(l // HEAD_DIM == r, 1.0, 0.0).astype(BF16)


def _head_sq_norms(x, head_sel):
    return _dot_nt(head_sel, x * x)


def _max_key_sq_norms(k_ref, kmax_ref, rows_per_step):
    n_lanes = k_ref.shape[2]
    l = lax.broadcasted_iota(jnp.int32, (n_lanes, LANES), 0)
    h = lax.broadcasted_iota(jnp.int32, (n_lanes, LANES), 1)
    sel_t = jnp.where(l // HEAD_DIM == h, 1.0, 0.0).astype(BF16)

    def body(c, mx):
        start = pl.multiple_of(c * rows_per_step, rows_per_step)
        kk = k_ref[0, pl.ds(start, rows_per_step), :]
        return jnp.maximum(mx, jnp.max(_dot(kk * kk, sel_t), axis=0, keepdims=True))
    row = lax.fori_loop(0, k_ref.shape[1] // rows_per_step, body, jnp.zeros((1, LANES), F32))
    r = lax.broadcasted_iota(jnp.int32, kmax_ref.shape, 0)
    c = lax.broadcasted_iota(jnp.int32, kmax_ref.shape, 1)
    col = jnp.sum(jnp.where(r == c, row, 0.0), axis=1, keepdims=True)
    kmax_ref[...] = jnp.broadcast_to(col, kmax_ref.shape)


def _softmax_underflowed(l_ref):
    return jnp.logical_not(jnp.min(l_ref[...]) >= MIN_SAFE_NORMALISER)


def _mixer_kernel(iq_ref, ik_ref, iwt_ref, aq_ref, ak_ref, avt_ref, lam_ref, dq_ref, dk_ref, dvt_ref, sw_ref,
                  oa_ref, ob_ref,
                  score_ref, m_ref, l_ref, acc_ref, kmax_ref, dm_ref, dl_ref, dacc_ref, dkmax_ref, bis_ref,
                  *, blk, top_k, lam_init):
    i = pl.program_id(1)
    n_chunks = i + 1
    n_maps = 2 * N_DIFF_HEADS
    lane = lax.broadcasted_iota(jnp.int32, (1, LANES), 1)
    lane_lo = lane < HEAD_DIM
    qcol = lax.broadcasted_iota(jnp.int32, (1, blk), 1)
    qpos = i * blk + qcol
    krow = lax.broadcasted_iota(jnp.int32, (blk, 1), 0)
    kf = float(top_k)

    aq_all = aq_ref[0]
    dq_all = dq_ref[0]
    head_sel = _head_selector(aq_all.shape[1])

    @pl.when(i == 0)
    def _():
        rows_per_step = math.gcd(4 * blk, ak_ref.shape[1])
        _max_key_sq_norms(ak_ref, kmax_ref, rows_per_step)
        _max_key_sq_norms(dk_ref, dkmax_ref, rows_per_step)

    d_maps = [_head_q(dq_all, g, lane_lo) for g in range(n_maps)]
    d_bound = jnp.sqrt(_head_sq_norms(dq_all, head_sel) * dkmax_ref[:, :1])
    d_shifts = [d_bound[g:g + 1] for g in range(n_maps)]

    def diff_chunk(c, diagonal, bounded, after=None, before_last=None):
        start = pl.multiple_of(c * blk, blk)
        if diagonal:
            bias = jnp.where(krow <= qcol, 0.0, MASK_VALUE)
        s_list, vt_list = [], []
        for h in range(N_DIFF_HEADS):
            cols = slice(h * LANES, (h + 1) * LANES)
            kk = dk_ref[0, pl.ds(start, blk), cols]
            for comp in range(2):
                g = 2 * h + comp
                s = _dot_nt(kk, d_maps[g])
                if bounded:
                    s = s - (d_shifts[g] if after is None else d_shifts[g] + after)
                s_list.append(s + bias if diagonal else s)
                vt_list.append(dvt_ref[0, c, cols, :])
        if bounded:
            _bounded_update(s_list, vt_list, dl_ref, dacc_ref, before_last)
        else:
            _softmax_update(s_list, vt_list, dm_ref, dl_ref, dacc_ref)

    _init_softmax_state(dm_ref, dl_ref, dacc_ref)

    iq_all = iq_ref[0]
    wt = iwt_ref[0]
    q_heads = [_head_q(iq_all, h, lane_lo) for h in range(N_IDX_HEADS)]
    w_rows = [wt[h:h + 1, :] for h in range(N_IDX_HEADS)]

    n_pairs = (n_chunks + 1) // 2

    def score_body(j, carry):
        mn, mx = carry
        for u in range(2):
            c = 2 * j + u
            start = pl.multiple_of(c * blk, blk)
            kk = ik_ref[0, pl.ds(start, blk), :]
            acc = jnp.zeros((blk, blk), F32)
            for h in range(N_IDX_HEADS):
                acc = acc + jnp.maximum(_dot_nt(kk, q_heads[h]), 0.0) * w_rows[h]
            causal = (c * blk + krow) <= qpos
            sc = jnp.where(causal, acc, -jnp.inf)
            score_ref[c] = sc
            mx = jnp.maximum(mx, jnp.max(sc, axis=0, keepdims=True))
            mn = jnp.minimum(mn, jnp.min(jnp.where(causal, acc, jnp.inf), axis=0, keepdims=True))
        return mn, mx

    mn, mx = lax.fori_loop(0, n_pairs, score_body,
                           (jnp.full((1, blk), jnp.inf, F32), jnp.full((1, blk), -jnp.inf, F32)))

    def pair_sum(hit):
        m = jnp.where(hit, 1.0, 0.0)
        return jnp.sum((m[0] + m[1]).reshape(blk // F32_SUBLANES, F32_SUBLANES, blk), axis=0)

    zero8 = jnp.zeros((F32_SUBLANES, blk), F32)

    def count_ge(cand):
        def body(j, acc):
            return acc + pair_sum(score_ref[pl.ds(2 * j, 2)] >= cand)
        return jnp.sum(lax.fori_loop(0, n_pairs, body, zero8), axis=0, keepdims=True)

    def count_ge_gt(cand):
        def body(j, carry):
            ge, gt = carry
            s = score_ref[pl.ds(2 * j, 2)]
            return ge + pair_sum(s >= cand), gt + pair_sum(s > cand)
        ge, gt = lax.fori_loop(0, n_pairs, body, (zero8, zero8))
        return jnp.sum(ge, axis=0, keepdims=True), jnp.sum(gt, axis=0, keepdims=True)

    n_valid = (qpos + 1).astype(F32)
    need = n_valid > kf
    ge0, gt0 = count_ge_gt(jnp.zeros((1, blk), F32))
    kth_is_zero = jnp.logical_and(gt0 < kf, ge0 >= kf)
    nonneg = ge0 >= kf
    lo0 = jnp.where(nonneg, 0.0, mn)
    cnt0 = jnp.where(nonneg, ge0, n_valid)
    above_max = jnp.where(mx > 0.0, jnp.minimum(mx * 2.0, F32_MAX), 1.0)
    hi0 = jnp.where(nonneg, above_max, 0.0)
    done0 = jnp.logical_or(jnp.logical_not(need), jnp.logical_or(kth_is_zero, cnt0 == kf))
    active0 = jnp.where(done0, 0.0, 1.0)

    def midpoint(st):
        return 0.5 * st[0] + 0.5 * st[1]

    def halve(st, count):
        mid = midpoint(st)
        return narrow(st, mid, count(mid))

    def narrow(st, mid, cnt):
        lo, hi, cnt_lo, active = st
        conv = jnp.logical_or(mid <= lo, mid >= hi)
        upd = jnp.logical_and(active > 0.0, jnp.logical_not(conv))
        ge = cnt >= kf
        up_lo = jnp.logical_and(upd, ge)
        up_hi = jnp.logical_and(upd, jnp.logical_not(ge))
        lo = jnp.where(up_lo, mid, lo)
        cnt_lo = jnp.where(up_lo, cnt, cnt_lo)
        hi = jnp.where(up_hi, mid, hi)
        finished = jnp.logical_or(conv, jnp.logical_and(up_lo, cnt == kf))
        return lo, hi, cnt_lo, jnp.where(finished, 0.0, active)

    n_fused = jnp.minimum(i, BISECT_FUSED_STEPS)
    for k, v in enumerate((lo0, hi0, cnt0, active0)):
        bis_ref[k] = v

    for pairs in range(1, score_ref.shape[0] // 2 + 1):
        @pl.when(n_pairs == pairs)
        def _(pairs=pairs):
            def fused_step(c, st):
                mid = midpoint(st)
                n_first = min(FUSED_COUNTS_UNDER_QK, (pairs + 1) // 2)
                parts = []
                for js in (range(0, n_first), range(n_first, pairs)):
                    acc = zero8
                    for j in js:
                        acc = acc + pair_sum(score_ref[2 * j:2 * j + 2] >= mid)
                    parts.append(jnp.sum(acc, axis=0, keepdims=True))
                diff_chunk(c, False, True, after=parts[0] * 0.0,
                           before_last=parts[1] * 0.0 if pairs > n_first else None)
                return narrow(st, mid, parts[0] + parts[1])
            st = lax.fori_loop(0, n_fused, fused_step, tuple(bis_ref[k] for k in range(4)))
            for k in range(4):
                bis_ref[k] = st[k]

    st = tuple(bis_ref[k] for k in range(4))
    st = lax.fori_loop(0, jnp.maximum(BISECT_FIRST_STEPS - n_fused, 0),
                       lambda _, s: halve(s, count_ge), st)

    def bis_cond(st):
        return jnp.logical_and(st[4] > 0.0, st[5] < MAX_BISECT_STEPS)

    def bis_body(st):
        lo, hi, cnt_lo, active = lax.fori_loop(0, BISECT_STEPS_PER_TEST, lambda _, s: halve(s, count_ge), st[:4])
        return lo, hi, cnt_lo, active, jnp.max(active), st[5] + BISECT_STEPS_PER_TEST

    lo, _, cnt_lo, _, _, _ = lax.while_loop(bis_cond, bis_body, st + (jnp.max(st[3]), jnp.int32(0)))
    thr = jnp.where(need, lo, -F32_MAX)

    excess = jnp.logical_and(need, cnt_lo > kf)

    @pl.when(jnp.max(jnp.where(excess, 1.0, 0.0)) > 0.0)
    def _():
        _, gt = count_ge_gt(thr)
        allow = kf - gt
        r_i = lax.broadcasted_iota(jnp.int32, (blk, blk), 0)
        c_i = lax.broadcasted_iota(jnp.int32, (blk, blk), 1)
        earlier = jnp.where(c_i < r_i, 1.0, 0.0).astype(BF16)

        def tie_body(c, seen):
            s = score_ref[c]
            eq = jnp.logical_and(s == thr, excess)
            eqf = jnp.where(eq, 1.0, 0.0)
            rank = seen + _dot(earlier, eqf.astype(BF16))
            drop = jnp.logical_and(eq, rank >= allow)
            score_ref[c] = jnp.where(drop, -jnp.inf, s)
            return seen + jnp.sum(eqf, axis=0, keepdims=True)

        lax.fori_loop(0, n_chunks, tie_body, jnp.zeros((1, blk), F32))

    qa_heads = [_head_q(aq_all, h, lane_lo) for h in range(N_DSA_HEADS)]

    def attend(bounded, shift):
        def att_body(c, carry):
            start = pl.multiple_of(c * blk, blk)
            bias = jnp.where(score_ref[c] >= thr, shift, MASK_VALUE)
            s_list, vt_list = [], []
            for h in range(N_DSA_HEADS):
                pair = slice((h // 2) * LANES, (h // 2 + 1) * LANES)
                kk = ak_ref[0, pl.ds(start, blk), pair]
                s_list.append(_dot_nt(kk, qa_heads[h]) + bias)
                vt_list.append(avt_ref[0, c, pair, :])
            if bounded:
                _bounded_update(s_list, vt_list, l_ref, acc_ref)
            else:
                _softmax_update(s_list, vt_list, m_ref, l_ref, acc_ref)
            return carry

        _init_softmax_state(m_ref, l_ref, acc_ref)
        lax.fori_loop(0, n_chunks, att_body, 0)

    bound = jnp.sqrt(_head_sq_norms(aq_all, head_sel) * kmax_ref[:, :1])
    attend(True, -jnp.max(bound, axis=0, keepdims=True))

    @pl.when(_softmax_underflowed(l_ref))
    def _():
        attend(False, 0.0)

    for j in range(N_DSA_HEADS // 2):
        a = acc_ref[2 * j] * (1.0 / l_ref[2 * j])
        b = acc_ref[2 * j + 1] * (1.0 / l_ref[2 * j + 1])
        o_t = jnp.concatenate([a[:HEAD_DIM], b[HEAD_DIM:]], axis=0)
        oa_ref[0, j * LANES:(j + 1) * LANES, :] = o_t.astype(oa_ref.dtype)

    def diff_rest(c, carry):
        diff_chunk(c, False, True)
        return carry

    lax.fori_loop(n_fused, i, diff_rest, 0)
    diff_chunk(i, True, True)

    @pl.when(_softmax_underflowed(dl_ref))
    def _():
        def body(c, carry):
            diff_chunk(c, False, False)
            return carry
        _init_softmax_state(dm_ref, dl_ref, dacc_ref)
        lax.fori_loop(0, i, body, 0)
        diff_chunk(i, True, False)

    lam_vecs = lam_ref[...]
    lam = (jnp.exp(jnp.sum(lam_vecs[0:1] * lam_vecs[1:2], axis=-1, keepdims=True))
           - jnp.exp(jnp.sum(lam_vecs[2:3] * lam_vecs[3:4], axis=-1, keepdims=True)) + lam_init)
    for h in range(N_DIFF_HEADS):
        out = dacc_ref[2 * h] * (1.0 / dl_ref[2 * h]) - dacc_ref[2 * h + 1] * (lam / dl_ref[2 * h + 1])
        out = out * lax.rsqrt(jnp.mean(out * out, axis=0, keepdims=True) + NORM_EPS)
        out = out * sw_ref[...] * (1.0 - lam_init)
        ob_ref[0, h * LANES:(h + 1) * LANES, :] = out.astype(ob_ref.dtype)


def _mixers(iq, ik2, iwt, aq, ak, avt, lam_vecs, dq, dk, dvt, subln_col, *, blk, lam_init):
    bsz, seq, _ = aq.shape
    top_k = min(TOPK_MAX, seq // 4)
    n_chunks = seq // blk
    assert n_chunks % 2 == 0, "key chunks are walked in pairs"
    qblk = lambda b, i: (b, i, 0)
    whole = lambda b, i: (b, 0, 0)
    whole_t = lambda b, i: (b, 0, 0, 0)
    n_maps = 2 * N_DIFF_HEADS
    q_spec = pl.BlockSpec((1, blk, 512), qblk)
    kv_spec = pl.BlockSpec((1, seq, 512), whole)
    vt_spec = pl.BlockSpec((1, n_chunks, 512, blk), whole_t)
    return pl.pallas_call(
        functools.partial(_mixer_kernel, blk=blk, top_k=top_k, lam_init=lam_init),
        grid=(bsz, n_chunks),
        in_specs=[q_spec,
                  pl.BlockSpec((1, seq, LANES), whole),
                  pl.BlockSpec((1, N_IDX_HEADS, blk), lambda b, i: (b, 0, i)),
                  q_spec, kv_spec, vt_spec,
                  _const_spec(lam_vecs.shape),
                  q_spec, kv_spec, vt_spec,
                  _const_spec((DIFF_V_DIM, 1))],
        out_specs=[pl.BlockSpec((1, 512, blk), lambda b, i: (b, 0, i))] * 2,
        out_shape=[jax.ShapeDtypeStruct((bsz, 512, seq), BF16)] * 2,
        scratch_shapes=[pltpu.VMEM((n_chunks, blk, blk), F32),
                        pltpu.VMEM((N_DSA_HEADS, 1, blk), F32),
                        pltpu.VMEM((N_DSA_HEADS, 1, blk), F32),
                        pltpu.VMEM((N_DSA_HEADS, LANES, blk), F32),
                        pltpu.VMEM((BF16_SUBLANES, LANES), F32),
                        pltpu.VMEM((n_maps, 1, blk), F32),
                        pltpu.VMEM((n_maps, 1, blk), F32),
                        pltpu.VMEM((n_maps, DIFF_V_DIM, blk), F32),
                        pltpu.VMEM((BF16_SUBLANES, LANES), F32),
                        pltpu.VMEM((4, 1, blk), F32)],
        compiler_params=pltpu.CompilerParams(
            dimension_semantics=("parallel", "arbitrary"), vmem_limit_bytes=VMEM_LIMIT_BYTES),
        name="mixers",
    )(iq, ik2, iwt, aq, ak, avt, lam_vecs, dq, dk, dvt, subln_col)


def _out_mlp_kernel(x_ref, ma_ref, mb_ref, g1_ref, sh_ref, sc_ref, g2_ref, n2_ref, nf_ref,
                    woa_ref, wob_ref, w1_ref, w2_ref, o_ref, *, ff_chunk):
    x = x_ref[0]
    o = _dot_tn(ma_ref[0], woa_ref[...]) + _dot_tn(mb_ref[0], wob_ref[...])
    x1 = x + g1_ref[0] * o
    ms = jnp.mean(x1 * x1, axis=-1, keepdims=True)
    h = x1 * lax.rsqrt(ms + NORM_EPS) * n2_ref[...]
    hb = (h * (1.0 + sc_ref[0]) + sh_ref[0]).astype(BF16)
    d_ff = w1_ref.shape[1]
    ff = jnp.zeros_like(x)
    for j in range(d_ff // ff_chunk):
        u = jnp.maximum(_dot(hb, w1_ref[:, j * ff_chunk:(j + 1) * ff_chunk]), 0.0)
        ff = ff + _dot((u * u).astype(BF16), w2_ref[j * ff_chunk:(j + 1) * ff_chunk, :])
    x2 = x1 + g2_ref[0] * ff
    ms2 = jnp.mean(x2 * x2, axis=-1, keepdims=True)
    o_ref[0] = x2 * lax.rsqrt(ms2 + NORM_EPS) * nf_ref[...]


def _out_mlp(x, mix_a, mix_b, g1, sh2, sc2, g2, n2, nf, wo_a, wo_b, w1, w2, *, tile):
    bsz, seq, d = x.shape
    tok = lambda b, i: (b, i, 0)
    per_b = lambda b, i: (b, 0, 0)
    mod_spec = pl.BlockSpec((1, 1, d), per_b)
    return pl.pallas_call(
        functools.partial(_out_mlp_kernel, ff_chunk=1024),
        grid=(bsz, seq // tile),
        in_specs=[pl.BlockSpec((1, tile, d), tok),
                  pl.BlockSpec((1, 512, tile), lambda b, i: (b, 0, i)),
                  pl.BlockSpec((1, 512, tile), lambda b, i: (b, 0, i)),
                  mod_spec, mod_spec, mod_spec, mod_spec,
                  _const_spec((1, d)), _const_spec((1, d)),
                  _const_spec(wo_a.shape), _const_spec(wo_b.shape),
                  _const_spec(w1.shape), _const_spec(w2.shape)],
        out_specs=pl.BlockSpec((1, tile, d), tok),
        out_shape=jax.ShapeDtypeStruct((bsz, seq, d), F32),
        compiler_params=pltpu.CompilerParams(
            dimension_semantics=("parallel", "parallel"), vmem_limit_bytes=VMEM_LIMIT_BYTES),
        name="out_mlp",
    )(x, mix_a, mix_b, g1, sh2, sc2, g2, n2, nf, wo_a, wo_b, w1, w2)


def _pad_cols(w, width):
    return jnp.pad(w, ((0, 0), (0, width - w.shape[1])))


def kernel(x, c, positions, w_ada, b_ada, norm1_w, w_in, idx_k_ln_w, idx_k_ln_b, lambda_q1, lambda_k1,
           lambda_q2, lambda_k2, subln_w, w_out, norm2_w, w_ff1, w_ff2, norm_f_w):
    bsz, seq, d = x.shape
    depth = w_ada.shape[0]
    assert depth == 1, "the fused final RMSNorm assumes a single layer"
    half = HEAD_DIM // 2
    inv_freq = ROPE_THETA ** (-jnp.arange(half, dtype=F32) / half)
    invf = jnp.tile(inv_freq, LANES // half).reshape(1, LANES)
    pos3 = positions.reshape(bsz, seq, 1)
    blk = min(ATT_BLOCK, seq)
    tile = min(512, seq)

    for l in range(depth):
        mod = _adaln(c, w_ada[l], b_ada[l])
        sh1, sc1, g1, sh2, sc2, g2 = [m.reshape(bsz, 1, d) for m in jnp.split(mod, 6, axis=-1)]

        w = w_in[l]
        wm = jnp.concatenate([w[:, :1024], w[:, 1536:2048], w[:, 2120:3144]], axis=1).astype(BF16)
        ws = _pad_cols(w[:, 2048:2112], LANES).astype(BF16)
        wwt = jnp.pad(w[:, 2112:2120].T, ((0, BF16_SUBLANES - N_IDX_HEADS), (0, 0))).astype(BF16)
        wavt = w[:, 1024:1536].T.astype(BF16)
        wdvt = w[:, 3144:3656].T.astype(BF16)
        lnw = _pad_cols(idx_k_ln_w[l].reshape(1, IDX_DIM), LANES)
        lnb = _pad_cols(idx_k_ln_b[l].reshape(1, IDX_DIM), LANES)

        aq, ak, avt, iq, ik2, iwt, dq, dk, dvt = _in_proj(
            x, pos3, sh1, sc1, norm1_w[l].reshape(1, d), invf, wm, ws, wwt, wavt, wdvt, lnw, lnb,
            tile=tile, chunk=blk)

        lam_vecs = jnp.stack([lambda_q1[l], lambda_k1[l], lambda_q2[l], lambda_k2[l]]).astype(F32)
        lam_init = 0.8 - 0.6 * math.exp(-0.3 * l)
        out_a, out_b = _mixers(iq, ik2, iwt, aq, ak, avt, lam_vecs, dq, dk, dvt,
                               subln_w[l].reshape(DIFF_V_DIM, 1), blk=blk, lam_init=lam_init)

        wo = w_out[l].astype(BF16)
        x = _out_mlp(x, out_a, out_b, g1, sh2, sc2, g2, norm2_w[l].reshape(1, d), norm_f_w.reshape(1, d),
                     wo[:512], wo[512:], w_ff1[l].astype(BF16), w_ff2[l].astype(BF16), tile=tile)
    return x
```

```python
import functools
import math

import jax
import jax.numpy as jnp
from jax import lax
from jax.experimental import pallas as pl
from jax.experimental.pallas import tpu as pltpu

F32 = jnp.float32
BF16 = jnp.bfloat16

HEAD_DIM = 64
N_DSA_HEADS = 8
N_IDX_HEADS = 8
IDX_DIM = 64
N_DIFF_HEADS = 4
DIFF_V_DIM = 128
TOPK_MAX = 256
ROPE_THETA = 10000.0
NORM_EPS = 1e-6
LN_EPS = 1e-5
LANES = 128
F32_SUBLANES = 8
BF16_SUBLANES = 16
ATT_BLOCK = 256
MASK_VALUE = -1e30
F32_MAX = float(jnp.finfo(jnp.float32).max)
MAX_BISECT_STEPS = 512
BISECT_FIRST_STEPS = 16
BISECT_FUSED_STEPS = 16
BISECT_STEPS_PER_TEST = 2
FUSED_COUNTS_UNDER_QK = 2
MIN_SAFE_NORMALISER = 2.0 ** -100
VMEM_LIMIT_BYTES = 56 * 1024 * 1024


def _dot(a, b):
    return jnp.dot(a, b, preferred_element_type=F32)


def _dot_nt(a, b):
    return lax.dot_general(a, b, (((1,), (1,)), ((), ())), preferred_element_type=F32)


def _dot_tn(a, b):
    return lax.dot_general(a, b, (((0,), (0,)), ((), ())), preferred_element_type=F32)


def _const_spec(shape):
    zeros = (0,) * len(shape)
    return pl.BlockSpec(shape, lambda *_: zeros, pipeline_mode=pl.Buffered(1))


def _adaln_kernel(c_ref, w_ref, b_ref, o_ref):
    c = c_ref[...]
    s = c / (1.0 + jnp.exp(-c))
    o_ref[...] = jnp.dot(s, w_ref[...], preferred_element_type=F32,
                         precision=lax.Precision.HIGHEST) + b_ref[...]


def _adaln(c, w, b):
    bsz, d = c.shape
    n = w.shape[1]
    tn = 1536
    return pl.pallas_call(
        _adaln_kernel,
        grid=(n // tn,),
        in_specs=[pl.BlockSpec((bsz, d), lambda j: (0, 0)),
                  pl.BlockSpec((d, tn), lambda j: (0, j)),
                  pl.BlockSpec((1, tn), lambda j: (0, j))],
        out_specs=pl.BlockSpec((bsz, tn), lambda j: (0, j)),
        out_shape=jax.ShapeDtypeStruct((bsz, n), F32),
        compiler_params=pltpu.CompilerParams(vmem_limit_bytes=VMEM_LIMIT_BYTES),
        name="adaln",
    )(c, w, b.reshape(1, n))


def _rope_group(x, cos, sin_signed, first_half):
    nxt = pltpu.roll(x, LANES - HEAD_DIM // 2, axis=1)
    prv = pltpu.roll(x, HEAD_DIM // 2, axis=1)
    return x * cos + jnp.where(first_half, nxt, prv) * sin_signed


def _in_proj_kernel(x_ref, pos_ref, sh_ref, sc_ref, nw_ref, invf_ref, wm_ref, ws_ref, wwt_ref,
                    wavt_ref, wdvt_ref, lnw_ref, lnb_ref,
                    aq_ref, ak_ref, avt_ref, iq_ref, ik_ref, iwt_ref, dq_ref, dk_ref, dvt_ref):
    x = x_ref[0]
    ms = jnp.mean(x * x, axis=-1, keepdims=True)
    h = x * lax.rsqrt(ms + NORM_EPS) * nw_ref[...]
    h = h * (1.0 + sc_ref[0]) + sh_ref[0]
    hb = h.astype(BF16)

    lane = lax.broadcasted_iota(jnp.int32, (1, LANES), 1)
    first_half = (lane % HEAD_DIM) < (HEAD_DIM // 2)
    ang = pos_ref[0].astype(F32) * invf_ref[...]
    cos = jnp.cos(ang)
    sin = jnp.sin(ang)
    sin_signed = jnp.where(first_half, -sin, sin)

    def roped(col0, out_ref, scale):
        wide = _dot(hb, wm_ref[:, col0:col0 + 4 * LANES])
        for g in range(4):
            y = _rope_group(wide[:, g * LANES:(g + 1) * LANES], cos, sin_signed, first_half)
            if scale != 1.0:
                y = y * scale
            out_ref[0, :, g * LANES:(g + 1) * LANES] = y.astype(out_ref.dtype)

    qscale = HEAD_DIM ** -0.5 * math.log2(math.e)
    roped(0, aq_ref, qscale)
    roped(512, ak_ref, 1.0)
    roped(1024, iq_ref, 1.0)
    roped(1536, dq_ref, qscale)
    roped(2048, dk_ref, 1.0)

    n_sub = avt_ref.shape[1]
    chunk = avt_ref.shape[3]
    for j in range(n_sub):
        hj = hb[j * chunk:(j + 1) * chunk]
        avt_ref[0, j] = _dot_nt(wavt_ref[...], hj).astype(avt_ref.dtype)
        dvt_ref[0, j] = _dot_nt(wdvt_ref[...], hj).astype(dvt_ref.dtype)

    sm = _dot(hb, ws_ref[...])
    lo_half = lane < IDX_DIM
    mu = jnp.sum(sm, axis=-1, keepdims=True) * (1.0 / IDX_DIM)
    d = jnp.where(lo_half, sm - mu, 0.0)
    var = jnp.sum(d * d, axis=-1, keepdims=True) * (1.0 / IDX_DIM)
    y = d * lax.rsqrt(var + LN_EPS) * lnw_ref[...] + lnb_ref[...]
    y = y + pltpu.roll(y, IDX_DIM, axis=1)
    ik_ref[0] = _rope_group(y, cos, sin_signed, first_half).astype(ik_ref.dtype)

    wt = _dot_nt(wwt_ref[...], hb)
    iwt_ref[0] = wt[:N_IDX_HEADS] * ((N_IDX_HEADS ** -0.5) * (IDX_DIM ** -0.5))


def _in_proj(x, pos3, sh1, sc1, nw, invf, wm, ws, wwt, wavt, wdvt, lnw, lnb, *, tile, chunk):
    bsz, seq, d = x.shape
    tok = lambda b, i: (b, i, 0)
    per_b = lambda b, i: (b, 0, 0)
    wide = jax.ShapeDtypeStruct((bsz, seq, 512), BF16)
    wide_t = jax.ShapeDtypeStruct((bsz, seq // chunk, 512, chunk), BF16)
    wide_spec = pl.BlockSpec((1, tile, 512), tok)
    wide_t_spec = pl.BlockSpec((1, tile // chunk, 512, chunk), lambda b, i: (b, i, 0, 0))
    out_shape = [wide, wide, wide_t, wide,
                 jax.ShapeDtypeStruct((bsz, seq, LANES), BF16),
                 jax.ShapeDtypeStruct((bsz, N_IDX_HEADS, seq), F32),
                 wide, wide, wide_t]
    out_specs = [wide_spec, wide_spec, wide_t_spec, wide_spec,
                 pl.BlockSpec((1, tile, LANES), tok),
                 pl.BlockSpec((1, N_IDX_HEADS, tile), lambda b, i: (b, 0, i)),
                 wide_spec, wide_spec, wide_t_spec]
    return pl.pallas_call(
        _in_proj_kernel,
        grid=(bsz, seq // tile),
        in_specs=[pl.BlockSpec((1, tile, d), tok),
                  pl.BlockSpec((1, tile, 1), tok),
                  pl.BlockSpec((1, 1, d), per_b),
                  pl.BlockSpec((1, 1, d), per_b),
                  _const_spec((1, d)),
                  _const_spec((1, LANES)),
                  _const_spec(wm.shape),
                  _const_spec(ws.shape),
                  _const_spec(wwt.shape),
                  _const_spec(wavt.shape),
                  _const_spec(wdvt.shape),
                  _const_spec((1, LANES)),
                  _const_spec((1, LANES))],
        out_specs=out_specs,
        out_shape=out_shape,
        compiler_params=pltpu.CompilerParams(
            dimension_semantics=("parallel", "parallel"), vmem_limit_bytes=VMEM_LIMIT_BYTES),
        name="in_proj",
    )(x, pos3, sh1, sc1, nw, invf, wm, ws, wwt, wavt, wdvt, lnw, lnb)


def _head_q(q_all, h, lane_lo):
    pair = q_all[:, (h // 2) * LANES:(h // 2 + 1) * LANES]
    keep = lane_lo if h % 2 == 0 else jnp.logical_not(lane_lo)
    return jnp.where(keep, pair, jnp.zeros_like(pair))


def _softmax_update(s_list, vt_list, m_ref, l_ref, acc_ref):
    n = len(s_list)
    dv = vt_list[0].shape[0]
    ones = jnp.ones((BF16_SUBLANES, vt_list[0].shape[1]), BF16)
    m_old = [m_ref[g] for g in range(n)]
    m_new = [jnp.maximum(m_old[g], jnp.max(s_list[g], axis=0, keepdims=True)) for g in range(n)]
    p_list = [jnp.exp2(s_list[g] - m_new[g]).astype(BF16) for g in range(n)]
    alpha = [jnp.exp2(m_old[g] - m_new[g]) for g in range(n)]
    pv = [_dot(jnp.concatenate([vt_list[g], ones], axis=0), p_list[g]) for g in range(n)]
    for g in range(n):
        m_ref[g] = m_new[g]
        l_ref[g] = alpha[g] * l_ref[g] + pv[g][dv:dv + 1]
        acc_ref[g] = alpha[g] * acc_ref[g] + pv[g][:dv]


def _init_softmax_state(m_ref, l_ref, acc_ref):
    m_ref[...] = jnp.full(m_ref.shape, MASK_VALUE, F32)
    l_ref[...] = jnp.zeros(l_ref.shape, F32)
    acc_ref[...] = jnp.zeros(acc_ref.shape, F32)


def _bounded_update(s_list, vt_list, l_ref, acc_ref, before_last=None):
    n = len(s_list)
    dv = vt_list[0].shape[0]
    ones = jnp.ones((BF16_SUBLANES, vt_list[0].shape[1]), BF16)
    p_list = [jnp.exp2(s_list[g]).astype(BF16) for g in range(n)]
    pv = [_dot(jnp.concatenate([vt_list[g], ones], axis=0), p_list[g]) for g in range(n)]
    for g in range(n):
        col_sum = pv[g][dv:dv + 1]
        if before_last is not None and g == n - 1:
            col_sum = col_sum + before_last
        l_ref[g] = l_ref[g] + col_sum
        acc_ref[g] = acc_ref[g] + pv[g][:dv]


def _head_selector(n_lanes):
    r = lax.broadcasted_iota(jnp.int32, (BF16_SUBLANES, n_lanes), 0)
    l = lax.broadcasted_iota(jnp.int32, (BF16_SUBLANES, n_lanes), 1)
    return jnp.where(l // HEAD_DIM == r, 1.0, 0.0).astype(BF16)


def _head_sq_norms(x, head_sel):
    return _dot_nt(head_sel, x * x)


def _max_key_sq_norms(k_ref, kmax_ref, rows_per_step):
    n_lanes = k_ref.shape[2]
    l = lax.broadcasted_iota(jnp.int32, (n_lanes, LANES), 0)
    h = lax.broadcasted_iota(jnp.int32, (n_lanes, LANES), 1)
    sel_t = jnp.where(l // HEAD_DIM == h, 1.0, 0.0).astype(BF16)

    def body(c, mx):
        start = pl.multiple_of(c * rows_per_step, rows_per_step)
        kk = k_ref[0, pl.ds(start, rows_per_step), :]
        return jnp.maximum(mx, jnp.max(_dot(kk * kk, sel_t), axis=0, keepdims=True))
    row = lax.fori_loop(0, k_ref.shape[1] // rows_per_step, body, jnp.zeros((1, LANES), F32))
    r = lax.broadcasted_iota(jnp.int32, kmax_ref.shape, 0)
    c = lax.broadcasted_iota(jnp.int32, kmax_ref.shape, 1)
    col = jnp.sum(jnp.where(r == c, row, 0.0), axis=1, keepdims=True)
    kmax_ref[...] = jnp.broadcast_to(col, kmax_ref.shape)


def _softmax_underflowed(l_ref):
    return jnp.logical_not(jnp.min(l_ref[...]) >= MIN_SAFE_NORMALISER)


def _mixer_kernel(iq_ref, ik_ref, iwt_ref, aq_ref, ak_ref, avt_ref, lam_ref, dq_ref, dk_ref, dvt_ref, sw_ref,
                  oa_ref, ob_ref,
                  score_ref, m_ref, l_ref, acc_ref, kmax_ref, dm_ref, dl_ref, dacc_ref, dkmax_ref, bis_ref,
                  *, blk, top_k, lam_init):
    i = pl.program_id(1)
    n_chunks = i + 1
    n_maps = 2 * N_DIFF_HEADS
    lane = lax.broadcasted_iota(jnp.int32, (1, LANES), 1)
    lane_lo = lane < HEAD_DIM
    qcol = lax.broadcasted_iota(jnp.int32, (1, blk), 1)
    qpos = i * blk + qcol
    krow = lax.broadcasted_iota(jnp.int32, (blk, 1), 0)
    kf = float(top_k)

    aq_all = aq_ref[0]
    dq_all = dq_ref[0]
    head_sel = _head_selector(aq_all.shape[1])

    @pl.when(i == 0)
    def _():
        rows_per_step = math.gcd(4 * blk, ak_ref.shape[1])
        _max_key_sq_norms(ak_ref, kmax_ref, rows_per_step)
        _max_key_sq_norms(dk_ref, dkmax_ref, rows_per_step)

    d_maps = [_head_q(dq_all, g, lane_lo) for g in range(n_maps)]
    d_bound = jnp.sqrt(_head_sq_norms(dq_all, head_sel) * dkmax_ref[:, :1])
    d_shifts = [d_bound[g:g + 1] for g in range(n_maps)]

    def diff_chunk(c, diagonal, bounded, after=None, before_last=None):
        start = pl.multiple_of(c * blk, blk)
        if diagonal:
            bias = jnp.where(krow <= qcol, 0.0, MASK_VALUE)
        s_list, vt_list = [], []
        for h in range(N_DIFF_HEADS):
            cols = slice(h * LANES, (h + 1) * LANES)
            kk = dk_ref[0, pl.ds(start, blk), cols]
            for comp in range(2):
                g = 2 * h + comp
                s = _dot_nt(kk, d_maps[g])
                if bounded:
                    s = s - (d_shifts[g] if after is None else d_shifts[g] + after)
                s_list.append(s + bias if diagonal else s)
                vt_list.append(dvt_ref[0, c, cols, :])
        if bounded:
            _bounded_update(s_list, vt_list, dl_ref, dacc_ref, before_last)
        else:
            _softmax_update(s_list, vt_list, dm_ref, dl_ref, dacc_ref)

    _init_softmax_state(dm_ref, dl_ref, dacc_ref)

    iq_all = iq_ref[0]
    wt = iwt_ref[0]
    q_heads = [_head_q(iq_all, h, lane_lo) for h in range(N_IDX_HEADS)]
    w_rows = [wt[h:h + 1, :] for h in range(N_IDX_HEADS)]

    n_pairs = (n_chunks + 1) // 2

    def pair_sum(hit):
        m = jnp.where(hit, 1.0, 0.0)
        return jnp.sum((m[0] + m[1]).reshape(blk // F32_SUBLANES, F32_SUBLANES, blk), axis=0)

    zero8 = jnp.zeros((F32_SUBLANES, blk), F32)

    def score_pair(j, mn, mx, w_first):
        for u in range(2):
            c = 2 * j + u
            start = pl.multiple_of(c * blk, blk)
            kk = ik_ref[0, pl.ds(start, blk), :]
            acc = jnp.zeros((blk, blk), F32)
            for h in range(N_IDX_HEADS):
                acc = acc + jnp.maximum(_dot_nt(kk, q_heads[h]), 0.0) * (w_first if h == 0 else w_rows[h])
            causal = (c * blk + krow) <= qpos
            sc = jnp.where(causal, acc, -jnp.inf)
            score_ref[c] = sc
            mx = jnp.maximum(mx, jnp.max(sc, axis=0, keepdims=True))
            mn = jnp.minimum(mn, jnp.min(jnp.where(causal, acc, jnp.inf), axis=0, keepdims=True))
        return mn, mx

    def zero_counts(j):
        s = score_ref[pl.ds(2 * j, 2)]
        return pair_sum(s >= 0.0), pair_sum(s > 0.0)

    def score_body(j, carry):
        mn, mx, ge, gt = carry
        ge_p, gt_p = zero_counts(j - 1)
        after = jnp.sum(ge_p + gt_p, axis=0, keepdims=True) * 0.0
        mn, mx = score_pair(j, mn, mx, w_rows[0] + after)
        return mn, mx, ge + ge_p, gt + gt_p

    mn, mx = score_pair(0, jnp.full((1, blk), jnp.inf, F32), jnp.full((1, blk), -jnp.inf, F32), w_rows[0])
    mn, mx, ge8, gt8 = lax.fori_loop(1, n_pairs, score_body, (mn, mx, zero8, zero8))
    ge_p, gt_p = zero_counts(n_pairs - 1)
    ge0 = jnp.sum(ge8 + ge_p, axis=0, keepdims=True)
    gt0 = jnp.sum(gt8 + gt_p, axis=0, keepdims=True)


    def count_ge(cand):
        def body(j, acc):
            return acc + pair_sum(score_ref[pl.ds(2 * j, 2)] >= cand)
        return jnp.sum(lax.fori_loop(0, n_pairs, body, zero8), axis=0, keepdims=True)

    def count_ge_gt(cand):
        def body(j, carry):
            ge, gt = carry
            s = score_ref[pl.ds(2 * j, 2)]
            return ge + pair_sum(s >= cand), gt + pair_sum(s > cand)
        ge, gt = lax.fori_loop(0, n_pairs, body, (zero8, zero8))
        return jnp.sum(ge, axis=0, keepdims=True), jnp.sum(gt, axis=0, keepdims=True)

    n_valid = (qpos + 1).astype(F32)
    need = n_valid > kf
    kth_is_zero = jnp.logical_and(gt0 < kf, ge0 >= kf)
    nonneg = ge0 >= kf
    lo0 = jnp.where(nonneg, 0.0, mn)
    cnt0 = jnp.where(nonneg, ge0, n_valid)
    above_max = jnp.where(mx > 0.0, jnp.minimum(mx * 2.0, F32_MAX), 1.0)
    hi0 = jnp.where(nonneg, above_max, 0.0)
    done0 = jnp.logical_or(jnp.logical_not(need), jnp.logical_or(kth_is_zero, cnt0 == kf))
    active0 = jnp.where(done0, 0.0, 1.0)

    def midpoint(st):
        return 0.5 * st[0] + 0.5 * st[1]

    def halve(st, count):
        mid = midpoint(st)
        return narrow(st, mid, count(mid))

    def narrow(st, mid, cnt):
        lo, hi, cnt_lo, active = st
        conv = jnp.logical_or(mid <= lo, mid >= hi)
        upd = jnp.logical_and(active > 0.0, jnp.logical_not(conv))
        ge = cnt >= kf
        up_lo = jnp.logical_and(upd, ge)
        up_hi = jnp.logical_and(upd, jnp.logical_not(ge))
        lo = jnp.where(up_lo, mid, lo)
        cnt_lo = jnp.where(up_lo, cnt, cnt_lo)
        hi = jnp.where(up_hi, mid, hi)
        finished = jnp.logical_or(conv, jnp.logical_and(up_lo, cnt == kf))
        return lo, hi, cnt_lo, jnp.where(finished, 0.0, active)

    n_fused = jnp.minimum(i, BISECT_FUSED_STEPS)
    for k, v in enumerate((lo0, hi0, cnt0, active0)):
        bis_ref[k] = v

    for pairs in range(1, score_ref.shape[0] // 2 + 1):
        @pl.when(n_pairs == pairs)
        def _(pairs=pairs):
            def fused_step(c, st):
                mid = midpoint(st)
                n_first = min(FUSED_COUNTS_UNDER_QK, (pairs + 1) // 2)
                parts = []
                for js in (range(0, n_first), range(n_first, pairs)):
                    acc = zero8
                    for j in js:
                        acc = acc + pair_sum(score_ref[2 * j:2 * j + 2] >= mid)
                    parts.append(jnp.sum(acc, axis=0, keepdims=True))
                diff_chunk(c, False, True, after=parts[0] * 0.0,
                           before_last=parts[1] * 0.0 if pairs > n_first else None)
                return narrow(st, mid, parts[0] + parts[1])
            st = lax.fori_loop(0, n_fused, fused_step, tuple(bis_ref[k] for k in range(4)))
            for k in range(4):
                bis_ref[k] = st[k]

    st = tuple(bis_ref[k] for k in range(4))
    st = lax.fori_loop(0, jnp.maximum(BISECT_FIRST_STEPS - n_fused, 0),
                       lambda _, s: halve(s, count_ge), st)

    def bis_cond(st):
        return jnp.logical_and(st[4] > 0.0, st[5] < MAX_BISECT_STEPS)

    def bis_body(st):
        lo, hi, cnt_lo, active = lax.fori_loop(0, BISECT_STEPS_PER_TEST, lambda _, s: halve(s, count_ge), st[:4])
        return lo, hi, cnt_lo, active, jnp.max(active), st[5] + BISECT_STEPS_PER_TEST

    lo, _, cnt_lo, _, _, _ = lax.while_loop(bis_cond, bis_body, st + (jnp.max(st[3]), jnp.int32(0)))
    thr = jnp.where(need, lo, -F32_MAX)

    excess = jnp.logical_and(need, cnt_lo > kf)

    @pl.when(jnp.max(jnp.where(excess, 1.0, 0.0)) > 0.0)
    def _():
        _, gt = count_ge_gt(thr)
        allow = kf - gt
        r_i = lax.broadcasted_iota(jnp.int32, (blk, blk), 0)
        c_i = lax.broadcasted_iota(jnp.int32, (blk, blk), 1)
        earlier = jnp.where(c_i < r_i, 1.0, 0.0).astype(BF16)

        def tie_body(c, seen):
            s = score_ref[c]
            eq = jnp.logical_and(s == thr, excess)
            eqf = jnp.where(eq, 1.0, 0.0)
            rank = seen + _dot(earlier, eqf.astype(BF16))
            drop = jnp.logical_and(eq, rank >= allow)
            score_ref[c] = jnp.where(drop, -jnp.inf, s)
            return seen + jnp.sum(eqf, axis=0, keepdims=True)

        lax.fori_loop(0, n_chunks, tie_body, jnp.zeros((1, blk), F32))

    qa_heads = [_head_q(aq_all, h, lane_lo) for h in range(N_DSA_HEADS)]

    def attend(bounded, shift):
        def att_body(c, carry):
            start = pl.multiple_of(c * blk, blk)
            bias = jnp.where(score_ref[c] >= thr, shift, MASK_VALUE)
            s_list, vt_list = [], []
            for h in range(N_DSA_HEADS):
                pair = slice((h // 2) * LANES, (h // 2 + 1) * LANES)
                kk = ak_ref[0, pl.ds(start, blk), pair]
                s_list.append(_dot_nt(kk, qa_heads[h]) + bias)
                vt_list.append(avt_ref[0, c, pair, :])
            if bounded:
                _bounded_update(s_list, vt_list, l_ref, acc_ref)
            else:
                _softmax_update(s_list, vt_list, m_ref, l_ref, acc_ref)
            return carry

        _init_softmax_state(m_ref, l_ref, acc_ref)
        lax.fori_loop(0, n_chunks, att_body, 0)

    bound = jnp.sqrt(_head_sq_norms(aq_all, head_sel) * kmax_ref[:, :1])
    attend(True, -jnp.max(bound, axis=0, keepdims=True))

    @pl.when(_softmax_underflowed(l_ref))
    def _():
        attend(False, 0.0)

    for j in range(N_DSA_HEADS // 2):
        a = acc_ref[2 * j] * (1.0 / l_ref[2 * j])
        b = acc_ref[2 * j + 1] * (1.0 / l_ref[2 * j + 1])
        o_t = jnp.concatenate([a[:HEAD_DIM], b[HEAD_DIM:]], axis=0)
        oa_ref[0, j * LANES:(j + 1) * LANES, :] = o_t.astype(oa_ref.dtype)

    def diff_rest(c, carry):
        diff_chunk(c, False, True)
        return carry

    lax.fori_loop(n_fused, i, diff_rest, 0)
    diff_chunk(i, True, True)

    @pl.when(_softmax_underflowed(dl_ref))
    def _():
        def body(c, carry):
            diff_chunk(c, False, False)
            return carry
        _init_softmax_state(dm_ref, dl_ref, dacc_ref)
        lax.fori_loop(0, i, body, 0)
        diff_chunk(i, True, False)

    lam_vecs = lam_ref[...]
    lam = (jnp.exp(jnp.sum(lam_vecs[0:1] * lam_vecs[1:2], axis=-1, keepdims=True))
           - jnp.exp(jnp.sum(lam_vecs[2:3] * lam_vecs[3:4], axis=-1, keepdims=True)) + lam_init)
    for h in range(N_DIFF_HEADS):
        out = dacc_ref[2 * h] * (1.0 / dl_ref[2 * h]) - dacc_ref[2 * h + 1] * (lam / dl_ref[2 * h + 1])
        out = out * lax.rsqrt(jnp.mean(out * out, axis=0, keepdims=True) + NORM_EPS)
        out = out * sw_ref[...] * (1.0 - lam_init)
        ob_ref[0, h * LANES:(h + 1) * LANES, :] = out.astype(ob_ref.dtype)


def _mixers(iq, ik2, iwt, aq, ak, avt, lam_vecs, dq, dk, dvt, subln_col, *, blk, lam_init):
    bsz, seq, _ = aq.shape
    top_k = min(TOPK_MAX, seq // 4)
    n_chunks = seq // blk
    assert n_chunks % 2 == 0, "key chunks are walked in pairs"
    qblk = lambda b, i: (b, i, 0)
    whole = lambda b, i: (b, 0, 0)
    whole_t = lambda b, i: (b, 0, 0, 0)
    n_maps = 2 * N_DIFF_HEADS
    q_spec = pl.BlockSpec((1, blk, 512), qblk)
    kv_spec = pl.BlockSpec((1, seq, 512), whole)
    vt_spec = pl.BlockSpec((1, n_chunks, 512, blk), whole_t)
    return pl.pallas_call(
        functools.partial(_mixer_kernel, blk=blk, top_k=top_k, lam_init=lam_init),
        grid=(bsz, n_chunks),
        in_specs=[q_spec,
                  pl.BlockSpec((1, seq, LANES), whole),
                  pl.BlockSpec((1, N_IDX_HEADS, blk), lambda b, i: (b, 0, i)),
                  q_spec, kv_spec, vt_spec,
                  _const_spec(lam_vecs.shape),
                  q_spec, kv_spec, vt_spec,
                  _const_spec((DIFF_V_DIM, 1))],
        out_specs=[pl.BlockSpec((1, 512, blk), lambda b, i: (b, 0, i))] * 2,
        out_shape=[jax.ShapeDtypeStruct((bsz, 512, seq), BF16)] * 2,
        scratch_shapes=[pltpu.VMEM((n_chunks, blk, blk), F32),
                        pltpu.VMEM((N_DSA_HEADS, 1, blk), F32),
                        pltpu.VMEM((N_DSA_HEADS, 1, blk), F32),
                        pltpu.VMEM((N_DSA_HEADS, LANES, blk), F32),
                        pltpu.VMEM((BF16_SUBLANES, LANES), F32),
                        pltpu.VMEM((n_maps, 1, blk), F32),
                        pltpu.VMEM((n_maps, 1, blk), F32),
                        pltpu.VMEM((n_maps, DIFF_V_DIM, blk), F32),
                        pltpu.VMEM((BF16_SUBLANES, LANES), F32),
                        pltpu.VMEM((4, 1, blk), F32)],
        compiler_params=pltpu.CompilerParams(
            dimension_semantics=("parallel", "arbitrary"), vmem_limit_bytes=VMEM_LIMIT_BYTES),
        name="mixers",
    )(iq, ik2, iwt, aq, ak, avt, lam_vecs, dq, dk, dvt, subln_col)


def _out_mlp_kernel(x_ref, ma_ref, mb_ref, g1_ref, sh_ref, sc_ref, g2_ref, n2_ref, nf_ref,
                    woa_ref, wob_ref, w1_ref, w2_ref, o_ref, *, ff_chunk):
    x = x_ref[0]
    o = _dot_tn(ma_ref[0], woa_ref[...]) + _dot_tn(mb_ref[0], wob_ref[...])
    x1 = x + g1_ref[0] * o
    ms = jnp.mean(x1 * x1, axis=-1, keepdims=True)
    h = x1 * lax.rsqrt(ms + NORM_EPS) * n2_ref[...]
    hb = (h * (1.0 + sc_ref[0]) + sh_ref[0]).astype(BF16)
    d_ff = w1_ref.shape[1]
    ff = jnp.zeros_like(x)
    for j in range(d_ff // ff_chunk):
        u = jnp.maximum(_dot(hb, w1_ref[:, j * ff_chunk:(j + 1) * ff_chunk]), 0.0)
        ff = ff + _dot((u * u).astype(BF16), w2_ref[j * ff_chunk:(j + 1) * ff_chunk, :])
    x2 = x1 + g2_ref[0] * ff
    ms2 = jnp.mean(x2 * x2, axis=-1, keepdims=True)
    o_ref[0] = x2 * lax.rsqrt(ms2 + NORM_EPS) * nf_ref[...]


def _out_mlp(x, mix_a, mix_b, g1, sh2, sc2, g2, n2, nf, wo_a, wo_b, w1, w2, *, tile):
    bsz, seq, d = x.shape
    tok = lambda b, i: (b, i, 0)
    per_b = lambda b, i: (b, 0, 0)
    mod_spec = pl.BlockSpec((1, 1, d), per_b)
    return pl.pallas_call(
        functools.partial(_out_mlp_kernel, ff_chunk=1024),
        grid=(bsz, seq // tile),
        in_specs=[pl.BlockSpec((1, tile, d), tok),
                  pl.BlockSpec((1, 512, tile), lambda b, i: (b, 0, i)),
                  pl.BlockSpec((1, 512, tile), lambda b, i: (b, 0, i)),
                  mod_spec, mod_spec, mod_spec, mod_spec,
                  _const_spec((1, d)), _const_spec((1, d)),
                  _const_spec(wo_a.shape), _const_spec(wo_b.shape),
                  _const_spec(w1.shape), _const_spec(w2.shape)],
        out_specs=pl.BlockSpec((1, tile, d), tok),
        out_shape=jax.ShapeDtypeStruct((bsz, seq, d), F32),
        compiler_params=pltpu.CompilerParams(
            dimension_semantics=("parallel", "parallel"), vmem_limit_bytes=VMEM_LIMIT_BYTES),
        name="out_mlp",
    )(x, mix_a, mix_b, g1, sh2, sc2, g2, n2, nf, wo_a, wo_b, w1, w2)


def _pad_cols(w, width):
    return jnp.pad(w, ((0, 0), (0, width - w.shape[1])))


def kernel(x, c, positions, w_ada, b_ada, norm1_w, w_in, idx_k_ln_w, idx_k_ln_b, lambda_q1, lambda_k1,
           lambda_q2, lambda_k2, subln_w, w_out, norm2_w, w_ff1, w_ff2, norm_f_w):
    bsz, seq, d = x.shape
    depth = w_ada.shape[0]
    assert depth == 1, "the fused final RMSNorm assumes a single layer"
    half = HEAD_DIM // 2
    inv_freq = ROPE_THETA ** (-jnp.arange(half, dtype=F32) / half)
    invf = jnp.tile(inv_freq, LANES // half).reshape(1, LANES)
    pos3 = positions.reshape(bsz, seq, 1)
    blk = min(ATT_BLOCK, seq)
    tile = min(512, seq)

    for l in range(depth):
        mod = _adaln(c, w_ada[l], b_ada[l])
        sh1, sc1, g1, sh2, sc2, g2 = [m.reshape(bsz, 1, d) for m in jnp.split(mod, 6, axis=-1)]

        w = w_in[l]
        wm = jnp.concatenate([w[:, :1024], w[:, 1536:2048], w[:, 2120:3144]], axis=1).astype(BF16)
        ws = _pad_cols(w[:, 2048:2112], LANES).astype(BF16)
        wwt = jnp.pad(w[:, 2112:2120].T, ((0, BF16_SUBLANES - N_IDX_HEADS), (0, 0))).astype(BF16)
        wavt = w[:, 1024:1536].T.astype(BF16)
        wdvt = w[:, 3144:3656].T.astype(BF16)
        lnw = _pad_cols(idx_k_ln_w[l].reshape(1, IDX_DIM), LANES)
        lnb = _pad_cols(idx_k_ln_b[l].reshape(1, IDX_DIM), LANES)

        aq, ak, avt, iq, ik2, iwt, dq, dk, dvt = _in_proj(
            x, pos3, sh1, sc1, norm1_w[l].reshape(1, d), invf, wm, ws, wwt, wavt, wdvt, lnw, lnb,
            tile=tile, chunk=blk)

        lam_vecs = jnp.stack([lambda_q1[l], lambda_k1[l], lambda_q2[l], lambda_k2[l]]).astype(F32)
        lam_init = 0.8 - 0.6 * math.exp(-0.3 * l)
        out_a, out_b = _mixers(iq, ik2, iwt, aq, ak, avt, lam_vecs, dq, dk, dvt,
                               subln_w[l].reshape(DIFF_V_DIM, 1), blk=blk, lam_init=lam_init)

        wo = w_out[l].astype(BF16)
        x = _out_mlp(x, out_a, out_b, g1, sh2, sc2, g2, norm2_w[l].reshape(1, d), norm_f_w.reshape(1, d),
                     wo[:512], wo[512:], w_ff1[l].astype(BF16), w_ff2[l].astype(BF16), tile=tile)
    return x
```

```python
import functools
import math

import jax
import jax.numpy as jnp
from jax import lax
from jax.experimental import pallas as pl
from jax.experimental.pallas import tpu as pltpu

F32 = jnp.float32
BF16 = jnp.bfloat16

HEAD_DIM = 64
N_DSA_HEADS = 8
N_IDX_HEADS = 8
IDX_DIM = 64
N_DIFF_HEADS = 4
DIFF_V_DIM = 128
TOPK_MAX = 256
ROPE_THETA = 10000.0
NORM_EPS = 1e-6
LN_EPS = 1e-5
LANES = 128
F32_SUBLANES = 8
BF16_SUBLANES = 16
ATT_BLOCK = 256
MASK_VALUE = -1e30
F32_MAX = float(jnp.finfo(jnp.float32).max)
MAX_BISECT_STEPS = 512
BISECT_FIRST_STEPS = 16
BISECT_FUSED_STEPS = 16
BISECT_STEPS_PER_TEST = 2
FUSED_COUNTS_UNDER_QK = 2
MIN_SAFE_NORMALISER = 2.0 ** -100
VMEM_LIMIT_BYTES = 56 * 1024 * 1024


def _dot(a, b):
    return jnp.dot(a, b, preferred_element_type=F32)


def _dot_nt(a, b):
    return lax.dot_general(a, b, (((1,), (1,)), ((), ())), preferred_element_type=F32)


def _dot_tn(a, b):
    return lax.dot_general(a, b, (((0,), (0,)), ((), ())), preferred_element_type=F32)


def _const_spec(shape):
    zeros = (0,) * len(shape)
    return pl.BlockSpec(shape, lambda *_: zeros, pipeline_mode=pl.Buffered(1))


def _adaln_kernel(c_ref, w_ref, b_ref, o_ref):
    c = c_ref[...]
    s = c / (1.0 + jnp.exp(-c))
    o_ref[...] = jnp.dot(s, w_ref[...], preferred_element_type=F32,
                         precision=lax.Precision.HIGHEST) + b_ref[...]


def _adaln(c, w, b):
    bsz, d = c.shape
    n = w.shape[1]
    tn = 1536
    return pl.pallas_call(
        _adaln_kernel,
        grid=(n // tn,),
        in_specs=[pl.BlockSpec((bsz, d), lambda j: (0, 0)),
                  pl.BlockSpec((d, tn), lambda j: (0, j)),
                  pl.BlockSpec((1, tn), lambda j: (0, j))],
        out_specs=pl.BlockSpec((bsz, tn), lambda j: (0, j)),
        out_shape=jax.ShapeDtypeStruct((bsz, n), F32),
        compiler_params=pltpu.CompilerParams(vmem_limit_bytes=VMEM_LIMIT_BYTES),
        name="adaln",
    )(c, w, b.reshape(1, n))


def _rope_group(x, cos, sin_signed, first_half):
    nxt = pltpu.roll(x, LANES - HEAD_DIM // 2, axis=1)
    prv = pltpu.roll(x, HEAD_DIM // 2, axis=1)
    return x * cos + jnp.where(first_half, nxt, prv) * sin_signed


def _in_proj_kernel(x_ref, pos_ref, sh_ref, sc_ref, nw_ref, invf_ref, wm_ref, ws_ref, wwt_ref,
                    wavt_ref, wdvt_ref, lnw_ref, lnb_ref,
                    aq_ref, ak_ref, avt_ref, iq_ref, ik_ref, iwt_ref, dq_ref, dk_ref, dvt_ref):
    x = x_ref[0]
    ms = jnp.mean(x * x, axis=-1, keepdims=True)
    h = x * lax.rsqrt(ms + NORM_EPS) * nw_ref[...]
    h = h * (1.0 + sc_ref[0]) + sh_ref[0]
    hb = h.astype(BF16)

    lane = lax.broadcasted_iota(jnp.int32, (1, LANES), 1)
    first_half = (lane % HEAD_DIM) < (HEAD_DIM // 2)
    ang = pos_ref[0].astype(F32) * invf_ref[...]
    cos = jnp.cos(ang)
    sin = jnp.sin(ang)
    sin_signed = jnp.where(first_half, -sin, sin)

    def roped(col0, out_ref, scale):
        wide = _dot(hb, wm_ref[:, col0:col0 + 4 * LANES])
        for g in range(4):
            y = _rope_group(wide[:, g * LANES:(g + 1) * LANES], cos, sin_signed, first_half)
            if scale != 1.0:
                y = y * scale
            out_ref[0, :, g * LANES:(g + 1) * LANES] = y.astype(out_ref.dtype)

    qscale = HEAD_DIM ** -0.5 * math.log2(math.e)
    roped(0, aq_ref, qscale)
    roped(512, ak_ref, 1.0)
    roped(1024, iq_ref, 1.0)
    roped(1536, dq_ref, qscale)
    roped(2048, dk_ref, 1.0)

    n_sub = avt_ref.shape[1]
    chunk = avt_ref.shape[3]
    for j in range(n_sub):
        hj = hb[j * chunk:(j + 1) * chunk]
        avt_ref[0, j] = _dot_nt(wavt_ref[...], hj).astype(avt_ref.dtype)
        dvt_ref[0, j] = _dot_nt(wdvt_ref[...], hj).astype(dvt_ref.dtype)

    sm = _dot(hb, ws_ref[...])
    lo_half = lane < IDX_DIM
    mu = jnp.sum(sm, axis=-1, keepdims=True) * (1.0 / IDX_DIM)
    d = jnp.where(lo_half, sm - mu, 0.0)
    var = jnp.sum(d * d, axis=-1, keepdims=True) * (1.0 / IDX_DIM)
    y = d * lax.rsqrt(var + LN_EPS) * lnw_ref[...] + lnb_ref[...]
    y = y + pltpu.roll(y, IDX_DIM, axis=1)
    ik_ref[0] = _rope_group(y, cos, sin_signed, first_half).astype(ik_ref.dtype)

    wt = _dot_nt(wwt_ref[...], hb)
    iwt_ref[0] = wt[:N_IDX_HEADS] * ((N_IDX_HEADS ** -0.5) * (IDX_DIM ** -0.5))


def _in_proj(x, pos3, sh1, sc1, nw, invf, wm, ws, wwt, wavt, wdvt, lnw, lnb, *, tile, chunk):
    bsz, seq, d = x.shape
    tok = lambda b, i: (b, i, 0)
    per_b = lambda b, i: (b, 0, 0)
    wide = jax.ShapeDtypeStruct((bsz, seq, 512), BF16)
    wide_t = jax.ShapeDtypeStruct((bsz, seq // chunk, 512, chunk), BF16)
    wide_spec = pl.BlockSpec((1, tile, 512), tok)
    wide_t_spec = pl.BlockSpec((1, tile // chunk, 512, chunk), lambda b, i: (b, i, 0, 0))
    out_shape = [wide, wide, wide_t, wide,
                 jax.ShapeDtypeStruct((bsz, seq, LANES), BF16),
                 jax.ShapeDtypeStruct((bsz, N_IDX_HEADS, seq), F32),
                 wide, wide, wide_t]
    out_specs = [wide_spec, wide_spec, wide_t_spec, wide_spec,
                 pl.BlockSpec((1, tile, LANES), tok),
                 pl.BlockSpec((1, N_IDX_HEADS, tile), lambda b, i: (b, 0, i)),
                 wide_spec, wide_spec, wide_t_spec]
    return pl.pallas_call(
        _in_proj_kernel,
        grid=(bsz, seq // tile),
        in_specs=[pl.BlockSpec((1, tile, d), tok),
                  pl.BlockSpec((1, tile, 1), tok),
                  pl.BlockSpec((1, 1, d), per_b),
                  pl.BlockSpec((1, 1, d), per_b),
                  _const_spec((1, d)),
                  _const_spec((1, LANES)),
                  _const_spec(wm.shape),
                  _const_spec(ws.shape),
                  _const_spec(wwt.shape),
                  _const_spec(wavt.shape),
                  _const_spec(wdvt.shape),
                  _const_spec((1, LANES)),
                  _const_spec((1, LANES))],
        out_specs=out_specs,
        out_shape=out_shape,
        compiler_params=pltpu.CompilerParams(
            dimension_semantics=("parallel", "parallel"), vmem_limit_bytes=VMEM_LIMIT_BYTES),
        name="in_proj",
    )(x, pos3, sh1, sc1, nw, invf, wm, ws, wwt, wavt, wdvt, lnw, lnb)


def _head_q(q_all, h, lane_lo):
    pair = q_all[:, (h // 2) * LANES:(h // 2 + 1) * LANES]
    keep = lane_lo if h % 2 == 0 else jnp.logical_not(lane_lo)
    return jnp.where(keep, pair, jnp.zeros_like(pair))


def _softmax_update(s_list, vt_list, m_ref, l_ref, acc_ref):
    n = len(s_list)
    dv = vt_list[0].shape[0]
    ones = jnp.ones((BF16_SUBLANES, vt_list[0].shape[1]), BF16)
    m_old = [m_ref[g] for g in range(n)]
    m_new = [jnp.maximum(m_old[g], jnp.max(s_list[g], axis=0, keepdims=True)) for g in range(n)]
    p_list = [jnp.exp2(s_list[g] - m_new[g]).astype(BF16) for g in range(n)]
    alpha = [jnp.exp2(m_old[g] - m_new[g]) for g in range(n)]
    pv = [_dot(jnp.concatenate([vt_list[g], ones], axis=0), p_list[g]) for g in range(n)]
    for g in range(n):
        m_ref[g] = m_new[g]
        l_ref[g] = alpha[g] * l_ref[g] + pv[g][dv:dv + 1]
        acc_ref[g] = alpha[g] * acc_ref[g] + pv[g][:dv]


def _init_softmax_state(m_ref, l_ref, acc_ref):
    m_ref[...] = jnp.full(m_ref.shape, MASK_VALUE, F32)
    l_ref[...] = jnp.zeros(l_ref.shape, F32)
    acc_ref[...] = jnp.zeros(acc_ref.shape, F32)


def _bounded_update(s_list, vt_list, l_ref, acc_ref, before_last=None):
    n = len(s_list)
    dv = vt_list[0].shape[0]
    ones = jnp.ones((BF16_SUBLANES, vt_list[0].shape[1]), BF16)
    p_list = [jnp.exp2(s_list[g]).astype(BF16) for g in range(n)]
    pv = [_dot(jnp.concatenate([vt_list[g], ones], axis=0), p_list[g]) for g in range(n)]
    for g in range(n):
        col_sum = pv[g][dv:dv + 1]
        if before_last is not None and g == n - 1:
            col_sum = col_sum + before_last
        l_ref[g] = l_ref[g] + col_sum
        acc_ref[g] = acc_ref[g] + pv[g][:dv]


def _head_selector(n_lanes):
    r = lax.broadcasted_iota(jnp.int32, (BF16_SUBLANES, n_lanes), 0)
    l = lax.broadcasted_iota(jnp.int32, (BF16_SUBLANES, n_lanes), 1)
    return jnp.where(l // HEAD_DIM == r, 1.0, 0.0).astype(BF16)


def _head_sq_norms(x, head_sel):
    return _dot_nt(head_sel, x * x)


def _max_key_sq_norms(k_ref, kmax_ref, rows_per_step):
    n_lanes = k_ref.shape[2]
    l = lax.broadcasted_iota(jnp.int32, (n_lanes, LANES), 0)
    h = lax.broadcasted_iota(jnp.int32, (n_lanes, LANES), 1)
    sel_t = jnp.where(l // HEAD_DIM == h, 1.0, 0.0).astype(BF16)

    def body(c, mx):
        start = pl.multiple_of(c * rows_per_step, rows_per_step)
        kk = k_ref[0, pl.ds(start, rows_per_step), :]
        return jnp.maximum(mx, jnp.max(_dot(kk * kk, sel_t), axis=0, keepdims=True))
    row = lax.fori_loop(0, k_ref.shape[1] // rows_per_step, body, jnp.zeros((1, LANES), F32))
    r = lax.broadcasted_iota(jnp.int32, kmax_ref.shape, 0)
    c = lax.broadcasted_iota(jnp.int32, kmax_ref.shape, 1)
    col = jnp.sum(jnp.where(r == c, row, 0.0), axis=1, keepdims=True)
    kmax_ref[...] = jnp.broadcast_to(col, kmax_ref.shape)


def _softmax_underflowed(l_ref):
    return jnp.logical_not(jnp.min(l_ref[...]) >= MIN_SAFE_NORMALISER)


def _mixer_kernel(iq_ref, ik_ref, iwt_ref, aq_ref, ak_ref, avt_ref, lam_ref, dq_ref, dk_ref, dvt_ref, sw_ref,
                  oa_ref, ob_ref,
                  score_ref, m_ref, l_ref, acc_ref, kmax_ref, dm_ref, dl_ref, dacc_ref, dkmax_ref, bis_ref,
                  *, blk, top_k, lam_init):
    i = pl.program_id(1)
    n_chunks = i + 1
    n_maps = 2 * N_DIFF_HEADS
    lane = lax.broadcasted_iota(jnp.int32, (1, LANES), 1)
    lane_lo = lane < HEAD_DIM
    qcol = lax.broadcasted_iota(jnp.int32, (1, blk), 1)
    qpos = i * blk + qcol
    krow = lax.broadcasted_iota(jnp.int32, (blk, 1), 0)
    kf = float(top_k)

    aq_all = aq_ref[0]
    dq_all = dq_ref[0]
    head_sel = _head_selector(aq_all.shape[1])

    @pl.when(i == 0)
    def _():
        rows_per_step = math.gcd(4 * blk, ak_ref.shape[1])
        _max_key_sq_norms(ak_ref, kmax_ref, rows_per_step)
        _max_key_sq_norms(dk_ref, dkmax_ref, rows_per_step)

    d_maps = [_head_q(dq_all, g, lane_lo) for g in range(n_maps)]
    d_bound = jnp.sqrt(_head_sq_norms(dq_all, head_sel) * dkmax_ref[:, :1])
    d_shifts = [d_bound[g:g + 1] for g in range(n_maps)]
    a_bound = jnp.sqrt(_head_sq_norms(aq_all, head_sel) * kmax_ref[:, :1])

    def diff_chunk(c, diagonal, bounded, after=None, before_last=None):
        start = pl.multiple_of(c * blk, blk)
        if diagonal:
            bias = jnp.where(krow <= qcol, 0.0, MASK_VALUE)
        s_list, vt_list = [], []
        for h in range(N_DIFF_HEADS):
            cols = slice(h * LANES, (h + 1) * LANES)
            kk = dk_ref[0, pl.ds(start, blk), cols]
            for comp in range(2):
                g = 2 * h + comp
                s = _dot_nt(kk, d_maps[g])
                if bounded:
                    s = s - (d_shifts[g] if after is None else d_shifts[g] + after)
                s_list.append(s + bias if diagonal else s)
                vt_list.append(dvt_ref[0, c, cols, :])
        if bounded:
            _bounded_update(s_list, vt_list, dl_ref, dacc_ref, before_last)
        else:
            _softmax_update(s_list, vt_list, dm_ref, dl_ref, dacc_ref)

    _init_softmax_state(dm_ref, dl_ref, dacc_ref)

    iq_all = iq_ref[0]
    wt = iwt_ref[0]
    q_heads = [_head_q(iq_all, h, lane_lo) for h in range(N_IDX_HEADS)]
    w_rows = [wt[h:h + 1, :] for h in range(N_IDX_HEADS)]

    n_pairs = (n_chunks + 1) // 2

    def pair_sum(hit):
        m = jnp.where(hit, 1.0, 0.0)
        return jnp.sum((m[0] + m[1]).reshape(blk // F32_SUBLANES, F32_SUBLANES, blk), axis=0)

    zero8 = jnp.zeros((F32_SUBLANES, blk), F32)

    def score_pair(j, mn, mx, w_first):
        for u in range(2):
            c = 2 * j + u
            start = pl.multiple_of(c * blk, blk)
            kk = ik_ref[0, pl.ds(start, blk), :]
            acc = jnp.zeros((blk, blk), F32)
            for h in range(N_IDX_HEADS):
                acc = acc + jnp.maximum(_dot_nt(kk, q_heads[h]), 0.0) * (w_first if h == 0 else w_rows[h])
            causal = (c * blk + krow) <= qpos
            sc = jnp.where(causal, acc, -jnp.inf)
            score_ref[c] = sc
            mx = jnp.maximum(mx, jnp.max(sc, axis=0, keepdims=True))
            mn = jnp.minimum(mn, jnp.min(jnp.where(causal, acc, jnp.inf), axis=0, keepdims=True))
        return mn, mx

    def zero_counts(j):
        s = score_ref[pl.ds(2 * j, 2)]
        return pair_sum(s >= 0.0), pair_sum(s > 0.0)

    def score_body(j, carry):
        mn, mx, ge, gt = carry
        ge_p, gt_p = zero_counts(j - 1)
        after = jnp.sum(ge_p + gt_p, axis=0, keepdims=True) * 0.0
        mn, mx = score_pair(j, mn, mx, w_rows[0] + after)
        return mn, mx, ge + ge_p, gt + gt_p

    mn, mx = score_pair(0, jnp.full((1, blk), jnp.inf, F32), jnp.full((1, blk), -jnp.inf, F32), w_rows[0])
    mn, mx, ge8, gt8 = lax.fori_loop(1, n_pairs, score_body, (mn, mx, zero8, zero8))
    ge_p, gt_p = zero_counts(n_pairs - 1)
    ge0 = jnp.sum(ge8 + ge_p, axis=0, keepdims=True)
    gt0 = jnp.sum(gt8 + gt_p, axis=0, keepdims=True)


    def count_ge(cand):
        def body(j, acc):
            return acc + pair_sum(score_ref[pl.ds(2 * j, 2)] >= cand)
        return jnp.sum(lax.fori_loop(0, n_pairs, body, zero8), axis=0, keepdims=True)

    def count_ge_gt(cand):
        def body(j, carry):
            ge, gt = carry
            s = score_ref[pl.ds(2 * j, 2)]
            return ge + pair_sum(s >= cand), gt + pair_sum(s > cand)
        ge, gt = lax.fori_loop(0, n_pairs, body, (zero8, zero8))
        return jnp.sum(ge, axis=0, keepdims=True), jnp.sum(gt, axis=0, keepdims=True)

    n_valid = (qpos + 1).astype(F32)
    need = n_valid > kf
    kth_is_zero = jnp.logical_and(gt0 < kf, ge0 >= kf)
    nonneg = ge0 >= kf
    lo0 = jnp.where(nonneg, 0.0, mn)
    cnt0 = jnp.where(nonneg, ge0, n_valid)
    above_max = jnp.where(mx > 0.0, jnp.minimum(mx * 2.0, F32_MAX), 1.0)
    hi0 = jnp.where(nonneg, above_max, 0.0)
    done0 = jnp.logical_or(jnp.logical_not(need), jnp.logical_or(kth_is_zero, cnt0 == kf))
    active0 = jnp.where(done0, 0.0, 1.0)

    def midpoint(st):
        return 0.5 * st[0] + 0.5 * st[1]

    def halve(st, count):
        mid = midpoint(st)
        return narrow(st, mid, count(mid))

    def narrow(st, mid, cnt):
        lo, hi, cnt_lo, active = st
        conv = jnp.logical_or(mid <= lo, mid >= hi)
        upd = jnp.logical_and(active > 0.0, jnp.logical_not(conv))
        ge = cnt >= kf
        up_lo = jnp.logical_and(upd, ge)
        up_hi = jnp.logical_and(upd, jnp.logical_not(ge))
        lo = jnp.where(up_lo, mid, lo)
        cnt_lo = jnp.where(up_lo, cnt, cnt_lo)
        hi = jnp.where(up_hi, mid, hi)
        finished = jnp.logical_or(conv, jnp.logical_and(up_lo, cnt == kf))
        return lo, hi, cnt_lo, jnp.where(finished, 0.0, active)

    n_fused = jnp.minimum(i, BISECT_FUSED_STEPS)
    for k, v in enumerate((lo0, hi0, cnt0, active0)):
        bis_ref[k] = v

    for pairs in range(1, score_ref.shape[0] // 2 + 1):
        @pl.when(n_pairs == pairs)
        def _(pairs=pairs):
            def fused_step(c, st):
                mid = midpoint(st)
                n_first = min(FUSED_COUNTS_UNDER_QK, (pairs + 1) // 2)
                parts = []
                for js in (range(0, n_first), range(n_first, pairs)):
                    acc = zero8
                    for j in js:
                        acc = acc + pair_sum(score_ref[2 * j:2 * j + 2] >= mid)
                    parts.append(jnp.sum(acc, axis=0, keepdims=True))
                diff_chunk(c, False, True, after=parts[0] * 0.0,
                           before_last=parts[1] * 0.0 if pairs > n_first else None)
                return narrow(st, mid, parts[0] + parts[1])
            st = lax.fori_loop(0, n_fused, fused_step, tuple(bis_ref[k] for k in range(4)))
            for k in range(4):
                bis_ref[k] = st[k]

    st = tuple(bis_ref[k] for k in range(4))
    st = lax.fori_loop(0, jnp.maximum(BISECT_FIRST_STEPS - n_fused, 0),
                       lambda _, s: halve(s, count_ge), st)

    def bis_cond(st):
        return jnp.logical_and(st[4] > 0.0, st[5] < MAX_BISECT_STEPS)

    def bis_body(st):
        lo, hi, cnt_lo, active = lax.fori_loop(0, BISECT_STEPS_PER_TEST, lambda _, s: halve(s, count_ge), st[:4])
        return lo, hi, cnt_lo, active, jnp.max(active), st[5] + BISECT_STEPS_PER_TEST

    lo, _, cnt_lo, _, _, _ = lax.while_loop(bis_cond, bis_body, st + (jnp.max(st[3]), jnp.int32(0)))
    thr = jnp.where(need, lo, -F32_MAX)

    excess = jnp.logical_and(need, cnt_lo > kf)

    @pl.when(jnp.max(jnp.where(excess, 1.0, 0.0)) > 0.0)
    def _():
        _, gt = count_ge_gt(thr)
        allow = kf - gt
        r_i = lax.broadcasted_iota(jnp.int32, (blk, blk), 0)
        c_i = lax.broadcasted_iota(jnp.int32, (blk, blk), 1)
        earlier = jnp.where(c_i < r_i, 1.0, 0.0).astype(BF16)

        def tie_body(c, seen):
            s = score_ref[c]
            eq = jnp.logical_and(s == thr, excess)
            eqf = jnp.where(eq, 1.0, 0.0)
            rank = seen + _dot(earlier, eqf.astype(BF16))
            drop = jnp.logical_and(eq, rank >= allow)
            score_ref[c] = jnp.where(drop, -jnp.inf, s)
            return seen + jnp.sum(eqf, axis=0, keepdims=True)

        lax.fori_loop(0, n_chunks, tie_body, jnp.zeros((1, blk), F32))

    qa_heads = [_head_q(aq_all, h, lane_lo) for h in range(N_DSA_HEADS)]

    def attend(bounded, shift):
        def att_body(c, carry):
            start = pl.multiple_of(c * blk, blk)
            bias = jnp.where(score_ref[c] >= thr, shift, MASK_VALUE)
            s_list, vt_list = [], []
            for h in range(N_DSA_HEADS):
                pair = slice((h // 2) * LANES, (h // 2 + 1) * LANES)
                kk = ak_ref[0, pl.ds(start, blk), pair]
                s_list.append(_dot_nt(kk, qa_heads[h]) + bias)
                vt_list.append(avt_ref[0, c, pair, :])
            if bounded:
                _bounded_update(s_list, vt_list, l_ref, acc_ref)
            else:
                _softmax_update(s_list, vt_list, m_ref, l_ref, acc_ref)
            return carry

        _init_softmax_state(m_ref, l_ref, acc_ref)
        lax.fori_loop(0, n_chunks, att_body, 0)

    def diff_rest(c, carry):
        diff_chunk(c, False, True)
        return carry

    lax.fori_loop(n_fused, i, diff_rest, 0)

    attend(True, -jnp.max(a_bound, axis=0, keepdims=True))

    @pl.when(_softmax_underflowed(l_ref))
    def _():
        attend(False, 0.0)

    diff_chunk(i, True, True)
    for j in range(N_DSA_HEADS // 2):
        a = acc_ref[2 * j] * (1.0 / l_ref[2 * j])
        b = acc_ref[2 * j + 1] * (1.0 / l_ref[2 * j + 1])
        o_t = jnp.concatenate([a[:HEAD_DIM], b[HEAD_DIM:]], axis=0)
        oa_ref[0, j * LANES:(j + 1) * LANES, :] = o_t.astype(oa_ref.dtype)


    @pl.when(_softmax_underflowed(dl_ref))
    def _():
        def body(c, carry):
            diff_chunk(c, False, False)
            return carry
        _init_softmax_state(dm_ref, dl_ref, dacc_ref)
        lax.fori_loop(0, i, body, 0)
        diff_chunk(i, True, False)

    lam_vecs = lam_ref[...]
    lam = (jnp.exp(jnp.sum(lam_vecs[0:1] * lam_vecs[1:2], axis=-1, keepdims=True))
           - jnp.exp(jnp.sum(lam_vecs[2:3] * lam_vecs[3:4], axis=-1, keepdims=True)) + lam_init)
    for h in range(N_DIFF_HEADS):
        out = dacc_ref[2 * h] * (1.0 / dl_ref[2 * h]) - dacc_ref[2 * h + 1] * (lam / dl_ref[2 * h + 1])
        out = out * lax.rsqrt(jnp.mean(out * out, axis=0, keepdims=True) + NORM_EPS)
        out = out * sw_ref[...] * (1.0 - lam_init)
        ob_ref[0, h * LANES:(h + 1) * LANES, :] = out.astype(ob_ref.dtype)


def _mixers(iq, ik2, iwt, aq, ak, avt, lam_vecs, dq, dk, dvt, subln_col, *, blk, lam_init):
    bsz, seq, _ = aq.shape
    top_k = min(TOPK_MAX, seq // 4)
    n_chunks = seq // blk
    assert n_chunks % 2 == 0, "key chunks are walked in pairs"
    qblk = lambda b, i: (b, i, 0)
    whole = lambda b, i: (b, 0, 0)
    whole_t = lambda b, i: (b, 0, 0, 0)
    n_maps = 2 * N_DIFF_HEADS
    q_spec = pl.BlockSpec((1, blk, 512), qblk)
    kv_spec = pl.BlockSpec((1, seq, 512), whole)
    vt_spec = pl.BlockSpec((1, n_chunks, 512, blk), whole_t)
    return pl.pallas_call(
        functools.partial(_mixer_kernel, blk=blk, top_k=top_k, lam_init=lam_init),
        grid=(bsz, n_chunks),
        in_specs=[q_spec,
                  pl.BlockSpec((1, seq, LANES), whole),
                  pl.BlockSpec((1, N_IDX_HEADS, blk), lambda b, i: (b, 0, i)),
                  q_spec, kv_spec, vt_spec,
                  _const_spec(lam_vecs.shape),
                  q_spec, kv_spec, vt_spec,
                  _const_spec((DIFF_V_DIM, 1))],
        out_specs=[pl.BlockSpec((1, 512, blk), lambda b, i: (b, 0, i))] * 2,
        out_shape=[jax.ShapeDtypeStruct((bsz, 512, seq), BF16)] * 2,
        scratch_shapes=[pltpu.VMEM((n_chunks, blk, blk), F32),
                        pltpu.VMEM((N_DSA_HEADS, 1, blk), F32),
                        pltpu.VMEM((N_DSA_HEADS, 1, blk), F32),
                        pltpu.VMEM((N_DSA_HEADS, LANES, blk), F32),
                        pltpu.VMEM((BF16_SUBLANES, LANES), F32),
                        pltpu.VMEM((n_maps, 1, blk), F32),
                        pltpu.VMEM((n_maps, 1, blk), F32),
                        pltpu.VMEM((n_maps, DIFF_V_DIM, blk), F32),
                        pltpu.VMEM((BF16_SUBLANES, LANES), F32),
                        pltpu.VMEM((4, 1, blk), F32)],
        compiler_params=pltpu.CompilerParams(
            dimension_semantics=("parallel", "arbitrary"), vmem_limit_bytes=VMEM_LIMIT_BYTES),
        name="mixers",
    )(iq, ik2, iwt, aq, ak, avt, lam_vecs, dq, dk, dvt, subln_col)


def _out_mlp_kernel(x_ref, ma_ref, mb_ref, g1_ref, sh_ref, sc_ref, g2_ref, n2_ref, nf_ref,
                    woa_ref, wob_ref, w1_ref, w2_ref, o_ref, *, ff_chunk):
    x = x_ref[0]
    o = _dot_tn(ma_ref[0], woa_ref[...]) + _dot_tn(mb_ref[0], wob_ref[...])
    x1 = x + g1_ref[0] * o
    ms = jnp.mean(x1 * x1, axis=-1, keepdims=True)
    h = x1 * lax.rsqrt(ms + NORM_EPS) * n2_ref[...]
    hb = (h * (1.0 + sc_ref[0]) + sh_ref[0]).astype(BF16)
    d_ff = w1_ref.shape[1]
    ff = jnp.zeros_like(x)
    for j in range(d_ff // ff_chunk):
        u = jnp.maximum(_dot(hb, w1_ref[:, j * ff_chunk:(j + 1) * ff_chunk]), 0.0)
        ff = ff + _dot((u * u).astype(BF16), w2_ref[j * ff_chunk:(j + 1) * ff_chunk, :])
    x2 = x1 + g2_ref[0] * ff
    ms2 = jnp.mean(x2 * x2, axis=-1, keepdims=True)
    o_ref[0] = x2 * lax.rsqrt(ms2 + NORM_EPS) * nf_ref[...]


def _out_mlp(x, mix_a, mix_b, g1, sh2, sc2, g2, n2, nf, wo_a, wo_b, w1, w2, *, tile):
    bsz, seq, d = x.shape
    tok = lambda b, i: (b, i, 0)
    per_b = lambda b, i: (b, 0, 0)
    mod_spec = pl.BlockSpec((1, 1, d), per_b)
    return pl.pallas_call(
        functools.partial(_out_mlp_kernel, ff_chunk=1024),
        grid=(bsz, seq // tile),
        in_specs=[pl.BlockSpec((1, tile, d), tok),
                  pl.BlockSpec((1, 512, tile), lambda b, i: (b, 0, i)),
                  pl.BlockSpec((1, 512, tile), lambda b, i: (b, 0, i)),
                  mod_spec, mod_spec, mod_spec, mod_spec,
                  _const_spec((1, d)), _const_spec((1, d)),
                  _const_spec(wo_a.shape), _const_spec(wo_b.shape),
                  _const_spec(w1.shape), _const_spec(w2.shape)],
        out_specs=pl.BlockSpec((1, tile, d), tok),
        out_shape=jax.ShapeDtypeStruct((bsz, seq, d), F32),
        compiler_params=pltpu.CompilerParams(
            dimension_semantics=("parallel", "parallel"), vmem_limit_bytes=VMEM_LIMIT_BYTES),
        name="out_mlp",
    )(x, mix_a, mix_b, g1, sh2, sc2, g2, n2, nf, wo_a, wo_b, w1, w2)


def _pad_cols(w, width):
    return jnp.pad(w, ((0, 0), (0, width - w.shape[1])))


def kernel(x, c, positions, w_ada, b_ada, norm1_w, w_in, idx_k_ln_w, idx_k_ln_b, lambda_q1, lambda_k1,
           lambda_q2, lambda_k2, subln_w, w_out, norm2_w, w_ff1, w_ff2, norm_f_w):
    bsz, seq, d = x.shape
    depth = w_ada.shape[0]
    assert depth == 1, "the fused final RMSNorm assumes a single layer"
    half = HEAD_DIM // 2
    inv_freq = ROPE_THETA ** (-jnp.arange(half, dtype=F32) / half)
    invf = jnp.tile(inv_freq, LANES // half).reshape(1, LANES)
    pos3 = positions.reshape(bsz, seq, 1)
    blk = min(ATT_BLOCK, seq)
    tile = min(512, seq)

    for l in range(depth):
        mod = _adaln(c, w_ada[l], b_ada[l])
        sh1, sc1, g1, sh2, sc2, g2 = [m.reshape(bsz, 1, d) for m in jnp.split(mod, 6, axis=-1)]

        w = w_in[l]
        wm = jnp.concatenate([w[:, :1024], w[:, 1536:2048], w[:, 2120:3144]], axis=1).astype(BF16)
        ws = _pad_cols(w[:, 2048:2112], LANES).astype(BF16)
        wwt = jnp.pad(w[:, 2112:2120].T, ((0, BF16_SUBLANES - N_IDX_HEADS), (0, 0))).astype(BF16)
        wavt = w[:, 1024:1536].T.astype(BF16)
        wdvt = w[:, 3144:3656].T.astype(BF16)
        lnw = _pad_cols(idx_k_ln_w[l].reshape(1, IDX_DIM), LANES)
        lnb = _pad_cols(idx_k_ln_b[l].reshape(1, IDX_DIM), LANES)

        aq, ak, avt, iq, ik2, iwt, dq, dk, dvt = _in_proj(
            x, pos3, sh1, sc1, norm1_w[l].reshape(1, d), invf, wm, ws, wwt, wavt, wdvt, lnw, lnb,
            tile=tile, chunk=blk)

        lam_vecs = jnp.stack([lambda_q1[l], lambda_k1[l], lambda_q2[l], lambda_k2[l]]).astype(F32)
        lam_init = 0.8 - 0.6 * math.exp(-0.3 * l)
        out_a, out_b = _mixers(iq, ik2, iwt, aq, ak, avt, lam_vecs, dq, dk, dvt,
                               subln_w[l].reshape(DIFF_V_DIM, 1), blk=blk, lam_init=lam_init)

        wo = w_out[l].astype(BF16)
        x = _out_mlp(x, out_a, out_b, g1, sh2, sc2, g2, norm2_w[l].reshape(1, d), norm_f_w.reshape(1, d),
                     wo[:512], wo[512:], w_ff1[l].astype(BF16), w_ff2[l].astype(BF16), tile=tile)
    return x
```

```python
import functools
import math

import jax
import jax.numpy as jnp
from jax import lax
from jax.experimental import pallas as pl
from jax.experimental.pallas import tpu as pltpu

F32 = jnp.float32
BF16 = jnp.bfloat16

HEAD_DIM = 64
N_DSA_HEADS = 8
N_IDX_HEADS = 8
IDX_DIM = 64
N_DIFF_HEADS = 4
DIFF_V_DIM = 128
TOPK_MAX = 256
ROPE_THETA = 10000.0
NORM_EPS = 1e-6
LN_EPS = 1e-5
LANES = 128
F32_SUBLANES = 8
BF16_SUBLANES = 16
ATT_BLOCK = 256
MASK_VALUE = -1e30
F32_MAX = float(jnp.finfo(jnp.float32).max)
MAX_BISECT_STEPS = 512
BISECT_FIRST_STEPS = 16
BISECT_FUSED_STEPS = 16
BISECT_STEPS_PER_TEST = 2
FUSED_COUNTS_UNDER_QK = 2
MIN_SAFE_NORMALISER = 2.0 ** -100
VMEM_LIMIT_BYTES = 56 * 1024 * 1024


def _dot(a, b):
    return jnp.dot(a, b, preferred_element_type=F32)


def _dot_nt(a, b):
    return lax.dot_general(a, b, (((1,), (1,)), ((), ())), preferred_element_type=F32)


def _dot_tn(a, b):
    return lax.dot_general(a, b, (((0,), (0,)), ((), ())), preferred_element_type=F32)


def _const_spec(shape):
    zeros = (0,) * len(shape)
    return pl.BlockSpec(shape, lambda *_: zeros, pipeline_mode=pl.Buffered(1))


def _adaln_kernel(c_ref, w_ref, b_ref, o_ref):
    c = c_ref[...]
    s = c / (1.0 + jnp.exp(-c))
    o_ref[...] = jnp.dot(s, w_ref[...], preferred_element_type=F32,
                         precision=lax.Precision.HIGHEST) + b_ref[...]


def _adaln(c, w, b):
    bsz, d = c.shape
    n = w.shape[1]
    tn = 1536
    return pl.pallas_call(
        _adaln_kernel,
        grid=(n // tn,),
        in_specs=[pl.BlockSpec((bsz, d), lambda j: (0, 0)),
                  pl.BlockSpec((d, tn), lambda j: (0, j)),
                  pl.BlockSpec((1, tn), lambda j: (0, j))],
        out_specs=pl.BlockSpec((bsz, tn), lambda j: (0, j)),
        out_shape=jax.ShapeDtypeStruct((bsz, n), F32),
        compiler_params=pltpu.CompilerParams(vmem_limit_bytes=VMEM_LIMIT_BYTES),
        name="adaln",
    )(c, w, b.reshape(1, n))


def _rope_group(x, cos, sin_signed, first_half):
    nxt = pltpu.roll(x, LANES - HEAD_DIM // 2, axis=1)
    prv = pltpu.roll(x, HEAD_DIM // 2, axis=1)
    return x * cos + jnp.where(first_half, nxt, prv) * sin_signed


def _in_proj_kernel(x_ref, pos_ref, sh_ref, sc_ref, nw_ref, invf_ref, wm_ref, ws_ref, wwt_ref,
                    wavt_ref, wdvt_ref, lnw_ref, lnb_ref,
                    aq_ref, ak_ref, avt_ref, iq_ref, ik_ref, iwt_ref, dq_ref, dk_ref, dvt_ref):
    x = x_ref[0]
    ms = jnp.mean(x * x, axis=-1, keepdims=True)
    h = x * lax.rsqrt(ms + NORM_EPS) * nw_ref[...]
    h = h * (1.0 + sc_ref[0]) + sh_ref[0]
    hb = h.astype(BF16)

    lane = lax.broadcasted_iota(jnp.int32, (1, LANES), 1)
    first_half = (lane % HEAD_DIM) < (HEAD_DIM // 2)
    ang = pos_ref[0].astype(F32) * invf_ref[...]
    cos = jnp.cos(ang)
    sin = jnp.sin(ang)
    sin_signed = jnp.where(first_half, -sin, sin)

    sm = _dot(hb, ws_ref[...])
    lo_half = lane < IDX_DIM
    mu = jnp.sum(sm, axis=-1, keepdims=True) * (1.0 / IDX_DIM)
    d = jnp.where(lo_half, sm - mu, 0.0)
    var = jnp.sum(d * d, axis=-1, keepdims=True) * (1.0 / IDX_DIM)
    y = d * lax.rsqrt(var + LN_EPS) * lnw_ref[...] + lnb_ref[...]
    y = y + pltpu.roll(y, IDX_DIM, axis=1)
    ik_ref[0] = _rope_group(y, cos, sin_signed, first_half).astype(ik_ref.dtype)

    wt = _dot_nt(wwt_ref[...], hb)
    iwt_ref[0] = wt[:N_IDX_HEADS] * ((N_IDX_HEADS ** -0.5) * (IDX_DIM ** -0.5))

    def roped(col0, out_ref, scale):
        wide = _dot(hb, wm_ref[:, col0:col0 + 4 * LANES])
        for g in range(4):
            y = _rope_group(wide[:, g * LANES:(g + 1) * LANES], cos, sin_signed, first_half)
            if scale != 1.0:
                y = y * scale
            out_ref[0, :, g * LANES:(g + 1) * LANES] = y.astype(out_ref.dtype)

    qscale = HEAD_DIM ** -0.5 * math.log2(math.e)
    roped(0, aq_ref, qscale)
    roped(512, ak_ref, 1.0)
    roped(1024, iq_ref, 1.0)
    roped(1536, dq_ref, qscale)
    roped(2048, dk_ref, 1.0)

    n_sub = avt_ref.shape[1]
    chunk = avt_ref.shape[3]
    for j in range(n_sub):
        hj = hb[j * chunk:(j + 1) * chunk]
        avt_ref[0, j] = _dot_nt(wavt_ref[...], hj).astype(avt_ref.dtype)
        dvt_ref[0, j] = _dot_nt(wdvt_ref[...], hj).astype(dvt_ref.dtype)


def _in_proj(x, pos3, sh1, sc1, nw, invf, wm, ws, wwt, wavt, wdvt, lnw, lnb, *, tile, chunk):
    bsz, seq, d = x.shape
    tok = lambda b, i: (b, i, 0)
    per_b = lambda b, i: (b, 0, 0)
    wide = jax.ShapeDtypeStruct((bsz, seq, 512), BF16)
    wide_t = jax.ShapeDtypeStruct((bsz, seq // chunk, 512, chunk), BF16)
    wide_spec = pl.BlockSpec((1, tile, 512), tok)
    wide_t_spec = pl.BlockSpec((1, tile // chunk, 512, chunk), lambda b, i: (b, i, 0, 0))
    out_shape = [wide, wide, wide_t, wide,
                 jax.ShapeDtypeStruct((bsz, seq, LANES), BF16),
                 jax.ShapeDtypeStruct((bsz, N_IDX_HEADS, seq), F32),
                 wide, wide, wide_t]
    out_specs = [wide_spec, wide_spec, wide_t_spec, wide_spec,
                 pl.BlockSpec((1, tile, LANES), tok),
                 pl.BlockSpec((1, N_IDX_HEADS, tile), lambda b, i: (b, 0, i)),
                 wide_spec, wide_spec, wide_t_spec]
    return pl.pallas_call(
        _in_proj_kernel,
        grid=(bsz, seq // tile),
        in_specs=[pl.BlockSpec((1, tile, d), tok),
                  pl.BlockSpec((1, tile, 1), tok),
                  pl.BlockSpec((1, 1, d), per_b),
                  pl.BlockSpec((1, 1, d), per_b),
                  _const_spec((1, d)),
                  _const_spec((1, LANES)),
                  _const_spec(wm.shape),
                  _const_spec(ws.shape),
                  _const_spec(wwt.shape),
                  _const_spec(wavt.shape),
                  _const_spec(wdvt.shape),
                  _const_spec((1, LANES)),
                  _const_spec((1, LANES))],
        out_specs=out_specs,
        out_shape=out_shape,
        compiler_params=pltpu.CompilerParams(
            dimension_semantics=("parallel", "parallel"), vmem_limit_bytes=VMEM_LIMIT_BYTES),
        name="in_proj",
    )(x, pos3, sh1, sc1, nw, invf, wm, ws, wwt, wavt, wdvt, lnw, lnb)


def _head_q(q_all, h, lane_lo):
    pair = q_all[:, (h // 2) * LANES:(h // 2 + 1) * LANES]
    keep = lane_lo if h % 2 == 0 else jnp.logical_not(lane_lo)
    return jnp.where(keep, pair, jnp.zeros_like(pair))


def _softmax_update(s_list, vt_list, m_ref, l_ref, acc_ref):
    n = len(s_list)
    dv = vt_list[0].shape[0]
    ones = jnp.ones((BF16_SUBLANES, vt_list[0].shape[1]), BF16)
    m_old = [m_ref[g] for g in range(n)]
    m_new = [jnp.maximum(m_old[g], jnp.max(s_list[g], axis=0, keepdims=True)) for g in range(n)]
    p_list = [jnp.exp2(s_list[g] - m_new[g]).astype(BF16) for g in range(n)]
    alpha = [jnp.exp2(m_old[g] - m_new[g]) for g in range(n)]
    pv = [_dot(jnp.concatenate([vt_list[g], ones], axis=0), p_list[g]) for g in range(n)]
    for g in range(n):
        m_ref[g] = m_new[g]
        l_ref[g] = alpha[g] * l_ref[g] + pv[g][dv:dv + 1]
        acc_ref[g] = alpha[g] * acc_ref[g] + pv[g][:dv]


def _init_softmax_state(m_ref, l_ref, acc_ref):
    m_ref[...] = jnp.full(m_ref.shape, MASK_VALUE, F32)
    l_ref[...] = jnp.zeros(l_ref.shape, F32)
    acc_ref[...] = jnp.zeros(acc_ref.shape, F32)


def _bounded_update(s_list, vt_list, l_ref, acc_ref, before_last=None):
    n = len(s_list)
    dv = vt_list[0].shape[0]
    ones = jnp.ones((BF16_SUBLANES, vt_list[0].shape[1]), BF16)
    p_list = [jnp.exp2(s_list[g]).astype(BF16) for g in range(n)]
    pv = [_dot(jnp.concatenate([vt_list[g], ones], axis=0), p_list[g]) for g in range(n)]
    for g in range(n):
        col_sum = pv[g][dv:dv + 1]
        if before_last is not None and g == n - 1:
            col_sum = col_sum + before_last
        l_ref[g] = l_ref[g] + col_sum
        acc_ref[g] = acc_ref[g] + pv[g][:dv]


def _head_selector(n_lanes):
    r = lax.broadcasted_iota(jnp.int32, (BF16_SUBLANES, n_lanes), 0)
    l = lax.broadcasted_iota(jnp.int32, (BF16_SUBLANES, n_lanes), 1)
    return jnp.where(l // HEAD_DIM == r, 1.0, 0.0).astype(BF16)


def _head_sq_norms(x, head_sel):
    return _dot_nt(head_sel, x * x)


def _max_key_sq_norms(k_ref, kmax_ref, rows_per_step):
    n_lanes = k_ref.shape[2]
    l = lax.broadcasted_iota(jnp.int32, (n_lanes, LANES), 0)
    h = lax.broadcasted_iota(jnp.int32, (n_lanes, LANES), 1)
    sel_t = jnp.where(l // HEAD_DIM == h, 1.0, 0.0).astype(BF16)

    def body(c, mx):
        start = pl.multiple_of(c * rows_per_step, rows_per_step)
        kk = k_ref[0, pl.ds(start, rows_per_step), :]
        return jnp.maximum(mx, jnp.max(_dot(kk * kk, sel_t), axis=0, keepdims=True))
    row = lax.fori_loop(0, k_ref.shape[1] // rows_per_step, body, jnp.zeros((1, LANES), F32))
    r = lax.broadcasted_iota(jnp.int32, kmax_ref.shape, 0)
    c = lax.broadcasted_iota(jnp.int32, kmax_ref.shape, 1)
    col = jnp.sum(jnp.where(r == c, row, 0.0), axis=1, keepdims=True)
    kmax_ref[...] = jnp.broadcast_to(col, kmax_ref.shape)


def _softmax_underflowed(l_ref):
    return jnp.logical_not(jnp.min(l_ref[...]) >= MIN_SAFE_NORMALISER)


def _mixer_kernel(iq_ref, ik_ref, iwt_ref, aq_ref, ak_ref, avt_ref, lam_ref, dq_ref, dk_ref, dvt_ref, sw_ref,
                  oa_ref, ob_ref,
                  score_ref, m_ref, l_ref, acc_ref, kmax_ref, dm_ref, dl_ref, dacc_ref, dkmax_ref, bis_ref,
                  *, blk, top_k, lam_init):
    i = pl.program_id(1)
    n_chunks = i + 1
    n_maps = 2 * N_DIFF_HEADS
    lane = lax.broadcasted_iota(jnp.int32, (1, LANES), 1)
    lane_lo = lane < HEAD_DIM
    qcol = lax.broadcasted_iota(jnp.int32, (1, blk), 1)
    qpos = i * blk + qcol
    krow = lax.broadcasted_iota(jnp.int32, (blk, 1), 0)
    kf = float(top_k)

    aq_all = aq_ref[0]
    dq_all = dq_ref[0]
    head_sel = _head_selector(aq_all.shape[1])

    @pl.when(i == 0)
    def _():
        rows_per_step = math.gcd(4 * blk, ak_ref.shape[1])
        _max_key_sq_norms(ak_ref, kmax_ref, rows_per_step)
        _max_key_sq_norms(dk_ref, dkmax_ref, rows_per_step)

    d_maps = [_head_q(dq_all, g, lane_lo) for g in range(n_maps)]
    d_bound = jnp.sqrt(_head_sq_norms(dq_all, head_sel) * dkmax_ref[:, :1])
    d_shifts = [d_bound[g:g + 1] for g in range(n_maps)]
    a_bound = jnp.sqrt(_head_sq_norms(aq_all, head_sel) * kmax_ref[:, :1])

    def diff_chunk(c, diagonal, bounded, after=None, before_last=None):
        start = pl.multiple_of(c * blk, blk)
        if diagonal:
            bias = jnp.where(krow <= qcol, 0.0, MASK_VALUE)
        s_list, vt_list = [], []
        for h in range(N_DIFF_HEADS):
            cols = slice(h * LANES, (h + 1) * LANES)
            kk = dk_ref[0, pl.ds(start, blk), cols]
            for comp in range(2):
                g = 2 * h + comp
                s = _dot_nt(kk, d_maps[g])
                if bounded:
                    s = s - (d_shifts[g] if after is None else d_shifts[g] + after)
                s_list.append(s + bias if diagonal else s)
                vt_list.append(dvt_ref[0, c, cols, :])
        if bounded:
            _bounded_update(s_list, vt_list, dl_ref, dacc_ref, before_last)
        else:
            _softmax_update(s_list, vt_list, dm_ref, dl_ref, dacc_ref)

    _init_softmax_state(dm_ref, dl_ref, dacc_ref)

    iq_all = iq_ref[0]
    wt = iwt_ref[0]
    q_heads = [_head_q(iq_all, h, lane_lo) for h in range(N_IDX_HEADS)]
    w_rows = [wt[h:h + 1, :] for h in range(N_IDX_HEADS)]

    n_pairs = (n_chunks + 1) // 2

    def pair_sum(hit):
        m = jnp.where(hit, 1.0, 0.0)
        return jnp.sum((m[0] + m[1]).reshape(blk // F32_SUBLANES, F32_SUBLANES, blk), axis=0)

    zero8 = jnp.zeros((F32_SUBLANES, blk), F32)

    def score_pair(j, mn, mx, w_first):
        for u in range(2):
            c = 2 * j + u
            start = pl.multiple_of(c * blk, blk)
            kk = ik_ref[0, pl.ds(start, blk), :]
            acc = jnp.zeros((blk, blk), F32)
            for h in range(N_IDX_HEADS):
                acc = acc + jnp.maximum(_dot_nt(kk, q_heads[h]), 0.0) * (w_first if h == 0 else w_rows[h])
            causal = (c * blk + krow) <= qpos
            sc = jnp.where(causal, acc, -jnp.inf)
            score_ref[c] = sc
            mx = jnp.maximum(mx, jnp.max(sc, axis=0, keepdims=True))
            mn = jnp.minimum(mn, jnp.min(jnp.where(causal, acc, jnp.inf), axis=0, keepdims=True))
        return mn, mx

    def zero_counts(j):
        s = score_ref[pl.ds(2 * j, 2)]
        return pair_sum(s >= 0.0), pair_sum(s > 0.0)

    def score_body(j, carry):
        mn, mx, ge, gt = carry
        ge_p, gt_p = zero_counts(j - 1)
        after = jnp.sum(ge_p + gt_p, axis=0, keepdims=True) * 0.0
        mn, mx = score_pair(j, mn, mx, w_rows[0] + after)
        return mn, mx, ge + ge_p, gt + gt_p

    mn, mx = score_pair(0, jnp.full((1, blk), jnp.inf, F32), jnp.full((1, blk), -jnp.inf, F32), w_rows[0])
    mn, mx, ge8, gt8 = lax.fori_loop(1, n_pairs, score_body, (mn, mx, zero8, zero8))
    ge_p, gt_p = zero_counts(n_pairs - 1)
    ge0 = jnp.sum(ge8 + ge_p, axis=0, keepdims=True)
    gt0 = jnp.sum(gt8 + gt_p, axis=0, keepdims=True)


    def count_ge(cand):
        def body(j, acc):
            return acc + pair_sum(score_ref[pl.ds(2 * j, 2)] >= cand)
        return jnp.sum(lax.fori_loop(0, n_pairs, body, zero8), axis=0, keepdims=True)

    def count_ge_gt(cand):
        def body(j, carry):
            ge, gt = carry
            s = score_ref[pl.ds(2 * j, 2)]
            return ge + pair_sum(s >= cand), gt + pair_sum(s > cand)
        ge, gt = lax.fori_loop(0, n_pairs, body, (zero8, zero8))
        return jnp.sum(ge, axis=0, keepdims=True), jnp.sum(gt, axis=0, keepdims=True)

    n_valid = (qpos + 1).astype(F32)
    need = n_valid > kf
    kth_is_zero = jnp.logical_and(gt0 < kf, ge0 >= kf)
    nonneg = ge0 >= kf
    lo0 = jnp.where(nonneg, 0.0, mn)
    cnt0 = jnp.where(nonneg, ge0, n_valid)
    above_max = jnp.where(mx > 0.0, jnp.minimum(mx * 2.0, F32_MAX), 1.0)
    hi0 = jnp.where(nonneg, above_max, 0.0)
    done0 = jnp.logical_or(jnp.logical_not(need), jnp.logical_or(kth_is_zero, cnt0 == kf))
    active0 = jnp.where(done0, 0.0, 1.0)

    def midpoint(st):
        return 0.5 * st[0] + 0.5 * st[1]

    def halve(st, count):
        mid = midpoint(st)
        return narrow(st, mid, count(mid))

    def narrow(st, mid, cnt):
        lo, hi, cnt_lo, active = st
        conv = jnp.logical_or(mid <= lo, mid >= hi)
        upd = jnp.logical_and(active > 0.0, jnp.logical_not(conv))
        ge = cnt >= kf
        up_lo = jnp.logical_and(upd, ge)
        up_hi = jnp.logical_and(upd, jnp.logical_not(ge))
        lo = jnp.where(up_lo, mid, lo)
        cnt_lo = jnp.where(up_lo, cnt, cnt_lo)
        hi = jnp.where(up_hi, mid, hi)
        finished = jnp.logical_or(conv, jnp.logical_and(up_lo, cnt == kf))
        return lo, hi, cnt_lo, jnp.where(finished, 0.0, active)

    n_fused = jnp.minimum(i, BISECT_FUSED_STEPS)
    for k, v in enumerate((lo0, hi0, cnt0, active0)):
        bis_ref[k] = v

    for pairs in range(1, score_ref.shape[0] // 2 + 1):
        @pl.when(n_pairs == pairs)
        def _(pairs=pairs):
            def fused_step(c, st):
                mid = midpoint(st)
                n_first = min(FUSED_COUNTS_UNDER_QK, (pairs + 1) // 2)
                parts = []
                for js in (range(0, n_first), range(n_first, pairs)):
                    acc = zero8
                    for j in js:
                        acc = acc + pair_sum(score_ref[2 * j:2 * j + 2] >= mid)
                    parts.append(jnp.sum(acc, axis=0, keepdims=True))
                diff_chunk(c, False, True, after=parts[0] * 0.0,
                           before_last=parts[1] * 0.0 if pairs > n_first else None)
                return narrow(st, mid, parts[0] + parts[1])
            st = lax.fori_loop(0, n_fused, fused_step, tuple(bis_ref[k] for k in range(4)))
            for k in range(4):
                bis_ref[k] = st[k]

    st = tuple(bis_ref[k] for k in range(4))
    st = lax.fori_loop(0, jnp.maximum(BISECT_FIRST_STEPS - n_fused, 0),
                       lambda _, s: halve(s, count_ge), st)

    def bis_cond(st):
        return jnp.logical_and(st[4] > 0.0, st[5] < MAX_BISECT_STEPS)

    def bis_body(st):
        lo, hi, cnt_lo, active = lax.fori_loop(0, BISECT_STEPS_PER_TEST, lambda _, s: halve(s, count_ge), st[:4])
        return lo, hi, cnt_lo, active, jnp.max(active), st[5] + BISECT_STEPS_PER_TEST

    lo, _, cnt_lo, _, _, _ = lax.while_loop(bis_cond, bis_body, st + (jnp.max(st[3]), jnp.int32(0)))
    thr = jnp.where(need, lo, -F32_MAX)

    excess = jnp.logical_and(need, cnt_lo > kf)

    @pl.when(jnp.max(jnp.where(excess, 1.0, 0.0)) > 0.0)
    def _():
        _, gt = count_ge_gt(thr)
        allow = kf - gt
        r_i = lax.broadcasted_iota(jnp.int32, (blk, blk), 0)
        c_i = lax.broadcasted_iota(jnp.int32, (blk, blk), 1)
        earlier = jnp.where(c_i < r_i, 1.0, 0.0).astype(BF16)

        def tie_body(c, seen):
            s = score_ref[c]
            eq = jnp.logical_and(s == thr, excess)
            eqf = jnp.where(eq, 1.0, 0.0)
            rank = seen + _dot(earlier, eqf.astype(BF16))
            drop = jnp.logical_and(eq, rank >= allow)
            score_ref[c] = jnp.where(drop, -jnp.inf, s)
            return seen + jnp.sum(eqf, axis=0, keepdims=True)

        lax.fori_loop(0, n_chunks, tie_body, jnp.zeros((1, blk), F32))

    qa_heads = [_head_q(aq_all, h, lane_lo) for h in range(N_DSA_HEADS)]

    def attend(bounded, shift):
        def att_body(c, carry):
            start = pl.multiple_of(c * blk, blk)
            bias = jnp.where(score_ref[c] >= thr, shift, MASK_VALUE)
            s_list, vt_list = [], []
            for h in range(N_DSA_HEADS):
                pair = slice((h // 2) * LANES, (h // 2 + 1) * LANES)
                kk = ak_ref[0, pl.ds(start, blk), pair]
                s_list.append(_dot_nt(kk, qa_heads[h]) + bias)
                vt_list.append(avt_ref[0, c, pair, :])
            if bounded:
                _bounded_update(s_list, vt_list, l_ref, acc_ref)
            else:
                _softmax_update(s_list, vt_list, m_ref, l_ref, acc_ref)
            return carry

        _init_softmax_state(m_ref, l_ref, acc_ref)
        lax.fori_loop(0, n_chunks, att_body, 0)

    def diff_rest(c, carry):
        diff_chunk(c, False, True)
        return carry

    lax.fori_loop(n_fused, i, diff_rest, 0)

    attend(True, -jnp.max(a_bound, axis=0, keepdims=True))

    @pl.when(_softmax_underflowed(l_ref))
    def _():
        attend(False, 0.0)

    diff_chunk(i, True, True)
    for j in range(N_DSA_HEADS // 2):
        a = acc_ref[2 * j] * (1.0 / l_ref[2 * j])
        b = acc_ref[2 * j + 1] * (1.0 / l_ref[2 * j + 1])
        o_t = jnp.concatenate([a[:HEAD_DIM], b[HEAD_DIM:]], axis=0)
        oa_ref[0, j * LANES:(j + 1) * LANES, :] = o_t.astype(oa_ref.dtype)


    @pl.when(_softmax_underflowed(dl_ref))
    def _():
        def body(c, carry):
            diff_chunk(c, False, False)
            return carry
        _init_softmax_state(dm_ref, dl_ref, dacc_ref)
        lax.fori_loop(0, i, body, 0)
        diff_chunk(i, True, False)

    lam_vecs = lam_ref[...]
    lam = (jnp.exp(jnp.sum(lam_vecs[0:1] * lam_vecs[1:2], axis=-1, keepdims=True))
           - jnp.exp(jnp.sum(lam_vecs[2:3] * lam_vecs[3:4], axis=-1, keepdims=True)) + lam_init)
    for h in range(N_DIFF_HEADS):
        out = dacc_ref[2 * h] * (1.0 / dl_ref[2 * h]) - dacc_ref[2 * h + 1] * (lam / dl_ref[2 * h + 1])
        out = out * lax.rsqrt(jnp.mean(out * out, axis=0, keepdims=True) + NORM_EPS)
        out = out * sw_ref[...] * (1.0 - lam_init)
        ob_ref[0, h * LANES:(h + 1) * LANES, :] = out.astype(ob_ref.dtype)


def _mixers(iq, ik2, iwt, aq, ak, avt, lam_vecs, dq, dk, dvt, subln_col, *, blk, lam_init):
    bsz, seq, _ = aq.shape
    top_k = min(TOPK_MAX, seq // 4)
    n_chunks = seq // blk
    assert n_chunks % 2 == 0, "key chunks are walked in pairs"
    qblk = lambda b, i: (b, i, 0)
    whole = lambda b, i: (b, 0, 0)
    whole_t = lambda b, i: (b, 0, 0, 0)
    n_maps = 2 * N_DIFF_HEADS
    q_spec = pl.BlockSpec((1, blk, 512), qblk)
    kv_spec = pl.BlockSpec((1, seq, 512), whole)
    vt_spec = pl.BlockSpec((1, n_chunks, 512, blk), whole_t)
    return pl.pallas_call(
        functools.partial(_mixer_kernel, blk=blk, top_k=top_k, lam_init=lam_init),
        grid=(bsz, n_chunks),
        in_specs=[q_spec,
                  pl.BlockSpec((1, seq, LANES), whole),
                  pl.BlockSpec((1, N_IDX_HEADS, blk), lambda b, i: (b, 0, i)),
                  q_spec, kv_spec, vt_spec,
                  _const_spec(lam_vecs.shape),
                  q_spec, kv_spec, vt_spec,
                  _const_spec((DIFF_V_DIM, 1))],
        out_specs=[pl.BlockSpec((1, 512, blk), lambda b, i: (b, 0, i))] * 2,
        out_shape=[jax.ShapeDtypeStruct((bsz, 512, seq), BF16)] * 2,
        scratch_shapes=[pltpu.VMEM((n_chunks, blk, blk), F32),
                        pltpu.VMEM((N_DSA_HEADS, 1, blk), F32),
                        pltpu.VMEM((N_DSA_HEADS, 1, blk), F32),
                        pltpu.VMEM((N_DSA_HEADS, LANES, blk), F32),
                        pltpu.VMEM((BF16_SUBLANES, LANES), F32),
                        pltpu.VMEM((n_maps, 1, blk), F32),
                        pltpu.VMEM((n_maps, 1, blk), F32),
                        pltpu.VMEM((n_maps, DIFF_V_DIM, blk), F32),
                        pltpu.VMEM((BF16_SUBLANES, LANES), F32),
                        pltpu.VMEM((4, 1, blk), F32)],
        compiler_params=pltpu.CompilerParams(
            dimension_semantics=("parallel", "arbitrary"), vmem_limit_bytes=VMEM_LIMIT_BYTES),
        name="mixers",
    )(iq, ik2, iwt, aq, ak, avt, lam_vecs, dq, dk, dvt, subln_col)


def _out_mlp_kernel(x_ref, ma_ref, mb_ref, g1_ref, sh_ref, sc_ref, g2_ref, n2_ref, nf_ref,
                    woa_ref, wob_ref, w1_ref, w2_ref, o_ref, *, ff_chunk):
    x = x_ref[0]
    o = _dot_tn(ma_ref[0], woa_ref[...]) + _dot_tn(mb_ref[0], wob_ref[...])
    x1 = x + g1_ref[0] * o
    ms = jnp.mean(x1 * x1, axis=-1, keepdims=True)
    h = x1 * lax.rsqrt(ms + NORM_EPS) * n2_ref[...]
    hb = (h * (1.0 + sc_ref[0]) + sh_ref[0]).astype(BF16)
    d_ff = w1_ref.shape[1]
    ff = jnp.zeros_like(x)
    for j in range(d_ff // ff_chunk):
        u = jnp.maximum(_dot(hb, w1_ref[:, j * ff_chunk:(j + 1) * ff_chunk]), 0.0)
        ff = ff + _dot((u * u).astype(BF16), w2_ref[j * ff_chunk:(j + 1) * ff_chunk, :])
    x2 = x1 + g2_ref[0] * ff
    ms2 = jnp.mean(x2 * x2, axis=-1, keepdims=True)
    o_ref[0] = x2 * lax.rsqrt(ms2 + NORM_EPS) * nf_ref[...]


def _out_mlp(x, mix_a, mix_b, g1, sh2, sc2, g2, n2, nf, wo_a, wo_b, w1, w2, *, tile):
    bsz, seq, d = x.shape
    tok = lambda b, i: (b, i, 0)
    per_b = lambda b, i: (b, 0, 0)
    mod_spec = pl.BlockSpec((1, 1, d), per_b)
    return pl.pallas_call(
        functools.partial(_out_mlp_kernel, ff_chunk=1024),
        grid=(bsz, seq // tile),
        in_specs=[pl.BlockSpec((1, tile, d), tok),
                  pl.BlockSpec((1, 512, tile), lambda b, i: (b, 0, i)),
                  pl.BlockSpec((1, 512, tile), lambda b, i: (b, 0, i)),
                  mod_spec, mod_spec, mod_spec, mod_spec,
                  _const_spec((1, d)), _const_spec((1, d)),
                  _const_spec(wo_a.shape), _const_spec(wo_b.shape),
                  _const_spec(w1.shape), _const_spec(w2.shape)],
        out_specs=pl.BlockSpec((1, tile, d), tok),
        out_shape=jax.ShapeDtypeStruct((bsz, seq, d), F32),
        compiler_params=pltpu.CompilerParams(
            dimension_semantics=("parallel", "parallel"), vmem_limit_bytes=VMEM_LIMIT_BYTES),
        name="out_mlp",
    )(x, mix_a, mix_b, g1, sh2, sc2, g2, n2, nf, wo_a, wo_b, w1, w2)


def _pad_cols(w, width):
    return jnp.pad(w, ((0, 0), (0, width - w.shape[1])))


def kernel(x, c, positions, w_ada, b_ada, norm1_w, w_in, idx_k_ln_w, idx_k_ln_b, lambda_q1, lambda_k1,
           lambda_q2, lambda_k2, subln_w, w_out, norm2_w, w_ff1, w_ff2, norm_f_w):
    bsz, seq, d = x.shape
    depth = w_ada.shape[0]
    assert depth == 1, "the fused final RMSNorm assumes a single layer"
    half = HEAD_DIM // 2
    inv_freq = ROPE_THETA ** (-jnp.arange(half, dtype=F32) / half)
    invf = jnp.tile(inv_freq, LANES // half).reshape(1, LANES)
    pos3 = positions.reshape(bsz, seq, 1)
    blk = min(ATT_BLOCK, seq)
    tile = min(512, seq)

    for l in range(depth):
        mod = _adaln(c, w_ada[l], b_ada[l])
        sh1, sc1, g1, sh2, sc2, g2 = [m.reshape(bsz, 1, d) for m in jnp.split(mod, 6, axis=-1)]

        w = w_in[l]
        wm = jnp.concatenate([w[:, :1024], w[:, 1536:2048], w[:, 2120:3144]], axis=1).astype(BF16)
        ws = _pad_cols(w[:, 2048:2112], LANES).astype(BF16)
        wwt = jnp.pad(w[:, 2112:2120].T, ((0, BF16_SUBLANES - N_IDX_HEADS), (0, 0))).astype(BF16)
        wavt = w[:, 1024:1536].T.astype(BF16)
        wdvt = w[:, 3144:3656].T.astype(BF16)
        lnw = _pad_cols(idx_k_ln_w[l].reshape(1, IDX_DIM), LANES)
        lnb = _pad_cols(idx_k_ln_b[l].reshape(1, IDX_DIM), LANES)

        aq, ak, avt, iq, ik2, iwt, dq, dk, dvt = _in_proj(
            x, pos3, sh1, sc1, norm1_w[l].reshape(1, d), invf, wm, ws, wwt, wavt, wdvt, lnw, lnb,
            tile=tile, chunk=blk)

        lam_vecs = jnp.stack([lambda_q1[l], lambda_k1[l], lambda_q2[l], lambda_k2[l]]).astype(F32)
        lam_init = 0.8 - 0.6 * math.exp(-0.3 * l)
        out_a, out_b = _mixers(iq, ik2, iwt, aq, ak, avt, lam_vecs, dq, dk, dvt,
                               subln_w[l].reshape(DIFF_V_DIM, 1), blk=blk, lam_init=lam_init)

        wo = w_out[l].astype(BF16)
        x = _out_mlp(x, out_a, out_b, g1, sh2, sc2, g2, norm2_w[l].reshape(1, d), norm_f_w.reshape(1, d),
                     wo[:512], wo[512:], w_ff1[l].astype(BF16), w_ff2[l].astype(BF16), tile=tile)
    return x
```

```python
import functools
import math

import jax
import jax.numpy as jnp
from jax import lax
from jax.experimental import pallas as pl
from jax.experimental.pallas import tpu as pltpu

F32 = jnp.float32
BF16 = jnp.bfloat16

HEAD_DIM = 64
N_DSA_HEADS = 8
N_IDX_HEADS = 8
IDX_DIM = 64
N_DIFF_HEADS = 4
DIFF_V_DIM = 128
TOPK_MAX = 256
ROPE_THETA = 10000.0
NORM_EPS = 1e-6
LN_EPS = 1e-5
LANES = 128
F32_SUBLANES = 8
BF16_SUBLANES = 16
ATT_BLOCK = 256
MASK_VALUE = -1e30
F32_MAX = float(jnp.finfo(jnp.float32).max)
MAX_BISECT_STEPS = 512
BISECT_FIRST_STEPS = 16
BISECT_FUSED_STEPS = 16
BISECT_STEPS_PER_TEST = 2
FUSED_COUNTS_UNDER_QK = 2
MIN_SAFE_NORMALISER = 2.0 ** -100
VMEM_LIMIT_BYTES = 56 * 1024 * 1024


def _dot(a, b):
    return jnp.dot(a, b, preferred_element_type=F32)


def _dot_nt(a, b):
    return lax.dot_general(a, b, (((1,), (1,)), ((), ())), preferred_element_type=F32)


def _dot_tn(a, b):
    return lax.dot_general(a, b, (((0,), (0,)), ((), ())), preferred_element_type=F32)


def _const_spec(shape):
    zeros = (0,) * len(shape)
    return pl.BlockSpec(shape, lambda *_: zeros, pipeline_mode=pl.Buffered(1))


def _adaln_kernel(c_ref, w_ref, b_ref, o_ref):
    c = c_ref[...]
    s = c / (1.0 + jnp.exp(-c))
    o_ref[...] = jnp.dot(s, w_ref[...], preferred_element_type=F32,
                         precision=lax.Precision.HIGHEST) + b_ref[...]


def _adaln(c, w, b):
    bsz, d = c.shape
    n = w.shape[1]
    tn = 1536
    return pl.pallas_call(
        _adaln_kernel,
        grid=(n // tn,),
        in_specs=[pl.BlockSpec((bsz, d), lambda j: (0, 0)),
                  pl.BlockSpec((d, tn), lambda j: (0, j)),
                  pl.BlockSpec((1, tn), lambda j: (0, j))],
        out_specs=pl.BlockSpec((bsz, tn), lambda j: (0, j)),
        out_shape=jax.ShapeDtypeStruct((bsz, n), F32),
        compiler_params=pltpu.CompilerParams(vmem_limit_bytes=VMEM_LIMIT_BYTES),
        name="adaln",
    )(c, w, b.reshape(1, n))


def _rope_group(x, cos, sin_signed, first_half):
    nxt = pltpu.roll(x, LANES - HEAD_DIM // 2, axis=1)
    prv = pltpu.roll(x, HEAD_DIM // 2, axis=1)
    return x * cos + jnp.where(first_half, nxt, prv) * sin_signed


def _in_proj_kernel(x_ref, pos_ref, sh_ref, sc_ref, nw_ref, invf_ref, wm_ref, ws_ref, wwt_ref,
                    wavt_ref, wdvt_ref, lnw_ref, lnb_ref,
                    aq_ref, ak_ref, avt_ref, iq_ref, ik_ref, iwt_ref, dq_ref, dk_ref, dvt_ref):
    x = x_ref[0]
    ms = jnp.mean(x * x, axis=-1, keepdims=True)
    h = x * lax.rsqrt(ms + NORM_EPS) * nw_ref[...]
    h = h * (1.0 + sc_ref[0]) + sh_ref[0]
    hb = h.astype(BF16)

    lane = lax.broadcasted_iota(jnp.int32, (1, LANES), 1)
    first_half = (lane % HEAD_DIM) < (HEAD_DIM // 2)
    ang = pos_ref[0].astype(F32) * invf_ref[...]
    cos = jnp.cos(ang)
    sin = jnp.sin(ang)
    sin_signed = jnp.where(first_half, -sin, sin)

    sm = _dot(hb, ws_ref[...])
    lo_half = lane < IDX_DIM
    mu = jnp.sum(sm, axis=-1, keepdims=True) * (1.0 / IDX_DIM)
    d = jnp.where(lo_half, sm - mu, 0.0)
    var = jnp.sum(d * d, axis=-1, keepdims=True) * (1.0 / IDX_DIM)
    y = d * lax.rsqrt(var + LN_EPS) * lnw_ref[...] + lnb_ref[...]
    y = y + pltpu.roll(y, IDX_DIM, axis=1)
    ik_ref[0] = _rope_group(y, cos, sin_signed, first_half).astype(ik_ref.dtype)

    wt = _dot_nt(wwt_ref[...], hb)
    iwt_ref[0] = wt[:N_IDX_HEADS] * ((N_IDX_HEADS ** -0.5) * (IDX_DIM ** -0.5))

    def roped(col0, out_ref, scale):
        wide = _dot(hb, wm_ref[:, col0:col0 + 4 * LANES])
        for g in range(4):
            y = _rope_group(wide[:, g * LANES:(g + 1) * LANES], cos, sin_signed, first_half)
            if scale != 1.0:
                y = y * scale
            out_ref[0, :, g * LANES:(g + 1) * LANES] = y.astype(out_ref.dtype)

    qscale = HEAD_DIM ** -0.5 * math.log2(math.e)
    roped(0, aq_ref, qscale)
    roped(512, ak_ref, 1.0)
    roped(1024, iq_ref, 1.0)
    roped(1536, dq_ref, qscale)
    roped(2048, dk_ref, 1.0)

    n_sub = avt_ref.shape[1]
    chunk = avt_ref.shape[3]
    for j in range(n_sub):
        hj = hb[j * chunk:(j + 1) * chunk]
        avt_ref[0, j] = _dot_nt(wavt_ref[...], hj).astype(avt_ref.dtype)
        dvt_ref[0, j] = _dot_nt(wdvt_ref[...], hj).astype(dvt_ref.dtype)


def _in_proj(x, pos3, sh1, sc1, nw, invf, wm, ws, wwt, wavt, wdvt, lnw, lnb, *, tile, chunk):
    bsz, seq, d = x.shape
    tok = lambda b, i: (b, i, 0)
    per_b = lambda b, i: (b, 0, 0)
    wide = jax.ShapeDtypeStruct((bsz, seq, 512), BF16)
    wide_t = jax.ShapeDtypeStruct((bsz, seq // chunk, 512, chunk), BF16)
    wide_spec = pl.BlockSpec((1, tile, 512), tok)
    wide_t_spec = pl.BlockSpec((1, tile // chunk, 512, chunk), lambda b, i: (b, i, 0, 0))
    out_shape = [wide, wide, wide_t, wide,
                 jax.ShapeDtypeStruct((bsz, seq, LANES), BF16),
                 jax.ShapeDtypeStruct((bsz, N_IDX_HEADS, seq), F32),
                 wide, wide, wide_t]
    out_specs = [wide_spec, wide_spec, wide_t_spec, wide_spec,
                 pl.BlockSpec((1, tile, LANES), tok),
                 pl.BlockSpec((1, N_IDX_HEADS, tile), lambda b, i: (b, 0, i)),
                 wide_spec, wide_spec, wide_t_spec]
    return pl.pallas_call(
        _in_proj_kernel,
        grid=(bsz, seq // tile),
        in_specs=[pl.BlockSpec((1, tile, d), tok),
                  pl.BlockSpec((1, tile, 1), tok),
                  pl.BlockSpec((1, 1, d), per_b),
                  pl.BlockSpec((1, 1, d), per_b),
                  _const_spec((1, d)),
                  _const_spec((1, LANES)),
                  _const_spec(wm.shape),
                  _const_spec(ws.shape),
                  _const_spec(wwt.shape),
                  _const_spec(wavt.shape),
                  _const_spec(wdvt.shape),
                  _const_spec((1, LANES)),
                  _const_spec((1, LANES))],
        out_specs=out_specs,
        out_shape=out_shape,
        compiler_params=pltpu.CompilerParams(
            dimension_semantics=("parallel", "parallel"), vmem_limit_bytes=VMEM_LIMIT_BYTES),
        name="in_proj",
    )(x, pos3, sh1, sc1, nw, invf, wm, ws, wwt, wavt, wdvt, lnw, lnb)


def _head_q(q_all, h, lane_lo):
    pair = q_all[:, (h // 2) * LANES:(h // 2 + 1) * LANES]
    keep = lane_lo if h % 2 == 0 else jnp.logical_not(lane_lo)
    return jnp.where(keep, pair, jnp.zeros_like(pair))


def _softmax_update(s_list, vt_list, m_ref, l_ref, acc_ref):
    n = len(s_list)
    dv = vt_list[0].shape[0]
    ones = jnp.ones((BF16_SUBLANES, vt_list[0].shape[1]), BF16)
    m_old = [m_ref[g] for g in range(n)]
    m_new = [jnp.maximum(m_old[g], jnp.max(s_list[g], axis=0, keepdims=True)) for g in range(n)]
    p_list = [jnp.exp2(s_list[g] - m_new[g]).astype(BF16) for g in range(n)]
    alpha = [jnp.exp2(m_old[g] - m_new[g]) for g in range(n)]
    pv = [_dot(jnp.concatenate([vt_list[g], ones], axis=0), p_list[g]) for g in range(n)]
    for g in range(n):
        m_ref[g] = m_new[g]
        l_ref[g] = alpha[g] * l_ref[g] + pv[g][dv:dv + 1]
        acc_ref[g] = alpha[g] * acc_ref[g] + pv[g][:dv]


def _init_softmax_state(m_ref, l_ref, acc_ref):
    m_ref[...] = jnp.full(m_ref.shape, MASK_VALUE, F32)
    l_ref[...] = jnp.zeros(l_ref.shape, F32)
    acc_ref[...] = jnp.zeros(acc_ref.shape, F32)


def _bounded_update(s_list, vt_list, l_ref, acc_ref, before_last=None):
    n = len(s_list)
    dv = vt_list[0].shape[0]
    ones = jnp.ones((BF16_SUBLANES, vt_list[0].shape[1]), BF16)
    p_list = [jnp.exp2(s_list[g]).astype(BF16) for g in range(n)]
    pv = [_dot(jnp.concatenate([vt_list[g], ones], axis=0), p_list[g]) for g in range(n)]
    for g in range(n):
        col_sum = pv[g][dv:dv + 1]
        if before_last is not None and g == n - 1:
            col_sum = col_sum + before_last
        l_ref[g] = l_ref[g] + col_sum
        acc_ref[g] = acc_ref[g] + pv[g][:dv]


def _head_selector(n_lanes):
    r = lax.broadcasted_iota(jnp.int32, (BF16_SUBLANES, n_lanes), 0)
    l = lax.broadcasted_iota(jnp.int32, (BF16_SUBLANES, n_lanes), 1)
    return jnp.where(l // HEAD_DIM == r, 1.0, 0.0).astype(BF16)


def _head_sq_norms(x, head_sel):
    return _dot_nt(head_sel, x * x)


def _max_key_sq_norms(k_ref, kmax_ref, rows_per_step):
    n_lanes = k_ref.shape[2]
    l = lax.broadcasted_iota(jnp.int32, (n_lanes, LANES), 0)
    h = lax.broadcasted_iota(jnp.int32, (n_lanes, LANES), 1)
    sel_t = jnp.where(l // HEAD_DIM == h, 1.0, 0.0).astype(BF16)

    def body(c, mx):
        start = pl.multiple_of(c * rows_per_step, rows_per_step)
        kk = k_ref[0, pl.ds(start, rows_per_step), :]
        return jnp.maximum(mx, jnp.max(_dot(kk * kk, sel_t), axis=0, keepdims=True))
    row = lax.fori_loop(0, k_ref.shape[1] // rows_per_step, body, jnp.zeros((1, LANES), F32))
    r = lax.broadcasted_iota(jnp.int32, kmax_ref.shape, 0)
    c = lax.broadcasted_iota(jnp.int32, kmax_ref.shape, 1)
    col = jnp.sum(jnp.where(r == c, row, 0.0), axis=1, keepdims=True)
    kmax_ref[...] = jnp.broadcast_to(col, kmax_ref.shape)


def _softmax_underflowed(l_ref):
    return jnp.logical_not(jnp.min(l_ref[...]) >= MIN_SAFE_NORMALISER)


def _mixer_kernel(iq_ref, ik_ref, iwt_ref, aq_ref, ak_ref, avt_ref, lam_ref, dq_ref, dk_ref, dvt_ref, sw_ref,
                  oa_ref, ob_ref,
                  score_ref, m_ref, l_ref, acc_ref, kmax_ref, dm_ref, dl_ref, dacc_ref, dkmax_ref, bis_ref,
                  *, blk, top_k, lam_init):
    i = pl.program_id(1)
    n_chunks = i + 1
    n_maps = 2 * N_DIFF_HEADS
    lane = lax.broadcasted_iota(jnp.int32, (1, LANES), 1)
    lane_lo = lane < HEAD_DIM
    qcol = lax.broadcasted_iota(jnp.int32, (1, blk), 1)
    qpos = i * blk + qcol
    krow = lax.broadcasted_iota(jnp.int32, (blk, 1), 0)
    kf = float(top_k)

    aq_all = aq_ref[0]
    dq_all = dq_ref[0]
    head_sel = _head_selector(aq_all.shape[1])

    @pl.when(i == 0)
    def _():
        rows_per_step = math.gcd(4 * blk, ak_ref.shape[1])
        _max_key_sq_norms(ak_ref, kmax_ref, rows_per_step)
        _max_key_sq_norms(dk_ref, dkmax_ref, rows_per_step)

    d_maps = [_head_q(dq_all, g, lane_lo) for g in range(n_maps)]
    d_bound = jnp.sqrt(_head_sq_norms(dq_all, head_sel) * dkmax_ref[:, :1])
    d_shifts = [d_bound[g:g + 1] for g in range(n_maps)]
    a_bound = jnp.sqrt(_head_sq_norms(aq_all, head_sel) * kmax_ref[:, :1])
    lam_vecs = lam_ref[...]
    lam = (jnp.exp(jnp.sum(lam_vecs[0:1] * lam_vecs[1:2], axis=-1, keepdims=True))
           - jnp.exp(jnp.sum(lam_vecs[2:3] * lam_vecs[3:4], axis=-1, keepdims=True)) + lam_init)

    def diff_chunk(c, diagonal, bounded, after=None, before_last=None):
        start = pl.multiple_of(c * blk, blk)
        if diagonal:
            bias = jnp.where(krow <= qcol, 0.0, MASK_VALUE)
        s_list, vt_list = [], []
        for h in range(N_DIFF_HEADS):
            cols = slice(h * LANES, (h + 1) * LANES)
            kk = dk_ref[0, pl.ds(start, blk), cols]
            for comp in range(2):
                g = 2 * h + comp
                s = _dot_nt(kk, d_maps[g])
                if bounded:
                    s = s - (d_shifts[g] if after is None else d_shifts[g] + after)
                s_list.append(s + bias if diagonal else s)
                vt_list.append(dvt_ref[0, c, cols, :])
        if bounded:
            _bounded_update(s_list, vt_list, dl_ref, dacc_ref, before_last)
        else:
            _softmax_update(s_list, vt_list, dm_ref, dl_ref, dacc_ref)

    _init_softmax_state(dm_ref, dl_ref, dacc_ref)

    iq_all = iq_ref[0]
    wt = iwt_ref[0]
    q_heads = [_head_q(iq_all, h, lane_lo) for h in range(N_IDX_HEADS)]
    w_rows = [wt[h:h + 1, :] for h in range(N_IDX_HEADS)]

    n_pairs = (n_chunks + 1) // 2

    def pair_sum(hit):
        m = jnp.where(hit, 1.0, 0.0)
        return jnp.sum((m[0] + m[1]).reshape(blk // F32_SUBLANES, F32_SUBLANES, blk), axis=0)

    zero8 = jnp.zeros((F32_SUBLANES, blk), F32)

    def score_pair(j, mn, mx, w_first):
        for u in range(2):
            c = 2 * j + u
            start = pl.multiple_of(c * blk, blk)
            kk = ik_ref[0, pl.ds(start, blk), :]
            acc = jnp.zeros((blk, blk), F32)
            for h in range(N_IDX_HEADS):
                acc = acc + jnp.maximum(_dot_nt(kk, q_heads[h]), 0.0) * (w_first if h == 0 else w_rows[h])
            causal = (c * blk + krow) <= qpos
            sc = jnp.where(causal, acc, -jnp.inf)
            score_ref[c] = sc
            mx = jnp.maximum(mx, jnp.max(sc, axis=0, keepdims=True))
            mn = jnp.minimum(mn, jnp.min(jnp.where(causal, acc, jnp.inf), axis=0, keepdims=True))
        return mn, mx

    def zero_counts(j):
        s = score_ref[pl.ds(2 * j, 2)]
        return pair_sum(s >= 0.0), pair_sum(s > 0.0)

    def score_body(j, carry):
        mn, mx, ge, gt = carry
        ge_p, gt_p = zero_counts(j - 1)
        after = jnp.sum(ge_p + gt_p, axis=0, keepdims=True) * 0.0
        mn, mx = score_pair(j, mn, mx, w_rows[0] + after)
        return mn, mx, ge + ge_p, gt + gt_p

    mn, mx = score_pair(0, jnp.full((1, blk), jnp.inf, F32), jnp.full((1, blk), -jnp.inf, F32), w_rows[0])
    mn, mx, ge8, gt8 = lax.fori_loop(1, n_pairs, score_body, (mn, mx, zero8, zero8))
    ge_p, gt_p = zero_counts(n_pairs - 1)
    ge0 = jnp.sum(ge8 + ge_p, axis=0, keepdims=True)
    gt0 = jnp.sum(gt8 + gt_p, axis=0, keepdims=True)


    def count_ge(cand):
        def body(j, acc):
            return acc + pair_sum(score_ref[pl.ds(2 * j, 2)] >= cand)
        return jnp.sum(lax.fori_loop(0, n_pairs, body, zero8), axis=0, keepdims=True)

    def count_ge_gt(cand):
        def body(j, carry):
            ge, gt = carry
            s = score_ref[pl.ds(2 * j, 2)]
            return ge + pair_sum(s >= cand), gt + pair_sum(s > cand)
        ge, gt = lax.fori_loop(0, n_pairs, body, (zero8, zero8))
        return jnp.sum(ge, axis=0, keepdims=True), jnp.sum(gt, axis=0, keepdims=True)

    n_valid = (qpos + 1).astype(F32)
    need = n_valid > kf
    kth_is_zero = jnp.logical_and(gt0 < kf, ge0 >= kf)
    nonneg = ge0 >= kf
    lo0 = jnp.where(nonneg, 0.0, mn)
    cnt0 = jnp.where(nonneg, ge0, n_valid)
    above_max = jnp.where(mx > 0.0, jnp.minimum(mx * 2.0, F32_MAX), 1.0)
    hi0 = jnp.where(nonneg, above_max, 0.0)
    done0 = jnp.logical_or(jnp.logical_not(need), jnp.logical_or(kth_is_zero, cnt0 == kf))
    active0 = jnp.where(done0, 0.0, 1.0)

    def midpoint(st):
        return 0.5 * st[0] + 0.5 * st[1]

    def halve(st, count):
        mid = midpoint(st)
        return narrow(st, mid, count(mid))

    def narrow(st, mid, cnt):
        lo, hi, cnt_lo, active = st
        conv = jnp.logical_or(mid <= lo, mid >= hi)
        upd = jnp.logical_and(active > 0.0, jnp.logical_not(conv))
        ge = cnt >= kf
        up_lo = jnp.logical_and(upd, ge)
        up_hi = jnp.logical_and(upd, jnp.logical_not(ge))
        lo = jnp.where(up_lo, mid, lo)
        cnt_lo = jnp.where(up_lo, cnt, cnt_lo)
        hi = jnp.where(up_hi, mid, hi)
        finished = jnp.logical_or(conv, jnp.logical_and(up_lo, cnt == kf))
        return lo, hi, cnt_lo, jnp.where(finished, 0.0, active)

    n_fused = jnp.minimum(i, BISECT_FUSED_STEPS)
    for k, v in enumerate((lo0, hi0, cnt0, active0)):
        bis_ref[k] = v

    for pairs in range(1, score_ref.shape[0] // 2 + 1):
        @pl.when(n_pairs == pairs)
        def _(pairs=pairs):
            def fused_step(c, st):
                mid = midpoint(st)
                n_first = min(FUSED_COUNTS_UNDER_QK, (pairs + 1) // 2)
                parts = []
                for js in (range(0, n_first), range(n_first, pairs)):
                    acc = zero8
                    for j in js:
                        acc = acc + pair_sum(score_ref[2 * j:2 * j + 2] >= mid)
                    parts.append(jnp.sum(acc, axis=0, keepdims=True))
                diff_chunk(c, False, True, after=parts[0] * 0.0,
                           before_last=parts[1] * 0.0 if pairs > n_first else None)
                return narrow(st, mid, parts[0] + parts[1])
            st = lax.fori_loop(0, n_fused, fused_step, tuple(bis_ref[k] for k in range(4)))
            for k in range(4):
                bis_ref[k] = st[k]

    st = tuple(bis_ref[k] for k in range(4))
    st = lax.fori_loop(0, jnp.maximum(BISECT_FIRST_STEPS - n_fused, 0),
                       lambda _, s: halve(s, count_ge), st)

    def bis_cond(st):
        return jnp.logical_and(st[4] > 0.0, st[5] < MAX_BISECT_STEPS)

    def bis_body(st):
        lo, hi, cnt_lo, active = lax.fori_loop(0, BISECT_STEPS_PER_TEST, lambda _, s: halve(s, count_ge), st[:4])
        return lo, hi, cnt_lo, active, jnp.max(active), st[5] + BISECT_STEPS_PER_TEST

    lo, _, cnt_lo, _, _, _ = lax.while_loop(bis_cond, bis_body, st + (jnp.max(st[3]), jnp.int32(0)))
    thr = jnp.where(need, lo, -F32_MAX)

    excess = jnp.logical_and(need, cnt_lo > kf)

    @pl.when(jnp.max(jnp.where(excess, 1.0, 0.0)) > 0.0)
    def _():
        _, gt = count_ge_gt(thr)
        allow = kf - gt
        r_i = lax.broadcasted_iota(jnp.int32, (blk, blk), 0)
        c_i = lax.broadcasted_iota(jnp.int32, (blk, blk), 1)
        earlier = jnp.where(c_i < r_i, 1.0, 0.0).astype(BF16)

        def tie_body(c, seen):
            s = score_ref[c]
            eq = jnp.logical_and(s == thr, excess)
            eqf = jnp.where(eq, 1.0, 0.0)
            rank = seen + _dot(earlier, eqf.astype(BF16))
            drop = jnp.logical_and(eq, rank >= allow)
            score_ref[c] = jnp.where(drop, -jnp.inf, s)
            return seen + jnp.sum(eqf, axis=0, keepdims=True)

        lax.fori_loop(0, n_chunks, tie_body, jnp.zeros((1, blk), F32))

    qa_heads = [_head_q(aq_all, h, lane_lo) for h in range(N_DSA_HEADS)]

    def attend(bounded, shift):
        def att_body(c, carry):
            start = pl.multiple_of(c * blk, blk)
            bias = jnp.where(score_ref[c] >= thr, shift, MASK_VALUE)
            s_list, vt_list = [], []
            for h in range(N_DSA_HEADS):
                pair = slice((h // 2) * LANES, (h // 2 + 1) * LANES)
                kk = ak_ref[0, pl.ds(start, blk), pair]
                s_list.append(_dot_nt(kk, qa_heads[h]) + bias)
                vt_list.append(avt_ref[0, c, pair, :])
            if bounded:
                _bounded_update(s_list, vt_list, l_ref, acc_ref)
            else:
                _softmax_update(s_list, vt_list, m_ref, l_ref, acc_ref)
            return carry

        _init_softmax_state(m_ref, l_ref, acc_ref)
        lax.fori_loop(0, n_chunks, att_body, 0)

    def diff_rest(c, carry):
        diff_chunk(c, False, True)
        return carry

    lax.fori_loop(n_fused, i, diff_rest, 0)

    attend(True, -jnp.max(a_bound, axis=0, keepdims=True))

    def write_outputs():
        for j in range(N_DSA_HEADS // 2):
            a = acc_ref[2 * j] * (1.0 / l_ref[2 * j])
            b = acc_ref[2 * j + 1] * (1.0 / l_ref[2 * j + 1])
            o_t = jnp.concatenate([a[:HEAD_DIM], b[HEAD_DIM:]], axis=0)
            oa_ref[0, j * LANES:(j + 1) * LANES, :] = o_t.astype(oa_ref.dtype)
        for h in range(N_DIFF_HEADS):
            out = (dacc_ref[2 * h] * (1.0 / dl_ref[2 * h])
                   - dacc_ref[2 * h + 1] * (lam / dl_ref[2 * h + 1]))
            out = out * lax.rsqrt(jnp.mean(out * out, axis=0, keepdims=True) + NORM_EPS)
            out = out * sw_ref[...] * (1.0 - lam_init)
            ob_ref[0, h * LANES:(h + 1) * LANES, :] = out.astype(ob_ref.dtype)

    diff_chunk(i, True, True)
    dsa_bad = _softmax_underflowed(l_ref)
    diff_bad = _softmax_underflowed(dl_ref)
    write_outputs()

    @pl.when(jnp.logical_or(dsa_bad, diff_bad))
    def _():
        @pl.when(dsa_bad)
        def _():
            attend(False, 0.0)

        @pl.when(diff_bad)
        def _():
            def body(c, carry):
                diff_chunk(c, False, False)
                return carry
            _init_softmax_state(dm_ref, dl_ref, dacc_ref)
            lax.fori_loop(0, i, body, 0)
            diff_chunk(i, True, False)

        write_outputs()


def _mixers(iq, ik2, iwt, aq, ak, avt, lam_vecs, dq, dk, dvt, subln_col, *, blk, lam_init):
    bsz, seq, _ = aq.shape
    top_k = min(TOPK_MAX, seq // 4)
    n_chunks = seq // blk
    assert n_chunks % 2 == 0, "key chunks are walked in pairs"
    qblk = lambda b, i: (b, i, 0)
    whole = lambda b, i: (b, 0, 0)
    whole_t = lambda b, i: (b, 0, 0, 0)
    n_maps = 2 * N_DIFF_HEADS
    q_spec = pl.BlockSpec((1, blk, 512), qblk)
    kv_spec = pl.BlockSpec((1, seq, 512), whole)
    vt_spec = pl.BlockSpec((1, n_chunks, 512, blk), whole_t)
    return pl.pallas_call(
        functools.partial(_mixer_kernel, blk=blk, top_k=top_k, lam_init=lam_init),
        grid=(bsz, n_chunks),
        in_specs=[q_spec,
                  pl.BlockSpec((1, seq, LANES), whole),
                  pl.BlockSpec((1, N_IDX_HEADS, blk), lambda b, i: (b, 0, i)),
                  q_spec, kv_spec, vt_spec,
                  _const_spec(lam_vecs.shape),
                  q_spec, kv_spec, vt_spec,
                  _const_spec((DIFF_V_DIM, 1))],
        out_specs=[pl.BlockSpec((1, 512, blk), lambda b, i: (b, 0, i))] * 2,
        out_shape=[jax.ShapeDtypeStruct((bsz, 512, seq), BF16)] * 2,
        scratch_shapes=[pltpu.VMEM((n_chunks, blk, blk), F32),
                        pltpu.VMEM((N_DSA_HEADS, 1, blk), F32),
                        pltpu.VMEM((N_DSA_HEADS, 1, blk), F32),
                        pltpu.VMEM((N_DSA_HEADS, LANES, blk), F32),
                        pltpu.VMEM((BF16_SUBLANES, LANES), F32),
                        pltpu.VMEM((n_maps, 1, blk), F32),
                        pltpu.VMEM((n_maps, 1, blk), F32),
                        pltpu.VMEM((n_maps, DIFF_V_DIM, blk), F32),
                        pltpu.VMEM((BF16_SUBLANES, LANES), F32),
                        pltpu.VMEM((4, 1, blk), F32)],
        compiler_params=pltpu.CompilerParams(
            dimension_semantics=("parallel", "arbitrary"), vmem_limit_bytes=VMEM_LIMIT_BYTES),
        name="mixers",
    )(iq, ik2, iwt, aq, ak, avt, lam_vecs, dq, dk, dvt, subln_col)


def _out_mlp_kernel(x_ref, ma_ref, mb_ref, g1_ref, sh_ref, sc_ref, g2_ref, n2_ref, nf_ref,
                    woa_ref, wob_ref, w1_ref, w2_ref, o_ref, *, ff_chunk):
    x = x_ref[0]
    o = _dot_tn(ma_ref[0], woa_ref[...]) + _dot_tn(mb_ref[0], wob_ref[...])
    x1 = x + g1_ref[0] * o
    ms = jnp.mean(x1 * x1, axis=-1, keepdims=True)
    h = x1 * lax.rsqrt(ms + NORM_EPS) * n2_ref[...]
    hb = (h * (1.0 + sc_ref[0]) + sh_ref[0]).astype(BF16)
    d_ff = w1_ref.shape[1]
    ff = jnp.zeros_like(x)
    for j in range(d_ff // ff_chunk):
        u = jnp.maximum(_dot(hb, w1_ref[:, j * ff_chunk:(j + 1) * ff_chunk]), 0.0)
        ff = ff + _dot((u * u).astype(BF16), w2_ref[j * ff_chunk:(j + 1) * ff_chunk, :])
    x2 = x1 + g2_ref[0] * ff
    ms2 = jnp.mean(x2 * x2, axis=-1, keepdims=True)
    o_ref[0] = x2 * lax.rsqrt(ms2 + NORM_EPS) * nf_ref[...]


def _out_mlp(x, mix_a, mix_b, g1, sh2, sc2, g2, n2, nf, wo_a, wo_b, w1, w2, *, tile):
    bsz, seq, d = x.shape
    tok = lambda b, i: (b, i, 0)
    per_b = lambda b, i: (b, 0, 0)
    mod_spec = pl.BlockSpec((1, 1, d), per_b)
    return pl.pallas_call(
        functools.partial(_out_mlp_kernel, ff_chunk=1024),
        grid=(bsz, seq // tile),
        in_specs=[pl.BlockSpec((1, tile, d), tok),
                  pl.BlockSpec((1, 512, tile), lambda b, i: (b, 0, i)),
                  pl.BlockSpec((1, 512, tile), lambda b, i: (b, 0, i)),
                  mod_spec, mod_spec, mod_spec, mod_spec,
                  _const_spec((1, d)), _const_spec((1, d)),
                  _const_spec(wo_a.shape), _const_spec(wo_b.shape),
                  _const_spec(w1.shape), _const_spec(w2.shape)],
        out_specs=pl.BlockSpec((1, tile, d), tok),
        out_shape=jax.ShapeDtypeStruct((bsz, seq, d), F32),
        compiler_params=pltpu.CompilerParams(
            dimension_semantics=("parallel", "parallel"), vmem_limit_bytes=VMEM_LIMIT_BYTES),
        name="out_mlp",
    )(x, mix_a, mix_b, g1, sh2, sc2, g2, n2, nf, wo_a, wo_b, w1, w2)


def _pad_cols(w, width):
    return jnp.pad(w, ((0, 0), (0, width - w.shape[1])))


def kernel(x, c, positions, w_ada, b_ada, norm1_w, w_in, idx_k_ln_w, idx_k_ln_b, lambda_q1, lambda_k1,
           lambda_q2, lambda_k2, subln_w, w_out, norm2_w, w_ff1, w_ff2, norm_f_w):
    bsz, seq, d = x.shape
    depth = w_ada.shape[0]
    assert depth == 1, "the fused final RMSNorm assumes a single layer"
    half = HEAD_DIM // 2
    inv_freq = ROPE_THETA ** (-jnp.arange(half, dtype=F32) / half)
    invf = jnp.tile(inv_freq, LANES // half).reshape(1, LANES)
    pos3 = positions.reshape(bsz, seq, 1)
    blk = min(ATT_BLOCK, seq)
    tile = min(512, seq)

    for l in range(depth):
        mod = _adaln(c, w_ada[l], b_ada[l])
        sh1, sc1, g1, sh2, sc2, g2 = [m.reshape(bsz, 1, d) for m in jnp.split(mod, 6, axis=-1)]

        w = w_in[l]
        wm = jnp.concatenate([w[:, :1024], w[:, 1536:2048], w[:, 2120:3144]], axis=1).astype(BF16)
        ws = _pad_cols(w[:, 2048:2112], LANES).astype(BF16)
        wwt = jnp.pad(w[:, 2112:2120].T, ((0, BF16_SUBLANES - N_IDX_HEADS), (0, 0))).astype(BF16)
        wavt = w[:, 1024:1536].T.astype(BF16)
        wdvt = w[:, 3144:3656].T.astype(BF16)
        lnw = _pad_cols(idx_k_ln_w[l].reshape(1, IDX_DIM), LANES)
        lnb = _pad_cols(idx_k_ln_b[l].reshape(1, IDX_DIM), LANES)

        aq, ak, avt, iq, ik2, iwt, dq, dk, dvt = _in_proj(
            x, pos3, sh1, sc1, norm1_w[l].reshape(1, d), invf, wm, ws, wwt, wavt, wdvt, lnw, lnb,
            tile=tile, chunk=blk)

        lam_vecs = jnp.stack([lambda_q1[l], lambda_k1[l], lambda_q2[l], lambda_k2[l]]).astype(F32)
        lam_init = 0.8 - 0.6 * math.exp(-0.3 * l)
        out_a, out_b = _mixers(iq, ik2, iwt, aq, ak, avt, lam_vecs, dq, dk, dvt,
                               subln_w[l].reshape(DIFF_V_DIM, 1), blk=blk, lam_init=lam_init)

        wo = w_out[l].astype(BF16)
        x = _out_mlp(x, out_a, out_b, g1, sh2, sc2, g2, norm2_w[l].reshape(1, d), norm_f_w.reshape(1, d),
                     wo[:512], wo[512:], w_ff1[l].astype(BF16), w_ff2[l].astype(BF16), tile=tile)
    return x
```

```python
import functools
import math

import jax
import jax.numpy as jnp
from jax import lax
from jax.experimental import pallas as pl
from jax.experimental.pallas import tpu as pltpu

F32 = jnp.float32
BF16 = jnp.bfloat16

HEAD_DIM = 64
N_DSA_HEADS = 8
N_IDX_HEADS = 8
IDX_DIM = 64
N_DIFF_HEADS = 4
DIFF_V_DIM = 128
TOPK_MAX = 256
ROPE_THETA = 10000.0
NORM_EPS = 1e-6
LN_EPS = 1e-5
LANES = 128
F32_SUBLANES = 8
BF16_SUBLANES = 16
ATT_BLOCK = 256
MASK_VALUE = -1e30
F32_MAX = float(jnp.finfo(jnp.float32).max)
MAX_BISECT_STEPS = 512
BISECT_FIRST_STEPS = 16
BISECT_FUSED_STEPS = 16
BISECT_STEPS_PER_TEST = 2
FUSED_COUNTS_UNDER_QK = 2
MIN_SAFE_NORMALISER = 2.0 ** -100
VMEM_LIMIT_BYTES = 56 * 1024 * 1024


def _dot(a, b):
    return jnp.dot(a, b, preferred_element_type=F32)


def _dot_nt(a, b):
    return lax.dot_general(a, b, (((1,), (1,)), ((), ())), preferred_element_type=F32)


def _dot_tn(a, b):
    return lax.dot_general(a, b, (((0,), (0,)), ((), ())), preferred_element_type=F32)


def _const_spec(shape):
    zeros = (0,) * len(shape)
    return pl.BlockSpec(shape, lambda *_: zeros, pipeline_mode=pl.Buffered(1))


def _adaln_kernel(c_ref, w_ref, b_ref, o_ref):
    c = c_ref[...]
    s = c / (1.0 + jnp.exp(-c))
    o_ref[...] = jnp.dot(s, w_ref[...], preferred_element_type=F32,
                         precision=lax.Precision.HIGHEST) + b_ref[...]


def _adaln(c, w, b):
    bsz, d = c.shape
    n = w.shape[1]
    tn = 1536
    return pl.pallas_call(
        _adaln_kernel,
        grid=(n // tn,),
        in_specs=[pl.BlockSpec((bsz, d), lambda j: (0, 0)),
                  pl.BlockSpec((d, tn), lambda j: (0, j)),
                  pl.BlockSpec((1, tn), lambda j: (0, j))],
        out_specs=pl.BlockSpec((bsz, tn), lambda j: (0, j)),
        out_shape=jax.ShapeDtypeStruct((bsz, n), F32),
        compiler_params=pltpu.CompilerParams(vmem_limit_bytes=VMEM_LIMIT_BYTES),
        name="adaln",
    )(c, w, b.reshape(1, n))


def _rope_group(x, cos, sin_signed, first_half):
    nxt = pltpu.roll(x, LANES - HEAD_DIM // 2, axis=1)
    prv = pltpu.roll(x, HEAD_DIM // 2, axis=1)
    return x * cos + jnp.where(first_half, nxt, prv) * sin_signed


def _in_proj_kernel(x_ref, pos_ref, sh_ref, sc_ref, nw_ref, invf_ref, wm_ref, ws_ref, wwt_ref,
                    wavt_ref, wdvt_ref, lnw_ref, lnb_ref,
                    aq_ref, ak_ref, avt_ref, iq_ref, ik_ref, iwt_ref, dq_ref, dk_ref, dvt_ref):
    x = x_ref[0]
    ms = jnp.mean(x * x, axis=-1, keepdims=True)
    h = x * lax.rsqrt(ms + NORM_EPS) * nw_ref[...]
    h = h * (1.0 + sc_ref[0]) + sh_ref[0]
    hb = h.astype(BF16)

    lane = lax.broadcasted_iota(jnp.int32, (1, LANES), 1)
    first_half = (lane % HEAD_DIM) < (HEAD_DIM // 2)
    ang = pos_ref[0].astype(F32) * invf_ref[...]
    cos = jnp.cos(ang)
    sin = jnp.sin(ang)
    sin_signed = jnp.where(first_half, -sin, sin)

    sm = _dot(hb, ws_ref[...])
    lo_half = lane < IDX_DIM
    mu = jnp.sum(sm, axis=-1, keepdims=True) * (1.0 / IDX_DIM)
    d = jnp.where(lo_half, sm - mu, 0.0)
    var = jnp.sum(d * d, axis=-1, keepdims=True) * (1.0 / IDX_DIM)
    y = d * lax.rsqrt(var + LN_EPS) * lnw_ref[...] + lnb_ref[...]
    y = y + pltpu.roll(y, IDX_DIM, axis=1)
    ik_ref[0] = _rope_group(y, cos, sin_signed, first_half).astype(ik_ref.dtype)

    wt = _dot_nt(wwt_ref[...], hb)
    iwt_ref[0] = wt[:N_IDX_HEADS] * ((N_IDX_HEADS ** -0.5) * (IDX_DIM ** -0.5))

    def roped(col0, out_ref, scale):
        wide = _dot(hb, wm_ref[:, col0:col0 + 4 * LANES])
        for g in range(4):
            y = _rope_group(wide[:, g * LANES:(g + 1) * LANES], cos, sin_signed, first_half)
            if scale != 1.0:
                y = y * scale
            out_ref[0, :, g * LANES:(g + 1) * LANES] = y.astype(out_ref.dtype)

    qscale = HEAD_DIM ** -0.5 * math.log2(math.e)
    roped(0, aq_ref, qscale)
    roped(512, ak_ref, 1.0)
    roped(1024, iq_ref, 1.0)
    roped(1536, dq_ref, qscale)
    roped(2048, dk_ref, 1.0)

    n_sub = avt_ref.shape[1]
    chunk = avt_ref.shape[3]
    for j in range(n_sub):
        hj = hb[j * chunk:(j + 1) * chunk]
        avt_ref[0, j] = _dot_nt(wavt_ref[...], hj).astype(avt_ref.dtype)
        dvt_ref[0, j] = _dot_nt(wdvt_ref[...], hj).astype(dvt_ref.dtype)


def _in_proj(x, pos3, sh1, sc1, nw, invf, wm, ws, wwt, wavt, wdvt, lnw, lnb, *, tile, chunk):
    bsz, seq, d = x.shape
    tok = lambda b, i: (b, i, 0)
    per_b = lambda b, i: (b, 0, 0)
    wide = jax.ShapeDtypeStruct((bsz, seq, 512), BF16)
    wide_t = jax.ShapeDtypeStruct((bsz, seq // chunk, 512, chunk), BF16)
    wide_spec = pl.BlockSpec((1, tile, 512), tok)
    wide_t_spec = pl.BlockSpec((1, tile // chunk, 512, chunk), lambda b, i: (b, i, 0, 0))
    out_shape = [wide, wide, wide_t, wide,
                 jax.ShapeDtypeStruct((bsz, seq, LANES), BF16),
                 jax.ShapeDtypeStruct((bsz, N_IDX_HEADS, seq), F32),
                 wide, wide, wide_t]
    out_specs = [wide_spec, wide_spec, wide_t_spec, wide_spec,
                 pl.BlockSpec((1, tile, LANES), tok),
                 pl.BlockSpec((1, N_IDX_HEADS, tile), lambda b, i: (b, 0, i)),
                 wide_spec, wide_spec, wide_t_spec]
    return pl.pallas_call(
        _in_proj_kernel,
        grid=(bsz, seq // tile),
        in_specs=[pl.BlockSpec((1, tile, d), tok),
                  pl.BlockSpec((1, tile, 1), tok),
                  pl.BlockSpec((1, 1, d), per_b),
                  pl.BlockSpec((1, 1, d), per_b),
                  _const_spec((1, d)),
                  _const_spec((1, LANES)),
                  _const_spec(wm.shape),
                  _const_spec(ws.shape),
                  _const_spec(wwt.shape),
                  _const_spec(wavt.shape),
                  _const_spec(wdvt.shape),
                  _const_spec((1, LANES)),
                  _const_spec((1, LANES))],
        out_specs=out_specs,
        out_shape=out_shape,
        compiler_params=pltpu.CompilerParams(
            dimension_semantics=("parallel", "parallel"), vmem_limit_bytes=VMEM_LIMIT_BYTES),
        name="in_proj",
    )(x, pos3, sh1, sc1, nw, invf, wm, ws, wwt, wavt, wdvt, lnw, lnb)


def _head_q(q_all, h, lane_lo):
    pair = q_all[:, (h // 2) * LANES:(h // 2 + 1) * LANES]
    keep = lane_lo if h % 2 == 0 else jnp.logical_not(lane_lo)
    return jnp.where(keep, pair, jnp.zeros_like(pair))


def _softmax_update(s_list, vt_list, m_ref, l_ref, acc_ref):
    n = len(s_list)
    dv = vt_list[0].shape[0]
    ones = jnp.ones((BF16_SUBLANES, vt_list[0].shape[1]), BF16)
    m_old = [m_ref[g] for g in range(n)]
    m_new = [jnp.maximum(m_old[g], jnp.max(s_list[g], axis=0, keepdims=True)) for g in range(n)]
    p_list = [jnp.exp2(s_list[g] - m_new[g]).astype(BF16) for g in range(n)]
    alpha = [jnp.exp2(m_old[g] - m_new[g]) for g in range(n)]
    pv = [_dot(jnp.concatenate([vt_list[g], ones], axis=0), p_list[g]) for g in range(n)]
    for g in range(n):
        m_ref[g] = m_new[g]
        l_ref[g] = alpha[g] * l_ref[g] + pv[g][dv:dv + 1]
        acc_ref[g] = alpha[g] * acc_ref[g] + pv[g][:dv]


def _init_softmax_state(m_ref, l_ref, acc_ref):
    m_ref[...] = jnp.full(m_ref.shape, MASK_VALUE, F32)
    l_ref[...] = jnp.zeros(l_ref.shape, F32)
    acc_ref[...] = jnp.zeros(acc_ref.shape, F32)


def _bounded_update(s_list, vt_list, l_ref, acc_ref, before_last=None):
    n = len(s_list)
    dv = vt_list[0].shape[0]
    ones = jnp.ones((BF16_SUBLANES, vt_list[0].shape[1]), BF16)
    p_list = [jnp.exp2(s_list[g]).astype(BF16) for g in range(n)]
    pv = [_dot(jnp.concatenate([vt_list[g], ones], axis=0), p_list[g]) for g in range(n)]
    for g in range(n):
        col_sum = pv[g][dv:dv + 1]
        if before_last is not None and g == n - 1:
            col_sum = col_sum + before_last
        l_ref[g] = l_ref[g] + col_sum
        acc_ref[g] = acc_ref[g] + pv[g][:dv]


def _head_selector(n_lanes):
    r = lax.broadcasted_iota(jnp.int32, (BF16_SUBLANES, n_lanes), 0)
    l = lax.broadcasted_iota(jnp.int32, (BF16_SUBLANES, n_lanes), 1)
    return jnp.where(l // HEAD_DIM == r, 1.0, 0.0).astype(BF16)


def _head_sq_norms(x, head_sel):
    return _dot_nt(head_sel, x * x)


def _max_key_sq_norms(k_ref, kmax_ref, rows_per_step):
    n_lanes = k_ref.shape[2]
    l = lax.broadcasted_iota(jnp.int32, (n_lanes, LANES), 0)
    h = lax.broadcasted_iota(jnp.int32, (n_lanes, LANES), 1)
    sel_t = jnp.where(l // HEAD_DIM == h, 1.0, 0.0).astype(BF16)

    def body(c, mx):
        start = pl.multiple_of(c * rows_per_step, rows_per_step)
        kk = k_ref[0, pl.ds(start, rows_per_step), :]
        return jnp.maximum(mx, jnp.max(_dot(kk * kk, sel_t), axis=0, keepdims=True))
    row = lax.fori_loop(0, k_ref.shape[1] // rows_per_step, body, jnp.zeros((1, LANES), F32))
    r = lax.broadcasted_iota(jnp.int32, kmax_ref.shape, 0)
    c = lax.broadcasted_iota(jnp.int32, kmax_ref.shape, 1)
    col = jnp.sum(jnp.where(r == c, row, 0.0), axis=1, keepdims=True)
    kmax_ref[...] = jnp.broadcast_to(col, kmax_ref.shape)


def _softmax_underflowed(l_ref):
    return jnp.logical_not(jnp.min(l_ref[...]) >= MIN_SAFE_NORMALISER)


def _mixer_kernel(iq_ref, ik_ref, iwt_ref, aq_ref, ak_ref, avt_ref, lam_ref, dq_ref, dk_ref, dvt_ref, sw_ref,
                  oa_ref, ob_ref,
                  score_ref, m_ref, l_ref, acc_ref, kmax_ref, dm_ref, dl_ref, dacc_ref, dkmax_ref, bis_ref,
                  *, blk, top_k, lam_init):
    i = pl.program_id(1)
    n_chunks = i + 1
    n_maps = 2 * N_DIFF_HEADS
    lane = lax.broadcasted_iota(jnp.int32, (1, LANES), 1)
    lane_lo = lane < HEAD_DIM
    qcol = lax.broadcasted_iota(jnp.int32, (1, blk), 1)
    qpos = i * blk + qcol
    krow = lax.broadcasted_iota(jnp.int32, (blk, 1), 0)
    kf = float(top_k)

    aq_all = aq_ref[0]
    dq_all = dq_ref[0]
    head_sel = _head_selector(aq_all.shape[1])

    @pl.when(i == 0)
    def _():
        rows_per_step = math.gcd(4 * blk, ak_ref.shape[1])
        _max_key_sq_norms(ak_ref, kmax_ref, rows_per_step)
        _max_key_sq_norms(dk_ref, dkmax_ref, rows_per_step)

    d_maps = [_head_q(dq_all, g, lane_lo) for g in range(n_maps)]
    d_bound = jnp.sqrt(_head_sq_norms(dq_all, head_sel) * dkmax_ref[:, :1])
    d_shifts = [d_bound[g:g + 1] for g in range(n_maps)]
    a_bound = jnp.sqrt(_head_sq_norms(aq_all, head_sel) * kmax_ref[:, :1])
    qa_heads = [_head_q(aq_all, h, lane_lo) for h in range(N_DSA_HEADS)]
    lam_vecs = lam_ref[...]
    lam = (jnp.exp(jnp.sum(lam_vecs[0:1] * lam_vecs[1:2], axis=-1, keepdims=True))
           - jnp.exp(jnp.sum(lam_vecs[2:3] * lam_vecs[3:4], axis=-1, keepdims=True)) + lam_init)

    def diff_chunk(c, diagonal, bounded, after=None, before_last=None):
        start = pl.multiple_of(c * blk, blk)
        if diagonal:
            bias = jnp.where(krow <= qcol, 0.0, MASK_VALUE)
        s_list, vt_list = [], []
        for h in range(N_DIFF_HEADS):
            cols = slice(h * LANES, (h + 1) * LANES)
            kk = dk_ref[0, pl.ds(start, blk), cols]
            for comp in range(2):
                g = 2 * h + comp
                s = _dot_nt(kk, d_maps[g])
                if bounded:
                    s = s - (d_shifts[g] if after is None else d_shifts[g] + after)
                s_list.append(s + bias if diagonal else s)
                vt_list.append(dvt_ref[0, c, cols, :])
        if bounded:
            _bounded_update(s_list, vt_list, dl_ref, dacc_ref, before_last)
        else:
            _softmax_update(s_list, vt_list, dm_ref, dl_ref, dacc_ref)

    _init_softmax_state(dm_ref, dl_ref, dacc_ref)

    iq_all = iq_ref[0]
    wt = iwt_ref[0]
    q_heads = [_head_q(iq_all, h, lane_lo) for h in range(N_IDX_HEADS)]
    w_rows = [wt[h:h + 1, :] for h in range(N_IDX_HEADS)]

    n_pairs = (n_chunks + 1) // 2

    def pair_sum(hit):
        m = jnp.where(hit, 1.0, 0.0)
        return jnp.sum((m[0] + m[1]).reshape(blk // F32_SUBLANES, F32_SUBLANES, blk), axis=0)

    zero8 = jnp.zeros((F32_SUBLANES, blk), F32)

    def score_pair(j, mn, mx, w_first):
        for u in range(2):
            c = 2 * j + u
            start = pl.multiple_of(c * blk, blk)
            kk = ik_ref[0, pl.ds(start, blk), :]
            acc = jnp.zeros((blk, blk), F32)
            for h in range(N_IDX_HEADS):
                acc = acc + jnp.maximum(_dot_nt(kk, q_heads[h]), 0.0) * (w_first if h == 0 else w_rows[h])
            causal = (c * blk + krow) <= qpos
            sc = jnp.where(causal, acc, -jnp.inf)
            score_ref[c] = sc
            mx = jnp.maximum(mx, jnp.max(sc, axis=0, keepdims=True))
            mn = jnp.minimum(mn, jnp.min(jnp.where(causal, acc, jnp.inf), axis=0, keepdims=True))
        return mn, mx

    def zero_counts(j):
        s = score_ref[pl.ds(2 * j, 2)]
        return pair_sum(s >= 0.0), pair_sum(s > 0.0)

    def score_body(j, carry):
        mn, mx, ge, gt = carry
        ge_p, gt_p = zero_counts(j - 1)
        after = jnp.sum(ge_p + gt_p, axis=0, keepdims=True) * 0.0
        mn, mx = score_pair(j, mn, mx, w_rows[0] + after)
        return mn, mx, ge + ge_p, gt + gt_p

    mn, mx = score_pair(0, jnp.full((1, blk), jnp.inf, F32), jnp.full((1, blk), -jnp.inf, F32), w_rows[0])
    mn, mx, ge8, gt8 = lax.fori_loop(1, n_pairs, score_body, (mn, mx, zero8, zero8))
    ge_p, gt_p = zero_counts(n_pairs - 1)
    ge0 = jnp.sum(ge8 + ge_p, axis=0, keepdims=True)
    gt0 = jnp.sum(gt8 + gt_p, axis=0, keepdims=True)


    def count_ge(cand):
        def body(j, acc):
            return acc + pair_sum(score_ref[pl.ds(2 * j, 2)] >= cand)
        return jnp.sum(lax.fori_loop(0, n_pairs, body, zero8), axis=0, keepdims=True)

    def count_ge_gt(cand):
        def body(j, carry):
            ge, gt = carry
            s = score_ref[pl.ds(2 * j, 2)]
            return ge + pair_sum(s >= cand), gt + pair_sum(s > cand)
        ge, gt = lax.fori_loop(0, n_pairs, body, (zero8, zero8))
        return jnp.sum(ge, axis=0, keepdims=True), jnp.sum(gt, axis=0, keepdims=True)

    n_valid = (qpos + 1).astype(F32)
    need = n_valid > kf
    kth_is_zero = jnp.logical_and(gt0 < kf, ge0 >= kf)
    nonneg = ge0 >= kf
    lo0 = jnp.where(nonneg, 0.0, mn)
    cnt0 = jnp.where(nonneg, ge0, n_valid)
    above_max = jnp.where(mx > 0.0, jnp.minimum(mx * 2.0, F32_MAX), 1.0)
    hi0 = jnp.where(nonneg, above_max, 0.0)
    done0 = jnp.logical_or(jnp.logical_not(need), jnp.logical_or(kth_is_zero, cnt0 == kf))
    active0 = jnp.where(done0, 0.0, 1.0)

    def midpoint(st):
        return 0.5 * st[0] + 0.5 * st[1]

    def halve(st, count):
        mid = midpoint(st)
        return narrow(st, mid, count(mid))

    def narrow(st, mid, cnt):
        lo, hi, cnt_lo, active = st
        conv = jnp.logical_or(mid <= lo, mid >= hi)
        upd = jnp.logical_and(active > 0.0, jnp.logical_not(conv))
        ge = cnt >= kf
        up_lo = jnp.logical_and(upd, ge)
        up_hi = jnp.logical_and(upd, jnp.logical_not(ge))
        lo = jnp.where(up_lo, mid, lo)
        cnt_lo = jnp.where(up_lo, cnt, cnt_lo)
        hi = jnp.where(up_hi, mid, hi)
        finished = jnp.logical_or(conv, jnp.logical_and(up_lo, cnt == kf))
        return lo, hi, cnt_lo, jnp.where(finished, 0.0, active)

    n_fused = jnp.minimum(i, BISECT_FUSED_STEPS)
    for k, v in enumerate((lo0, hi0, cnt0, active0)):
        bis_ref[k] = v

    for pairs in range(1, score_ref.shape[0] // 2 + 1):
        @pl.when(n_pairs == pairs)
        def _(pairs=pairs):
            def fused_step(c, st):
                mid = midpoint(st)
                n_first = min(FUSED_COUNTS_UNDER_QK, (pairs + 1) // 2)
                parts = []
                for js in (range(0, n_first), range(n_first, pairs)):
                    acc = zero8
                    for j in js:
                        acc = acc + pair_sum(score_ref[2 * j:2 * j + 2] >= mid)
                    parts.append(jnp.sum(acc, axis=0, keepdims=True))
                diff_chunk(c, False, True, after=parts[0] * 0.0,
                           before_last=parts[1] * 0.0 if pairs > n_first else None)
                return narrow(st, mid, parts[0] + parts[1])
            st = lax.fori_loop(0, n_fused, fused_step, tuple(bis_ref[k] for k in range(4)))
            for k in range(4):
                bis_ref[k] = st[k]

    st = tuple(bis_ref[k] for k in range(4))
    st = lax.fori_loop(0, jnp.maximum(BISECT_FIRST_STEPS - n_fused, 0),
                       lambda _, s: halve(s, count_ge), st)

    def bis_cond(st):
        return jnp.logical_and(st[4] > 0.0, st[5] < MAX_BISECT_STEPS)

    def bis_body(st):
        lo, hi, cnt_lo, active = lax.fori_loop(0, BISECT_STEPS_PER_TEST, lambda _, s: halve(s, count_ge), st[:4])
        return lo, hi, cnt_lo, active, jnp.max(active), st[5] + BISECT_STEPS_PER_TEST

    lo, _, cnt_lo, _, _, _ = lax.while_loop(bis_cond, bis_body, st + (jnp.float32(1.0), jnp.int32(0)))
    thr = jnp.where(need, lo, -F32_MAX)

    excess = jnp.logical_and(need, cnt_lo > kf)

    @pl.when(jnp.max(jnp.where(excess, 1.0, 0.0)) > 0.0)
    def _():
        _, gt = count_ge_gt(thr)
        allow = kf - gt
        r_i = lax.broadcasted_iota(jnp.int32, (blk, blk), 0)
        c_i = lax.broadcasted_iota(jnp.int32, (blk, blk), 1)
        earlier = jnp.where(c_i < r_i, 1.0, 0.0).astype(BF16)

        def tie_body(c, seen):
            s = score_ref[c]
            eq = jnp.logical_and(s == thr, excess)
            eqf = jnp.where(eq, 1.0, 0.0)
            rank = seen + _dot(earlier, eqf.astype(BF16))
            drop = jnp.logical_and(eq, rank >= allow)
            score_ref[c] = jnp.where(drop, -jnp.inf, s)
            return seen + jnp.sum(eqf, axis=0, keepdims=True)

        lax.fori_loop(0, n_chunks, tie_body, jnp.zeros((1, blk), F32))

    def attend(bounded, shift):
        def att_body(c, carry):
            start = pl.multiple_of(c * blk, blk)
            bias = jnp.where(score_ref[c] >= thr, shift, MASK_VALUE)
            s_list, vt_list = [], []
            for h in range(N_DSA_HEADS):
                pair = slice((h // 2) * LANES, (h // 2 + 1) * LANES)
                kk = ak_ref[0, pl.ds(start, blk), pair]
                s_list.append(_dot_nt(kk, qa_heads[h]) + bias)
                vt_list.append(avt_ref[0, c, pair, :])
            if bounded:
                _bounded_update(s_list, vt_list, l_ref, acc_ref)
            else:
                _softmax_update(s_list, vt_list, m_ref, l_ref, acc_ref)
            return carry

        _init_softmax_state(m_ref, l_ref, acc_ref)
        lax.fori_loop(0, n_chunks, att_body, 0)

    if score_ref.shape[0] - 1 > BISECT_FUSED_STEPS:
        def diff_rest(c, carry):
            diff_chunk(c, False, True)
            return carry

        lax.fori_loop(n_fused, i, diff_rest, 0)

    attend(True, -jnp.max(a_bound, axis=0, keepdims=True))

    def write_outputs():
        for j in range(N_DSA_HEADS // 2):
            a = acc_ref[2 * j] * (1.0 / l_ref[2 * j])
            b = acc_ref[2 * j + 1] * (1.0 / l_ref[2 * j + 1])
            o_t = jnp.concatenate([a[:HEAD_DIM], b[HEAD_DIM:]], axis=0)
            oa_ref[0, j * LANES:(j + 1) * LANES, :] = o_t.astype(oa_ref.dtype)
        for h in range(N_DIFF_HEADS):
            out = (dacc_ref[2 * h] * (1.0 / dl_ref[2 * h])
                   - dacc_ref[2 * h + 1] * (lam / dl_ref[2 * h + 1]))
            out = out * lax.rsqrt(jnp.mean(out * out, axis=0, keepdims=True) + NORM_EPS)
            out = out * sw_ref[...] * (1.0 - lam_init)
            ob_ref[0, h * LANES:(h + 1) * LANES, :] = out.astype(ob_ref.dtype)

    diff_chunk(i, True, True)
    dsa_bad = _softmax_underflowed(l_ref)
    diff_bad = _softmax_underflowed(dl_ref)
    write_outputs()

    @pl.when(jnp.logical_or(dsa_bad, diff_bad))
    def _():
        @pl.when(dsa_bad)
        def _():
            attend(False, 0.0)

        @pl.when(diff_bad)
        def _():
            def body(c, carry):
                diff_chunk(c, False, False)
                return carry
            _init_softmax_state(dm_ref, dl_ref, dacc_ref)
            lax.fori_loop(0, i, body, 0)
            diff_chunk(i, True, False)

        write_outputs()


def _mixers(iq, ik2, iwt, aq, ak, avt, lam_vecs, dq, dk, dvt, subln_col, *, blk, lam_init):
    bsz, seq, _ = aq.shape
    top_k = min(TOPK_MAX, seq // 4)
    n_chunks = seq // blk
    assert n_chunks % 2 == 0, "key chunks are walked in pairs"
    qblk = lambda b, i: (b, i, 0)
    whole = lambda b, i: (b, 0, 0)
    whole_t = lambda b, i: (b, 0, 0, 0)
    n_maps = 2 * N_DIFF_HEADS
    q_spec = pl.BlockSpec((1, blk, 512), qblk)
    kv_spec = pl.BlockSpec((1, seq, 512), whole)
    vt_spec = pl.BlockSpec((1, n_chunks, 512, blk), whole_t)
    return pl.pallas_call(
        functools.partial(_mixer_kernel, blk=blk, top_k=top_k, lam_init=lam_init),
        grid=(bsz, n_chunks),
        in_specs=[q_spec,
                  pl.BlockSpec((1, seq, LANES), whole),
                  pl.BlockSpec((1, N_IDX_HEADS, blk), lambda b, i: (b, 0, i)),
                  q_spec, kv_spec, vt_spec,
                  _const_spec(lam_vecs.shape),
                  q_spec, kv_spec, vt_spec,
                  _const_spec((DIFF_V_DIM, 1))],
        out_specs=[pl.BlockSpec((1, 512, blk), lambda b, i: (b, 0, i))] * 2,
        out_shape=[jax.ShapeDtypeStruct((bsz, 512, seq), BF16)] * 2,
        scratch_shapes=[pltpu.VMEM((n_chunks, blk, blk), F32),
                        pltpu.VMEM((N_DSA_HEADS, 1, blk), F32),
                        pltpu.VMEM((N_DSA_HEADS, 1, blk), F32),
                        pltpu.VMEM((N_DSA_HEADS, LANES, blk), F32),
                        pltpu.VMEM((BF16_SUBLANES, LANES), F32),
                        pltpu.VMEM((n_maps, 1, blk), F32),
                        pltpu.VMEM((n_maps, 1, blk), F32),
                        pltpu.VMEM((n_maps, DIFF_V_DIM, blk), F32),
                        pltpu.VMEM((BF16_SUBLANES, LANES), F32),
                        pltpu.VMEM((4, 1, blk), F32)],
        compiler_params=pltpu.CompilerParams(
            dimension_semantics=("parallel", "arbitrary"), vmem_limit_bytes=VMEM_LIMIT_BYTES),
        name="mixers",
    )(iq, ik2, iwt, aq, ak, avt, lam_vecs, dq, dk, dvt, subln_col)


def _out_mlp_kernel(x_ref, ma_ref, mb_ref, g1_ref, sh_ref, sc_ref, g2_ref, n2_ref, nf_ref,
                    woa_ref, wob_ref, w1_ref, w2_ref, o_ref, *, ff_chunk):
    x = x_ref[0]
    o = _dot_tn(ma_ref[0], woa_ref[...]) + _dot_tn(mb_ref[0], wob_ref[...])
    x1 = x + g1_ref[0] * o
    ms = jnp.mean(x1 * x1, axis=-1, keepdims=True)
    h = x1 * lax.rsqrt(ms + NORM_EPS) * n2_ref[...]
    hb = (h * (1.0 + sc_ref[0]) + sh_ref[0]).astype(BF16)
    d_ff = w1_ref.shape[1]
    ff = jnp.zeros_like(x)
    for j in range(d_ff // ff_chunk):
        u = jnp.maximum(_dot(hb, w1_ref[:, j * ff_chunk:(j + 1) * ff_chunk]), 0.0)
        ff = ff + _dot((u * u).astype(BF16), w2_ref[j * ff_chunk:(j + 1) * ff_chunk, :])
    x2 = x1 + g2_ref[0] * ff
    ms2 = jnp.mean(x2 * x2, axis=-1, keepdims=True)
    o_ref[0] = x2 * lax.rsqrt(ms2 + NORM_EPS) * nf_ref[...]


def _out_mlp(x, mix_a, mix_b, g1, sh2, sc2, g2, n2, nf, wo_a, wo_b, w1, w2, *, tile):
    bsz, seq, d = x.shape
    tok = lambda b, i: (b, i, 0)
    per_b = lambda b, i: (b, 0, 0)
    mod_spec = pl.BlockSpec((1, 1, d), per_b)
    return pl.pallas_call(
        functools.partial(_out_mlp_kernel, ff_chunk=1024),
        grid=(bsz, seq // tile),
        in_specs=[pl.BlockSpec((1, tile, d), tok),
                  pl.BlockSpec((1, 512, tile), lambda b, i: (b, 0, i)),
                  pl.BlockSpec((1, 512, tile), lambda b, i: (b, 0, i)),
                  mod_spec, mod_spec, mod_spec, mod_spec,
                  _const_spec((1, d)), _const_spec((1, d)),
                  _const_spec(wo_a.shape), _const_spec(wo_b.shape),
                  _const_spec(w1.shape), _const_spec(w2.shape)],
        out_specs=pl.BlockSpec((1, tile, d), tok),
        out_shape=jax.ShapeDtypeStruct((bsz, seq, d), F32),
        compiler_params=pltpu.CompilerParams(
            dimension_semantics=("parallel", "parallel"), vmem_limit_bytes=VMEM_LIMIT_BYTES),
        name="out_mlp",
    )(x, mix_a, mix_b, g1, sh2, sc2, g2, n2, nf, wo_a, wo_b, w1, w2)


def _pad_cols(w, width):
    return jnp.pad(w, ((0, 0), (0, width - w.shape[1])))


def kernel(x, c, positions, w_ada, b_ada, norm1_w, w_in, idx_k_ln_w, idx_k_ln_b, lambda_q1, lambda_k1,
           lambda_q2, lambda_k2, subln_w, w_out, norm2_w, w_ff1, w_ff2, norm_f_w):
    bsz, seq, d = x.shape
    depth = w_ada.shape[0]
    assert depth == 1, "the fused final RMSNorm assumes a single layer"
    half = HEAD_DIM // 2
    inv_freq = ROPE_THETA ** (-jnp.arange(half, dtype=F32) / half)
    invf = jnp.tile(inv_freq, LANES // half).reshape(1, LANES)
    pos3 = positions.reshape(bsz, seq, 1)
    blk = min(ATT_BLOCK, seq)
    tile = min(512, seq)

    for l in range(depth):
        mod = _adaln(c, w_ada[l], b_ada[l])
        sh1, sc1, g1, sh2, sc2, g2 = [m.reshape(bsz, 1, d) for m in jnp.split(mod, 6, axis=-1)]

        w = w_in[l]
        wm = jnp.concatenate([w[:, :1024], w[:, 1536:2048], w[:, 2120:3144]], axis=1).astype(BF16)
        ws = _pad_cols(w[:, 2048:2112], LANES).astype(BF16)
        wwt = jnp.pad(w[:, 2112:2120].T, ((0, BF16_SUBLANES - N_IDX_HEADS), (0, 0))).astype(BF16)
        wavt = w[:, 1024:1536].T.astype(BF16)
        wdvt = w[:, 3144:3656].T.astype(BF16)
        lnw = _pad_cols(idx_k_ln_w[l].reshape(1, IDX_DIM), LANES)
        lnb = _pad_cols(idx_k_ln_b[l].reshape(1, IDX_DIM), LANES)

        aq, ak, avt, iq, ik2, iwt, dq, dk, dvt = _in_proj(
            x, pos3, sh1, sc1, norm1_w[l].reshape(1, d), invf, wm, ws, wwt, wavt, wdvt, lnw, lnb,
            tile=tile, chunk=blk)

        lam_vecs = jnp.stack([lambda_q1[l], lambda_k1[l], lambda_q2[l], lambda_k2[l]]).astype(F32)
        lam_init = 0.8 - 0.6 * math.exp(-0.3 * l)
        out_a, out_b = _mixers(iq, ik2, iwt, aq, ak, avt, lam_vecs, dq, dk, dvt,
                               subln_w[l].reshape(DIFF_V_DIM, 1), blk=blk, lam_init=lam_init)

        wo = w_out[l].astype(BF16)
        x = _out_mlp(x, out_a, out_b, g1, sh2, sc2, g2, norm2_w[l].reshape(1, d), norm_f_w.reshape(1, d),
                     wo[:512], wo[512:], w_ff1[l].astype(BF16), w_ff2[l].astype(BF16), tile=tile)
    return x
```

```python
import functools
import math

import jax
import jax.numpy as jnp
from jax import lax
from jax.experimental import pallas as pl
from jax.experimental.pallas import tpu as pltpu

F32 = jnp.float32
BF16 = jnp.bfloat16

HEAD_DIM = 64
N_DSA_HEADS = 8
N_IDX_HEADS = 8
IDX_DIM = 64
N_DIFF_HEADS = 4
DIFF_V_DIM = 128
TOPK_MAX = 256
ROPE_THETA = 10000.0
NORM_EPS = 1e-6
LN_EPS = 1e-5
LANES = 128
F32_SUBLANES = 8
BF16_SUBLANES = 16
ATT_BLOCK = 256
MASK_VALUE = -1e30
F32_MAX = float(jnp.finfo(jnp.float32).max)
MAX_BISECT_STEPS = 512
BISECT_FIRST_STEPS = 16
BISECT_FUSED_STEPS = 16
BISECT_STEPS_PER_TEST = 2
FUSED_COUNTS_UNDER_QK = 2
MIN_SAFE_NORMALISER = 2.0 ** -100
VMEM_LIMIT_BYTES = 56 * 1024 * 1024


def _dot(a, b):
    return jnp.dot(a, b, preferred_element_type=F32)


def _dot_nt(a, b):
    return lax.dot_general(a, b, (((1,), (1,)), ((), ())), preferred_element_type=F32)


def _dot_tn(a, b):
    return lax.dot_general(a, b, (((0,), (0,)), ((), ())), preferred_element_type=F32)


def _const_spec(shape):
    zeros = (0,) * len(shape)
    return pl.BlockSpec(shape, lambda *_: zeros, pipeline_mode=pl.Buffered(1))


def _adaln_kernel(c_ref, w_ref, b_ref, o_ref):
    c = c_ref[...]
    s = c / (1.0 + jnp.exp(-c))
    o_ref[...] = jnp.dot(s, w_ref[...], preferred_element_type=F32,
                         precision=lax.Precision.HIGHEST) + b_ref[...]


def _adaln(c, w, b):
    bsz, d = c.shape
    n = w.shape[1]
    tn = 1536
    return pl.pallas_call(
        _adaln_kernel,
        grid=(n // tn,),
        in_specs=[pl.BlockSpec((bsz, d), lambda j: (0, 0)),
                  pl.BlockSpec((d, tn), lambda j: (0, j)),
                  pl.BlockSpec((1, tn), lambda j: (0, j))],
        out_specs=pl.BlockSpec((bsz, tn), lambda j: (0, j)),
        out_shape=jax.ShapeDtypeStruct((bsz, n), F32),
        compiler_params=pltpu.CompilerParams(vmem_limit_bytes=VMEM_LIMIT_BYTES),
        name="adaln",
    )(c, w, b.reshape(1, n))


def _rope_group(x, cos, sin_signed, first_half):
    nxt = pltpu.roll(x, LANES - HEAD_DIM // 2, axis=1)
    prv = pltpu.roll(x, HEAD_DIM // 2, axis=1)
    return x * cos + jnp.where(first_half, nxt, prv) * sin_signed


def _in_proj_kernel(x_ref, pos_ref, sh_ref, sc_ref, nw_ref, invf_ref, wm_ref, ws_ref, wwt_ref,
                    wavt_ref, wdvt_ref, lnw_ref, lnb_ref,
                    aq_ref, ak_ref, avt_ref, iq_ref, ik_ref, iwt_ref, dq_ref, dk_ref, dvt_ref):
    x = x_ref[0]
    ms = jnp.mean(x * x, axis=-1, keepdims=True)
    h = x * lax.rsqrt(ms + NORM_EPS) * nw_ref[...]
    h = h * (1.0 + sc_ref[0]) + sh_ref[0]
    hb = h.astype(BF16)

    lane = lax.broadcasted_iota(jnp.int32, (1, LANES), 1)
    first_half = (lane % HEAD_DIM) < (HEAD_DIM // 2)
    ang = pos_ref[0].astype(F32) * invf_ref[...]
    cos = jnp.cos(ang)
    sin = jnp.sin(ang)
    sin_signed = jnp.where(first_half, -sin, sin)

    sm = _dot(hb, ws_ref[...])
    lo_half = lane < IDX_DIM
    mu = jnp.sum(sm, axis=-1, keepdims=True) * (1.0 / IDX_DIM)
    d = jnp.where(lo_half, sm - mu, 0.0)
    var = jnp.sum(d * d, axis=-1, keepdims=True) * (1.0 / IDX_DIM)
    y = d * lax.rsqrt(var + LN_EPS) * lnw_ref[...] + lnb_ref[...]
    y = y + pltpu.roll(y, IDX_DIM, axis=1)
    ik_ref[0] = _rope_group(y, cos, sin_signed, first_half).astype(ik_ref.dtype)

    wt = _dot_nt(wwt_ref[...], hb)
    iwt_ref[0] = wt[:N_IDX_HEADS] * ((N_IDX_HEADS ** -0.5) * (IDX_DIM ** -0.5))

    def roped(col0, out_ref, scale):
        wide = _dot(hb, wm_ref[:, col0:col0 + 4 * LANES])
        for g in range(4):
            y = _rope_group(wide[:, g * LANES:(g + 1) * LANES], cos, sin_signed, first_half)
            if scale != 1.0:
                y = y * scale
            out_ref[0, :, g * LANES:(g + 1) * LANES] = y.astype(out_ref.dtype)

    qscale = HEAD_DIM ** -0.5 * math.log2(math.e)
    roped(0, aq_ref, qscale)
    roped(512, ak_ref, 1.0)
    roped(1024, iq_ref, 1.0)
    roped(1536, dq_ref, qscale)
    roped(2048, dk_ref, 1.0)

    n_sub = avt_ref.shape[1]
    chunk = avt_ref.shape[3]
    for j in range(n_sub):
        hj = hb[j * chunk:(j + 1) * chunk]
        avt_ref[0, j] = _dot_nt(wavt_ref[...], hj).astype(avt_ref.dtype)
        dvt_ref[0, j] = _dot_nt(wdvt_ref[...], hj).astype(dvt_ref.dtype)


def _in_proj(x, pos3, sh1, sc1, nw, invf, wm, ws, wwt, wavt, wdvt, lnw, lnb, *, tile, chunk):
    bsz, seq, d = x.shape
    tok = lambda b, i: (b, i, 0)
    per_b = lambda b, i: (b, 0, 0)
    wide = jax.ShapeDtypeStruct((bsz, seq, 512), BF16)
    wide_t = jax.ShapeDtypeStruct((bsz, seq // chunk, 512, chunk), BF16)
    wide_spec = pl.BlockSpec((1, tile, 512), tok)
    wide_t_spec = pl.BlockSpec((1, tile // chunk, 512, chunk), lambda b, i: (b, i, 0, 0))
    out_shape = [wide, wide, wide_t, wide,
                 jax.ShapeDtypeStruct((bsz, seq, LANES), BF16),
                 jax.ShapeDtypeStruct((bsz, N_IDX_HEADS, seq), F32),
                 wide, wide, wide_t]
    out_specs = [wide_spec, wide_spec, wide_t_spec, wide_spec,
                 pl.BlockSpec((1, tile, LANES), tok),
                 pl.BlockSpec((1, N_IDX_HEADS, tile), lambda b, i: (b, 0, i)),
                 wide_spec, wide_spec, wide_t_spec]
    return pl.pallas_call(
        _in_proj_kernel,
        grid=(bsz, seq // tile),
        in_specs=[pl.BlockSpec((1, tile, d), tok),
                  pl.BlockSpec((1, tile, 1), tok),
                  pl.BlockSpec((1, 1, d), per_b),
                  pl.BlockSpec((1, 1, d), per_b),
                  _const_spec((1, d)),
                  _const_spec((1, LANES)),
                  _const_spec(wm.shape),
                  _const_spec(ws.shape),
                  _const_spec(wwt.shape),
                  _const_spec(wavt.shape),
                  _const_spec(wdvt.shape),
                  _const_spec((1, LANES)),
                  _const_spec((1, LANES))],
        out_specs=out_specs,
        out_shape=out_shape,
        compiler_params=pltpu.CompilerParams(
            dimension_semantics=("parallel", "parallel"), vmem_limit_bytes=VMEM_LIMIT_BYTES),
        name="in_proj",
    )(x, pos3, sh1, sc1, nw, invf, wm, ws, wwt, wavt, wdvt, lnw, lnb)


def _head_q(q_all, h, lane_lo):
    pair = q_all[:, (h // 2) * LANES:(h // 2 + 1) * LANES]
    keep = lane_lo if h % 2 == 0 else jnp.logical_not(lane_lo)
    return jnp.where(keep, pair, jnp.zeros_like(pair))


def _softmax_update(s_list, vt_list, m_ref, l_ref, acc_ref):
    n = len(s_list)
    dv = vt_list[0].shape[0]
    ones = jnp.ones((BF16_SUBLANES, vt_list[0].shape[1]), BF16)
    m_old = [m_ref[g] for g in range(n)]
    m_new = [jnp.maximum(m_old[g], jnp.max(s_list[g], axis=0, keepdims=True)) for g in range(n)]
    p_list = [jnp.exp2(s_list[g] - m_new[g]).astype(BF16) for g in range(n)]
    alpha = [jnp.exp2(m_old[g] - m_new[g]) for g in range(n)]
    pv = [_dot(jnp.concatenate([vt_list[g], ones], axis=0), p_list[g]) for g in range(n)]
    for g in range(n):
        m_ref[g] = m_new[g]
        l_ref[g] = alpha[g] * l_ref[g] + pv[g][dv:dv + 1]
        acc_ref[g] = alpha[g] * acc_ref[g] + pv[g][:dv]


def _init_softmax_state(m_ref, l_ref, acc_ref):
    m_ref[...] = jnp.full(m_ref.shape, MASK_VALUE, F32)
    l_ref[...] = jnp.zeros(l_ref.shape, F32)
    acc_ref[...] = jnp.zeros(acc_ref.shape, F32)


def _bounded_update(s_list, vt_list, l_ref, acc_ref, before_last=None):
    n = len(s_list)
    dv = vt_list[0].shape[0]
    ones = jnp.ones((BF16_SUBLANES, vt_list[0].shape[1]), BF16)
    p_list = [jnp.exp2(s_list[g]).astype(BF16) for g in range(n)]
    pv = [_dot(jnp.concatenate([vt_list[g], ones], axis=0), p_list[g]) for g in range(n)]
    for g in range(n):
        col_sum = pv[g][dv:dv + 1]
        if before_last is not None and g == n - 1:
            col_sum = col_sum + before_last
        l_ref[g] = l_ref[g] + col_sum
        acc_ref[g] = acc_ref[g] + pv[g][:dv]


def _head_selector(n_lanes):
    r = lax.broadcasted_iota(jnp.int32, (BF16_SUBLANES, n_lanes), 0)
    l = lax.broadcasted_iota(jnp.int32, (BF16_SUBLANES, n_lanes), 1)
    return jnp.where(l // HEAD_DIM == r, 1.0, 0.0).astype(BF16)


def _head_sq_norms(x, head_sel):
    return _dot_nt(head_sel, x * x)


def _max_key_sq_norms(k_ref, kmax_ref, rows_per_step):
    n_lanes = k_ref.shape[2]
    l = lax.broadcasted_iota(jnp.int32, (n_lanes, LANES), 0)
    h = lax.broadcasted_iota(jnp.int32, (n_lanes, LANES), 1)
    sel_t = jnp.where(l // HEAD_DIM == h, 1.0, 0.0).astype(BF16)

    def body(c, mx):
        start = pl.multiple_of(c * rows_per_step, rows_per_step)
        kk = k_ref[0, pl.ds(start, rows_per_step), :]
        return jnp.maximum(mx, jnp.max(_dot(kk * kk, sel_t), axis=0, keepdims=True))
    row = lax.fori_loop(0, k_ref.shape[1] // rows_per_step, body, jnp.zeros((1, LANES), F32))
    r = lax.broadcasted_iota(jnp.int32, kmax_ref.shape, 0)
    c = lax.broadcasted_iota(jnp.int32, kmax_ref.shape, 1)
    col = jnp.sum(jnp.where(r == c, row, 0.0), axis=1, keepdims=True)
    kmax_ref[...] = jnp.broadcast_to(col, kmax_ref.shape)


def _softmax_underflowed(l_ref):
    return jnp.logical_not(jnp.min(l_ref[...]) >= MIN_SAFE_NORMALISER)


def _mixer_kernel(iq_ref, ik_ref, iwt_ref, aq_ref, ak_ref, avt_ref, lam_ref, dq_ref, dk_ref, dvt_ref, sw_ref,
                  oa_ref, ob_ref,
                  score_ref, m_ref, l_ref, acc_ref, kmax_ref, dm_ref, dl_ref, dacc_ref, dkmax_ref, bis_ref,
                  *, blk, top_k, lam_init):
    i = pl.program_id(1)
    n_chunks = i + 1
    n_maps = 2 * N_DIFF_HEADS
    lane = lax.broadcasted_iota(jnp.int32, (1, LANES), 1)
    lane_lo = lane < HEAD_DIM
    qcol = lax.broadcasted_iota(jnp.int32, (1, blk), 1)
    qpos = i * blk + qcol
    krow = lax.broadcasted_iota(jnp.int32, (blk, 1), 0)
    kf = float(top_k)

    aq_all = aq_ref[0]
    dq_all = dq_ref[0]
    head_sel = _head_selector(aq_all.shape[1])

    @pl.when(i == 0)
    def _():
        rows_per_step = math.gcd(4 * blk, ak_ref.shape[1])
        _max_key_sq_norms(ak_ref, kmax_ref, rows_per_step)
        _max_key_sq_norms(dk_ref, dkmax_ref, rows_per_step)

    d_maps = [_head_q(dq_all, g, lane_lo) for g in range(n_maps)]
    d_bound = jnp.sqrt(_head_sq_norms(dq_all, head_sel) * dkmax_ref[:, :1])
    d_shifts = [d_bound[g:g + 1] for g in range(n_maps)]
    a_bound = jnp.sqrt(_head_sq_norms(aq_all, head_sel) * kmax_ref[:, :1])
    qa_heads = [_head_q(aq_all, h, lane_lo) for h in range(N_DSA_HEADS)]
    lam_vecs = lam_ref[...]
    lam = (jnp.exp(jnp.sum(lam_vecs[0:1] * lam_vecs[1:2], axis=-1, keepdims=True))
           - jnp.exp(jnp.sum(lam_vecs[2:3] * lam_vecs[3:4], axis=-1, keepdims=True)) + lam_init)

    def diff_chunk(c, diagonal, bounded, after=None, before_last=None):
        start = pl.multiple_of(c * blk, blk)
        if diagonal:
            bias = jnp.where(krow <= qcol, 0.0, MASK_VALUE)
        s_list, vt_list = [], []
        for h in range(N_DIFF_HEADS):
            cols = slice(h * LANES, (h + 1) * LANES)
            kk = dk_ref[0, pl.ds(start, blk), cols]
            for comp in range(2):
                g = 2 * h + comp
                s = _dot_nt(kk, d_maps[g])
                if bounded:
                    s = s - (d_shifts[g] if after is None else d_shifts[g] + after)
                s_list.append(s + bias if diagonal else s)
                vt_list.append(dvt_ref[0, c, cols, :])
        if bounded:
            _bounded_update(s_list, vt_list, dl_ref, dacc_ref, before_last)
        else:
            _softmax_update(s_list, vt_list, dm_ref, dl_ref, dacc_ref)

    _init_softmax_state(dm_ref, dl_ref, dacc_ref)

    iq_all = iq_ref[0]
    wt = iwt_ref[0]
    q_heads = [_head_q(iq_all, h, lane_lo) for h in range(N_IDX_HEADS)]
    w_rows = [wt[h:h + 1, :] for h in range(N_IDX_HEADS)]

    n_pairs = (n_chunks + 1) // 2

    def pair_sum(hit):
        m = jnp.where(hit, 1.0, 0.0)
        return jnp.sum((m[0] + m[1]).reshape(blk // F32_SUBLANES, F32_SUBLANES, blk), axis=0)

    zero8 = jnp.zeros((F32_SUBLANES, blk), F32)

    def score_pair(j, mn, mx, w_first):
        for u in range(2):
            c = 2 * j + u
            start = pl.multiple_of(c * blk, blk)
            kk = ik_ref[0, pl.ds(start, blk), :]
            acc = jnp.zeros((blk, blk), F32)
            for h in range(N_IDX_HEADS):
                acc = acc + jnp.maximum(_dot_nt(kk, q_heads[h]), 0.0) * (w_first if h == 0 else w_rows[h])
            causal = (c * blk + krow) <= qpos
            sc = jnp.where(causal, acc, -jnp.inf)
            score_ref[c] = sc
            mx = jnp.maximum(mx, jnp.max(sc, axis=0, keepdims=True))
            mn = jnp.minimum(mn, jnp.min(jnp.where(causal, acc, jnp.inf), axis=0, keepdims=True))
        return mn, mx

    def zero_counts(j):
        s = score_ref[pl.ds(2 * j, 2)]
        return pair_sum(s >= 0.0), pair_sum(s > 0.0)

    def score_body(j, carry):
        mn, mx, ge, gt = carry
        ge_p, gt_p = zero_counts(j - 1)
        after = jnp.sum(ge_p + gt_p, axis=0, keepdims=True) * 0.0
        mn, mx = score_pair(j, mn, mx, w_rows[0] + after)
        return mn, mx, ge + ge_p, gt + gt_p

    mn, mx = score_pair(0, jnp.full((1, blk), jnp.inf, F32), jnp.full((1, blk), -jnp.inf, F32), w_rows[0])
    mn, mx, ge8, gt8 = lax.fori_loop(1, n_pairs, score_body, (mn, mx, zero8, zero8))
    ge_p, gt_p = zero_counts(n_pairs - 1)
    ge0 = jnp.sum(ge8 + ge_p, axis=0, keepdims=True)
    gt0 = jnp.sum(gt8 + gt_p, axis=0, keepdims=True)


    def count_ge(cand):
        def body(j, acc):
            return acc + pair_sum(score_ref[pl.ds(2 * j, 2)] >= cand)
        return jnp.sum(lax.fori_loop(0, n_pairs, body, zero8), axis=0, keepdims=True)

    def count_ge_gt(cand):
        def body(j, carry):
            ge, gt = carry
            s = score_ref[pl.ds(2 * j, 2)]
            return ge + pair_sum(s >= cand), gt + pair_sum(s > cand)
        ge, gt = lax.fori_loop(0, n_pairs, body, (zero8, zero8))
        return jnp.sum(ge, axis=0, keepdims=True), jnp.sum(gt, axis=0, keepdims=True)

    n_valid = (qpos + 1).astype(F32)
    need = n_valid > kf
    kth_is_zero = jnp.logical_and(gt0 < kf, ge0 >= kf)
    nonneg = ge0 >= kf
    lo0 = jnp.where(nonneg, 0.0, mn)
    cnt0 = jnp.where(nonneg, ge0, n_valid)
    above_max = jnp.where(mx > 0.0, jnp.minimum(mx * 2.0, F32_MAX), 1.0)
    hi0 = jnp.where(nonneg, above_max, 0.0)
    done0 = jnp.logical_or(jnp.logical_not(need), jnp.logical_or(kth_is_zero, cnt0 == kf))
    active0 = jnp.where(done0, 0.0, 1.0)

    def midpoint(st):
        return 0.5 * st[0] + 0.5 * st[1]

    def halve(st, count):
        mid = midpoint(st)
        return narrow(st, mid, count(mid))

    def narrow(st, mid, cnt):
        lo, hi, cnt_lo, active = st
        conv = jnp.logical_or(mid <= lo, mid >= hi)
        upd = jnp.logical_and(active > 0.0, jnp.logical_not(conv))
        ge = cnt >= kf
        up_lo = jnp.logical_and(upd, ge)
        up_hi = jnp.logical_and(upd, jnp.logical_not(ge))
        lo = jnp.where(up_lo, mid, lo)
        cnt_lo = jnp.where(up_lo, cnt, cnt_lo)
        hi = jnp.where(up_hi, mid, hi)
        finished = jnp.logical_or(conv, jnp.logical_and(up_lo, cnt == kf))
        return lo, hi, cnt_lo, jnp.where(finished, 0.0, active)

    n_fused = jnp.minimum(i, BISECT_FUSED_STEPS)
    for k, v in enumerate((lo0, hi0, cnt0, active0)):
        bis_ref[k] = v

    for pairs in range(1, score_ref.shape[0] // 2 + 1):
        @pl.when(n_pairs == pairs)
        def _(pairs=pairs):
            def fused_step(c, st):
                mid = midpoint(st)
                n_first = min(FUSED_COUNTS_UNDER_QK, (pairs + 1) // 2)
                parts = []
                for js in (range(0, n_first), range(n_first, pairs)):
                    acc = zero8
                    for j in js:
                        acc = acc + pair_sum(score_ref[2 * j:2 * j + 2] >= mid)
                    parts.append(jnp.sum(acc, axis=0, keepdims=True))
                diff_chunk(c, False, True, after=parts[0] * 0.0,
                           before_last=parts[1] * 0.0 if pairs > n_first else None)
                return narrow(st, mid, parts[0] + parts[1])
            st = lax.fori_loop(0, n_fused, fused_step, tuple(bis_ref[k] for k in range(4)))
            for k in range(4):
                bis_ref[k] = st[k]

    st = tuple(bis_ref[k] for k in range(4))
    st = lax.fori_loop(0, jnp.maximum(BISECT_FIRST_STEPS - n_fused, 0),
                       lambda _, s: halve(s, count_ge), st)

    def bis_cond(st):
        return jnp.logical_and(st[4] > 0.0, st[6] < MAX_BISECT_STEPS)

    def bis_body(st):
        lo, hi, cnt_lo, active = lax.fori_loop(0, BISECT_STEPS_PER_TEST, lambda _, s: halve(s, count_ge), st[:4])
        tied = jnp.max(jnp.where(jnp.logical_and(need, cnt_lo > kf), 1.0, 0.0))
        return lo, hi, cnt_lo, active, jnp.max(active), tied, st[6] + BISECT_STEPS_PER_TEST

    lo, _, cnt_lo, _, _, tied, _ = lax.while_loop(
        bis_cond, bis_body, st + (jnp.float32(1.0), jnp.float32(0.0), jnp.int32(0)))
    thr = jnp.where(need, lo, -F32_MAX)

    excess = jnp.logical_and(need, cnt_lo > kf)

    @pl.when(tied > 0.0)
    def _():
        _, gt = count_ge_gt(thr)
        allow = kf - gt
        r_i = lax.broadcasted_iota(jnp.int32, (blk, blk), 0)
        c_i = lax.broadcasted_iota(jnp.int32, (blk, blk), 1)
        earlier = jnp.where(c_i < r_i, 1.0, 0.0).astype(BF16)

        def tie_body(c, seen):
            s = score_ref[c]
            eq = jnp.logical_and(s == thr, excess)
            eqf = jnp.where(eq, 1.0, 0.0)
            rank = seen + _dot(earlier, eqf.astype(BF16))
            drop = jnp.logical_and(eq, rank >= allow)
            score_ref[c] = jnp.where(drop, -jnp.inf, s)
            return seen + jnp.sum(eqf, axis=0, keepdims=True)

        lax.fori_loop(0, n_chunks, tie_body, jnp.zeros((1, blk), F32))

    def attend(bounded, shift):
        def att_body(c, carry):
            start = pl.multiple_of(c * blk, blk)
            bias = jnp.where(score_ref[c] >= thr, shift, MASK_VALUE)
            s_list, vt_list = [], []
            for h in range(N_DSA_HEADS):
                pair = slice((h // 2) * LANES, (h // 2 + 1) * LANES)
                kk = ak_ref[0, pl.ds(start, blk), pair]
                s_list.append(_dot_nt(kk, qa_heads[h]) + bias)
                vt_list.append(avt_ref[0, c, pair, :])
            if bounded:
                _bounded_update(s_list, vt_list, l_ref, acc_ref)
            else:
                _softmax_update(s_list, vt_list, m_ref, l_ref, acc_ref)
            return carry

        _init_softmax_state(m_ref, l_ref, acc_ref)
        lax.fori_loop(0, n_chunks, att_body, 0)

    if score_ref.shape[0] - 1 > BISECT_FUSED_STEPS:
        def diff_rest(c, carry):
            diff_chunk(c, False, True)
            return carry

        lax.fori_loop(n_fused, i, diff_rest, 0)

    attend(True, -jnp.max(a_bound, axis=0, keepdims=True))

    def write_outputs():
        for j in range(N_DSA_HEADS // 2):
            a = acc_ref[2 * j] * (1.0 / l_ref[2 * j])
            b = acc_ref[2 * j + 1] * (1.0 / l_ref[2 * j + 1])
            o_t = jnp.concatenate([a[:HEAD_DIM], b[HEAD_DIM:]], axis=0)
            oa_ref[0, j * LANES:(j + 1) * LANES, :] = o_t.astype(oa_ref.dtype)
        for h in range(N_DIFF_HEADS):
            out = (dacc_ref[2 * h] * (1.0 / dl_ref[2 * h])
                   - dacc_ref[2 * h + 1] * (lam / dl_ref[2 * h + 1]))
            out = out * lax.rsqrt(jnp.mean(out * out, axis=0, keepdims=True) + NORM_EPS)
            out = out * sw_ref[...] * (1.0 - lam_init)
            ob_ref[0, h * LANES:(h + 1) * LANES, :] = out.astype(ob_ref.dtype)

    diff_chunk(i, True, True)
    dsa_bad = _softmax_underflowed(l_ref)
    diff_bad = _softmax_underflowed(dl_ref)
    write_outputs()

    @pl.when(jnp.logical_or(dsa_bad, diff_bad))
    def _():
        @pl.when(dsa_bad)
        def _():
            attend(False, 0.0)

        @pl.when(diff_bad)
        def _():
            def body(c, carry):
                diff_chunk(c, False, False)
                return carry
            _init_softmax_state(dm_ref, dl_ref, dacc_ref)
            lax.fori_loop(0, i, body, 0)
            diff_chunk(i, True, False)

        write_outputs()


def _mixers(iq, ik2, iwt, aq, ak, avt, lam_vecs, dq, dk, dvt, subln_col, *, blk, lam_init):
    bsz, seq, _ = aq.shape
    top_k = min(TOPK_MAX, seq // 4)
    n_chunks = seq // blk
    assert n_chunks % 2 == 0, "key chunks are walked in pairs"
    qblk = lambda b, i: (b, i, 0)
    whole = lambda b, i: (b, 0, 0)
    whole_t = lambda b, i: (b, 0, 0, 0)
    n_maps = 2 * N_DIFF_HEADS
    q_spec = pl.BlockSpec((1, blk, 512), qblk)
    kv_spec = pl.BlockSpec((1, seq, 512), whole)
    vt_spec = pl.BlockSpec((1, n_chunks, 512, blk), whole_t)
    return pl.pallas_call(
        functools.partial(_mixer_kernel, blk=blk, top_k=top_k, lam_init=lam_init),
        grid=(bsz, n_chunks),
        in_specs=[q_spec,
                  pl.BlockSpec((1, seq, LANES), whole),
                  pl.BlockSpec((1, N_IDX_HEADS, blk), lambda b, i: (b, 0, i)),
                  q_spec, kv_spec, vt_spec,
                  _const_spec(lam_vecs.shape),
                  q_spec, kv_spec, vt_spec,
                  _const_spec((DIFF_V_DIM, 1))],
        out_specs=[pl.BlockSpec((1, 512, blk), lambda b, i: (b, 0, i))] * 2,
        out_shape=[jax.ShapeDtypeStruct((bsz, 512, seq), BF16)] * 2,
        scratch_shapes=[pltpu.VMEM((n_chunks, blk, blk), F32),
                        pltpu.VMEM((N_DSA_HEADS, 1, blk), F32),
                        pltpu.VMEM((N_DSA_HEADS, 1, blk), F32),
                        pltpu.VMEM((N_DSA_HEADS, LANES, blk), F32),
                        pltpu.VMEM((BF16_SUBLANES, LANES), F32),
                        pltpu.VMEM((n_maps, 1, blk), F32),
                        pltpu.VMEM((n_maps, 1, blk), F32),
                        pltpu.VMEM((n_maps, DIFF_V_DIM, blk), F32),
                        pltpu.VMEM((BF16_SUBLANES, LANES), F32),
                        pltpu.VMEM((4, 1, blk), F32)],
        compiler_params=pltpu.CompilerParams(
            dimension_semantics=("parallel", "arbitrary"), vmem_limit_bytes=VMEM_LIMIT_BYTES),
        name="mixers",
    )(iq, ik2, iwt, aq, ak, avt, lam_vecs, dq, dk, dvt, subln_col)


def _out_mlp_kernel(x_ref, ma_ref, mb_ref, g1_ref, sh_ref, sc_ref, g2_ref, n2_ref, nf_ref,
                    woa_ref, wob_ref, w1_ref, w2_ref, o_ref, *, ff_chunk):
    x = x_ref[0]
    o = _dot_tn(ma_ref[0], woa_ref[...]) + _dot_tn(mb_ref[0], wob_ref[...])
    x1 = x + g1_ref[0] * o
    ms = jnp.mean(x1 * x1, axis=-1, keepdims=True)
    h = x1 * lax.rsqrt(ms + NORM_EPS) * n2_ref[...]
    hb = (h * (1.0 + sc_ref[0]) + sh_ref[0]).astype(BF16)
    d_ff = w1_ref.shape[1]
    ff = jnp.zeros_like(x)
    for j in range(d_ff // ff_chunk):
        u = jnp.maximum(_dot(hb, w1_ref[:, j * ff_chunk:(j + 1) * ff_chunk]), 0.0)
        ff = ff + _dot((u * u).astype(BF16), w2_ref[j * ff_chunk:(j + 1) * ff_chunk, :])
    x2 = x1 + g2_ref[0] * ff
    ms2 = jnp.mean(x2 * x2, axis=-1, keepdims=True)
    o_ref[0] = x2 * lax.rsqrt(ms2 + NORM_EPS) * nf_ref[...]


def _out_mlp(x, mix_a, mix_b, g1, sh2, sc2, g2, n2, nf, wo_a, wo_b, w1, w2, *, tile):
    bsz, seq, d = x.shape
    tok = lambda b, i: (b, i, 0)
    per_b = lambda b, i: (b, 0, 0)
    mod_spec = pl.BlockSpec((1, 1, d), per_b)
    return pl.pallas_call(
        functools.partial(_out_mlp_kernel, ff_chunk=1024),
        grid=(bsz, seq // tile),
        in_specs=[pl.BlockSpec((1, tile, d), tok),
                  pl.BlockSpec((1, 512, tile), lambda b, i: (b, 0, i)),
                  pl.BlockSpec((1, 512, tile), lambda b, i: (b, 0, i)),
                  mod_spec, mod_spec, mod_spec, mod_spec,
                  _const_spec((1, d)), _const_spec((1, d)),
                  _const_spec(wo_a.shape), _const_spec(wo_b.shape),
                  _const_spec(w1.shape), _const_spec(w2.shape)],
        out_specs=pl.BlockSpec((1, tile, d), tok),
        out_shape=jax.ShapeDtypeStruct((bsz, seq, d), F32),
        compiler_params=pltpu.CompilerParams(
            dimension_semantics=("parallel", "parallel"), vmem_limit_bytes=VMEM_LIMIT_BYTES),
        name="out_mlp",
    )(x, mix_a, mix_b, g1, sh2, sc2, g2, n2, nf, wo_a, wo_b, w1, w2)


def _pad_cols(w, width):
    return jnp.pad(w, ((0, 0), (0, width - w.shape[1])))


def kernel(x, c, positions, w_ada, b_ada, norm1_w, w_in, idx_k_ln_w, idx_k_ln_b, lambda_q1, lambda_k1,
           lambda_q2, lambda_k2, subln_w, w_out, norm2_w, w_ff1, w_ff2, norm_f_w):
    bsz, seq, d = x.shape
    depth = w_ada.shape[0]
    assert depth == 1, "the fused final RMSNorm assumes a single layer"
    half = HEAD_DIM // 2
    inv_freq = ROPE_THETA ** (-jnp.arange(half, dtype=F32) / half)
    invf = jnp.tile(inv_freq, LANES // half).reshape(1, LANES)
    pos3 = positions.reshape(bsz, seq, 1)
    blk = min(ATT_BLOCK, seq)
    tile = min(512, seq)

    for l in range(depth):
        mod = _adaln(c, w_ada[l], b_ada[l])
        sh1, sc1, g1, sh2, sc2, g2 = [m.reshape(bsz, 1, d) for m in jnp.split(mod, 6, axis=-1)]

        w = w_in[l]
        wm = jnp.concatenate([w[:, :1024], w[:, 1536:2048], w[:, 2120:3144]], axis=1).astype(BF16)
        ws = _pad_cols(w[:, 2048:2112], LANES).astype(BF16)
        wwt = jnp.pad(w[:, 2112:2120].T, ((0, BF16_SUBLANES - N_IDX_HEADS), (0, 0))).astype(BF16)
        wavt = w[:, 1024:1536].T.astype(BF16)
        wdvt = w[:, 3144:3656].T.astype(BF16)
        lnw = _pad_cols(idx_k_ln_w[l].reshape(1, IDX_DIM), LANES)
        lnb = _pad_cols(idx_k_ln_b[l].reshape(1, IDX_DIM), LANES)

        aq, ak, avt, iq, ik2, iwt, dq, dk, dvt = _in_proj(
            x, pos3, sh1, sc1, norm1_w[l].reshape(1, d), invf, wm, ws, wwt, wavt, wdvt, lnw, lnb,
            tile=tile, chunk=blk)

        lam_vecs = jnp.stack([lambda_q1[l], lambda_k1[l], lambda_q2[l], lambda_k2[l]]).astype(F32)
        lam_init = 0.8 - 0.6 * math.exp(-0.3 * l)
        out_a, out_b = _mixers(iq, ik2, iwt, aq, ak, avt, lam_vecs, dq, dk, dvt,
                               subln_w[l].reshape(DIFF_V_DIM, 1), blk=blk, lam_init=lam_init)

        wo = w_out[l].astype(BF16)
        x = _out_mlp(x, out_a, out_b, g1, sh2, sc2, g2, norm2_w[l].reshape(1, d), norm_f_w.reshape(1, d),
                     wo[:512], wo[512:], w_ff1[l].astype(BF16), w_ff2[l].astype(BF16), tile=tile)
    return x
```

```python
import functools
import math

import jax
import jax.numpy as jnp
from jax import lax
from jax.experimental import pallas as pl
from jax.experimental.pallas import tpu as pltpu

F32 = jnp.float32
BF16 = jnp.bfloat16

HEAD_DIM = 64
N_DSA_HEADS = 8
N_IDX_HEADS = 8
IDX_DIM = 64
N_DIFF_HEADS = 4
DIFF_V_DIM = 128
TOPK_MAX = 256
ROPE_THETA = 10000.0
NORM_EPS = 1e-6
LN_EPS = 1e-5
LANES = 128
F32_SUBLANES = 8
BF16_SUBLANES = 16
ATT_BLOCK = 256
MASK_VALUE = -1e30
F32_MAX = float(jnp.finfo(jnp.float32).max)
MAX_BISECT_STEPS = 512
BISECT_FIRST_STEPS = 16
BISECT_FUSED_STEPS = 16
BISECT_STEPS_PER_TEST = 2
FUSED_COUNTS_UNDER_QK = 2
MIN_SAFE_NORMALISER = 2.0 ** -100
VMEM_LIMIT_BYTES = 56 * 1024 * 1024


def _dot(a, b):
    return jnp.dot(a, b, preferred_element_type=F32)


def _dot_nt(a, b):
    return lax.dot_general(a, b, (((1,), (1,)), ((), ())), preferred_element_type=F32)


def _dot_tn(a, b):
    return lax.dot_general(a, b, (((0,), (0,)), ((), ())), preferred_element_type=F32)


def _const_spec(shape):
    zeros = (0,) * len(shape)
    return pl.BlockSpec(shape, lambda *_: zeros, pipeline_mode=pl.Buffered(1))


def _adaln_kernel(c_ref, w_ref, b_ref, o_ref):
    c = c_ref[...]
    s = c / (1.0 + jnp.exp(-c))
    o_ref[...] = jnp.dot(s, w_ref[...], preferred_element_type=F32,
                         precision=lax.Precision.HIGHEST) + b_ref[...]


def _adaln(c, w, b):
    bsz, d = c.shape
    n = w.shape[1]
    tn = 1536
    return pl.pallas_call(
        _adaln_kernel,
        grid=(n // tn,),
        in_specs=[pl.BlockSpec((bsz, d), lambda j: (0, 0)),
                  pl.BlockSpec((d, tn), lambda j: (0, j)),
                  pl.BlockSpec((1, tn), lambda j: (0, j))],
        out_specs=pl.BlockSpec((bsz, tn), lambda j: (0, j)),
        out_shape=jax.ShapeDtypeStruct((bsz, n), F32),
        compiler_params=pltpu.CompilerParams(vmem_limit_bytes=VMEM_LIMIT_BYTES),
        name="adaln",
    )(c, w, b.reshape(1, n))


def _rope_group(x, cos, sin_signed, first_half):
    nxt = pltpu.roll(x, LANES - HEAD_DIM // 2, axis=1)
    prv = pltpu.roll(x, HEAD_DIM // 2, axis=1)
    return x * cos + jnp.where(first_half, nxt, prv) * sin_signed


def _in_proj_kernel(x_ref, pos_ref, sh_ref, sc_ref, nw_ref, invf_ref, *refs, consecutive):
    if consecutive:
        cstep_ref, sstep_ref, *refs = refs
    (wm_ref, ws_ref, wwt_ref, wavt_ref, wdvt_ref, lnw_ref, lnb_ref,
     aq_ref, ak_ref, avt_ref, iq_ref, ik_ref, iwt_ref, dq_ref, dk_ref, dvt_ref) = refs
    x = x_ref[0]
    ms = jnp.mean(x * x, axis=-1, keepdims=True)
    h = x * lax.rsqrt(ms + NORM_EPS) * nw_ref[...]
    h = h * (1.0 + sc_ref[0]) + sh_ref[0]
    hb = h.astype(BF16)

    lane = lax.broadcasted_iota(jnp.int32, (1, LANES), 1)
    first_half = (lane % HEAD_DIM) < (HEAD_DIM // 2)
    if consecutive:
        a0 = pos_ref[0, 0:1, :].astype(F32) * invf_ref[...]
        c0, s0 = jnp.cos(a0), jnp.sin(a0)
        cos = c0 * cstep_ref[...] - s0 * sstep_ref[...]
        sin = s0 * cstep_ref[...] + c0 * sstep_ref[...]
    else:
        ang = pos_ref[0].astype(F32) * invf_ref[...]
        cos = jnp.cos(ang)
        sin = jnp.sin(ang)
    sin_signed = jnp.where(first_half, -sin, sin)

    sm = _dot(hb, ws_ref[...])
    lo_half = lane < IDX_DIM
    mu = jnp.sum(sm, axis=-1, keepdims=True) * (1.0 / IDX_DIM)
    d = jnp.where(lo_half, sm - mu, 0.0)
    var = jnp.sum(d * d, axis=-1, keepdims=True) * (1.0 / IDX_DIM)
    y = d * lax.rsqrt(var + LN_EPS) * lnw_ref[...] + lnb_ref[...]
    y = y + pltpu.roll(y, IDX_DIM, axis=1)
    ik_ref[0] = _rope_group(y, cos, sin_signed, first_half).astype(ik_ref.dtype)

    wt = _dot_nt(wwt_ref[...], hb)
    iwt_ref[0] = wt[:N_IDX_HEADS] * ((N_IDX_HEADS ** -0.5) * (IDX_DIM ** -0.5))

    def roped(col0, out_ref, scale):
        wide = _dot(hb, wm_ref[:, col0:col0 + 4 * LANES])
        for g in range(4):
            y = _rope_group(wide[:, g * LANES:(g + 1) * LANES], cos, sin_signed, first_half)
            if scale != 1.0:
                y = y * scale
            out_ref[0, :, g * LANES:(g + 1) * LANES] = y.astype(out_ref.dtype)

    qscale = HEAD_DIM ** -0.5 * math.log2(math.e)
    roped(0, aq_ref, qscale)
    roped(512, ak_ref, 1.0)
    roped(1024, iq_ref, 1.0)
    roped(1536, dq_ref, qscale)
    roped(2048, dk_ref, 1.0)

    n_sub = avt_ref.shape[1]
    chunk = avt_ref.shape[3]
    for j in range(n_sub):
        hj = hb[j * chunk:(j + 1) * chunk]
        avt_ref[0, j] = _dot_nt(wavt_ref[...], hj).astype(avt_ref.dtype)
        dvt_ref[0, j] = _dot_nt(wdvt_ref[...], hj).astype(dvt_ref.dtype)


def _in_proj(x, pos3, sh1, sc1, nw, invf, wm, ws, wwt, wavt, wdvt, lnw, lnb, *, tile, chunk, consecutive):
    bsz, seq, d = x.shape
    tables, table_specs = [], []
    if consecutive:
        step_ang = jnp.arange(tile, dtype=F32)[:, None] * invf
        tables = [jnp.cos(step_ang), jnp.sin(step_ang)]
        table_specs = [_const_spec((tile, LANES))] * 2
    tok = lambda b, i: (b, i, 0)
    per_b = lambda b, i: (b, 0, 0)
    wide = jax.ShapeDtypeStruct((bsz, seq, 512), BF16)
    wide_t = jax.ShapeDtypeStruct((bsz, seq // chunk, 512, chunk), BF16)
    wide_spec = pl.BlockSpec((1, tile, 512), tok)
    wide_t_spec = pl.BlockSpec((1, tile // chunk, 512, chunk), lambda b, i: (b, i, 0, 0))
    out_shape = [wide, wide, wide_t, wide,
                 jax.ShapeDtypeStruct((bsz, seq, LANES), BF16),
                 jax.ShapeDtypeStruct((bsz, N_IDX_HEADS, seq), F32),
                 wide, wide, wide_t]
    out_specs = [wide_spec, wide_spec, wide_t_spec, wide_spec,
                 pl.BlockSpec((1, tile, LANES), tok),
                 pl.BlockSpec((1, N_IDX_HEADS, tile), lambda b, i: (b, 0, i)),
                 wide_spec, wide_spec, wide_t_spec]
    return pl.pallas_call(
        functools.partial(_in_proj_kernel, consecutive=consecutive),
        grid=(bsz, seq // tile),
        in_specs=[pl.BlockSpec((1, tile, d), tok),
                  pl.BlockSpec((1, tile, 1), tok),
                  pl.BlockSpec((1, 1, d), per_b),
                  pl.BlockSpec((1, 1, d), per_b),
                  _const_spec((1, d)),
                  _const_spec((1, LANES)),
                  *table_specs,
                  _const_spec(wm.shape),
                  _const_spec(ws.shape),
                  _const_spec(wwt.shape),
                  _const_spec(wavt.shape),
                  _const_spec(wdvt.shape),
                  _const_spec((1, LANES)),
                  _const_spec((1, LANES))],
        out_specs=out_specs,
        out_shape=out_shape,
        compiler_params=pltpu.CompilerParams(
            dimension_semantics=("parallel", "parallel"), vmem_limit_bytes=VMEM_LIMIT_BYTES),
        name="in_proj",
    )(x, pos3, sh1, sc1, nw, invf, *tables, wm, ws, wwt, wavt, wdvt, lnw, lnb)


def _head_q(q_all, h, lane_lo):
    pair = q_all[:, (h // 2) * LANES:(h // 2 + 1) * LANES]
    keep = lane_lo if h % 2 == 0 else jnp.logical_not(lane_lo)
    return jnp.where(keep, pair, jnp.zeros_like(pair))


def _softmax_update(s_list, vt_list, m_ref, l_ref, acc_ref):
    n = len(s_list)
    dv = vt_list[0].shape[0]
    ones = jnp.ones((BF16_SUBLANES, vt_list[0].shape[1]), BF16)
    m_old = [m_ref[g] for g in range(n)]
    m_new = [jnp.maximum(m_old[g], jnp.max(s_list[g], axis=0, keepdims=True)) for g in range(n)]
    p_list = [jnp.exp2(s_list[g] - m_new[g]).astype(BF16) for g in range(n)]
    alpha = [jnp.exp2(m_old[g] - m_new[g]) for g in range(n)]
    pv = [_dot(jnp.concatenate([vt_list[g], ones], axis=0), p_list[g]) for g in range(n)]
    for g in range(n):
        m_ref[g] = m_new[g]
        l_ref[g] = alpha[g] * l_ref[g] + pv[g][dv:dv + 1]
        acc_ref[g] = alpha[g] * acc_ref[g] + pv[g][:dv]


def _init_softmax_state(m_ref, l_ref, acc_ref):
    m_ref[...] = jnp.full(m_ref.shape, MASK_VALUE, F32)
    l_ref[...] = jnp.zeros(l_ref.shape, F32)
    acc_ref[...] = jnp.zeros(acc_ref.shape, F32)


def _bounded_update(s_list, vt_list, l_ref, acc_ref, before_last=None):
    n = len(s_list)
    dv = vt_list[0].shape[0]
    ones = jnp.ones((BF16_SUBLANES, vt_list[0].shape[1]), BF16)
    p_list = [jnp.exp2(s_list[g]).astype(BF16) for g in range(n)]
    pv = [_dot(jnp.concatenate([vt_list[g], ones], axis=0), p_list[g]) for g in range(n)]
    for g in range(n):
        col_sum = pv[g][dv:dv + 1]
        if before_last is not None and g == n - 1:
            col_sum = col_sum + before_last
        l_ref[g] = l_ref[g] + col_sum
        acc_ref[g] = acc_ref[g] + pv[g][:dv]


def _head_selector(n_lanes):
    r = lax.broadcasted_iota(jnp.int32, (BF16_SUBLANES, n_lanes), 0)
    l = lax.broadcasted_iota(jnp.int32, (BF16_SUBLANES, n_lanes), 1)
    return jnp.where(l // HEAD_DIM == r, 1.0, 0.0).astype(BF16)


def _head_sq_norms(x, head_sel):
    return _dot_nt(head_sel, x * x)


def _max_key_sq_norms(k_ref, kmax_ref, rows_per_step):
    n_lanes = k_ref.shape[2]
    l = lax.broadcasted_iota(jnp.int32, (n_lanes, LANES), 0)
    h = lax.broadcasted_iota(jnp.int32, (n_lanes, LANES), 1)
    sel_t = jnp.where(l // HEAD_DIM == h, 1.0, 0.0).astype(BF16)

    def body(c, mx):
        start = pl.multiple_of(c * rows_per_step, rows_per_step)
        kk = k_ref[0, pl.ds(start, rows_per_step), :]
        return jnp.maximum(mx, jnp.max(_dot(kk * kk, sel_t), axis=0, keepdims=True))
    row = lax.fori_loop(0, k_ref.shape[1] // rows_per_step, body, jnp.zeros((1, LANES), F32))
    r = lax.broadcasted_iota(jnp.int32, kmax_ref.shape, 0)
    c = lax.broadcasted_iota(jnp.int32, kmax_ref.shape, 1)
    col = jnp.sum(jnp.where(r == c, row, 0.0), axis=1, keepdims=True)
    kmax_ref[...] = jnp.broadcast_to(col, kmax_ref.shape)


def _softmax_underflowed(l_ref):
    return jnp.logical_not(jnp.min(l_ref[...]) >= MIN_SAFE_NORMALISER)


def _mixer_kernel(iq_ref, ik_ref, iwt_ref, aq_ref, ak_ref, avt_ref, lam_ref, dq_ref, dk_ref, dvt_ref, sw_ref,
                  oa_ref, ob_ref,
                  score_ref, m_ref, l_ref, acc_ref, kmax_ref, dm_ref, dl_ref, dacc_ref, dkmax_ref, bis_ref,
                  *, blk, top_k, lam_init):
    i = pl.program_id(1)
    n_chunks = i + 1
    n_maps = 2 * N_DIFF_HEADS
    lane = lax.broadcasted_iota(jnp.int32, (1, LANES), 1)
    lane_lo = lane < HEAD_DIM
    qcol = lax.broadcasted_iota(jnp.int32, (1, blk), 1)
    qpos = i * blk + qcol
    krow = lax.broadcasted_iota(jnp.int32, (blk, 1), 0)
    kf = float(top_k)

    aq_all = aq_ref[0]
    dq_all = dq_ref[0]
    head_sel = _head_selector(aq_all.shape[1])

    @pl.when(i == 0)
    def _():
        rows_per_step = math.gcd(4 * blk, ak_ref.shape[1])
        _max_key_sq_norms(ak_ref, kmax_ref, rows_per_step)
        _max_key_sq_norms(dk_ref, dkmax_ref, rows_per_step)

    d_maps = [_head_q(dq_all, g, lane_lo) for g in range(n_maps)]
    d_bound = jnp.sqrt(_head_sq_norms(dq_all, head_sel) * dkmax_ref[:, :1])
    d_shifts = [d_bound[g:g + 1] for g in range(n_maps)]
    a_bound = jnp.sqrt(_head_sq_norms(aq_all, head_sel) * kmax_ref[:, :1])
    qa_heads = [_head_q(aq_all, h, lane_lo) for h in range(N_DSA_HEADS)]
    lam_vecs = lam_ref[...]
    lam = (jnp.exp(jnp.sum(lam_vecs[0:1] * lam_vecs[1:2], axis=-1, keepdims=True))
           - jnp.exp(jnp.sum(lam_vecs[2:3] * lam_vecs[3:4], axis=-1, keepdims=True)) + lam_init)

    def diff_chunk(c, diagonal, bounded, after=None, before_last=None):
        start = pl.multiple_of(c * blk, blk)
        if diagonal:
            bias = jnp.where(krow <= qcol, 0.0, MASK_VALUE)
        s_list, vt_list = [], []
        for h in range(N_DIFF_HEADS):
            cols = slice(h * LANES, (h + 1) * LANES)
            kk = dk_ref[0, pl.ds(start, blk), cols]
            for comp in range(2):
                g = 2 * h + comp
                s = _dot_nt(kk, d_maps[g])
                if bounded:
                    s = s - (d_shifts[g] if after is None else d_shifts[g] + after)
                s_list.append(s + bias if diagonal else s)
                vt_list.append(dvt_ref[0, c, cols, :])
        if bounded:
            _bounded_update(s_list, vt_list, dl_ref, dacc_ref, before_last)
        else:
            _softmax_update(s_list, vt_list, dm_ref, dl_ref, dacc_ref)

    _init_softmax_state(dm_ref, dl_ref, dacc_ref)

    iq_all = iq_ref[0]
    wt = iwt_ref[0]
    q_heads = [_head_q(iq_all, h, lane_lo) for h in range(N_IDX_HEADS)]
    w_rows = [wt[h:h + 1, :] for h in range(N_IDX_HEADS)]

    n_pairs = (n_chunks + 1) // 2

    def pair_sum(hit):
        m = jnp.where(hit, 1.0, 0.0)
        return jnp.sum((m[0] + m[1]).reshape(blk // F32_SUBLANES, F32_SUBLANES, blk), axis=0)

    zero8 = jnp.zeros((F32_SUBLANES, blk), F32)

    def score_pair(j, mn, mx, w_first):
        for u in range(2):
            c = 2 * j + u
            start = pl.multiple_of(c * blk, blk)
            kk = ik_ref[0, pl.ds(start, blk), :]
            acc = jnp.zeros((blk, blk), F32)
            for h in range(N_IDX_HEADS):
                acc = acc + jnp.maximum(_dot_nt(kk, q_heads[h]), 0.0) * (w_first if h == 0 else w_rows[h])
            causal = (c * blk + krow) <= qpos
            sc = jnp.where(causal, acc, -jnp.inf)
            score_ref[c] = sc
            mx = jnp.maximum(mx, jnp.max(sc, axis=0, keepdims=True))
            mn = jnp.minimum(mn, jnp.min(jnp.where(causal, acc, jnp.inf), axis=0, keepdims=True))
        return mn, mx

    def zero_counts(j):
        s = score_ref[pl.ds(2 * j, 2)]
        return pair_sum(s >= 0.0), pair_sum(s > 0.0)

    def score_body(j, carry):
        mn, mx, ge, gt = carry
        ge_p, gt_p = zero_counts(j - 1)
        after = jnp.sum(ge_p + gt_p, axis=0, keepdims=True) * 0.0
        mn, mx = score_pair(j, mn, mx, w_rows[0] + after)
        return mn, mx, ge + ge_p, gt + gt_p

    mn, mx = score_pair(0, jnp.full((1, blk), jnp.inf, F32), jnp.full((1, blk), -jnp.inf, F32), w_rows[0])
    mn, mx, ge8, gt8 = lax.fori_loop(1, n_pairs, score_body, (mn, mx, zero8, zero8))
    ge_p, gt_p = zero_counts(n_pairs - 1)
    ge0 = jnp.sum(ge8 + ge_p, axis=0, keepdims=True)
    gt0 = jnp.sum(gt8 + gt_p, axis=0, keepdims=True)


    def count_ge(cand):
        def body(j, acc):
            return acc + pair_sum(score_ref[pl.ds(2 * j, 2)] >= cand)
        return jnp.sum(lax.fori_loop(0, n_pairs, body, zero8), axis=0, keepdims=True)

    def count_ge_gt(cand):
        def body(j, carry):
            ge, gt = carry
            s = score_ref[pl.ds(2 * j, 2)]
            return ge + pair_sum(s >= cand), gt + pair_sum(s > cand)
        ge, gt = lax.fori_loop(0, n_pairs, body, (zero8, zero8))
        return jnp.sum(ge, axis=0, keepdims=True), jnp.sum(gt, axis=0, keepdims=True)

    n_valid = (qpos + 1).astype(F32)
    need = n_valid > kf
    kth_is_zero = jnp.logical_and(gt0 < kf, ge0 >= kf)
    nonneg = ge0 >= kf
    lo0 = jnp.where(nonneg, 0.0, mn)
    cnt0 = jnp.where(nonneg, ge0, n_valid)
    above_max = jnp.where(mx > 0.0, jnp.minimum(mx * 2.0, F32_MAX), 1.0)
    hi0 = jnp.where(nonneg, above_max, 0.0)
    done0 = jnp.logical_or(jnp.logical_not(need), jnp.logical_or(kth_is_zero, cnt0 == kf))
    active0 = jnp.where(done0, 0.0, 1.0)

    def midpoint(st):
        return 0.5 * st[0] + 0.5 * st[1]

    def halve(st, count):
        mid = midpoint(st)
        return narrow(st, mid, count(mid))

    def narrow(st, mid, cnt):
        lo, hi, cnt_lo, active = st
        conv = jnp.logical_or(mid <= lo, mid >= hi)
        upd = jnp.logical_and(active > 0.0, jnp.logical_not(conv))
        ge = cnt >= kf
        up_lo = jnp.logical_and(upd, ge)
        up_hi = jnp.logical_and(upd, jnp.logical_not(ge))
        lo = jnp.where(up_lo, mid, lo)
        cnt_lo = jnp.where(up_lo, cnt, cnt_lo)
        hi = jnp.where(up_hi, mid, hi)
        finished = jnp.logical_or(conv, jnp.logical_and(up_lo, cnt == kf))
        return lo, hi, cnt_lo, jnp.where(finished, 0.0, active)

    n_fused = jnp.minimum(i, BISECT_FUSED_STEPS)
    for k, v in enumerate((lo0, hi0, cnt0, active0)):
        bis_ref[k] = v

    for pairs in range(1, score_ref.shape[0] // 2 + 1):
        @pl.when(n_pairs == pairs)
        def _(pairs=pairs):
            def fused_step(c, st):
                mid = midpoint(st)
                n_first = min(FUSED_COUNTS_UNDER_QK, (pairs + 1) // 2)
                parts = []
                for js in (range(0, n_first), range(n_first, pairs)):
                    acc = zero8
                    for j in js:
                        acc = acc + pair_sum(score_ref[2 * j:2 * j + 2] >= mid)
                    parts.append(jnp.sum(acc, axis=0, keepdims=True))
                diff_chunk(c, False, True, after=parts[0] * 0.0,
                           before_last=parts[1] * 0.0 if pairs > n_first else None)
                return narrow(st, mid, parts[0] + parts[1])
            st = lax.fori_loop(0, n_fused, fused_step, tuple(bis_ref[k] for k in range(4)))
            for k in range(4):
                bis_ref[k] = st[k]

    st = tuple(bis_ref[k] for k in range(4))
    st = lax.fori_loop(0, jnp.maximum(BISECT_FIRST_STEPS - n_fused, 0),
                       lambda _, s: halve(s, count_ge), st)

    def bis_cond(st):
        return jnp.logical_and(st[4] > 0.0, st[6] < MAX_BISECT_STEPS)

    def bis_body(st):
        lo, hi, cnt_lo, active = lax.fori_loop(0, BISECT_STEPS_PER_TEST, lambda _, s: halve(s, count_ge), st[:4])
        tied = jnp.max(jnp.where(jnp.logical_and(need, cnt_lo > kf), 1.0, 0.0))
        return lo, hi, cnt_lo, active, jnp.max(active), tied, st[6] + BISECT_STEPS_PER_TEST

    lo, _, cnt_lo, _, _, tied, _ = lax.while_loop(
        bis_cond, bis_body, st + (jnp.float32(1.0), jnp.float32(0.0), jnp.int32(0)))
    thr = jnp.where(need, lo, -F32_MAX)

    excess = jnp.logical_and(need, cnt_lo > kf)

    @pl.when(tied > 0.0)
    def _():
        _, gt = count_ge_gt(thr)
        allow = kf - gt
        r_i = lax.broadcasted_iota(jnp.int32, (blk, blk), 0)
        c_i = lax.broadcasted_iota(jnp.int32, (blk, blk), 1)
        earlier = jnp.where(c_i < r_i, 1.0, 0.0).astype(BF16)

        def tie_body(c, seen):
            s = score_ref[c]
            eq = jnp.logical_and(s == thr, excess)
            eqf = jnp.where(eq, 1.0, 0.0)
            rank = seen + _dot(earlier, eqf.astype(BF16))
            drop = jnp.logical_and(eq, rank >= allow)
            score_ref[c] = jnp.where(drop, -jnp.inf, s)
            return seen + jnp.sum(eqf, axis=0, keepdims=True)

        lax.fori_loop(0, n_chunks, tie_body, jnp.zeros((1, blk), F32))

    def attend(bounded, shift):
        def att_body(c, carry):
            start = pl.multiple_of(c * blk, blk)
            bias = jnp.where(score_ref[c] >= thr, shift, MASK_VALUE)
            s_list, vt_list = [], []
            for h in range(N_DSA_HEADS):
                pair = slice((h // 2) * LANES, (h // 2 + 1) * LANES)
                kk = ak_ref[0, pl.ds(start, blk), pair]
                s_list.append(_dot_nt(kk, qa_heads[h]) + bias)
                vt_list.append(avt_ref[0, c, pair, :])
            if bounded:
                _bounded_update(s_list, vt_list, l_ref, acc_ref)
            else:
                _softmax_update(s_list, vt_list, m_ref, l_ref, acc_ref)
            return carry

        _init_softmax_state(m_ref, l_ref, acc_ref)
        lax.fori_loop(0, n_chunks, att_body, 0)

    if score_ref.shape[0] - 1 > BISECT_FUSED_STEPS:
        def diff_rest(c, carry):
            diff_chunk(c, False, True)
            return carry

        lax.fori_loop(n_fused, i, diff_rest, 0)

    attend(True, -jnp.max(a_bound, axis=0, keepdims=True))

    def write_outputs():
        for j in range(N_DSA_HEADS // 2):
            a = acc_ref[2 * j] * (1.0 / l_ref[2 * j])
            b = acc_ref[2 * j + 1] * (1.0 / l_ref[2 * j + 1])
            o_t = jnp.concatenate([a[:HEAD_DIM], b[HEAD_DIM:]], axis=0)
            oa_ref[0, j * LANES:(j + 1) * LANES, :] = o_t.astype(oa_ref.dtype)
        for h in range(N_DIFF_HEADS):
            out = (dacc_ref[2 * h] * (1.0 / dl_ref[2 * h])
                   - dacc_ref[2 * h + 1] * (lam / dl_ref[2 * h + 1]))
            out = out * lax.rsqrt(jnp.mean(out * out, axis=0, keepdims=True) + NORM_EPS)
            out = out * sw_ref[...] * (1.0 - lam_init)
            ob_ref[0, h * LANES:(h + 1) * LANES, :] = out.astype(ob_ref.dtype)

    diff_chunk(i, True, True)
    dsa_bad = _softmax_underflowed(l_ref)
    diff_bad = _softmax_underflowed(dl_ref)
    write_outputs()

    @pl.when(jnp.logical_or(dsa_bad, diff_bad))
    def _():
        @pl.when(dsa_bad)
        def _():
            attend(False, 0.0)

        @pl.when(diff_bad)
        def _():
            def body(c, carry):
                diff_chunk(c, False, False)
                return carry
            _init_softmax_state(dm_ref, dl_ref, dacc_ref)
            lax.fori_loop(0, i, body, 0)
            diff_chunk(i, True, False)

        write_outputs()


def _mixers(iq, ik2, iwt, aq, ak, avt, lam_vecs, dq, dk, dvt, subln_col, *, blk, lam_init):
    bsz, seq, _ = aq.shape
    top_k = min(TOPK_MAX, seq // 4)
    n_chunks = seq // blk
    assert n_chunks % 2 == 0, "key chunks are walked in pairs"
    qblk = lambda b, i: (b, i, 0)
    whole = lambda b, i: (b, 0, 0)
    whole_t = lambda b, i: (b, 0, 0, 0)
    n_maps = 2 * N_DIFF_HEADS
    q_spec = pl.BlockSpec((1, blk, 512), qblk)
    kv_spec = pl.BlockSpec((1, seq, 512), whole)
    vt_spec = pl.BlockSpec((1, n_chunks, 512, blk), whole_t)
    return pl.pallas_call(
        functools.partial(_mixer_kernel, blk=blk, top_k=top_k, lam_init=lam_init),
        grid=(bsz, n_chunks),
        in_specs=[q_spec,
                  pl.BlockSpec((1, seq, LANES), whole),
                  pl.BlockSpec((1, N_IDX_HEADS, blk), lambda b, i: (b, 0, i)),
                  q_spec, kv_spec, vt_spec,
                  _const_spec(lam_vecs.shape),
                  q_spec, kv_spec, vt_spec,
                  _const_spec((DIFF_V_DIM, 1))],
        out_specs=[pl.BlockSpec((1, 512, blk), lambda b, i: (b, 0, i))] * 2,
        out_shape=[jax.ShapeDtypeStruct((bsz, 512, seq), BF16)] * 2,
        scratch_shapes=[pltpu.VMEM((n_chunks, blk, blk), F32),
                        pltpu.VMEM((N_DSA_HEADS, 1, blk), F32),
                        pltpu.VMEM((N_DSA_HEADS, 1, blk), F32),
                        pltpu.VMEM((N_DSA_HEADS, LANES, blk), F32),
                        pltpu.VMEM((BF16_SUBLANES, LANES), F32),
                        pltpu.VMEM((n_maps, 1, blk), F32),
                        pltpu.VMEM((n_maps, 1, blk), F32),
                        pltpu.VMEM((n_maps, DIFF_V_DIM, blk), F32),
                        pltpu.VMEM((BF16_SUBLANES, LANES), F32),
                        pltpu.VMEM((4, 1, blk), F32)],
        compiler_params=pltpu.CompilerParams(
            dimension_semantics=("parallel", "arbitrary"), vmem_limit_bytes=VMEM_LIMIT_BYTES),
        name="mixers",
    )(iq, ik2, iwt, aq, ak, avt, lam_vecs, dq, dk, dvt, subln_col)


def _out_mlp_kernel(x_ref, ma_ref, mb_ref, g1_ref, sh_ref, sc_ref, g2_ref, n2_ref, nf_ref,
                    woa_ref, wob_ref, w1_ref, w2_ref, o_ref, *, ff_chunk):
    x = x_ref[0]
    o = _dot_tn(ma_ref[0], woa_ref[...]) + _dot_tn(mb_ref[0], wob_ref[...])
    x1 = x + g1_ref[0] * o
    ms = jnp.mean(x1 * x1, axis=-1, keepdims=True)
    h = x1 * lax.rsqrt(ms + NORM_EPS) * n2_ref[...]
    hb = (h * (1.0 + sc_ref[0]) + sh_ref[0]).astype(BF16)
    d_ff = w1_ref.shape[1]
    ff = jnp.zeros_like(x)
    for j in range(d_ff // ff_chunk):
        u = jnp.maximum(_dot(hb, w1_ref[:, j * ff_chunk:(j + 1) * ff_chunk]), 0.0)
        ff = ff + _dot((u * u).astype(BF16), w2_ref[j * ff_chunk:(j + 1) * ff_chunk, :])
    x2 = x1 + g2_ref[0] * ff
    ms2 = jnp.mean(x2 * x2, axis=-1, keepdims=True)
    o_ref[0] = x2 * lax.rsqrt(ms2 + NORM_EPS) * nf_ref[...]


def _out_mlp(x, mix_a, mix_b, g1, sh2, sc2, g2, n2, nf, wo_a, wo_b, w1, w2, *, tile):
    bsz, seq, d = x.shape
    tok = lambda b, i: (b, i, 0)
    per_b = lambda b, i: (b, 0, 0)
    mod_spec = pl.BlockSpec((1, 1, d), per_b)
    return pl.pallas_call(
        functools.partial(_out_mlp_kernel, ff_chunk=1024),
        grid=(bsz, seq // tile),
        in_specs=[pl.BlockSpec((1, tile, d), tok),
                  pl.BlockSpec((1, 512, tile), lambda b, i: (b, 0, i)),
                  pl.BlockSpec((1, 512, tile), lambda b, i: (b, 0, i)),
                  mod_spec, mod_spec, mod_spec, mod_spec,
                  _const_spec((1, d)), _const_spec((1, d)),
                  _const_spec(wo_a.shape), _const_spec(wo_b.shape),
                  _const_spec(w1.shape), _const_spec(w2.shape)],
        out_specs=pl.BlockSpec((1, tile, d), tok),
        out_shape=jax.ShapeDtypeStruct((bsz, seq, d), F32),
        compiler_params=pltpu.CompilerParams(
            dimension_semantics=("parallel", "parallel"), vmem_limit_bytes=VMEM_LIMIT_BYTES),
        name="out_mlp",
    )(x, mix_a, mix_b, g1, sh2, sc2, g2, n2, nf, wo_a, wo_b, w1, w2)


def _pad_cols(w, width):
    return jnp.pad(w, ((0, 0), (0, width - w.shape[1])))


def kernel(x, c, positions, w_ada, b_ada, norm1_w, w_in, idx_k_ln_w, idx_k_ln_b, lambda_q1, lambda_k1,
           lambda_q2, lambda_k2, subln_w, w_out, norm2_w, w_ff1, w_ff2, norm_f_w):
    bsz, seq, d = x.shape
    depth = w_ada.shape[0]
    assert depth == 1, "the fused final RMSNorm assumes a single layer"
    half = HEAD_DIM // 2
    inv_freq = ROPE_THETA ** (-jnp.arange(half, dtype=F32) / half)
    invf = jnp.tile(inv_freq, LANES // half).reshape(1, LANES)
    pos3 = positions.reshape(bsz, seq, 1)
    blk = min(ATT_BLOCK, seq)
    tile = min(512, seq)

    for l in range(depth):
        mod = _adaln(c, w_ada[l], b_ada[l])
        sh1, sc1, g1, sh2, sc2, g2 = [m.reshape(bsz, 1, d) for m in jnp.split(mod, 6, axis=-1)]

        w = w_in[l]
        wm = jnp.concatenate([w[:, :1024], w[:, 1536:2048], w[:, 2120:3144]], axis=1).astype(BF16)
        ws = _pad_cols(w[:, 2048:2112], LANES).astype(BF16)
        wwt = jnp.pad(w[:, 2112:2120].T, ((0, BF16_SUBLANES - N_IDX_HEADS), (0, 0))).astype(BF16)
        wavt = w[:, 1024:1536].T.astype(BF16)
        wdvt = w[:, 3144:3656].T.astype(BF16)
        lnw = _pad_cols(idx_k_ln_w[l].reshape(1, IDX_DIM), LANES)
        lnb = _pad_cols(idx_k_ln_b[l].reshape(1, IDX_DIM), LANES)

        consecutive = jnp.all(positions[:, 1:] - positions[:, :-1] == 1)
        aq, ak, avt, iq, ik2, iwt, dq, dk, dvt = lax.cond(
            consecutive,
            functools.partial(_in_proj, tile=tile, chunk=blk, consecutive=True),
            functools.partial(_in_proj, tile=tile, chunk=blk, consecutive=False),
            x, pos3, sh1, sc1, norm1_w[l].reshape(1, d), invf, wm, ws, wwt, wavt, wdvt, lnw, lnb)

        lam_vecs = jnp.stack([lambda_q1[l], lambda_k1[l], lambda_q2[l], lambda_k2[l]]).astype(F32)
        lam_init = 0.8 - 0.6 * math.exp(-0.3 * l)
        out_a, out_b = _mixers(iq, ik2, iwt, aq, ak, avt, lam_vecs, dq, dk, dvt,
                               subln_w[l].reshape(DIFF_V_DIM, 1), blk=blk, lam_init=lam_init)

        wo = w_out[l].astype(BF16)
        x = _out_mlp(x, out_a, out_b, g1, sh2, sc2, g2, norm2_w[l].reshape(1, d), norm_f_w.reshape(1, d),
                     wo[:512], wo[512:], w_ff1[l].astype(BF16), w_ff2[l].astype(BF16), tile=tile)
    return x
```

```python
import functools
import math

import jax
import jax.numpy as jnp
from jax import lax
from jax.experimental import pallas as pl
from jax.experimental.pallas import tpu as pltpu

F32 = jnp.float32
BF16 = jnp.bfloat16

HEAD_DIM = 64
N_DSA_HEADS = 8
N_IDX_HEADS = 8
IDX_DIM = 64
N_DIFF_HEADS = 4
DIFF_V_DIM = 128
TOPK_MAX = 256
ROPE_THETA = 10000.0
NORM_EPS = 1e-6
LN_EPS = 1e-5
LANES = 128
F32_SUBLANES = 8
BF16_SUBLANES = 16
ATT_BLOCK = 256
MASK_VALUE = -1e30
F32_MAX = float(jnp.finfo(jnp.float32).max)
MAX_BISECT_STEPS = 512
BISECT_FIRST_STEPS = 16
BISECT_FUSED_STEPS = 16
BISECT_STEPS_PER_TEST = 2
FUSED_COUNTS_UNDER_QK = 2
MIN_SAFE_NORMALISER = 2.0 ** -100
VMEM_LIMIT_BYTES = 56 * 1024 * 1024


def _dot(a, b):
    return jnp.dot(a, b, preferred_element_type=F32)


def _dot_nt(a, b):
    return lax.dot_general(a, b, (((1,), (1,)), ((), ())), preferred_element_type=F32)


def _dot_tn(a, b):
    return lax.dot_general(a, b, (((0,), (0,)), ((), ())), preferred_element_type=F32)


def _const_spec(shape):
    zeros = (0,) * len(shape)
    return pl.BlockSpec(shape, lambda *_: zeros, pipeline_mode=pl.Buffered(1))


def _adaln_kernel(c_ref, w_ref, b_ref, o_ref):
    c = c_ref[...]
    s = c / (1.0 + jnp.exp(-c))
    o_ref[...] = jnp.dot(s, w_ref[...], preferred_element_type=F32,
                         precision=lax.Precision.HIGHEST) + b_ref[...]


def _adaln(c, w, b):
    bsz, d = c.shape
    n = w.shape[1]
    tn = 1536
    return pl.pallas_call(
        _adaln_kernel,
        grid=(n // tn,),
        in_specs=[pl.BlockSpec((bsz, d), lambda j: (0, 0)),
                  pl.BlockSpec((d, tn), lambda j: (0, j)),
                  pl.BlockSpec((1, tn), lambda j: (0, j))],
        out_specs=pl.BlockSpec((bsz, tn), lambda j: (0, j)),
        out_shape=jax.ShapeDtypeStruct((bsz, n), F32),
        compiler_params=pltpu.CompilerParams(vmem_limit_bytes=VMEM_LIMIT_BYTES),
        name="adaln",
    )(c, w, b.reshape(1, n))


def _rope_group(x, cos, sin_signed, first_half):
    nxt = pltpu.roll(x, LANES - HEAD_DIM // 2, axis=1)
    prv = pltpu.roll(x, HEAD_DIM // 2, axis=1)
    return x * cos + jnp.where(first_half, nxt, prv) * sin_signed


def _in_proj_kernel(x_ref, pos_ref, sh_ref, sc_ref, nw_ref, invf_ref, wm_ref, ws_ref, wwt_ref,
                    wavt_ref, wdvt_ref, lnw_ref, lnb_ref,
                    aq_ref, ak_ref, avt_ref, iq_ref, ik_ref, iwt_ref, dq_ref, dk_ref, dvt_ref):
    x = x_ref[0]
    ms = jnp.mean(x * x, axis=-1, keepdims=True)
    h = x * lax.rsqrt(ms + NORM_EPS) * nw_ref[...]
    h = h * (1.0 + sc_ref[0]) + sh_ref[0]
    hb = h.astype(BF16)

    lane = lax.broadcasted_iota(jnp.int32, (1, LANES), 1)
    first_half = (lane % HEAD_DIM) < (HEAD_DIM // 2)
    ang = pos_ref[0].astype(F32) * invf_ref[...]
    cos = jnp.cos(ang)
    sin = jnp.sin(ang)
    sin_signed = jnp.where(first_half, -sin, sin)

    sm = _dot(hb, ws_ref[...])
    lo_half = lane < IDX_DIM
    mu = jnp.sum(sm, axis=-1, keepdims=True) * (1.0 / IDX_DIM)
    d = jnp.where(lo_half, sm - mu, 0.0)
    var = jnp.sum(d * d, axis=-1, keepdims=True) * (1.0 / IDX_DIM)
    y = d * lax.rsqrt(var + LN_EPS) * lnw_ref[...] + lnb_ref[...]
    y = y + pltpu.roll(y, IDX_DIM, axis=1)
    ik_ref[0] = _rope_group(y, cos, sin_signed, first_half).astype(ik_ref.dtype)

    wt = _dot_nt(wwt_ref[...], hb)
    iwt_ref[0] = wt[:N_IDX_HEADS] * ((N_IDX_HEADS ** -0.5) * (IDX_DIM ** -0.5))

    def roped(col0, out_ref, scale):
        wide = _dot(hb, wm_ref[:, col0:col0 + 4 * LANES])
        for g in range(4):
            y = _rope_group(wide[:, g * LANES:(g + 1) * LANES], cos, sin_signed, first_half)
            if scale != 1.0:
                y = y * scale
            out_ref[0, :, g * LANES:(g + 1) * LANES] = y.astype(out_ref.dtype)

    qscale = HEAD_DIM ** -0.5 * math.log2(math.e)
    roped(0, aq_ref, qscale)
    roped(512, ak_ref, 1.0)
    roped(1024, iq_ref, 1.0)
    roped(1536, dq_ref, qscale)
    roped(2048, dk_ref, 1.0)

    n_sub = avt_ref.shape[1]
    chunk = avt_ref.shape[3]
    for j in range(n_sub):
        hj = hb[j * chunk:(j + 1) * chunk]
        avt_ref[0, j] = _dot_nt(wavt_ref[...], hj).astype(avt_ref.dtype)
        dvt_ref[0, j] = _dot_nt(wdvt_ref[...], hj).astype(dvt_ref.dtype)


def _in_proj(x, pos3, sh1, sc1, nw, invf, wm, ws, wwt, wavt, wdvt, lnw, lnb, *, tile, chunk):
    bsz, seq, d = x.shape
    tok = lambda b, i: (b, i, 0)
    per_b = lambda b, i: (b, 0, 0)
    wide = jax.ShapeDtypeStruct((bsz, seq, 512), BF16)
    wide_t = jax.ShapeDtypeStruct((bsz, seq // chunk, 512, chunk), BF16)
    wide_spec = pl.BlockSpec((1, tile, 512), tok)
    wide_t_spec = pl.BlockSpec((1, tile // chunk, 512, chunk), lambda b, i: (b, i, 0, 0))
    out_shape = [wide, wide, wide_t, wide,
                 jax.ShapeDtypeStruct((bsz, seq, LANES), BF16),
                 jax.ShapeDtypeStruct((bsz, N_IDX_HEADS, seq), F32),
                 wide, wide, wide_t]
    out_specs = [wide_spec, wide_spec, wide_t_spec, wide_spec,
                 pl.BlockSpec((1, tile, LANES), tok),
                 pl.BlockSpec((1, N_IDX_HEADS, tile), lambda b, i: (b, 0, i)),
                 wide_spec, wide_spec, wide_t_spec]
    return pl.pallas_call(
        _in_proj_kernel,
        grid=(bsz, seq // tile),
        in_specs=[pl.BlockSpec((1, tile, d), tok),
                  pl.BlockSpec((1, tile, 1), tok),
                  pl.BlockSpec((1, 1, d), per_b),
                  pl.BlockSpec((1, 1, d), per_b),
                  _const_spec((1, d)),
                  _const_spec((1, LANES)),
                  _const_spec(wm.shape),
                  _const_spec(ws.shape),
                  _const_spec(wwt.shape),
                  _const_spec(wavt.shape),
                  _const_spec(wdvt.shape),
                  _const_spec((1, LANES)),
                  _const_spec((1, LANES))],
        out_specs=out_specs,
        out_shape=out_shape,
        compiler_params=pltpu.CompilerParams(
            dimension_semantics=("parallel", "parallel"), vmem_limit_bytes=VMEM_LIMIT_BYTES),
        name="in_proj",
    )(x, pos3, sh1, sc1, nw, invf, wm, ws, wwt, wavt, wdvt, lnw, lnb)


def _head_q(q_all, h, lane_lo):
    pair = q_all[:, (h // 2) * LANES:(h // 2 + 1) * LANES]
    keep = lane_lo if h % 2 == 0 else jnp.logical_not(lane_lo)
    return jnp.where(keep, pair, jnp.zeros_like(pair))


def _softmax_update(s_list, vt_list, m_ref, l_ref, acc_ref):
    n = len(s_list)
    dv = vt_list[0].shape[0]
    ones = jnp.ones((BF16_SUBLANES, vt_list[0].shape[1]), BF16)
    m_old = [m_ref[g] for g in range(n)]
    m_new = [jnp.maximum(m_old[g], jnp.max(s_list[g], axis=0, keepdims=True)) for g in range(n)]
    p_list = [jnp.exp2(s_list[g] - m_new[g]).astype(BF16) for g in range(n)]
    alpha = [jnp.exp2(m_old[g] - m_new[g]) for g in range(n)]
    pv = [_dot(jnp.concatenate([vt_list[g], ones], axis=0), p_list[g]) for g in range(n)]
    for g in range(n):
        m_ref[g] = m_new[g]
        l_ref[g] = alpha[g] * l_ref[g] + pv[g][dv:dv + 1]
        acc_ref[g] = alpha[g] * acc_ref[g] + pv[g][:dv]


def _init_softmax_state(m_ref, l_ref, acc_ref):
    m_ref[...] = jnp.full(m_ref.shape, MASK_VALUE, F32)
    l_ref[...] = jnp.zeros(l_ref.shape, F32)
    acc_ref[...] = jnp.zeros(acc_ref.shape, F32)


def _bounded_update(s_list, vt_list, l_ref, acc_ref, before_sums=None):
    n = len(s_list)
    dv = vt_list[0].shape[0]
    ones = jnp.ones((BF16_SUBLANES, vt_list[0].shape[1]), BF16)
    p_list = [jnp.exp2(s_list[g]).astype(BF16) for g in range(n)]
    pv = [_dot(jnp.concatenate([vt_list[g], ones], axis=0), p_list[g]) for g in range(n)]
    for g in range(n):
        col_sum = pv[g][dv:dv + 1]
        if before_sums is not None and g == 1:
            col_sum = col_sum + before_sums
        l_ref[g] = l_ref[g] + col_sum
        acc_ref[g] = acc_ref[g] + pv[g][:dv]


def _head_selector(n_lanes):
    r = lax.broadcasted_iota(jnp.int32, (BF16_SUBLANES, n_lanes), 0)
    l = lax.broadcasted_iota(jnp.int32, (BF16_SUBLANES, n_lanes), 1)
    return jnp.where(l // HEAD_DIM == r, 1.0, 0.0).astype(BF16)


def _head_sq_norms(x, head_sel):
    return _dot_nt(head_sel, x * x)


def _max_key_sq_norms(k_ref, kmax_ref, rows_per_step):
    n_lanes = k_ref.shape[2]
    l = lax.broadcasted_iota(jnp.int32, (n_lanes, LANES), 0)
    h = lax.broadcasted_iota(jnp.int32, (n_lanes, LANES), 1)
    sel_t = jnp.where(l // HEAD_DIM == h, 1.0, 0.0).astype(BF16)

    def body(c, mx):
        start = pl.multiple_of(c * rows_per_step, rows_per_step)
        kk = k_ref[0, pl.ds(start, rows_per_step), :]
        return jnp.maximum(mx, jnp.max(_dot(kk * kk, sel_t), axis=0, keepdims=True))
    row = lax.fori_loop(0, k_ref.shape[1] // rows_per_step, body, jnp.zeros((1, LANES), F32))
    r = lax.broadcasted_iota(jnp.int32, kmax_ref.shape, 0)
    c = lax.broadcasted_iota(jnp.int32, kmax_ref.shape, 1)
    col = jnp.sum(jnp.where(r == c, row, 0.0), axis=1, keepdims=True)
    kmax_ref[...] = jnp.broadcast_to(col, kmax_ref.shape)


def _softmax_underflowed(l_ref):
    return jnp.logical_not(jnp.min(l_ref[...]) >= MIN_SAFE_NORMALISER)


def _mixer_kernel(iq_ref, ik_ref, iwt_ref, aq_ref, ak_ref, avt_ref, lam_ref, dq_ref, dk_ref, dvt_ref, sw_ref,
                  oa_ref, ob_ref,
                  score_ref, m_ref, l_ref, acc_ref, kmax_ref, dm_ref, dl_ref, dacc_ref, dkmax_ref, bis_ref,
                  *, blk, top_k, lam_init):
    i = pl.program_id(1)
    n_chunks = i + 1
    n_maps = 2 * N_DIFF_HEADS
    lane = lax.broadcasted_iota(jnp.int32, (1, LANES), 1)
    lane_lo = lane < HEAD_DIM
    qcol = lax.broadcasted_iota(jnp.int32, (1, blk), 1)
    qpos = i * blk + qcol
    krow = lax.broadcasted_iota(jnp.int32, (blk, 1), 0)
    kf = float(top_k)

    aq_all = aq_ref[0]
    dq_all = dq_ref[0]
    head_sel = _head_selector(aq_all.shape[1])

    @pl.when(i == 0)
    def _():
        rows_per_step = math.gcd(4 * blk, ak_ref.shape[1])
        _max_key_sq_norms(ak_ref, kmax_ref, rows_per_step)
        _max_key_sq_norms(dk_ref, dkmax_ref, rows_per_step)

    d_maps = [_head_q(dq_all, g, lane_lo) for g in range(n_maps)]
    d_bound = jnp.sqrt(_head_sq_norms(dq_all, head_sel) * dkmax_ref[:, :1])
    d_shifts = [d_bound[g:g + 1] for g in range(n_maps)]
    a_bound = jnp.sqrt(_head_sq_norms(aq_all, head_sel) * kmax_ref[:, :1])
    qa_heads = [_head_q(aq_all, h, lane_lo) for h in range(N_DSA_HEADS)]
    lam_vecs = lam_ref[...]
    lam = (jnp.exp(jnp.sum(lam_vecs[0:1] * lam_vecs[1:2], axis=-1, keepdims=True))
           - jnp.exp(jnp.sum(lam_vecs[2:3] * lam_vecs[3:4], axis=-1, keepdims=True)) + lam_init)

    def diff_chunk(c, diagonal, bounded, after=None, before_sums=None):
        start = pl.multiple_of(c * blk, blk)
        if diagonal:
            bias = jnp.where(krow <= qcol, 0.0, MASK_VALUE)
        s_list, vt_list = [], []
        for h in range(N_DIFF_HEADS):
            cols = slice(h * LANES, (h + 1) * LANES)
            kk = dk_ref[0, pl.ds(start, blk), cols]
            for comp in range(2):
                g = 2 * h + comp
                s = _dot_nt(kk, d_maps[g])
                if bounded:
                    s = s - (d_shifts[g] if after is None else d_shifts[g] + after)
                s_list.append(s + bias if diagonal else s)
                vt_list.append(dvt_ref[0, c, cols, :])
        if bounded:
            _bounded_update(s_list, vt_list, dl_ref, dacc_ref, before_sums)
        else:
            _softmax_update(s_list, vt_list, dm_ref, dl_ref, dacc_ref)

    _init_softmax_state(dm_ref, dl_ref, dacc_ref)

    iq_all = iq_ref[0]
    wt = iwt_ref[0]
    q_heads = [_head_q(iq_all, h, lane_lo) for h in range(N_IDX_HEADS)]
    w_rows = [wt[h:h + 1, :] for h in range(N_IDX_HEADS)]

    n_pairs = (n_chunks + 1) // 2

    def pair_sum(hit):
        m = jnp.where(hit, 1.0, 0.0)
        return jnp.sum((m[0] + m[1]).reshape(blk // F32_SUBLANES, F32_SUBLANES, blk), axis=0)

    zero8 = jnp.zeros((F32_SUBLANES, blk), F32)

    def score_pair(j, mn, mx, w_first):
        for u in range(2):
            c = 2 * j + u
            start = pl.multiple_of(c * blk, blk)
            kk = ik_ref[0, pl.ds(start, blk), :]
            acc = jnp.zeros((blk, blk), F32)
            for h in range(N_IDX_HEADS):
                acc = acc + jnp.maximum(_dot_nt(kk, q_heads[h]), 0.0) * (w_first if h == 0 else w_rows[h])
            causal = (c * blk + krow) <= qpos
            sc = jnp.where(causal, acc, -jnp.inf)
            score_ref[c] = sc
            mx = jnp.maximum(mx, jnp.max(sc, axis=0, keepdims=True))
            mn = jnp.minimum(mn, jnp.min(jnp.where(causal, acc, jnp.inf), axis=0, keepdims=True))
        return mn, mx

    def zero_counts(j):
        s = score_ref[pl.ds(2 * j, 2)]
        return pair_sum(s >= 0.0), pair_sum(s > 0.0)

    def score_body(j, carry):
        mn, mx, ge, gt = carry
        ge_p, gt_p = zero_counts(j - 1)
        after = jnp.sum(ge_p + gt_p, axis=0, keepdims=True) * 0.0
        mn, mx = score_pair(j, mn, mx, w_rows[0] + after)
        return mn, mx, ge + ge_p, gt + gt_p

    mn, mx = score_pair(0, jnp.full((1, blk), jnp.inf, F32), jnp.full((1, blk), -jnp.inf, F32), w_rows[0])
    mn, mx, ge8, gt8 = lax.fori_loop(1, n_pairs, score_body, (mn, mx, zero8, zero8))
    ge_p, gt_p = zero_counts(n_pairs - 1)
    ge0 = jnp.sum(ge8 + ge_p, axis=0, keepdims=True)
    gt0 = jnp.sum(gt8 + gt_p, axis=0, keepdims=True)


    def count_ge(cand):
        def body(j, acc):
            return acc + pair_sum(score_ref[pl.ds(2 * j, 2)] >= cand)
        return jnp.sum(lax.fori_loop(0, n_pairs, body, zero8), axis=0, keepdims=True)

    def count_ge_gt(cand):
        def body(j, carry):
            ge, gt = carry
            s = score_ref[pl.ds(2 * j, 2)]
            return ge + pair_sum(s >= cand), gt + pair_sum(s > cand)
        ge, gt = lax.fori_loop(0, n_pairs, body, (zero8, zero8))
        return jnp.sum(ge, axis=0, keepdims=True), jnp.sum(gt, axis=0, keepdims=True)

    n_valid = (qpos + 1).astype(F32)
    need = n_valid > kf
    kth_is_zero = jnp.logical_and(gt0 < kf, ge0 >= kf)
    nonneg = ge0 >= kf
    lo0 = jnp.where(nonneg, 0.0, mn)
    cnt0 = jnp.where(nonneg, ge0, n_valid)
    above_max = jnp.where(mx > 0.0, jnp.minimum(mx * 2.0, F32_MAX), 1.0)
    hi0 = jnp.where(nonneg, above_max, 0.0)
    done0 = jnp.logical_or(jnp.logical_not(need), jnp.logical_or(kth_is_zero, cnt0 == kf))
    active0 = jnp.where(done0, 0.0, 1.0)

    def midpoint(st):
        return 0.5 * st[0] + 0.5 * st[1]

    def halve(st, count):
        mid = midpoint(st)
        return narrow(st, mid, count(mid))

    def narrow(st, mid, cnt):
        lo, hi, cnt_lo, active = st
        conv = jnp.logical_or(mid <= lo, mid >= hi)
        upd = jnp.logical_and(active > 0.0, jnp.logical_not(conv))
        ge = cnt >= kf
        up_lo = jnp.logical_and(upd, ge)
        up_hi = jnp.logical_and(upd, jnp.logical_not(ge))
        lo = jnp.where(up_lo, mid, lo)
        cnt_lo = jnp.where(up_lo, cnt, cnt_lo)
        hi = jnp.where(up_hi, mid, hi)
        finished = jnp.logical_or(conv, jnp.logical_and(up_lo, cnt == kf))
        return lo, hi, cnt_lo, jnp.where(finished, 0.0, active)

    n_fused = jnp.minimum(i, BISECT_FUSED_STEPS)
    for k, v in enumerate((lo0, hi0, cnt0, active0)):
        bis_ref[k] = v

    for pairs in range(1, score_ref.shape[0] // 2 + 1):
        @pl.when(n_pairs == pairs)
        def _(pairs=pairs):
            def fused_step(c, st):
                mid = midpoint(st)
                n_first = min(FUSED_COUNTS_UNDER_QK, (pairs + 1) // 2)
                parts = []
                for js in (range(0, n_first), range(n_first, pairs)):
                    acc = zero8
                    for j in js:
                        acc = acc + pair_sum(score_ref[2 * j:2 * j + 2] >= mid)
                    parts.append(jnp.sum(acc, axis=0, keepdims=True))
                diff_chunk(c, False, True, after=parts[0] * 0.0,
                           before_sums=parts[1] * 0.0 if pairs > n_first else None)
                return narrow(st, mid, parts[0] + parts[1])
            st = lax.fori_loop(0, n_fused, fused_step, tuple(bis_ref[k] for k in range(4)))
            for k in range(4):
                bis_ref[k] = st[k]

    st = tuple(bis_ref[k] for k in range(4))
    st = lax.fori_loop(0, jnp.maximum(BISECT_FIRST_STEPS - n_fused, 0),
                       lambda _, s: halve(s, count_ge), st)

    def bis_cond(st):
        return jnp.logical_and(st[4] > 0.0, st[6] < MAX_BISECT_STEPS)

    def bis_body(st):
        lo, hi, cnt_lo, active = lax.fori_loop(0, BISECT_STEPS_PER_TEST, lambda _, s: halve(s, count_ge), st[:4])
        tied = jnp.max(jnp.where(jnp.logical_and(need, cnt_lo > kf), 1.0, 0.0))
        return lo, hi, cnt_lo, active, jnp.max(active), tied, st[6] + BISECT_STEPS_PER_TEST

    lo, _, cnt_lo, _, _, tied, _ = lax.while_loop(
        bis_cond, bis_body, st + (jnp.float32(1.0), jnp.float32(0.0), jnp.int32(0)))
    thr = jnp.where(need, lo, -F32_MAX)

    excess = jnp.logical_and(need, cnt_lo > kf)

    @pl.when(tied > 0.0)
    def _():
        _, gt = count_ge_gt(thr)
        allow = kf - gt
        r_i = lax.broadcasted_iota(jnp.int32, (blk, blk), 0)
        c_i = lax.broadcasted_iota(jnp.int32, (blk, blk), 1)
        earlier = jnp.where(c_i < r_i, 1.0, 0.0).astype(BF16)

        def tie_body(c, seen):
            s = score_ref[c]
            eq = jnp.logical_and(s == thr, excess)
            eqf = jnp.where(eq, 1.0, 0.0)
            rank = seen + _dot(earlier, eqf.astype(BF16))
            drop = jnp.logical_and(eq, rank >= allow)
            score_ref[c] = jnp.where(drop, -jnp.inf, s)
            return seen + jnp.sum(eqf, axis=0, keepdims=True)

        lax.fori_loop(0, n_chunks, tie_body, jnp.zeros((1, blk), F32))

    def attend(bounded, shift):
        def att_body(c, carry):
            start = pl.multiple_of(c * blk, blk)
            bias = jnp.where(score_ref[c] >= thr, shift, MASK_VALUE)
            s_list, vt_list = [], []
            for h in range(N_DSA_HEADS):
                pair = slice((h // 2) * LANES, (h // 2 + 1) * LANES)
                kk = ak_ref[0, pl.ds(start, blk), pair]
                s_list.append(_dot_nt(kk, qa_heads[h]) + bias)
                vt_list.append(avt_ref[0, c, pair, :])
            if bounded:
                _bounded_update(s_list, vt_list, l_ref, acc_ref)
            else:
                _softmax_update(s_list, vt_list, m_ref, l_ref, acc_ref)
            return carry

        _init_softmax_state(m_ref, l_ref, acc_ref)
        lax.fori_loop(0, n_chunks, att_body, 0)

    if score_ref.shape[0] - 1 > BISECT_FUSED_STEPS:
        def diff_rest(c, carry):
            diff_chunk(c, False, True)
            return carry

        lax.fori_loop(n_fused, i, diff_rest, 0)

    attend(True, -jnp.max(a_bound, axis=0, keepdims=True))

    def write_outputs():
        for j in range(N_DSA_HEADS // 2):
            a = acc_ref[2 * j] * (1.0 / l_ref[2 * j])
            b = acc_ref[2 * j + 1] * (1.0 / l_ref[2 * j + 1])
            o_t = jnp.concatenate([a[:HEAD_DIM], b[HEAD_DIM:]], axis=0)
            oa_ref[0, j * LANES:(j + 1) * LANES, :] = o_t.astype(oa_ref.dtype)
        for h in range(N_DIFF_HEADS):
            out = (dacc_ref[2 * h] * (1.0 / dl_ref[2 * h])
                   - dacc_ref[2 * h + 1] * (lam / dl_ref[2 * h + 1]))
            out = out * lax.rsqrt(jnp.mean(out * out, axis=0, keepdims=True) + NORM_EPS)
            out = out * sw_ref[...] * (1.0 - lam_init)
            ob_ref[0, h * LANES:(h + 1) * LANES, :] = out.astype(ob_ref.dtype)

    diff_chunk(i, True, True)
    dsa_bad = _softmax_underflowed(l_ref)
    diff_bad = _softmax_underflowed(dl_ref)
    write_outputs()

    @pl.when(jnp.logical_or(dsa_bad, diff_bad))
    def _():
        @pl.when(dsa_bad)
        def _():
            attend(False, 0.0)

        @pl.when(diff_bad)
        def _():
            def body(c, carry):
                diff_chunk(c, False, False)
                return carry
            _init_softmax_state(dm_ref, dl_ref, dacc_ref)
            lax.fori_loop(0, i, body, 0)
            diff_chunk(i, True, False)

        write_outputs()


def _mixers(iq, ik2, iwt, aq, ak, avt, lam_vecs, dq, dk, dvt, subln_col, *, blk, lam_init):
    bsz, seq, _ = aq.shape
    top_k = min(TOPK_MAX, seq // 4)
    n_chunks = seq // blk
    assert n_chunks % 2 == 0, "key chunks are walked in pairs"
    qblk = lambda b, i: (b, i, 0)
    whole = lambda b, i: (b, 0, 0)
    whole_t = lambda b, i: (b, 0, 0, 0)
    n_maps = 2 * N_DIFF_HEADS
    q_spec = pl.BlockSpec((1, blk, 512), qblk)
    kv_spec = pl.BlockSpec((1, seq, 512), whole)
    vt_spec = pl.BlockSpec((1, n_chunks, 512, blk), whole_t)
    return pl.pallas_call(
        functools.partial(_mixer_kernel, blk=blk, top_k=top_k, lam_init=lam_init),
        grid=(bsz, n_chunks),
        in_specs=[q_spec,
                  pl.BlockSpec((1, seq, LANES), whole),
                  pl.BlockSpec((1, N_IDX_HEADS, blk), lambda b, i: (b, 0, i)),
                  q_spec, kv_spec, vt_spec,
                  _const_spec(lam_vecs.shape),
                  q_spec, kv_spec, vt_spec,
                  _const_spec((DIFF_V_DIM, 1))],
        out_specs=[pl.BlockSpec((1, 512, blk), lambda b, i: (b, 0, i))] * 2,
        out_shape=[jax.ShapeDtypeStruct((bsz, 512, seq), BF16)] * 2,
        scratch_shapes=[pltpu.VMEM((n_chunks, blk, blk), F32),
                        pltpu.VMEM((N_DSA_HEADS, 1, blk), F32),
                        pltpu.VMEM((N_DSA_HEADS, 1, blk), F32),
                        pltpu.VMEM((N_DSA_HEADS, LANES, blk), F32),
                        pltpu.VMEM((BF16_SUBLANES, LANES), F32),
                        pltpu.VMEM((n_maps, 1, blk), F32),
                        pltpu.VMEM((n_maps, 1, blk), F32),
                        pltpu.VMEM((n_maps, DIFF_V_DIM, blk), F32),
                        pltpu.VMEM((BF16_SUBLANES, LANES), F32),
                        pltpu.VMEM((4, 1, blk), F32)],
        compiler_params=pltpu.CompilerParams(
            dimension_semantics=("parallel", "arbitrary"), vmem_limit_bytes=VMEM_LIMIT_BYTES),
        name="mixers",
    )(iq, ik2, iwt, aq, ak, avt, lam_vecs, dq, dk, dvt, subln_col)


def _out_mlp_kernel(x_ref, ma_ref, mb_ref, g1_ref, sh_ref, sc_ref, g2_ref, n2_ref, nf_ref,
                    woa_ref, wob_ref, w1_ref, w2_ref, o_ref, *, ff_chunk):
    x = x_ref[0]
    o = _dot_tn(ma_ref[0], woa_ref[...]) + _dot_tn(mb_ref[0], wob_ref[...])
    x1 = x + g1_ref[0] * o
    ms = jnp.mean(x1 * x1, axis=-1, keepdims=True)
    h = x1 * lax.rsqrt(ms + NORM_EPS) * n2_ref[...]
    hb = (h * (1.0 + sc_ref[0]) + sh_ref[0]).astype(BF16)
    d_ff = w1_ref.shape[1]
    ff = jnp.zeros_like(x)
    for j in range(d_ff // ff_chunk):
        u = jnp.maximum(_dot(hb, w1_ref[:, j * ff_chunk:(j + 1) * ff_chunk]), 0.0)
        ff = ff + _dot((u * u).astype(BF16), w2_ref[j * ff_chunk:(j + 1) * ff_chunk, :])
    x2 = x1 + g2_ref[0] * ff
    ms2 = jnp.mean(x2 * x2, axis=-1, keepdims=True)
    o_ref[0] = x2 * lax.rsqrt(ms2 + NORM_EPS) * nf_ref[...]


def _out_mlp(x, mix_a, mix_b, g1, sh2, sc2, g2, n2, nf, wo_a, wo_b, w1, w2, *, tile):
    bsz, seq, d = x.shape
    tok = lambda b, i: (b, i, 0)
    per_b = lambda b, i: (b, 0, 0)
    mod_spec = pl.BlockSpec((1, 1, d), per_b)
    return pl.pallas_call(
        functools.partial(_out_mlp_kernel, ff_chunk=1024),
        grid=(bsz, seq // tile),
        in_specs=[pl.BlockSpec((1, tile, d), tok),
                  pl.BlockSpec((1, 512, tile), lambda b, i: (b, 0, i)),
                  pl.BlockSpec((1, 512, tile), lambda b, i: (b, 0, i)),
                  mod_spec, mod_spec, mod_spec, mod_spec,
                  _const_spec((1, d)), _const_spec((1, d)),
                  _const_spec(wo_a.shape), _const_spec(wo_b.shape),
                  _const_spec(w1.shape), _const_spec(w2.shape)],
        out_specs=pl.BlockSpec((1, tile, d), tok),
        out_shape=jax.ShapeDtypeStruct((bsz, seq, d), F32),
        compiler_params=pltpu.CompilerParams(
            dimension_semantics=("parallel", "parallel"), vmem_limit_bytes=VMEM_LIMIT_BYTES),
        name="out_mlp",
    )(x, mix_a, mix_b, g1, sh2, sc2, g2, n2, nf, wo_a, wo_b, w1, w2)


def _pad_cols(w, width):
    return jnp.pad(w, ((0, 0), (0, width - w.shape[1])))


def kernel(x, c, positions, w_ada, b_ada, norm1_w, w_in, idx_k_ln_w, idx_k_ln_b, lambda_q1, lambda_k1,
           lambda_q2, lambda_k2, subln_w, w_out, norm2_w, w_ff1, w_ff2, norm_f_w):
    bsz, seq, d = x.shape
    depth = w_ada.shape[0]
    assert depth == 1, "the fused final RMSNorm assumes a single layer"
    half = HEAD_DIM // 2
    inv_freq = ROPE_THETA ** (-jnp.arange(half, dtype=F32) / half)
    invf = jnp.tile(inv_freq, LANES // half).reshape(1, LANES)
    pos3 = positions.reshape(bsz, seq, 1)
    blk = min(ATT_BLOCK, seq)
    tile = min(512, seq)

    for l in range(depth):
        mod = _adaln(c, w_ada[l], b_ada[l])
        sh1, sc1, g1, sh2, sc2, g2 = [m.reshape(bsz, 1, d) for m in jnp.split(mod, 6, axis=-1)]

        w = w_in[l]
        wm = jnp.concatenate([w[:, :1024], w[:, 1536:2048], w[:, 2120:3144]], axis=1).astype(BF16)
        ws = _pad_cols(w[:, 2048:2112], LANES).astype(BF16)
        wwt = jnp.pad(w[:, 2112:2120].T, ((0, BF16_SUBLANES - N_IDX_HEADS), (0, 0))).astype(BF16)
        wavt = w[:, 1024:1536].T.astype(BF16)
        wdvt = w[:, 3144:3656].T.astype(BF16)
        lnw = _pad_cols(idx_k_ln_w[l].reshape(1, IDX_DIM), LANES)
        lnb = _pad_cols(idx_k_ln_b[l].reshape(1, IDX_DIM), LANES)

        aq, ak, avt, iq, ik2, iwt, dq, dk, dvt = _in_proj(
            x, pos3, sh1, sc1, norm1_w[l].reshape(1, d), invf, wm, ws, wwt, wavt, wdvt, lnw, lnb,
            tile=tile, chunk=blk)

        lam_vecs = jnp.stack([lambda_q1[l], lambda_k1[l], lambda_q2[l], lambda_k2[l]]).astype(F32)
        lam_init = 0.8 - 0.6 * math.exp(-0.3 * l)
        out_a, out_b = _mixers(iq, ik2, iwt, aq, ak, avt, lam_vecs, dq, dk, dvt,
                               subln_w[l].reshape(DIFF_V_DIM, 1), blk=blk, lam_init=lam_init)

        wo = w_out[l].astype(BF16)
        x = _out_mlp(x, out_a, out_b, g1, sh2, sc2, g2, norm2_w[l].reshape(1, d), norm_f_w.reshape(1, d),
                     wo[:512], wo[512:], w_ff1[l].astype(BF16), w_ff2[l].astype(BF16), tile=tile)
    return x
```

```python
import functools
import math

import jax
import jax.numpy as jnp
from jax import lax
from jax.experimental import pallas as pl
from jax.experimental.pallas import tpu as pltpu

F32 = jnp.float32
BF16 = jnp.bfloat16

HEAD_DIM = 64
N_DSA_HEADS = 8
N_IDX_HEADS = 8
IDX_DIM = 64
N_DIFF_HEADS = 4
DIFF_V_DIM = 128
TOPK_MAX = 256
ROPE_THETA = 10000.0
NORM_EPS = 1e-6
LN_EPS = 1e-5
LANES = 128
F32_SUBLANES = 8
BF16_SUBLANES = 16
ATT_BLOCK = 256
MASK_VALUE = -1e30
F32_MAX = float(jnp.finfo(jnp.float32).max)
MAX_BISECT_STEPS = 512
BISECT_FIRST_STEPS = 16
BISECT_FUSED_STEPS = 16
BISECT_STEPS_PER_TEST = 2
FUSED_COUNTS_UNDER_QK = 2
MIN_SAFE_NORMALISER = 2.0 ** -100
VMEM_LIMIT_BYTES = 56 * 1024 * 1024


def _dot(a, b):
    return jnp.dot(a, b, preferred_element_type=F32)


def _dot_nt(a, b):
    return lax.dot_general(a, b, (((1,), (1,)), ((), ())), preferred_element_type=F32)


def _dot_tn(a, b):
    return lax.dot_general(a, b, (((0,), (0,)), ((), ())), preferred_element_type=F32)


def _const_spec(shape):
    zeros = (0,) * len(shape)
    return pl.BlockSpec(shape, lambda *_: zeros, pipeline_mode=pl.Buffered(1))


def _adaln_kernel(c_ref, w_ref, b_ref, o_ref):
    c = c_ref[...]
    s = c / (1.0 + jnp.exp(-c))
    o_ref[...] = jnp.dot(s, w_ref[...], preferred_element_type=F32,
                         precision=lax.Precision.HIGHEST) + b_ref[...]


def _adaln(c, w, b):
    bsz, d = c.shape
    n = w.shape[1]
    tn = 1536
    return pl.pallas_call(
        _adaln_kernel,
        grid=(n // tn,),
        in_specs=[pl.BlockSpec((bsz, d), lambda j: (0, 0)),
                  pl.BlockSpec((d, tn), lambda j: (0, j)),
                  pl.BlockSpec((1, tn), lambda j: (0, j))],
        out_specs=pl.BlockSpec((bsz, tn), lambda j: (0, j)),
        out_shape=jax.ShapeDtypeStruct((bsz, n), F32),
        compiler_params=pltpu.CompilerParams(vmem_limit_bytes=VMEM_LIMIT_BYTES),
        name="adaln",
    )(c, w, b.reshape(1, n))


def _rope_group(x, cos, sin_signed, first_half):
    nxt = pltpu.roll(x, LANES - HEAD_DIM // 2, axis=1)
    prv = pltpu.roll(x, HEAD_DIM // 2, axis=1)
    return x * cos + jnp.where(first_half, nxt, prv) * sin_signed


def _in_proj_kernel(x_ref, pos_ref, sh_ref, sc_ref, nw_ref, invf_ref, wm_ref, ws_ref, wwt_ref,
                    wavt_ref, wdvt_ref, lnw_ref, lnb_ref,
                    aq_ref, ak_ref, avt_ref, iq_ref, ik_ref, iwt_ref, dq_ref, dk_ref, dvt_ref):
    x = x_ref[0]
    ms = jnp.mean(x * x, axis=-1, keepdims=True)
    h = x * lax.rsqrt(ms + NORM_EPS) * nw_ref[...]
    h = h * (1.0 + sc_ref[0]) + sh_ref[0]
    hb = h.astype(BF16)

    lane = lax.broadcasted_iota(jnp.int32, (1, LANES), 1)
    first_half = (lane % HEAD_DIM) < (HEAD_DIM // 2)
    ang = pos_ref[0].astype(F32) * invf_ref[...]
    cos = jnp.cos(ang)
    sin = jnp.sin(ang)
    sin_signed = jnp.where(first_half, -sin, sin)

    sm = _dot(hb, ws_ref[...])
    lo_half = lane < IDX_DIM
    mu = jnp.sum(sm, axis=-1, keepdims=True) * (1.0 / IDX_DIM)
    d = jnp.where(lo_half, sm - mu, 0.0)
    var = jnp.sum(d * d, axis=-1, keepdims=True) * (1.0 / IDX_DIM)
    y = d * lax.rsqrt(var + LN_EPS) * lnw_ref[...] + lnb_ref[...]
    y = y + pltpu.roll(y, IDX_DIM, axis=1)
    ik_ref[0] = _rope_group(y, cos, sin_signed, first_half).astype(ik_ref.dtype)

    wt = _dot_nt(wwt_ref[...], hb)
    iwt_ref[0] = wt[:N_IDX_HEADS] * ((N_IDX_HEADS ** -0.5) * (IDX_DIM ** -0.5))

    def roped(col0, out_ref, scale):
        wide = _dot(hb, wm_ref[:, col0:col0 + 4 * LANES])
        for g in range(4):
            y = _rope_group(wide[:, g * LANES:(g + 1) * LANES], cos, sin_signed, first_half)
            if scale != 1.0:
                y = y * scale
            out_ref[0, :, g * LANES:(g + 1) * LANES] = y.astype(out_ref.dtype)

    qscale = HEAD_DIM ** -0.5 * math.log2(math.e)
    roped(0, aq_ref, qscale)
    roped(512, ak_ref, 1.0)
    roped(1024, iq_ref, 1.0)
    roped(1536, dq_ref, qscale)
    roped(2048, dk_ref, 1.0)

    n_sub = avt_ref.shape[1]
    chunk = avt_ref.shape[3]
    for j in range(n_sub):
        hj = hb[j * chunk:(j + 1) * chunk]
        avt_ref[0, j] = _dot_nt(wavt_ref[...], hj).astype(avt_ref.dtype)
        dvt_ref[0, j] = _dot_nt(wdvt_ref[...], hj).astype(dvt_ref.dtype)


def _in_proj(x, pos3, sh1, sc1, nw, invf, wm, ws, wwt, wavt, wdvt, lnw, lnb, *, tile, chunk):
    bsz, seq, d = x.shape
    tok = lambda b, i: (b, i, 0)
    per_b = lambda b, i: (b, 0, 0)
    wide = jax.ShapeDtypeStruct((bsz, seq, 512), BF16)
    wide_t = jax.ShapeDtypeStruct((bsz, seq // chunk, 512, chunk), BF16)
    wide_spec = pl.BlockSpec((1, tile, 512), tok)
    wide_t_spec = pl.BlockSpec((1, tile // chunk, 512, chunk), lambda b, i: (b, i, 0, 0))
    out_shape = [wide, wide, wide_t, wide,
                 jax.ShapeDtypeStruct((bsz, seq, LANES), BF16),
                 jax.ShapeDtypeStruct((bsz, N_IDX_HEADS, seq), F32),
                 wide, wide, wide_t]
    out_specs = [wide_spec, wide_spec, wide_t_spec, wide_spec,
                 pl.BlockSpec((1, tile, LANES), tok),
                 pl.BlockSpec((1, N_IDX_HEADS, tile), lambda b, i: (b, 0, i)),
                 wide_spec, wide_spec, wide_t_spec]
    return pl.pallas_call(
        _in_proj_kernel,
        grid=(bsz, seq // tile),
        in_specs=[pl.BlockSpec((1, tile, d), tok),
                  pl.BlockSpec((1, tile, 1), tok),
                  pl.BlockSpec((1, 1, d), per_b),
                  pl.BlockSpec((1, 1, d), per_b),
                  _const_spec((1, d)),
                  _const_spec((1, LANES)),
                  _const_spec(wm.shape),
                  _const_spec(ws.shape),
                  _const_spec(wwt.shape),
                  _const_spec(wavt.shape),
                  _const_spec(wdvt.shape),
                  _const_spec((1, LANES)),
                  _const_spec((1, LANES))],
        out_specs=out_specs,
        out_shape=out_shape,
        compiler_params=pltpu.CompilerParams(
            dimension_semantics=("parallel", "parallel"), vmem_limit_bytes=VMEM_LIMIT_BYTES),
        name="in_proj",
    )(x, pos3, sh1, sc1, nw, invf, wm, ws, wwt, wavt, wdvt, lnw, lnb)


def _head_q(q_all, h, lane_lo):
    pair = q_all[:, (h // 2) * LANES:(h // 2 + 1) * LANES]
    keep = lane_lo if h % 2 == 0 else jnp.logical_not(lane_lo)
    return jnp.where(keep, pair, jnp.zeros_like(pair))


def _head_q_t(q_all, h):
    pair_t = q_all[:, (h // 2) * LANES:(h // 2 + 1) * LANES].astype(F32).T
    row = lax.broadcasted_iota(jnp.int32, (LANES, 1), 0)
    keep = (row < HEAD_DIM) if h % 2 == 0 else (row >= HEAD_DIM)
    return jnp.where(keep, pair_t, 0.0).astype(BF16)


def _softmax_update(s_list, vt_list, m_ref, l_ref, acc_ref):
    n = len(s_list)
    dv = vt_list[0].shape[0]
    ones = jnp.ones((BF16_SUBLANES, vt_list[0].shape[1]), BF16)
    m_old = [m_ref[g] for g in range(n)]
    m_new = [jnp.maximum(m_old[g], jnp.max(s_list[g], axis=0, keepdims=True)) for g in range(n)]
    p_list = [jnp.exp2(s_list[g] - m_new[g]).astype(BF16) for g in range(n)]
    alpha = [jnp.exp2(m_old[g] - m_new[g]) for g in range(n)]
    pv = [_dot(jnp.concatenate([vt_list[g], ones], axis=0), p_list[g]) for g in range(n)]
    for g in range(n):
        m_ref[g] = m_new[g]
        l_ref[g] = alpha[g] * l_ref[g] + pv[g][dv:dv + 1]
        acc_ref[g] = alpha[g] * acc_ref[g] + pv[g][:dv]


def _init_softmax_state(m_ref, l_ref, acc_ref):
    m_ref[...] = jnp.full(m_ref.shape, MASK_VALUE, F32)
    l_ref[...] = jnp.zeros(l_ref.shape, F32)
    acc_ref[...] = jnp.zeros(acc_ref.shape, F32)


def _bounded_update(s_list, vt_list, l_ref, acc_ref, before_sums=None):
    n = len(s_list)
    dv = vt_list[0].shape[0]
    ones = jnp.ones((BF16_SUBLANES, vt_list[0].shape[1]), BF16)
    p_list = [jnp.exp2(s_list[g]).astype(BF16) for g in range(n)]
    pv = [_dot(jnp.concatenate([vt_list[g], ones], axis=0), p_list[g]) for g in range(n)]
    for g in range(n):
        col_sum = pv[g][dv:dv + 1]
        if before_sums is not None and g == 1:
            col_sum = col_sum + before_sums
        l_ref[g] = l_ref[g] + col_sum
        acc_ref[g] = acc_ref[g] + pv[g][:dv]


def _head_selector(n_lanes):
    r = lax.broadcasted_iota(jnp.int32, (BF16_SUBLANES, n_lanes), 0)
    l = lax.broadcasted_iota(jnp.int32, (BF16_SUBLANES, n_lanes), 1)
    return jnp.where(l // HEAD_DIM == r, 1.0, 0.0).astype(BF16)


def _head_sq_norms(x, head_sel):
    return _dot_nt(head_sel, x * x)


def _max_key_sq_norms(k_ref, kmax_ref, rows_per_step):
    n_lanes = k_ref.shape[2]
    l = lax.broadcasted_iota(jnp.int32, (n_lanes, LANES), 0)
    h = lax.broadcasted_iota(jnp.int32, (n_lanes, LANES), 1)
    sel_t = jnp.where(l // HEAD_DIM == h, 1.0, 0.0).astype(BF16)

    def body(c, mx):
        start = pl.multiple_of(c * rows_per_step, rows_per_step)
        kk = k_ref[0, pl.ds(start, rows_per_step), :]
        return jnp.maximum(mx, jnp.max(_dot(kk * kk, sel_t), axis=0, keepdims=True))
    row = lax.fori_loop(0, k_ref.shape[1] // rows_per_step, body, jnp.zeros((1, LANES), F32))
    r = lax.broadcasted_iota(jnp.int32, kmax_ref.shape, 0)
    c = lax.broadcasted_iota(jnp.int32, kmax_ref.shape, 1)
    col = jnp.sum(jnp.where(r == c, row, 0.0), axis=1, keepdims=True)
    kmax_ref[...] = jnp.broadcast_to(col, kmax_ref.shape)


def _softmax_underflowed(l_ref):
    return jnp.logical_not(jnp.min(l_ref[...]) >= MIN_SAFE_NORMALISER)


def _mixer_kernel(iq_ref, ik_ref, iwt_ref, aq_ref, ak_ref, avt_ref, lam_ref, dq_ref, dk_ref, dvt_ref, sw_ref,
                  oa_ref, ob_ref,
                  score_ref, m_ref, l_ref, acc_ref, kmax_ref, dm_ref, dl_ref, dacc_ref, dkmax_ref, bis_ref,
                  *, blk, top_k, lam_init):
    i = pl.program_id(1)
    n_chunks = i + 1
    n_maps = 2 * N_DIFF_HEADS
    lane = lax.broadcasted_iota(jnp.int32, (1, LANES), 1)
    lane_lo = lane < HEAD_DIM
    qcol = lax.broadcasted_iota(jnp.int32, (1, blk), 1)
    qpos = i * blk + qcol
    krow = lax.broadcasted_iota(jnp.int32, (blk, 1), 0)
    kf = float(top_k)

    aq_all = aq_ref[0]
    dq_all = dq_ref[0]
    head_sel = _head_selector(aq_all.shape[1])

    @pl.when(i == 0)
    def _():
        rows_per_step = math.gcd(4 * blk, ak_ref.shape[1])
        _max_key_sq_norms(ak_ref, kmax_ref, rows_per_step)
        _max_key_sq_norms(dk_ref, dkmax_ref, rows_per_step)

    d_maps = [_head_q_t(dq_all, g) for g in range(n_maps)]
    d_bound = jnp.sqrt(_head_sq_norms(dq_all, head_sel) * dkmax_ref[:, :1])
    d_shifts = [d_bound[g:g + 1] for g in range(n_maps)]
    a_bound = jnp.sqrt(_head_sq_norms(aq_all, head_sel) * kmax_ref[:, :1])
    qa_heads = [_head_q_t(aq_all, h) for h in range(N_DSA_HEADS)]
    lam_vecs = lam_ref[...]
    lam = (jnp.exp(jnp.sum(lam_vecs[0:1] * lam_vecs[1:2], axis=-1, keepdims=True))
           - jnp.exp(jnp.sum(lam_vecs[2:3] * lam_vecs[3:4], axis=-1, keepdims=True)) + lam_init)

    def diff_chunk(c, diagonal, bounded, after=None, before_sums=None):
        start = pl.multiple_of(c * blk, blk)
        if diagonal:
            bias = jnp.where(krow <= qcol, 0.0, MASK_VALUE)
        s_list, vt_list = [], []
        for h in range(N_DIFF_HEADS):
            cols = slice(h * LANES, (h + 1) * LANES)
            kk = dk_ref[0, pl.ds(start, blk), cols]
            for comp in range(2):
                g = 2 * h + comp
                s = _dot(kk, d_maps[g])
                if bounded:
                    s = s - (d_shifts[g] if after is None else d_shifts[g] + after)
                s_list.append(s + bias if diagonal else s)
                vt_list.append(dvt_ref[0, c, cols, :])
        if bounded:
            _bounded_update(s_list, vt_list, dl_ref, dacc_ref, before_sums)
        else:
            _softmax_update(s_list, vt_list, dm_ref, dl_ref, dacc_ref)

    _init_softmax_state(dm_ref, dl_ref, dacc_ref)

    iq_all = iq_ref[0]
    wt = iwt_ref[0]
    q_heads = [_head_q_t(iq_all, h) for h in range(N_IDX_HEADS)]
    w_rows = [wt[h:h + 1, :] for h in range(N_IDX_HEADS)]

    n_pairs = (n_chunks + 1) // 2

    def pair_sum(hit):
        m = jnp.where(hit, 1.0, 0.0)
        return jnp.sum((m[0] + m[1]).reshape(blk // F32_SUBLANES, F32_SUBLANES, blk), axis=0)

    zero8 = jnp.zeros((F32_SUBLANES, blk), F32)

    def score_pair(j, mn, mx, w_first):
        for u in range(2):
            c = 2 * j + u
            start = pl.multiple_of(c * blk, blk)
            kk = ik_ref[0, pl.ds(start, blk), :]
            acc = jnp.zeros((blk, blk), F32)
            for h in range(N_IDX_HEADS):
                acc = acc + jnp.maximum(_dot(kk, q_heads[h]), 0.0) * (w_first if h == 0 else w_rows[h])
            causal = (c * blk + krow) <= qpos
            sc = jnp.where(causal, acc, -jnp.inf)
            score_ref[c] = sc
            mx = jnp.maximum(mx, jnp.max(sc, axis=0, keepdims=True))
            mn = jnp.minimum(mn, jnp.min(jnp.where(causal, acc, jnp.inf), axis=0, keepdims=True))
        return mn, mx

    def zero_counts(j):
        s = score_ref[pl.ds(2 * j, 2)]
        return pair_sum(s >= 0.0), pair_sum(s > 0.0)

    def score_body(j, carry):
        mn, mx, ge, gt = carry
        ge_p, gt_p = zero_counts(j - 1)
        after = jnp.sum(ge_p + gt_p, axis=0, keepdims=True) * 0.0
        mn, mx = score_pair(j, mn, mx, w_rows[0] + after)
        return mn, mx, ge + ge_p, gt + gt_p

    mn, mx = score_pair(0, jnp.full((1, blk), jnp.inf, F32), jnp.full((1, blk), -jnp.inf, F32), w_rows[0])
    mn, mx, ge8, gt8 = lax.fori_loop(1, n_pairs, score_body, (mn, mx, zero8, zero8))
    ge_p, gt_p = zero_counts(n_pairs - 1)
    ge0 = jnp.sum(ge8 + ge_p, axis=0, keepdims=True)
    gt0 = jnp.sum(gt8 + gt_p, axis=0, keepdims=True)


    def count_ge(cand):
        def body(j, acc):
            return acc + pair_sum(score_ref[pl.ds(2 * j, 2)] >= cand)
        return jnp.sum(lax.fori_loop(0, n_pairs, body, zero8), axis=0, keepdims=True)

    def count_ge_gt(cand):
        def body(j, carry):
            ge, gt = carry
            s = score_ref[pl.ds(2 * j, 2)]
            return ge + pair_sum(s >= cand), gt + pair_sum(s > cand)
        ge, gt = lax.fori_loop(0, n_pairs, body, (zero8, zero8))
        return jnp.sum(ge, axis=0, keepdims=True), jnp.sum(gt, axis=0, keepdims=True)

    n_valid = (qpos + 1).astype(F32)
    need = n_valid > kf
    kth_is_zero = jnp.logical_and(gt0 < kf, ge0 >= kf)
    nonneg = ge0 >= kf
    lo0 = jnp.where(nonneg, 0.0, mn)
    cnt0 = jnp.where(nonneg, ge0, n_valid)
    above_max = jnp.where(mx > 0.0, jnp.minimum(mx * 2.0, F32_MAX), 1.0)
    hi0 = jnp.where(nonneg, above_max, 0.0)
    done0 = jnp.logical_or(jnp.logical_not(need), jnp.logical_or(kth_is_zero, cnt0 == kf))
    active0 = jnp.where(done0, 0.0, 1.0)

    def midpoint(st):
        return 0.5 * st[0] + 0.5 * st[1]

    def halve(st, count):
        mid = midpoint(st)
        return narrow(st, mid, count(mid))

    def narrow(st, mid, cnt):
        lo, hi, cnt_lo, active = st
        conv = jnp.logical_or(mid <= lo, mid >= hi)
        upd = jnp.logical_and(active > 0.0, jnp.logical_not(conv))
        ge = cnt >= kf
        up_lo = jnp.logical_and(upd, ge)
        up_hi = jnp.logical_and(upd, jnp.logical_not(ge))
        lo = jnp.where(up_lo, mid, lo)
        cnt_lo = jnp.where(up_lo, cnt, cnt_lo)
        hi = jnp.where(up_hi, mid, hi)
        finished = jnp.logical_or(conv, jnp.logical_and(up_lo, cnt == kf))
        return lo, hi, cnt_lo, jnp.where(finished, 0.0, active)

    n_fused = jnp.minimum(i, BISECT_FUSED_STEPS)
    for k, v in enumerate((lo0, hi0, cnt0, active0)):
        bis_ref[k] = v

    for pairs in range(1, score_ref.shape[0] // 2 + 1):
        @pl.when(n_pairs == pairs)
        def _(pairs=pairs):
            def fused_step(c, st):
                mid = midpoint(st)
                n_first = min(FUSED_COUNTS_UNDER_QK, (pairs + 1) // 2)
                parts = []
                for js in (range(0, n_first), range(n_first, pairs)):
                    acc = zero8
                    for j in js:
                        acc = acc + pair_sum(score_ref[2 * j:2 * j + 2] >= mid)
                    parts.append(jnp.sum(acc, axis=0, keepdims=True))
                diff_chunk(c, False, True, after=parts[0] * 0.0,
                           before_sums=parts[1] * 0.0 if pairs > n_first else None)
                return narrow(st, mid, parts[0] + parts[1])
            st = lax.fori_loop(0, n_fused, fused_step, tuple(bis_ref[k] for k in range(4)))
            for k in range(4):
                bis_ref[k] = st[k]

    st = tuple(bis_ref[k] for k in range(4))
    st = lax.fori_loop(0, jnp.maximum(BISECT_FIRST_STEPS - n_fused, 0),
                       lambda _, s: halve(s, count_ge), st)

    def bis_cond(st):
        return jnp.logical_and(st[4] > 0.0, st[6] < MAX_BISECT_STEPS)

    def bis_body(st):
        lo, hi, cnt_lo, active = lax.fori_loop(0, BISECT_STEPS_PER_TEST, lambda _, s: halve(s, count_ge), st[:4])
        tied = jnp.max(jnp.where(jnp.logical_and(need, cnt_lo > kf), 1.0, 0.0))
        return lo, hi, cnt_lo, active, jnp.max(active), tied, st[6] + BISECT_STEPS_PER_TEST

    lo, _, cnt_lo, _, _, tied, _ = lax.while_loop(
        bis_cond, bis_body, st + (jnp.float32(1.0), jnp.float32(0.0), jnp.int32(0)))
    thr = jnp.where(need, lo, -F32_MAX)

    excess = jnp.logical_and(need, cnt_lo > kf)

    @pl.when(tied > 0.0)
    def _():
        _, gt = count_ge_gt(thr)
        allow = kf - gt
        r_i = lax.broadcasted_iota(jnp.int32, (blk, blk), 0)
        c_i = lax.broadcasted_iota(jnp.int32, (blk, blk), 1)
        earlier = jnp.where(c_i < r_i, 1.0, 0.0).astype(BF16)

        def tie_body(c, seen):
            s = score_ref[c]
            eq = jnp.logical_and(s == thr, excess)
            eqf = jnp.where(eq, 1.0, 0.0)
            rank = seen + _dot(earlier, eqf.astype(BF16))
            drop = jnp.logical_and(eq, rank >= allow)
            score_ref[c] = jnp.where(drop, -jnp.inf, s)
            return seen + jnp.sum(eqf, axis=0, keepdims=True)

        lax.fori_loop(0, n_chunks, tie_body, jnp.zeros((1, blk), F32))

    def attend(bounded, shift):
        def att_body(c, carry):
            start = pl.multiple_of(c * blk, blk)
            bias = jnp.where(score_ref[c] >= thr, shift, MASK_VALUE)
            s_list, vt_list = [], []
            for h in range(N_DSA_HEADS):
                pair = slice((h // 2) * LANES, (h // 2 + 1) * LANES)
                kk = ak_ref[0, pl.ds(start, blk), pair]
                s_list.append(_dot(kk, qa_heads[h]) + bias)
                vt_list.append(avt_ref[0, c, pair, :])
            if bounded:
                _bounded_update(s_list, vt_list, l_ref, acc_ref)
            else:
                _softmax_update(s_list, vt_list, m_ref, l_ref, acc_ref)
            return carry

        _init_softmax_state(m_ref, l_ref, acc_ref)
        lax.fori_loop(0, n_chunks, att_body, 0)

    if score_ref.shape[0] - 1 > BISECT_FUSED_STEPS:
        def diff_rest(c, carry):
            diff_chunk(c, False, True)
            return carry

        lax.fori_loop(n_fused, i, diff_rest, 0)

    attend(True, -jnp.max(a_bound, axis=0, keepdims=True))

    def write_outputs():
        for j in range(N_DSA_HEADS // 2):
            a = acc_ref[2 * j] * (1.0 / l_ref[2 * j])
            b = acc_ref[2 * j + 1] * (1.0 / l_ref[2 * j + 1])
            o_t = jnp.concatenate([a[:HEAD_DIM], b[HEAD_DIM:]], axis=0)
            oa_ref[0, j * LANES:(j + 1) * LANES, :] = o_t.astype(oa_ref.dtype)
        for h in range(N_DIFF_HEADS):
            out = (dacc_ref[2 * h] * (1.0 / dl_ref[2 * h])
                   - dacc_ref[2 * h + 1] * (lam / dl_ref[2 * h + 1]))
            out = out * lax.rsqrt(jnp.mean(out * out, axis=0, keepdims=True) + NORM_EPS)
            out = out * sw_ref[...] * (1.0 - lam_init)
            ob_ref[0, h * LANES:(h + 1) * LANES, :] = out.astype(ob_ref.dtype)

    diff_chunk(i, True, True)
    dsa_bad = _softmax_underflowed(l_ref)
    diff_bad = _softmax_underflowed(dl_ref)
    write_outputs()

    @pl.when(jnp.logical_or(dsa_bad, diff_bad))
    def _():
        @pl.when(dsa_bad)
        def _():
            attend(False, 0.0)

        @pl.when(diff_bad)
        def _():
            def body(c, carry):
                diff_chunk(c, False, False)
                return carry
            _init_softmax_state(dm_ref, dl_ref, dacc_ref)
            lax.fori_loop(0, i, body, 0)
            diff_chunk(i, True, False)

        write_outputs()


def _mixers(iq, ik2, iwt, aq, ak, avt, lam_vecs, dq, dk, dvt, subln_col, *, blk, lam_init):
    bsz, seq, _ = aq.shape
    top_k = min(TOPK_MAX, seq // 4)
    n_chunks = seq // blk
    assert n_chunks % 2 == 0, "key chunks are walked in pairs"
    qblk = lambda b, i: (b, i, 0)
    whole = lambda b, i: (b, 0, 0)
    whole_t = lambda b, i: (b, 0, 0, 0)
    n_maps = 2 * N_DIFF_HEADS
    q_spec = pl.BlockSpec((1, blk, 512), qblk)
    kv_spec = pl.BlockSpec((1, seq, 512), whole)
    vt_spec = pl.BlockSpec((1, n_chunks, 512, blk), whole_t)
    return pl.pallas_call(
        functools.partial(_mixer_kernel, blk=blk, top_k=top_k, lam_init=lam_init),
        grid=(bsz, n_chunks),
        in_specs=[q_spec,
                  pl.BlockSpec((1, seq, LANES), whole),
                  pl.BlockSpec((1, N_IDX_HEADS, blk), lambda b, i: (b, 0, i)),
                  q_spec, kv_spec, vt_spec,
                  _const_spec(lam_vecs.shape),
                  q_spec, kv_spec, vt_spec,
                  _const_spec((DIFF_V_DIM, 1))],
        out_specs=[pl.BlockSpec((1, 512, blk), lambda b, i: (b, 0, i))] * 2,
        out_shape=[jax.ShapeDtypeStruct((bsz, 512, seq), BF16)] * 2,
        scratch_shapes=[pltpu.VMEM((n_chunks, blk, blk), F32),
                        pltpu.VMEM((N_DSA_HEADS, 1, blk), F32),
                        pltpu.VMEM((N_DSA_HEADS, 1, blk), F32),
                        pltpu.VMEM((N_DSA_HEADS, LANES, blk), F32),
                        pltpu.VMEM((BF16_SUBLANES, LANES), F32),
                        pltpu.VMEM((n_maps, 1, blk), F32),
                        pltpu.VMEM((n_maps, 1, blk), F32),
                        pltpu.VMEM((n_maps, DIFF_V_DIM, blk), F32),
                        pltpu.VMEM((BF16_SUBLANES, LANES), F32),
                        pltpu.VMEM((4, 1, blk), F32)],
        compiler_params=pltpu.CompilerParams(
            dimension_semantics=("parallel", "arbitrary"), vmem_limit_bytes=VMEM_LIMIT_BYTES),
        name="mixers",
    )(iq, ik2, iwt, aq, ak, avt, lam_vecs, dq, dk, dvt, subln_col)


def _out_mlp_kernel(x_ref, ma_ref, mb_ref, g1_ref, sh_ref, sc_ref, g2_ref, n2_ref, nf_ref,
                    woa_ref, wob_ref, w1_ref, w2_ref, o_ref, *, ff_chunk):
    x = x_ref[0]
    o = _dot_tn(ma_ref[0], woa_ref[...]) + _dot_tn(mb_ref[0], wob_ref[...])
    x1 = x + g1_ref[0] * o
    ms = jnp.mean(x1 * x1, axis=-1, keepdims=True)
    h = x1 * lax.rsqrt(ms + NORM_EPS) * n2_ref[...]
    hb = (h * (1.0 + sc_ref[0]) + sh_ref[0]).astype(BF16)
    d_ff = w1_ref.shape[1]
    ff = jnp.zeros_like(x)
    for j in range(d_ff // ff_chunk):
        u = jnp.maximum(_dot(hb, w1_ref[:, j * ff_chunk:(j + 1) * ff_chunk]), 0.0)
        ff = ff + _dot((u * u).astype(BF16), w2_ref[j * ff_chunk:(j + 1) * ff_chunk, :])
    x2 = x1 + g2_ref[0] * ff
    ms2 = jnp.mean(x2 * x2, axis=-1, keepdims=True)
    o_ref[0] = x2 * lax.rsqrt(ms2 + NORM_EPS) * nf_ref[...]


def _out_mlp(x, mix_a, mix_b, g1, sh2, sc2, g2, n2, nf, wo_a, wo_b, w1, w2, *, tile):
    bsz, seq, d = x.shape
    tok = lambda b, i: (b, i, 0)
    per_b = lambda b, i: (b, 0, 0)
    mod_spec = pl.BlockSpec((1, 1, d), per_b)
    return pl.pallas_call(
        functools.partial(_out_mlp_kernel, ff_chunk=1024),
        grid=(bsz, seq // tile),
        in_specs=[pl.BlockSpec((1, tile, d), tok),
                  pl.BlockSpec((1, 512, tile), lambda b, i: (b, 0, i)),
                  pl.BlockSpec((1, 512, tile), lambda b, i: (b, 0, i)),
                  mod_spec, mod_spec, mod_spec, mod_spec,
                  _const_spec((1, d)), _const_spec((1, d)),
                  _const_spec(wo_a.shape), _const_spec(wo_b.shape),
                  _const_spec(w1.shape), _const_spec(w2.shape)],
        out_specs=pl.BlockSpec((1, tile, d), tok),
        out_shape=jax.ShapeDtypeStruct((bsz, seq, d), F32),
        compiler_params=pltpu.CompilerParams(
            dimension_semantics=("parallel", "parallel"), vmem_limit_bytes=VMEM_LIMIT_BYTES),
        name="out_mlp",
    )(x, mix_a, mix_b, g1, sh2, sc2, g2, n2, nf, wo_a, wo_b, w1, w2)


def _pad_cols(w, width):
    return jnp.pad(w, ((0, 0), (0, width - w.shape[1])))


def kernel(x, c, positions, w_ada, b_ada, norm1_w, w_in, idx_k_ln_w, idx_k_ln_b, lambda_q1, lambda_k1,
           lambda_q2, lambda_k2, subln_w, w_out, norm2_w, w_ff1, w_ff2, norm_f_w):
    bsz, seq, d = x.shape
    depth = w_ada.shape[0]
    assert depth == 1, "the fused final RMSNorm assumes a single layer"
    half = HEAD_DIM // 2
    inv_freq = ROPE_THETA ** (-jnp.arange(half, dtype=F32) / half)
    invf = jnp.tile(inv_freq, LANES // half).reshape(1, LANES)
    pos3 = positions.reshape(bsz, seq, 1)
    blk = min(ATT_BLOCK, seq)
    tile = min(512, seq)

    for l in range(depth):
        mod = _adaln(c, w_ada[l], b_ada[l])
        sh1, sc1, g1, sh2, sc2, g2 = [m.reshape(bsz, 1, d) for m in jnp.split(mod, 6, axis=-1)]

        w = w_in[l]
        wm = jnp.concatenate([w[:, :1024], w[:, 1536:2048], w[:, 2120:3144]], axis=1).astype(BF16)
        ws = _pad_cols(w[:, 2048:2112], LANES).astype(BF16)
        wwt = jnp.pad(w[:, 2112:2120].T, ((0, BF16_SUBLANES - N_IDX_HEADS), (0, 0))).astype(BF16)
        wavt = w[:, 1024:1536].T.astype(BF16)
        wdvt = w[:, 3144:3656].T.astype(BF16)
        lnw = _pad_cols(idx_k_ln_w[l].reshape(1, IDX_DIM), LANES)
        lnb = _pad_cols(idx_k_ln_b[l].reshape(1, IDX_DIM), LANES)

        aq, ak, avt, iq, ik2, iwt, dq, dk, dvt = _in_proj(
            x, pos3, sh1, sc1, norm1_w[l].reshape(1, d), invf, wm, ws, wwt, wavt, wdvt, lnw, lnb,
            tile=tile, chunk=blk)

        lam_vecs = jnp.stack([lambda_q1[l], lambda_k1[l], lambda_q2[l], lambda_k2[l]]).astype(F32)
        lam_init = 0.8 - 0.6 * math.exp(-0.3 * l)
        out_a, out_b = _mixers(iq, ik2, iwt, aq, ak, avt, lam_vecs, dq, dk, dvt,
                               subln_w[l].reshape(DIFF_V_DIM, 1), blk=blk, lam_init=lam_init)

        wo = w_out[l].astype(BF16)
        x = _out_mlp(x, out_a, out_b, g1, sh2, sc2, g2, norm2_w[l].reshape(1, d), norm_f_w.reshape(1, d),
                     wo[:512], wo[512:], w_ff1[l].astype(BF16), w_ff2[l].astype(BF16), tile=tile)
    return x
```

```python
import functools
import math

import jax
import jax.numpy as jnp
from jax import lax
from jax.experimental import pallas as pl
from jax.experimental.pallas import tpu as pltpu

F32 = jnp.float32
BF16 = jnp.bfloat16

HEAD_DIM = 64
N_DSA_HEADS = 8
N_IDX_HEADS = 8
IDX_DIM = 64
N_DIFF_HEADS = 4
DIFF_V_DIM = 128
TOPK_MAX = 256
ROPE_THETA = 10000.0
NORM_EPS = 1e-6
LN_EPS = 1e-5
LANES = 128
F32_SUBLANES = 8
BF16_SUBLANES = 16
ATT_BLOCK = 256
MASK_VALUE = -1e30
F32_MAX = float(jnp.finfo(jnp.float32).max)
MAX_BISECT_STEPS = 512
BISECT_FIRST_STEPS = 16
BISECT_FUSED_STEPS = 16
BISECT_STEPS_PER_TEST = 2
FUSED_COUNTS_UNDER_QK = 2
MIN_SAFE_NORMALISER = 2.0 ** -100
VMEM_LIMIT_BYTES = 56 * 1024 * 1024


def _dot(a, b):
    return jnp.dot(a, b, preferred_element_type=F32)


def _dot_nt(a, b):
    return lax.dot_general(a, b, (((1,), (1,)), ((), ())), preferred_element_type=F32)


def _dot_tn(a, b):
    return lax.dot_general(a, b, (((0,), (0,)), ((), ())), preferred_element_type=F32)


def _const_spec(shape):
    zeros = (0,) * len(shape)
    return pl.BlockSpec(shape, lambda *_: zeros, pipeline_mode=pl.Buffered(1))


def _adaln_kernel(c_ref, w_ref, b_ref, o_ref):
    c = c_ref[...]
    s = c / (1.0 + jnp.exp(-c))
    o_ref[...] = jnp.dot(s, w_ref[...], preferred_element_type=F32,
                         precision=lax.Precision.HIGHEST) + b_ref[...]


def _adaln(c, w, b):
    bsz, d = c.shape
    n = w.shape[1]
    tn = 1536
    return pl.pallas_call(
        _adaln_kernel,
        grid=(n // tn,),
        in_specs=[pl.BlockSpec((bsz, d), lambda j: (0, 0)),
                  pl.BlockSpec((d, tn), lambda j: (0, j)),
                  pl.BlockSpec((1, tn), lambda j: (0, j))],
        out_specs=pl.BlockSpec((bsz, tn), lambda j: (0, j)),
        out_shape=jax.ShapeDtypeStruct((bsz, n), F32),
        compiler_params=pltpu.CompilerParams(vmem_limit_bytes=VMEM_LIMIT_BYTES),
        name="adaln",
    )(c, w, b.reshape(1, n))


def _rope_group(x, cos, sin_signed, first_half):
    nxt = pltpu.roll(x, LANES - HEAD_DIM // 2, axis=1)
    prv = pltpu.roll(x, HEAD_DIM // 2, axis=1)
    return x * cos + jnp.where(first_half, nxt, prv) * sin_signed


def _in_proj_kernel(x_ref, pos_ref, sh_ref, sc_ref, nw_ref, invf_ref, wm_ref, ws_ref, wwt_ref,
                    wavt_ref, wdvt_ref, lnw_ref, lnb_ref,
                    aq_ref, ak_ref, avt_ref, iq_ref, ik_ref, iwt_ref, dq_ref, dk_ref, dvt_ref):
    x = x_ref[0]
    ms = jnp.mean(x * x, axis=-1, keepdims=True)
    h = x * lax.rsqrt(ms + NORM_EPS) * nw_ref[...]
    h = h * (1.0 + sc_ref[0]) + sh_ref[0]
    hb = h.astype(BF16)

    lane = lax.broadcasted_iota(jnp.int32, (1, LANES), 1)
    first_half = (lane % HEAD_DIM) < (HEAD_DIM // 2)
    ang = pos_ref[0].astype(F32) * invf_ref[...]
    cos = jnp.cos(ang)
    sin = jnp.sin(ang)
    sin_signed = jnp.where(first_half, -sin, sin)

    sm = _dot(hb, ws_ref[...])
    lo_half = lane < IDX_DIM
    mu = jnp.sum(sm, axis=-1, keepdims=True) * (1.0 / IDX_DIM)
    d = jnp.where(lo_half, sm - mu, 0.0)
    var = jnp.sum(d * d, axis=-1, keepdims=True) * (1.0 / IDX_DIM)
    y = d * lax.rsqrt(var + LN_EPS) * lnw_ref[...] + lnb_ref[...]
    y = y + pltpu.roll(y, IDX_DIM, axis=1)
    ik_ref[0] = _rope_group(y, cos, sin_signed, first_half).astype(ik_ref.dtype)

    wt = _dot_nt(wwt_ref[...], hb)
    iwt_ref[0] = wt[:N_IDX_HEADS] * ((N_IDX_HEADS ** -0.5) * (IDX_DIM ** -0.5))

    def roped(col0, out_ref, scale):
        wide = _dot(hb, wm_ref[:, col0:col0 + 4 * LANES])
        for g in range(4):
            y = _rope_group(wide[:, g * LANES:(g + 1) * LANES], cos, sin_signed, first_half)
            if scale != 1.0:
                y = y * scale
            out_ref[0, :, g * LANES:(g + 1) * LANES] = y.astype(out_ref.dtype)

    qscale = HEAD_DIM ** -0.5 * math.log2(math.e)
    roped(0, aq_ref, qscale)
    roped(512, ak_ref, 1.0)
    roped(1024, iq_ref, 1.0)
    roped(1536, dq_ref, qscale)
    roped(2048, dk_ref, 1.0)

    n_sub = avt_ref.shape[1]
    chunk = avt_ref.shape[3]
    for j in range(n_sub):
        hj = hb[j * chunk:(j + 1) * chunk]
        avt_ref[0, j] = _dot_nt(wavt_ref[...], hj).astype(avt_ref.dtype)
        dvt_ref[0, j] = _dot_nt(wdvt_ref[...], hj).astype(dvt_ref.dtype)


def _in_proj(x, pos3, sh1, sc1, nw, invf, wm, ws, wwt, wavt, wdvt, lnw, lnb, *, tile, chunk):
    bsz, seq, d = x.shape
    tok = lambda b, i: (b, i, 0)
    per_b = lambda b, i: (b, 0, 0)
    wide = jax.ShapeDtypeStruct((bsz, seq, 512), BF16)
    wide_t = jax.ShapeDtypeStruct((bsz, seq // chunk, 512, chunk), BF16)
    wide_spec = pl.BlockSpec((1, tile, 512), tok)
    wide_t_spec = pl.BlockSpec((1, tile // chunk, 512, chunk), lambda b, i: (b, i, 0, 0))
    out_shape = [wide, wide, wide_t, wide,
                 jax.ShapeDtypeStruct((bsz, seq, LANES), BF16),
                 jax.ShapeDtypeStruct((bsz, N_IDX_HEADS, seq), F32),
                 wide, wide, wide_t]
    out_specs = [wide_spec, wide_spec, wide_t_spec, wide_spec,
                 pl.BlockSpec((1, tile, LANES), tok),
                 pl.BlockSpec((1, N_IDX_HEADS, tile), lambda b, i: (b, 0, i)),
                 wide_spec, wide_spec, wide_t_spec]
    return pl.pallas_call(
        _in_proj_kernel,
        grid=(bsz, seq // tile),
        in_specs=[pl.BlockSpec((1, tile, d), tok),
                  pl.BlockSpec((1, tile, 1), tok),
                  pl.BlockSpec((1, 1, d), per_b),
                  pl.BlockSpec((1, 1, d), per_b),
                  _const_spec((1, d)),
                  _const_spec((1, LANES)),
                  _const_spec(wm.shape),
                  _const_spec(ws.shape),
                  _const_spec(wwt.shape),
                  _const_spec(wavt.shape),
                  _const_spec(wdvt.shape),
                  _const_spec((1, LANES)),
                  _const_spec((1, LANES))],
        out_specs=out_specs,
        out_shape=out_shape,
        compiler_params=pltpu.CompilerParams(
            dimension_semantics=("parallel", "parallel"), vmem_limit_bytes=VMEM_LIMIT_BYTES),
        name="in_proj",
    )(x, pos3, sh1, sc1, nw, invf, wm, ws, wwt, wavt, wdvt, lnw, lnb)


def _head_q_t(q_all, h):
    pair_t = q_all[:, (h // 2) * LANES:(h // 2 + 1) * LANES].astype(F32).T
    row = lax.broadcasted_iota(jnp.int32, (LANES, 1), 0)
    keep = (row < HEAD_DIM) if h % 2 == 0 else (row >= HEAD_DIM)
    return jnp.where(keep, pair_t, 0.0).astype(BF16)


def _softmax_update(s_list, vt_list, m_ref, l_ref, acc_ref):
    n = len(s_list)
    dv = vt_list[0].shape[0]
    ones = jnp.ones((BF16_SUBLANES, vt_list[0].shape[1]), BF16)
    m_old = [m_ref[g] for g in range(n)]
    m_new = [jnp.maximum(m_old[g], jnp.max(s_list[g], axis=0, keepdims=True)) for g in range(n)]
    p_list = [jnp.exp2(s_list[g] - m_new[g]).astype(BF16) for g in range(n)]
    alpha = [jnp.exp2(m_old[g] - m_new[g]) for g in range(n)]
    pv = [_dot(jnp.concatenate([vt_list[g], ones], axis=0), p_list[g]) for g in range(n)]
    for g in range(n):
        m_ref[g] = m_new[g]
        l_ref[g] = alpha[g] * l_ref[g] + pv[g][dv:dv + 1]
        acc_ref[g] = alpha[g] * acc_ref[g] + pv[g][:dv]


def _init_softmax_state(m_ref, l_ref, acc_ref):
    m_ref[...] = jnp.full(m_ref.shape, MASK_VALUE, F32)
    l_ref[...] = jnp.zeros(l_ref.shape, F32)
    acc_ref[...] = jnp.zeros(acc_ref.shape, F32)


def _bounded_update(s_list, vt_list, l_ref, acc_ref, before_sums=None):
    n = len(s_list)
    dv = vt_list[0].shape[0]
    ones = jnp.ones((BF16_SUBLANES, vt_list[0].shape[1]), BF16)
    p_list = [jnp.exp2(s_list[g]).astype(BF16) for g in range(n)]
    pv = [_dot(jnp.concatenate([vt_list[g], ones], axis=0), p_list[g]) for g in range(n)]
    for g in range(n):
        col_sum = pv[g][dv:dv + 1]
        if before_sums is not None and g == 1:
            col_sum = col_sum + before_sums
        l_ref[g] = l_ref[g] + col_sum
        acc_ref[g] = acc_ref[g] + pv[g][:dv]


def _head_selector(n_lanes):
    r = lax.broadcasted_iota(jnp.int32, (BF16_SUBLANES, n_lanes), 0)
    l = lax.broadcasted_iota(jnp.int32, (BF16_SUBLANES, n_lanes), 1)
    return jnp.where(l // HEAD_DIM == r, 1.0, 0.0).astype(BF16)


def _head_sq_norms(x, head_sel):
    return _dot_nt(head_sel, x * x)


def _max_key_sq_norms(k_ref, kmax_ref, rows_per_step):
    n_lanes = k_ref.shape[2]
    l = lax.broadcasted_iota(jnp.int32, (n_lanes, LANES), 0)
    h = lax.broadcasted_iota(jnp.int32, (n_lanes, LANES), 1)
    sel_t = jnp.where(l // HEAD_DIM == h, 1.0, 0.0).astype(BF16)

    def body(c, mx):
        start = pl.multiple_of(c * rows_per_step, rows_per_step)
        kk = k_ref[0, pl.ds(start, rows_per_step), :]
        return jnp.maximum(mx, jnp.max(_dot(kk * kk, sel_t), axis=0, keepdims=True))
    row = lax.fori_loop(0, k_ref.shape[1] // rows_per_step, body, jnp.zeros((1, LANES), F32))
    r = lax.broadcasted_iota(jnp.int32, kmax_ref.shape, 0)
    c = lax.broadcasted_iota(jnp.int32, kmax_ref.shape, 1)
    col = jnp.sum(jnp.where(r == c, row, 0.0), axis=1, keepdims=True)
    kmax_ref[...] = jnp.broadcast_to(col, kmax_ref.shape)


def _softmax_underflowed(l_ref):
    return jnp.logical_not(jnp.min(l_ref[...]) >= MIN_SAFE_NORMALISER)


def _mixer_kernel(iq_ref, ik_ref, iwt_ref, aq_ref, ak_ref, avt_ref, lam_ref, dq_ref, dk_ref, dvt_ref, sw_ref,
                  oa_ref, ob_ref,
                  score_ref, m_ref, l_ref, acc_ref, kmax_ref, dm_ref, dl_ref, dacc_ref, dkmax_ref, bis_ref,
                  *, blk, top_k, lam_init):
    i = pl.program_id(1)
    n_chunks = i + 1
    n_maps = 2 * N_DIFF_HEADS
    qcol = lax.broadcasted_iota(jnp.int32, (1, blk), 1)
    qpos = i * blk + qcol
    krow = lax.broadcasted_iota(jnp.int32, (blk, 1), 0)
    kf = float(top_k)

    aq_all = aq_ref[0]
    dq_all = dq_ref[0]
    head_sel = _head_selector(aq_all.shape[1])

    @pl.when(i == 0)
    def _():
        rows_per_step = math.gcd(4 * blk, ak_ref.shape[1])
        _max_key_sq_norms(ak_ref, kmax_ref, rows_per_step)
        _max_key_sq_norms(dk_ref, dkmax_ref, rows_per_step)

    d_maps = [_head_q_t(dq_all, g) for g in range(n_maps)]
    d_bound = jnp.sqrt(_head_sq_norms(dq_all, head_sel) * dkmax_ref[:, :1])
    d_shifts = [d_bound[g:g + 1] for g in range(n_maps)]
    a_bound = jnp.sqrt(_head_sq_norms(aq_all, head_sel) * kmax_ref[:, :1])
    qa_heads = [_head_q_t(aq_all, h) for h in range(N_DSA_HEADS)]
    lam_vecs = lam_ref[...]
    lam = (jnp.exp(jnp.sum(lam_vecs[0:1] * lam_vecs[1:2], axis=-1, keepdims=True))
           - jnp.exp(jnp.sum(lam_vecs[2:3] * lam_vecs[3:4], axis=-1, keepdims=True)) + lam_init)

    def diff_chunk(c, diagonal, bounded, after=None, before_sums=None):
        start = pl.multiple_of(c * blk, blk)
        if diagonal:
            bias = jnp.where(krow <= qcol, 0.0, MASK_VALUE)
        s_list, vt_list = [], []
        for h in range(N_DIFF_HEADS):
            cols = slice(h * LANES, (h + 1) * LANES)
            kk = dk_ref[0, pl.ds(start, blk), cols]
            for comp in range(2):
                g = 2 * h + comp
                s = _dot(kk, d_maps[g])
                if bounded:
                    s = s - (d_shifts[g] if after is None else d_shifts[g] + after)
                s_list.append(s + bias if diagonal else s)
                vt_list.append(dvt_ref[0, c, cols, :])
        if bounded:
            _bounded_update(s_list, vt_list, dl_ref, dacc_ref, before_sums)
        else:
            _softmax_update(s_list, vt_list, dm_ref, dl_ref, dacc_ref)

    _init_softmax_state(dm_ref, dl_ref, dacc_ref)

    iq_all = iq_ref[0]
    wt = iwt_ref[0]
    q_heads = [_head_q_t(iq_all, h) for h in range(N_IDX_HEADS)]
    w_rows = [wt[h:h + 1, :] for h in range(N_IDX_HEADS)]

    n_pairs = (n_chunks + 1) // 2

    def pair_sum(hit):
        m = jnp.where(hit, 1.0, 0.0)
        return jnp.sum((m[0] + m[1]).reshape(blk // F32_SUBLANES, F32_SUBLANES, blk), axis=0)

    zero8 = jnp.zeros((F32_SUBLANES, blk), F32)

    def score_pair(j, mn, mx, w_first):
        for u in range(2):
            c = 2 * j + u
            start = pl.multiple_of(c * blk, blk)
            kk = ik_ref[0, pl.ds(start, blk), :]
            acc = jnp.zeros((blk, blk), F32)
            for h in range(N_IDX_HEADS):
                acc = acc + jnp.maximum(_dot(kk, q_heads[h]), 0.0) * (w_first if h == 0 else w_rows[h])
            causal = (c * blk + krow) <= qpos
            sc = jnp.where(causal, acc, -jnp.inf)
            score_ref[c] = sc
            mx = jnp.maximum(mx, jnp.max(sc, axis=0, keepdims=True))
            mn = jnp.minimum(mn, jnp.min(jnp.where(causal, acc, jnp.inf), axis=0, keepdims=True))
        return mn, mx

    def zero_counts(j):
        s = score_ref[pl.ds(2 * j, 2)]
        return pair_sum(s >= 0.0), pair_sum(s > 0.0)

    def score_body(j, carry):
        mn, mx, ge, gt = carry
        ge_p, gt_p = zero_counts(j - 1)
        after = jnp.sum(ge_p + gt_p, axis=0, keepdims=True) * 0.0
        mn, mx = score_pair(j, mn, mx, w_rows[0] + after)
        return mn, mx, ge + ge_p, gt + gt_p

    mn, mx = score_pair(0, jnp.full((1, blk), jnp.inf, F32), jnp.full((1, blk), -jnp.inf, F32), w_rows[0])
    mn, mx, ge8, gt8 = lax.fori_loop(1, n_pairs, score_body, (mn, mx, zero8, zero8))
    ge_p, gt_p = zero_counts(n_pairs - 1)
    ge0 = jnp.sum(ge8 + ge_p, axis=0, keepdims=True)
    gt0 = jnp.sum(gt8 + gt_p, axis=0, keepdims=True)


    def count_ge(cand):
        def body(j, acc):
            return acc + pair_sum(score_ref[pl.ds(2 * j, 2)] >= cand)
        return jnp.sum(lax.fori_loop(0, n_pairs, body, zero8), axis=0, keepdims=True)

    def count_ge_gt(cand):
        def body(j, carry):
            ge, gt = carry
            s = score_ref[pl.ds(2 * j, 2)]
            return ge + pair_sum(s >= cand), gt + pair_sum(s > cand)
        ge, gt = lax.fori_loop(0, n_pairs, body, (zero8, zero8))
        return jnp.sum(ge, axis=0, keepdims=True), jnp.sum(gt, axis=0, keepdims=True)

    n_valid = (qpos + 1).astype(F32)
    need = n_valid > kf
    kth_is_zero = jnp.logical_and(gt0 < kf, ge0 >= kf)
    nonneg = ge0 >= kf
    lo0 = jnp.where(nonneg, 0.0, mn)
    cnt0 = jnp.where(nonneg, ge0, n_valid)
    above_max = jnp.where(mx > 0.0, jnp.minimum(mx * 2.0, F32_MAX), 1.0)
    hi0 = jnp.where(nonneg, above_max, 0.0)
    done0 = jnp.logical_or(jnp.logical_not(need), jnp.logical_or(kth_is_zero, cnt0 == kf))
    active0 = jnp.where(done0, 0.0, 1.0)

    def midpoint(st):
        return 0.5 * st[0] + 0.5 * st[1]

    def halve(st, count):
        mid = midpoint(st)
        return narrow(st, mid, count(mid))

    def narrow(st, mid, cnt):
        lo, hi, cnt_lo, active = st
        conv = jnp.logical_or(mid <= lo, mid >= hi)
        upd = jnp.logical_and(active > 0.0, jnp.logical_not(conv))
        ge = cnt >= kf
        up_lo = jnp.logical_and(upd, ge)
        up_hi = jnp.logical_and(upd, jnp.logical_not(ge))
        lo = jnp.where(up_lo, mid, lo)
        cnt_lo = jnp.where(up_lo, cnt, cnt_lo)
        hi = jnp.where(up_hi, mid, hi)
        finished = jnp.logical_or(conv, jnp.logical_and(up_lo, cnt == kf))
        return lo, hi, cnt_lo, jnp.where(finished, 0.0, active)

    n_fused = jnp.minimum(i, BISECT_FUSED_STEPS)
    for k, v in enumerate((lo0, hi0, cnt0, active0)):
        bis_ref[k] = v

    for pairs in range(1, score_ref.shape[0] // 2 + 1):
        @pl.when(n_pairs == pairs)
        def _(pairs=pairs):
            def fused_step(c, st):
                mid = midpoint(st)
                n_first = min(FUSED_COUNTS_UNDER_QK, (pairs + 1) // 2)
                parts = []
                for js in (range(0, n_first), range(n_first, pairs)):
                    acc = zero8
                    for j in js:
                        acc = acc + pair_sum(score_ref[2 * j:2 * j + 2] >= mid)
                    parts.append(jnp.sum(acc, axis=0, keepdims=True))
                diff_chunk(c, False, True, after=parts[0] * 0.0,
                           before_sums=parts[1] * 0.0 if pairs > n_first else None)
                return narrow(st, mid, parts[0] + parts[1])
            st = lax.fori_loop(0, n_fused, fused_step, tuple(bis_ref[k] for k in range(4)))
            for k in range(4):
                bis_ref[k] = st[k]

    st = tuple(bis_ref[k] for k in range(4))
    st = lax.fori_loop(0, jnp.maximum(BISECT_FIRST_STEPS - n_fused, 0),
                       lambda _, s: halve(s, count_ge), st)

    def bis_cond(st):
        return jnp.logical_and(st[4] > 0.0, st[6] < MAX_BISECT_STEPS)

    def bis_body(st):
        lo, hi, cnt_lo, active = lax.fori_loop(0, BISECT_STEPS_PER_TEST, lambda _, s: halve(s, count_ge), st[:4])
        tied = jnp.max(jnp.where(jnp.logical_and(need, cnt_lo > kf), 1.0, 0.0))
        return lo, hi, cnt_lo, active, jnp.max(active), tied, st[6] + BISECT_STEPS_PER_TEST

    lo, _, cnt_lo, _, _, tied, _ = lax.while_loop(
        bis_cond, bis_body, st + (jnp.float32(1.0), jnp.float32(0.0), jnp.int32(0)))
    thr = jnp.where(need, lo, -F32_MAX)

    excess = jnp.logical_and(need, cnt_lo > kf)

    @pl.when(tied > 0.0)
    def _():
        _, gt = count_ge_gt(thr)
        allow = kf - gt
        r_i = lax.broadcasted_iota(jnp.int32, (blk, blk), 0)
        c_i = lax.broadcasted_iota(jnp.int32, (blk, blk), 1)
        earlier = jnp.where(c_i < r_i, 1.0, 0.0).astype(BF16)

        def tie_body(c, seen):
            s = score_ref[c]
            eq = jnp.logical_and(s == thr, excess)
            eqf = jnp.where(eq, 1.0, 0.0)
            rank = seen + _dot(earlier, eqf.astype(BF16))
            drop = jnp.logical_and(eq, rank >= allow)
            score_ref[c] = jnp.where(drop, -jnp.inf, s)
            return seen + jnp.sum(eqf, axis=0, keepdims=True)

        lax.fori_loop(0, n_chunks, tie_body, jnp.zeros((1, blk), F32))

    def attend(bounded, shift):
        def att_body(c, carry):
            start = pl.multiple_of(c * blk, blk)
            bias = jnp.where(score_ref[c] >= thr, shift, MASK_VALUE)
            s_list, vt_list = [], []
            for h in range(N_DSA_HEADS):
                pair = slice((h // 2) * LANES, (h // 2 + 1) * LANES)
                kk = ak_ref[0, pl.ds(start, blk), pair]
                s_list.append(_dot(kk, qa_heads[h]) + bias)
                vt_list.append(avt_ref[0, c, pair, :])
            if bounded:
                _bounded_update(s_list, vt_list, l_ref, acc_ref)
            else:
                _softmax_update(s_list, vt_list, m_ref, l_ref, acc_ref)
            return carry

        _init_softmax_state(m_ref, l_ref, acc_ref)
        lax.fori_loop(0, n_chunks, att_body, 0)

    if score_ref.shape[0] - 1 > BISECT_FUSED_STEPS:
        def diff_rest(c, carry):
            diff_chunk(c, False, True)
            return carry

        lax.fori_loop(n_fused, i, diff_rest, 0)

    attend(True, -jnp.max(a_bound, axis=0, keepdims=True))

    def write_outputs():
        for j in range(N_DSA_HEADS // 2):
            a = acc_ref[2 * j] * (1.0 / l_ref[2 * j])
            b = acc_ref[2 * j + 1] * (1.0 / l_ref[2 * j + 1])
            o_t = jnp.concatenate([a[:HEAD_DIM], b[HEAD_DIM:]], axis=0)
            oa_ref[0, j * LANES:(j + 1) * LANES, :] = o_t.astype(oa_ref.dtype)
        for h in range(N_DIFF_HEADS):
            out = (dacc_ref[2 * h] * (1.0 / dl_ref[2 * h])
                   - dacc_ref[2 * h + 1] * (lam / dl_ref[2 * h + 1]))
            out = out * lax.rsqrt(jnp.mean(out * out, axis=0, keepdims=True) + NORM_EPS)
            out = out * sw_ref[...] * (1.0 - lam_init)
            ob_ref[0, h * LANES:(h + 1) * LANES, :] = out.astype(ob_ref.dtype)

    diff_chunk(i, True, True)
    dsa_bad = _softmax_underflowed(l_ref)
    diff_bad = _softmax_underflowed(dl_ref)
    write_outputs()

    @pl.when(jnp.logical_or(dsa_bad, diff_bad))
    def _():
        @pl.when(dsa_bad)
        def _():
            attend(False, 0.0)

        @pl.when(diff_bad)
        def _():
            def body(c, carry):
                diff_chunk(c, False, False)
                return carry
            _init_softmax_state(dm_ref, dl_ref, dacc_ref)
            lax.fori_loop(0, i, body, 0)
            diff_chunk(i, True, False)

        write_outputs()


def _mixers(iq, ik2, iwt, aq, ak, avt, lam_vecs, dq, dk, dvt, subln_col, *, blk, lam_init):
    bsz, seq, _ = aq.shape
    top_k = min(TOPK_MAX, seq // 4)
    n_chunks = seq // blk
    assert n_chunks % 2 == 0, "key chunks are walked in pairs"
    qblk = lambda b, i: (b, i, 0)
    whole = lambda b, i: (b, 0, 0)
    whole_t = lambda b, i: (b, 0, 0, 0)
    n_maps = 2 * N_DIFF_HEADS
    q_spec = pl.BlockSpec((1, blk, 512), qblk)
    kv_spec = pl.BlockSpec((1, seq, 512), whole)
    vt_spec = pl.BlockSpec((1, n_chunks, 512, blk), whole_t)
    return pl.pallas_call(
        functools.partial(_mixer_kernel, blk=blk, top_k=top_k, lam_init=lam_init),
        grid=(bsz, n_chunks),
        in_specs=[q_spec,
                  pl.BlockSpec((1, seq, LANES), whole),
                  pl.BlockSpec((1, N_IDX_HEADS, blk), lambda b, i: (b, 0, i)),
                  q_spec, kv_spec, vt_spec,
                  _const_spec(lam_vecs.shape),
                  q_spec, kv_spec, vt_spec,
                  _const_spec((DIFF_V_DIM, 1))],
        out_specs=[pl.BlockSpec((1, 512, blk), lambda b, i: (b, 0, i))] * 2,
        out_shape=[jax.ShapeDtypeStruct((bsz, 512, seq), BF16)] * 2,
        scratch_shapes=[pltpu.VMEM((n_chunks, blk, blk), F32),
                        pltpu.VMEM((N_DSA_HEADS, 1, blk), F32),
                        pltpu.VMEM((N_DSA_HEADS, 1, blk), F32),
                        pltpu.VMEM((N_DSA_HEADS, LANES, blk), F32),
                        pltpu.VMEM((BF16_SUBLANES, LANES), F32),
                        pltpu.VMEM((n_maps, 1, blk), F32),
                        pltpu.VMEM((n_maps, 1, blk), F32),
                        pltpu.VMEM((n_maps, DIFF_V_DIM, blk), F32),
                        pltpu.VMEM((BF16_SUBLANES, LANES), F32),
                        pltpu.VMEM((4, 1, blk), F32)],
        compiler_params=pltpu.CompilerParams(
            dimension_semantics=("parallel", "arbitrary"), vmem_limit_bytes=VMEM_LIMIT_BYTES),
        name="mixers",
    )(iq, ik2, iwt, aq, ak, avt, lam_vecs, dq, dk, dvt, subln_col)


def _out_mlp_kernel(x_ref, ma_ref, mb_ref, g1_ref, sh_ref, sc_ref, g2_ref, n2_ref, nf_ref,
                    woa_ref, wob_ref, w1_ref, w2_ref, o_ref, *, ff_chunk):
    x = x_ref[0]
    o = _dot_tn(ma_ref[0], woa_ref[...]) + _dot_tn(mb_ref[0], wob_ref[...])
    x1 = x + g1_ref[0] * o
    ms = jnp.mean(x1 * x1, axis=-1, keepdims=True)
    h = x1 * lax.rsqrt(ms + NORM_EPS) * n2_ref[...]
    hb = (h * (1.0 + sc_ref[0]) + sh_ref[0]).astype(BF16)
    d_ff = w1_ref.shape[1]
    ff = jnp.zeros_like(x)
    for j in range(d_ff // ff_chunk):
        u = jnp.maximum(_dot(hb, w1_ref[:, j * ff_chunk:(j + 1) * ff_chunk]), 0.0)
        ff = ff + _dot((u * u).astype(BF16), w2_ref[j * ff_chunk:(j + 1) * ff_chunk, :])
    x2 = x1 + g2_ref[0] * ff
    ms2 = jnp.mean(x2 * x2, axis=-1, keepdims=True)
    o_ref[0] = x2 * lax.rsqrt(ms2 + NORM_EPS) * nf_ref[...]


def _out_mlp(x, mix_a, mix_b, g1, sh2, sc2, g2, n2, nf, wo_a, wo_b, w1, w2, *, tile):
    bsz, seq, d = x.shape
    tok = lambda b, i: (b, i, 0)
    per_b = lambda b, i: (b, 0, 0)
    mod_spec = pl.BlockSpec((1, 1, d), per_b)
    return pl.pallas_call(
        functools.partial(_out_mlp_kernel, ff_chunk=1024),
        grid=(bsz, seq // tile),
        in_specs=[pl.BlockSpec((1, tile, d), tok),
                  pl.BlockSpec((1, 512, tile), lambda b, i: (b, 0, i)),
                  pl.BlockSpec((1, 512, tile), lambda b, i: (b, 0, i)),
                  mod_spec, mod_spec, mod_spec, mod_spec,
                  _const_spec((1, d)), _const_spec((1, d)),
                  _const_spec(wo_a.shape), _const_spec(wo_b.shape),
                  _const_spec(w1.shape), _const_spec(w2.shape)],
        out_specs=pl.BlockSpec((1, tile, d), tok),
        out_shape=jax.ShapeDtypeStruct((bsz, seq, d), F32),
        compiler_params=pltpu.CompilerParams(
            dimension_semantics=("parallel", "parallel"), vmem_limit_bytes=VMEM_LIMIT_BYTES),
        name="out_mlp",
    )(x, mix_a, mix_b, g1, sh2, sc2, g2, n2, nf, wo_a, wo_b, w1, w2)


def _pad_cols(w, width):
    return jnp.pad(w, ((0, 0), (0, width - w.shape[1])))


def kernel(x, c, positions, w_ada, b_ada, norm1_w, w_in, idx_k_ln_w, idx_k_ln_b, lambda_q1, lambda_k1,
           lambda_q2, lambda_k2, subln_w, w_out, norm2_w, w_ff1, w_ff2, norm_f_w):
    bsz, seq, d = x.shape
    depth = w_ada.shape[0]
    assert depth == 1, "the fused final RMSNorm assumes a single layer"
    half = HEAD_DIM // 2
    inv_freq = ROPE_THETA ** (-jnp.arange(half, dtype=F32) / half)
    invf = jnp.tile(inv_freq, LANES // half).reshape(1, LANES)
    pos3 = positions.reshape(bsz, seq, 1)
    blk = min(ATT_BLOCK, seq)
    tile = min(512, seq)

    for l in range(depth):
        mod = _adaln(c, w_ada[l], b_ada[l])
        sh1, sc1, g1, sh2, sc2, g2 = [m.reshape(bsz, 1, d) for m in jnp.split(mod, 6, axis=-1)]

        w = w_in[l]
        wm = jnp.concatenate([w[:, :1024], w[:, 1536:2048], w[:, 2120:3144]], axis=1).astype(BF16)
        ws = _pad_cols(w[:, 2048:2112], LANES).astype(BF16)
        wwt = jnp.pad(w[:, 2112:2120].T, ((0, BF16_SUBLANES - N_IDX_HEADS), (0, 0))).astype(BF16)
        wavt = w[:, 1024:1536].T.astype(BF16)
        wdvt = w[:, 3144:3656].T.astype(BF16)
        lnw = _pad_cols(idx_k_ln_w[l].reshape(1, IDX_DIM), LANES)
        lnb = _pad_cols(idx_k_ln_b[l].reshape(1, IDX_DIM), LANES)

        aq, ak, avt, iq, ik2, iwt, dq, dk, dvt = _in_proj(
            x, pos3, sh1, sc1, norm1_w[l].reshape(1, d), invf, wm, ws, wwt, wavt, wdvt, lnw, lnb,
            tile=tile, chunk=blk)

        lam_vecs = jnp.stack([lambda_q1[l], lambda_k1[l], lambda_q2[l], lambda_k2[l]]).astype(F32)
        lam_init = 0.8 - 0.6 * math.exp(-0.3 * l)
        out_a, out_b = _mixers(iq, ik2, iwt, aq, ak, avt, lam_vecs, dq, dk, dvt,
                               subln_w[l].reshape(DIFF_V_DIM, 1), blk=blk, lam_init=lam_init)

        wo = w_out[l].astype(BF16)
        x = _out_mlp(x, out_a, out_b, g1, sh2, sc2, g2, norm2_w[l].reshape(1, d), norm_f_w.reshape(1, d),
                     wo[:512], wo[512:], w_ff1[l].astype(BF16), w_ff2[l].astype(BF16), tile=tile)
    return x
```

```python
import functools
import math

import jax
import jax.numpy as jnp
from jax import lax
from jax.experimental import pallas as pl
from jax.experimental.pallas import tpu as pltpu

F32 = jnp.float32
BF16 = jnp.bfloat16

HEAD_DIM = 64
N_DSA_HEADS = 8
N_IDX_HEADS = 8
IDX_DIM = 64
N_DIFF_HEADS = 4
DIFF_V_DIM = 128
TOPK_MAX = 256
ROPE_THETA = 10000.0
NORM_EPS = 1e-6
LN_EPS = 1e-5
LANES = 128
F32_SUBLANES = 8
BF16_SUBLANES = 16
ATT_BLOCK = 256
MASK_VALUE = -1e30
F32_MAX = float(jnp.finfo(jnp.float32).max)
MAX_BISECT_STEPS = 512
BISECT_FIRST_STEPS = 16
BISECT_FUSED_STEPS = 16
BISECT_STEPS_PER_TEST = 2
FUSED_COUNTS_UNDER_QK = 2
MIN_SAFE_NORMALISER = 2.0 ** -100
VMEM_LIMIT_BYTES = 56 * 1024 * 1024


def _dot(a, b):
    return jnp.dot(a, b, preferred_element_type=F32)


def _dot_nt(a, b):
    return lax.dot_general(a, b, (((1,), (1,)), ((), ())), preferred_element_type=F32)


def _dot_tn(a, b):
    return lax.dot_general(a, b, (((0,), (0,)), ((), ())), preferred_element_type=F32)


def _const_spec(shape):
    zeros = (0,) * len(shape)
    return pl.BlockSpec(shape, lambda *_: zeros, pipeline_mode=pl.Buffered(1))


def _adaln_kernel(c_ref, w_ref, b_ref, o_ref):
    c = c_ref[...]
    s = c / (1.0 + jnp.exp(-c))
    o_ref[...] = jnp.dot(s, w_ref[...], preferred_element_type=F32,
                         precision=lax.Precision.HIGHEST) + b_ref[...]


def _adaln(c, w, b):
    bsz, d = c.shape
    n = w.shape[1]
    tn = 1536
    return pl.pallas_call(
        _adaln_kernel,
        grid=(n // tn,),
        in_specs=[pl.BlockSpec((bsz, d), lambda j: (0, 0)),
                  pl.BlockSpec((d, tn), lambda j: (0, j)),
                  pl.BlockSpec((1, tn), lambda j: (0, j))],
        out_specs=pl.BlockSpec((bsz, tn), lambda j: (0, j)),
        out_shape=jax.ShapeDtypeStruct((bsz, n), F32),
        compiler_params=pltpu.CompilerParams(vmem_limit_bytes=VMEM_LIMIT_BYTES),
        name="adaln",
    )(c, w, b.reshape(1, n))


def _rope_group(x, cos, sin_signed, first_half):
    nxt = pltpu.roll(x, LANES - HEAD_DIM // 2, axis=1)
    prv = pltpu.roll(x, HEAD_DIM // 2, axis=1)
    return x * cos + jnp.where(first_half, nxt, prv) * sin_signed


def _in_proj_kernel(x_ref, pos_ref, sh_ref, sc_ref, nw_ref, invf_ref, wm_ref, ws_ref, wwt_ref,
                    wavt_ref, wdvt_ref, lnw_ref, lnb_ref,
                    aq_ref, ak_ref, avt_ref, iq_ref, ik_ref, iwt_ref, dq_ref, dk_ref, dvt_ref):
    x = x_ref[0]
    ms = jnp.mean(x * x, axis=-1, keepdims=True)
    h = x * lax.rsqrt(ms + NORM_EPS) * nw_ref[...]
    h = h * (1.0 + sc_ref[0]) + sh_ref[0]
    hb = h.astype(BF16)

    lane = lax.broadcasted_iota(jnp.int32, (1, LANES), 1)
    first_half = (lane % HEAD_DIM) < (HEAD_DIM // 2)
    ang = pos_ref[0].astype(F32) * invf_ref[...]
    cos = jnp.cos(ang)
    sin = jnp.sin(ang)
    sin_signed = jnp.where(first_half, -sin, sin)

    sm = _dot(hb, ws_ref[...])
    lo_half = lane < IDX_DIM
    mu = jnp.sum(sm, axis=-1, keepdims=True) * (1.0 / IDX_DIM)
    d = jnp.where(lo_half, sm - mu, 0.0)
    var = jnp.sum(d * d, axis=-1, keepdims=True) * (1.0 / IDX_DIM)
    y = d * lax.rsqrt(var + LN_EPS) * lnw_ref[...] + lnb_ref[...]
    y = y + pltpu.roll(y, IDX_DIM, axis=1)
    ik_ref[0] = _rope_group(y, cos, sin_signed, first_half).astype(ik_ref.dtype)

    wt = _dot_nt(wwt_ref[...], hb)
    iwt_ref[0] = wt[:N_IDX_HEADS] * ((N_IDX_HEADS ** -0.5) * (IDX_DIM ** -0.5))

    def roped(col0, out_ref, scale):
        wide = _dot(hb, wm_ref[:, col0:col0 + 4 * LANES])
        for g in range(4):
            y = _rope_group(wide[:, g * LANES:(g + 1) * LANES], cos, sin_signed, first_half)
            if scale != 1.0:
                y = y * scale
            out_ref[0, :, g * LANES:(g + 1) * LANES] = y.astype(out_ref.dtype)

    qscale = HEAD_DIM ** -0.5 * math.log2(math.e)
    roped(0, aq_ref, qscale)
    roped(512, ak_ref, 1.0)
    roped(1024, iq_ref, 1.0)
    roped(1536, dq_ref, qscale)
    roped(2048, dk_ref, 1.0)

    n_sub = avt_ref.shape[1]
    chunk = avt_ref.shape[3]
    for j in range(n_sub):
        hj = hb[j * chunk:(j + 1) * chunk]
        avt_ref[0, j] = _dot_nt(wavt_ref[...], hj).astype(avt_ref.dtype)
        dvt_ref[0, j] = _dot_nt(wdvt_ref[...], hj).astype(dvt_ref.dtype)


def _in_proj(x, pos3, sh1, sc1, nw, invf, wm, ws, wwt, wavt, wdvt, lnw, lnb, *, tile, chunk):
    bsz, seq, d = x.shape
    tok = lambda b, i: (b, i, 0)
    per_b = lambda b, i: (b, 0, 0)
    wide = jax.ShapeDtypeStruct((bsz, seq, 512), BF16)
    wide_t = jax.ShapeDtypeStruct((bsz, seq // chunk, 512, chunk), BF16)
    wide_spec = pl.BlockSpec((1, tile, 512), tok)
    wide_t_spec = pl.BlockSpec((1, tile // chunk, 512, chunk), lambda b, i: (b, i, 0, 0))
    out_shape = [wide, wide, wide_t, wide,
                 jax.ShapeDtypeStruct((bsz, seq, LANES), BF16),
                 jax.ShapeDtypeStruct((bsz, N_IDX_HEADS, seq), F32),
                 wide, wide, wide_t]
    out_specs = [wide_spec, wide_spec, wide_t_spec, wide_spec,
                 pl.BlockSpec((1, tile, LANES), tok),
                 pl.BlockSpec((1, N_IDX_HEADS, tile), lambda b, i: (b, 0, i)),
                 wide_spec, wide_spec, wide_t_spec]
    return pl.pallas_call(
        _in_proj_kernel,
        grid=(bsz, seq // tile),
        in_specs=[pl.BlockSpec((1, tile, d), tok),
                  pl.BlockSpec((1, tile, 1), tok),
                  pl.BlockSpec((1, 1, d), per_b),
                  pl.BlockSpec((1, 1, d), per_b),
                  _const_spec((1, d)),
                  _const_spec((1, LANES)),
                  _const_spec(wm.shape),
                  _const_spec(ws.shape),
                  _const_spec(wwt.shape),
                  _const_spec(wavt.shape),
                  _const_spec(wdvt.shape),
                  _const_spec((1, LANES)),
                  _const_spec((1, LANES))],
        out_specs=out_specs,
        out_shape=out_shape,
        compiler_params=pltpu.CompilerParams(
            dimension_semantics=("parallel", "parallel"), vmem_limit_bytes=VMEM_LIMIT_BYTES),
        name="in_proj",
    )(x, pos3, sh1, sc1, nw, invf, wm, ws, wwt, wavt, wdvt, lnw, lnb)


def _head_q_t(q_all, h):
    pair_t = q_all[:, (h // 2) * LANES:(h // 2 + 1) * LANES].astype(F32).T
    row = lax.broadcasted_iota(jnp.int32, (LANES, 1), 0)
    keep = (row < HEAD_DIM) if h % 2 == 0 else (row >= HEAD_DIM)
    return jnp.where(keep, pair_t, 0.0).astype(BF16)


def _softmax_update(s_list, vt_list, m_ref, l_ref, acc_ref):
    n = len(s_list)
    dv = vt_list[0].shape[0]
    ones = jnp.ones((BF16_SUBLANES, vt_list[0].shape[1]), BF16)
    m_old = [m_ref[g] for g in range(n)]
    m_new = [jnp.maximum(m_old[g], jnp.max(s_list[g], axis=0, keepdims=True)) for g in range(n)]
    p_list = [jnp.exp2(s_list[g] - m_new[g]).astype(BF16) for g in range(n)]
    alpha = [jnp.exp2(m_old[g] - m_new[g]) for g in range(n)]
    pv = [_dot(jnp.concatenate([vt_list[g], ones], axis=0), p_list[g]) for g in range(n)]
    for g in range(n):
        m_ref[g] = m_new[g]
        l_ref[g] = alpha[g] * l_ref[g] + pv[g][dv:dv + 1]
        acc_ref[g] = alpha[g] * acc_ref[g] + pv[g][:dv]


def _init_softmax_state(m_ref, l_ref, acc_ref):
    m_ref[...] = jnp.full(m_ref.shape, MASK_VALUE, F32)
    l_ref[...] = jnp.zeros(l_ref.shape, F32)
    acc_ref[...] = jnp.zeros(acc_ref.shape, F32)


def _bounded_update(s_list, vt_list, l_ref, acc_ref, before_sums=None, first=0):
    n = len(s_list)
    dv = vt_list[0].shape[0]
    ones = jnp.ones((BF16_SUBLANES, vt_list[0].shape[1]), BF16)
    p_list = [jnp.exp2(s_list[g]).astype(BF16) for g in range(n)]
    pv = [_dot(jnp.concatenate([vt_list[g], ones], axis=0), p_list[g]) for g in range(n)]
    for g in range(n):
        col_sum = pv[g][dv:dv + 1]
        if before_sums is not None and g == 1:
            col_sum = col_sum + before_sums
        l_ref[first + g] = l_ref[first + g] + col_sum
        acc_ref[first + g] = acc_ref[first + g] + pv[g][:dv]
    return pv[n - 1][dv:dv + 1] * 0.0


def _head_selector(n_lanes):
    r = lax.broadcasted_iota(jnp.int32, (BF16_SUBLANES, n_lanes), 0)
    l = lax.broadcasted_iota(jnp.int32, (BF16_SUBLANES, n_lanes), 1)
    return jnp.where(l // HEAD_DIM == r, 1.0, 0.0).astype(BF16)


def _head_sq_norms(x, head_sel):
    return _dot_nt(head_sel, x * x)


def _max_key_sq_norms(k_ref, kmax_ref, rows_per_step):
    n_lanes = k_ref.shape[2]
    l = lax.broadcasted_iota(jnp.int32, (n_lanes, LANES), 0)
    h = lax.broadcasted_iota(jnp.int32, (n_lanes, LANES), 1)
    sel_t = jnp.where(l // HEAD_DIM == h, 1.0, 0.0).astype(BF16)

    def body(c, mx):
        start = pl.multiple_of(c * rows_per_step, rows_per_step)
        kk = k_ref[0, pl.ds(start, rows_per_step), :]
        return jnp.maximum(mx, jnp.max(_dot(kk * kk, sel_t), axis=0, keepdims=True))
    row = lax.fori_loop(0, k_ref.shape[1] // rows_per_step, body, jnp.zeros((1, LANES), F32))
    r = lax.broadcasted_iota(jnp.int32, kmax_ref.shape, 0)
    c = lax.broadcasted_iota(jnp.int32, kmax_ref.shape, 1)
    col = jnp.sum(jnp.where(r == c, row, 0.0), axis=1, keepdims=True)
    kmax_ref[...] = jnp.broadcast_to(col, kmax_ref.shape)


def _softmax_underflowed(l_ref):
    return jnp.logical_not(jnp.min(l_ref[...]) >= MIN_SAFE_NORMALISER)


def _mixer_kernel(iq_ref, ik_ref, iwt_ref, aq_ref, ak_ref, avt_ref, lam_ref, dq_ref, dk_ref, dvt_ref, sw_ref,
                  oa_ref, ob_ref,
                  score_ref, m_ref, l_ref, acc_ref, kmax_ref, dm_ref, dl_ref, dacc_ref, dkmax_ref, bis_ref,
                  *, blk, top_k, lam_init):
    i = pl.program_id(1)
    n_chunks = i + 1
    n_maps = 2 * N_DIFF_HEADS
    qcol = lax.broadcasted_iota(jnp.int32, (1, blk), 1)
    qpos = i * blk + qcol
    krow = lax.broadcasted_iota(jnp.int32, (blk, 1), 0)
    kf = float(top_k)

    aq_all = aq_ref[0]
    dq_all = dq_ref[0]
    head_sel = _head_selector(aq_all.shape[1])

    @pl.when(i == 0)
    def _():
        rows_per_step = math.gcd(4 * blk, ak_ref.shape[1])
        _max_key_sq_norms(ak_ref, kmax_ref, rows_per_step)
        _max_key_sq_norms(dk_ref, dkmax_ref, rows_per_step)

    d_maps = [_head_q_t(dq_all, g) for g in range(n_maps)]
    d_bound = jnp.sqrt(_head_sq_norms(dq_all, head_sel) * dkmax_ref[:, :1])
    d_shifts = [d_bound[g:g + 1] for g in range(n_maps)]
    a_bound = jnp.sqrt(_head_sq_norms(aq_all, head_sel) * kmax_ref[:, :1])
    qa_heads = [_head_q_t(aq_all, h) for h in range(N_DSA_HEADS)]
    lam_vecs = lam_ref[...]
    lam = (jnp.exp(jnp.sum(lam_vecs[0:1] * lam_vecs[1:2], axis=-1, keepdims=True))
           - jnp.exp(jnp.sum(lam_vecs[2:3] * lam_vecs[3:4], axis=-1, keepdims=True)) + lam_init)

    def diff_chunk(c, diagonal, bounded, after=None, before_sums=None, heads=range(N_DIFF_HEADS)):
        start = pl.multiple_of(c * blk, blk)
        if diagonal:
            bias = jnp.where(krow <= qcol, 0.0, MASK_VALUE)
        s_list, vt_list = [], []
        for h in heads:
            cols = slice(h * LANES, (h + 1) * LANES)
            kk = dk_ref[0, pl.ds(start, blk), cols]
            for comp in range(2):
                g = 2 * h + comp
                s = _dot(kk, d_maps[g])
                if bounded:
                    s = s - (d_shifts[g] if after is None else d_shifts[g] + after)
                s_list.append(s + bias if diagonal else s)
                vt_list.append(dvt_ref[0, c, cols, :])
        if bounded:
            return _bounded_update(s_list, vt_list, dl_ref, dacc_ref, before_sums, first=2 * heads[0])
        assert len(heads) == N_DIFF_HEADS
        _softmax_update(s_list, vt_list, dm_ref, dl_ref, dacc_ref)

    _init_softmax_state(dm_ref, dl_ref, dacc_ref)

    iq_all = iq_ref[0]
    wt = iwt_ref[0]
    q_heads = [_head_q_t(iq_all, h) for h in range(N_IDX_HEADS)]
    w_rows = [wt[h:h + 1, :] for h in range(N_IDX_HEADS)]

    n_pairs = (n_chunks + 1) // 2

    def pair_sum(hit):
        m = jnp.where(hit, 1.0, 0.0)
        return jnp.sum((m[0] + m[1]).reshape(blk // F32_SUBLANES, F32_SUBLANES, blk), axis=0)

    zero8 = jnp.zeros((F32_SUBLANES, blk), F32)

    def score_pair(j, mn, mx, w_first):
        for u in range(2):
            c = 2 * j + u
            start = pl.multiple_of(c * blk, blk)
            kk = ik_ref[0, pl.ds(start, blk), :]
            acc = jnp.zeros((blk, blk), F32)
            for h in range(N_IDX_HEADS):
                acc = acc + jnp.maximum(_dot(kk, q_heads[h]), 0.0) * (w_first if h == 0 else w_rows[h])
            causal = (c * blk + krow) <= qpos
            sc = jnp.where(causal, acc, -jnp.inf)
            score_ref[c] = sc
            mx = jnp.maximum(mx, jnp.max(sc, axis=0, keepdims=True))
            mn = jnp.minimum(mn, jnp.min(jnp.where(causal, acc, jnp.inf), axis=0, keepdims=True))
        return mn, mx

    def zero_counts(j):
        s = score_ref[pl.ds(2 * j, 2)]
        return pair_sum(s >= 0.0), pair_sum(s > 0.0)

    def score_body(j, carry):
        mn, mx, ge, gt = carry
        ge_p, gt_p = zero_counts(j - 1)
        after = jnp.sum(ge_p + gt_p, axis=0, keepdims=True) * 0.0
        mn, mx = score_pair(j, mn, mx, w_rows[0] + after)
        return mn, mx, ge + ge_p, gt + gt_p

    mn, mx = score_pair(0, jnp.full((1, blk), jnp.inf, F32), jnp.full((1, blk), -jnp.inf, F32), w_rows[0])
    mn, mx, ge8, gt8 = lax.fori_loop(1, n_pairs, score_body, (mn, mx, zero8, zero8))
    ge_p, gt_p = zero_counts(n_pairs - 1)
    ge0 = jnp.sum(ge8 + ge_p, axis=0, keepdims=True)
    gt0 = jnp.sum(gt8 + gt_p, axis=0, keepdims=True)


    def count_ge(cand):
        def body(j, acc):
            return acc + pair_sum(score_ref[pl.ds(2 * j, 2)] >= cand)
        return jnp.sum(lax.fori_loop(0, n_pairs, body, zero8), axis=0, keepdims=True)

    def count_ge_gt(cand):
        def body(j, carry):
            ge, gt = carry
            s = score_ref[pl.ds(2 * j, 2)]
            return ge + pair_sum(s >= cand), gt + pair_sum(s > cand)
        ge, gt = lax.fori_loop(0, n_pairs, body, (zero8, zero8))
        return jnp.sum(ge, axis=0, keepdims=True), jnp.sum(gt, axis=0, keepdims=True)

    n_valid = (qpos + 1).astype(F32)
    need = n_valid > kf
    kth_is_zero = jnp.logical_and(gt0 < kf, ge0 >= kf)
    nonneg = ge0 >= kf
    lo0 = jnp.where(nonneg, 0.0, mn)
    cnt0 = jnp.where(nonneg, ge0, n_valid)
    above_max = jnp.where(mx > 0.0, jnp.minimum(mx * 2.0, F32_MAX), 1.0)
    hi0 = jnp.where(nonneg, above_max, 0.0)
    done0 = jnp.logical_or(jnp.logical_not(need), jnp.logical_or(kth_is_zero, cnt0 == kf))
    active0 = jnp.where(done0, 0.0, 1.0)

    def midpoint(st):
        return 0.5 * st[0] + 0.5 * st[1]

    def halve(st, count):
        mid = midpoint(st)
        return narrow(st, mid, count(mid))

    def narrow(st, mid, cnt):
        lo, hi, cnt_lo, active = st
        conv = jnp.logical_or(mid <= lo, mid >= hi)
        upd = jnp.logical_and(active > 0.0, jnp.logical_not(conv))
        ge = cnt >= kf
        up_lo = jnp.logical_and(upd, ge)
        up_hi = jnp.logical_and(upd, jnp.logical_not(ge))
        lo = jnp.where(up_lo, mid, lo)
        cnt_lo = jnp.where(up_lo, cnt, cnt_lo)
        hi = jnp.where(up_hi, mid, hi)
        finished = jnp.logical_or(conv, jnp.logical_and(up_lo, cnt == kf))
        return lo, hi, cnt_lo, jnp.where(finished, 0.0, active)

    n_fused = jnp.minimum(i, BISECT_FUSED_STEPS)
    for k, v in enumerate((lo0, hi0, cnt0, active0)):
        bis_ref[k] = v

    for pairs in range(1, score_ref.shape[0] // 2 + 1):
        @pl.when(n_pairs == pairs)
        def _(pairs=pairs):
            def fused_step(c, st):
                mid = midpoint(st)
                n_first = min(FUSED_COUNTS_UNDER_QK, (pairs + 1) // 2)
                parts = []
                for js in (range(0, n_first), range(n_first, pairs)):
                    acc = zero8
                    for j in js:
                        acc = acc + pair_sum(score_ref[2 * j:2 * j + 2] >= mid)
                    parts.append(jnp.sum(acc, axis=0, keepdims=True))
                half = N_DIFF_HEADS // 2
                done = diff_chunk(c, False, True, after=parts[0] * 0.0, heads=range(0, half),
                                  before_sums=parts[1] * 0.0 if pairs > n_first else None)
                diff_chunk(c, False, True, after=done, heads=range(half, N_DIFF_HEADS))
                return narrow(st, mid, parts[0] + parts[1])
            st = lax.fori_loop(0, n_fused, fused_step, tuple(bis_ref[k] for k in range(4)))
            for k in range(4):
                bis_ref[k] = st[k]

    st = tuple(bis_ref[k] for k in range(4))
    st = lax.fori_loop(0, jnp.maximum(BISECT_FIRST_STEPS - n_fused, 0),
                       lambda _, s: halve(s, count_ge), st)

    def bis_cond(st):
        return jnp.logical_and(st[4] > 0.0, st[6] < MAX_BISECT_STEPS)

    def bis_body(st):
        lo, hi, cnt_lo, active = lax.fori_loop(0, BISECT_STEPS_PER_TEST, lambda _, s: halve(s, count_ge), st[:4])
        tied = jnp.max(jnp.where(jnp.logical_and(need, cnt_lo > kf), 1.0, 0.0))
        return lo, hi, cnt_lo, active, jnp.max(active), tied, st[6] + BISECT_STEPS_PER_TEST

    lo, _, cnt_lo, _, _, tied, _ = lax.while_loop(
        bis_cond, bis_body, st + (jnp.float32(1.0), jnp.float32(0.0), jnp.int32(0)))
    thr = jnp.where(need, lo, -F32_MAX)

    excess = jnp.logical_and(need, cnt_lo > kf)

    @pl.when(tied > 0.0)
    def _():
        _, gt = count_ge_gt(thr)
        allow = kf - gt
        r_i = lax.broadcasted_iota(jnp.int32, (blk, blk), 0)
        c_i = lax.broadcasted_iota(jnp.int32, (blk, blk), 1)
        earlier = jnp.where(c_i < r_i, 1.0, 0.0).astype(BF16)

        def tie_body(c, seen):
            s = score_ref[c]
            eq = jnp.logical_and(s == thr, excess)
            eqf = jnp.where(eq, 1.0, 0.0)
            rank = seen + _dot(earlier, eqf.astype(BF16))
            drop = jnp.logical_and(eq, rank >= allow)
            score_ref[c] = jnp.where(drop, -jnp.inf, s)
            return seen + jnp.sum(eqf, axis=0, keepdims=True)

        lax.fori_loop(0, n_chunks, tie_body, jnp.zeros((1, blk), F32))

    def attend(bounded, shift):
        def att_body(c, carry):
            start = pl.multiple_of(c * blk, blk)
            bias = jnp.where(score_ref[c] >= thr, shift, MASK_VALUE)
            s_list, vt_list = [], []
            for h in range(N_DSA_HEADS):
                pair = slice((h // 2) * LANES, (h // 2 + 1) * LANES)
                kk = ak_ref[0, pl.ds(start, blk), pair]
                s_list.append(_dot(kk, qa_heads[h]) + bias)
                vt_list.append(avt_ref[0, c, pair, :])
            if bounded:
                _bounded_update(s_list, vt_list, l_ref, acc_ref)
            else:
                _softmax_update(s_list, vt_list, m_ref, l_ref, acc_ref)
            return carry

        _init_softmax_state(m_ref, l_ref, acc_ref)
        lax.fori_loop(0, n_chunks, att_body, 0)

    if score_ref.shape[0] - 1 > BISECT_FUSED_STEPS:
        def diff_rest(c, carry):
            diff_chunk(c, False, True)
            return carry

        lax.fori_loop(n_fused, i, diff_rest, 0)

    attend(True, -jnp.max(a_bound, axis=0, keepdims=True))

    def write_outputs():
        for j in range(N_DSA_HEADS // 2):
            a = acc_ref[2 * j] * (1.0 / l_ref[2 * j])
            b = acc_ref[2 * j + 1] * (1.0 / l_ref[2 * j + 1])
            o_t = jnp.concatenate([a[:HEAD_DIM], b[HEAD_DIM:]], axis=0)
            oa_ref[0, j * LANES:(j + 1) * LANES, :] = o_t.astype(oa_ref.dtype)
        for h in range(N_DIFF_HEADS):
            out = (dacc_ref[2 * h] * (1.0 / dl_ref[2 * h])
                   - dacc_ref[2 * h + 1] * (lam / dl_ref[2 * h + 1]))
            out = out * lax.rsqrt(jnp.mean(out * out, axis=0, keepdims=True) + NORM_EPS)
            out = out * sw_ref[...] * (1.0 - lam_init)
            ob_ref[0, h * LANES:(h + 1) * LANES, :] = out.astype(ob_ref.dtype)

    diff_chunk(i, True, True)
    dsa_bad = _softmax_underflowed(l_ref)
    diff_bad = _softmax_underflowed(dl_ref)
    write_outputs()

    @pl.when(jnp.logical_or(dsa_bad, diff_bad))
    def _():
        @pl.when(dsa_bad)
        def _():
            attend(False, 0.0)

        @pl.when(diff_bad)
        def _():
            def body(c, carry):
                diff_chunk(c, False, False)
                return carry
            _init_softmax_state(dm_ref, dl_ref, dacc_ref)
            lax.fori_loop(0, i, body, 0)
            diff_chunk(i, True, False)

        write_outputs()


def _mixers(iq, ik2, iwt, aq, ak, avt, lam_vecs, dq, dk, dvt, subln_col, *, blk, lam_init):
    bsz, seq, _ = aq.shape
    top_k = min(TOPK_MAX, seq // 4)
    n_chunks = seq // blk
    assert n_chunks % 2 == 0, "key chunks are walked in pairs"
    qblk = lambda b, i: (b, i, 0)
    whole = lambda b, i: (b, 0, 0)
    whole_t = lambda b, i: (b, 0, 0, 0)
    n_maps = 2 * N_DIFF_HEADS
    q_spec = pl.BlockSpec((1, blk, 512), qblk)
    kv_spec = pl.BlockSpec((1, seq, 512), whole)
    vt_spec = pl.BlockSpec((1, n_chunks, 512, blk), whole_t)
    return pl.pallas_call(
        functools.partial(_mixer_kernel, blk=blk, top_k=top_k, lam_init=lam_init),
        grid=(bsz, n_chunks),
        in_specs=[q_spec,
                  pl.BlockSpec((1, seq, LANES), whole),
                  pl.BlockSpec((1, N_IDX_HEADS, blk), lambda b, i: (b, 0, i)),
                  q_spec, kv_spec, vt_spec,
                  _const_spec(lam_vecs.shape),
                  q_spec, kv_spec, vt_spec,
                  _const_spec((DIFF_V_DIM, 1))],
        out_specs=[pl.BlockSpec((1, 512, blk), lambda b, i: (b, 0, i))] * 2,
        out_shape=[jax.ShapeDtypeStruct((bsz, 512, seq), BF16)] * 2,
        scratch_shapes=[pltpu.VMEM((n_chunks, blk, blk), F32),
                        pltpu.VMEM((N_DSA_HEADS, 1, blk), F32),
                        pltpu.VMEM((N_DSA_HEADS, 1, blk), F32),
                        pltpu.VMEM((N_DSA_HEADS, LANES, blk), F32),
                        pltpu.VMEM((BF16_SUBLANES, LANES), F32),
                        pltpu.VMEM((n_maps, 1, blk), F32),
                        pltpu.VMEM((n_maps, 1, blk), F32),
                        pltpu.VMEM((n_maps, DIFF_V_DIM, blk), F32),
                        pltpu.VMEM((BF16_SUBLANES, LANES), F32),
                        pltpu.VMEM((4, 1, blk), F32)],
        compiler_params=pltpu.CompilerParams(
            dimension_semantics=("parallel", "arbitrary"), vmem_limit_bytes=VMEM_LIMIT_BYTES),
        name="mixers",
    )(iq, ik2, iwt, aq, ak, avt, lam_vecs, dq, dk, dvt, subln_col)


def _out_mlp_kernel(x_ref, ma_ref, mb_ref, g1_ref, sh_ref, sc_ref, g2_ref, n2_ref, nf_ref,
                    woa_ref, wob_ref, w1_ref, w2_ref, o_ref, *, ff_chunk):
    x = x_ref[0]
    o = _dot_tn(ma_ref[0], woa_ref[...]) + _dot_tn(mb_ref[0], wob_ref[...])
    x1 = x + g1_ref[0] * o
    ms = jnp.mean(x1 * x1, axis=-1, keepdims=True)
    h = x1 * lax.rsqrt(ms + NORM_EPS) * n2_ref[...]
    hb = (h * (1.0 + sc_ref[0]) + sh_ref[0]).astype(BF16)
    d_ff = w1_ref.shape[1]
    ff = jnp.zeros_like(x)
    for j in range(d_ff // ff_chunk):
        u = jnp.maximum(_dot(hb, w1_ref[:, j * ff_chunk:(j + 1) * ff_chunk]), 0.0)
        ff = ff + _dot((u * u).astype(BF16), w2_ref[j * ff_chunk:(j + 1) * ff_chunk, :])
    x2 = x1 + g2_ref[0] * ff
    ms2 = jnp.mean(x2 * x2, axis=-1, keepdims=True)
    o_ref[0] = x2 * lax.rsqrt(ms2 + NORM_EPS) * nf_ref[...]


def _out_mlp(x, mix_a, mix_b, g1, sh2, sc2, g2, n2, nf, wo_a, wo_b, w1, w2, *, tile):
    bsz, seq, d = x.shape
    tok = lambda b, i: (b, i, 0)
    per_b = lambda b, i: (b, 0, 0)
    mod_spec = pl.BlockSpec((1, 1, d), per_b)
    return pl.pallas_call(
        functools.partial(_out_mlp_kernel, ff_chunk=1024),
        grid=(bsz, seq // tile),
        in_specs=[pl.BlockSpec((1, tile, d), tok),
                  pl.BlockSpec((1, 512, tile), lambda b, i: (b, 0, i)),
                  pl.BlockSpec((1, 512, tile), lambda b, i: (b, 0, i)),
                  mod_spec, mod_spec, mod_spec, mod_spec,
                  _const_spec((1, d)), _const_spec((1, d)),
                  _const_spec(wo_a.shape), _const_spec(wo_b.shape),
                  _const_spec(w1.shape), _const_spec(w2.shape)],
        out_specs=pl.BlockSpec((1, tile, d), tok),
        out_shape=jax.ShapeDtypeStruct((bsz, seq, d), F32),
        compiler_params=pltpu.CompilerParams(
            dimension_semantics=("parallel", "parallel"), vmem_limit_bytes=VMEM_LIMIT_BYTES),
        name="out_mlp",
    )(x, mix_a, mix_b, g1, sh2, sc2, g2, n2, nf, wo_a, wo_b, w1, w2)


def _pad_cols(w, width):
    return jnp.pad(w, ((0, 0), (0, width - w.shape[1])))


def kernel(x, c, positions, w_ada, b_ada, norm1_w, w_in, idx_k_ln_w, idx_k_ln_b, lambda_q1, lambda_k1,
           lambda_q2, lambda_k2, subln_w, w_out, norm2_w, w_ff1, w_ff2, norm_f_w):
    bsz, seq, d = x.shape
    depth = w_ada.shape[0]
    assert depth == 1, "the fused final RMSNorm assumes a single layer"
    half = HEAD_DIM // 2
    inv_freq = ROPE_THETA ** (-jnp.arange(half, dtype=F32) / half)
    invf = jnp.tile(inv_freq, LANES // half).reshape(1, LANES)
    pos3 = positions.reshape(bsz, seq, 1)
    blk = min(ATT_BLOCK, seq)
    tile = min(512, seq)

    for l in range(depth):
        mod = _adaln(c, w_ada[l], b_ada[l])
        sh1, sc1, g1, sh2, sc2, g2 = [m.reshape(bsz, 1, d) for m in jnp.split(mod, 6, axis=-1)]

        w = w_in[l]
        wm = jnp.concatenate([w[:, :1024], w[:, 1536:2048], w[:, 2120:3144]], axis=1).astype(BF16)
        ws = _pad_cols(w[:, 2048:2112], LANES).astype(BF16)
        wwt = jnp.pad(w[:, 2112:2120].T, ((0, BF16_SUBLANES - N_IDX_HEADS), (0, 0))).astype(BF16)
        wavt = w[:, 1024:1536].T.astype(BF16)
        wdvt = w[:, 3144:3656].T.astype(BF16)
        lnw = _pad_cols(idx_k_ln_w[l].reshape(1, IDX_DIM), LANES)
        lnb = _pad_cols(idx_k_ln_b[l].reshape(1, IDX_DIM), LANES)

        aq, ak, avt, iq, ik2, iwt, dq, dk, dvt = _in_proj(
            x, pos3, sh1, sc1, norm1_w[l].reshape(1, d), invf, wm, ws, wwt, wavt, wdvt, lnw, lnb,
            tile=tile, chunk=blk)

        lam_vecs = jnp.stack([lambda_q1[l], lambda_k1[l], lambda_q2[l], lambda_k2[l]]).astype(F32)
        lam_init = 0.8 - 0.6 * math.exp(-0.3 * l)
        out_a, out_b = _mixers(iq, ik2, iwt, aq, ak, avt, lam_vecs, dq, dk, dvt,
                               subln_w[l].reshape(DIFF_V_DIM, 1), blk=blk, lam_init=lam_init)

        wo = w_out[l].astype(BF16)
        x = _out_mlp(x, out_a, out_b, g1, sh2, sc2, g2, norm2_w[l].reshape(1, d), norm_f_w.reshape(1, d),
                     wo[:512], wo[512:], w_ff1[l].astype(BF16), w_ff2[l].astype(BF16), tile=tile)
    return x
```
